```python
import math
import jax, jax.numpy as jnp
from jax import lax
import numpy as np

D_MODEL = 1024
BATCH = 4
SEQ = 8192
DEPTH = 1
DEC_BATCH = 128
DEC_SEQ = 4
PAST_LEN = 16384
PAGE_SIZE = 128

POOL_WIDTH = D_MODEL // 2
POOL_WINDOWS = (2, 4, 8, 16)
N_POOL_GROUPS = len(POOL_WINDOWS)
POOL_GROUP_DIM = POOL_WIDTH // N_POOL_GROUPS
POOL_BUF = max(POOL_WINDOWS) - 1
MLA_HEADS = 4
QK_NOPE_DIM = 128
QK_ROPE_DIM = 64
V_HEAD_DIM = 128
Q_RANK = D_MODEL // 4
KV_RANK = D_MODEL // 8
ROPE_THETA = 10000.0
Q_BLOCK = 128
IN_WIDTH = POOL_WIDTH + Q_RANK + KV_RANK + QK_ROPE_DIM
MIX_WIDTH = POOL_WIDTH + MLA_HEADS * V_HEAD_DIM
MEM_TOKENS = 256
MEM_HEADS = 4
MEM_HEAD_DIM = 128
MEM_WIDTH = MEM_HEADS * MEM_HEAD_DIM
N_EXPERTS = 32
TOP_K = 4
D_EXPERT = D_MODEL
SWIGLU_LIMIT = 7.0
SWIGLU_ALPHA = 1.702
EXPERT_BLOCK = 128
LN_EPS = 1e-5
RMS_EPS = 1e-6
DEEPNORM_ALPHA = (2.0 * DEPTH) ** 0.25
DEEPNORM_BETA = (8.0 * DEPTH) ** -0.25

kernel_name = "hybrid_pool_mla_moe_deepnorm_step"


def layer_norm(x, g, b):
    xf = x.astype(jnp.float32)
    mu = jnp.mean(xf, axis=-1, keepdims=True)
    var = jnp.mean(jnp.square(xf - mu), axis=-1, keepdims=True)
    y = (xf - mu) * lax.rsqrt(var + LN_EPS) * g.astype(jnp.float32) + b.astype(jnp.float32)
    return y.astype(x.dtype)


def rms_norm(x, g):
    xf = x.astype(jnp.float32)
    y = xf * lax.rsqrt(jnp.mean(jnp.square(xf), axis=-1, keepdims=True) + RMS_EPS) * g.astype(jnp.float32)
    return y.astype(x.dtype)


def apply_rope(x, pos):
    half = QK_ROPE_DIM // 2
    inv_freq = ROPE_THETA ** (-jnp.arange(half, dtype=jnp.float32) / half)
    ang = pos.astype(jnp.float32)[:, None] * inv_freq[None, :]
    cos = jnp.cos(ang)[None, :, None, :]
    sin = jnp.sin(ang)[None, :, None, :]
    xf = x.astype(jnp.float32)
    x1, x2 = xf[..., :half], xf[..., half:]
    return jnp.concatenate([x1 * cos - x2 * sin, x2 * cos + x1 * sin], axis=-1).astype(x.dtype)


def mixer_projections(x, pos, w_in, q_norm_g, w_uq, kv_norm_g, w_uk):
    b, t, _ = x.shape
    h = x @ w_in
    o1 = POOL_WIDTH
    o2 = o1 + Q_RANK
    o3 = o2 + KV_RANK
    u = h[..., :o1]
    q = (rms_norm(h[..., o1:o2], q_norm_g) @ w_uq).reshape(b, t, MLA_HEADS, QK_NOPE_DIM + QK_ROPE_DIM)
    q_lat = jnp.einsum('bthn,rhn->bthr', q[..., :QK_NOPE_DIM], w_uk)
    q_rope = apply_rope(q[..., QK_NOPE_DIM:], pos)
    latent = rms_norm(h[..., o2:o3], kv_norm_g)
    k_rope = apply_rope(h[..., o3:][:, :, None, :], pos)[:, :, 0, :]
    return u, q_lat, q_rope, latent, k_rope


def pool_mixer(u_prev, u, pos0, w_pool, pool_scale):
    b, t, c = u.shape
    u_ext = jnp.concatenate([u_prev.astype(u.dtype), u], axis=1)
    cs = jnp.cumsum(u_ext.astype(jnp.float32), axis=1)
    cs = jnp.concatenate([jnp.zeros((b, 1, c), jnp.float32), cs], axis=1)
    end = cs[:, POOL_BUF + 1:POOL_BUF + 1 + t]
    pos = pos0 + jnp.arange(t)
    uf = u.astype(jnp.float32)
    outs = []
    for g, w in enumerate(POOL_WINDOWS):
        sl = slice(g * POOL_GROUP_DIM, (g + 1) * POOL_GROUP_DIM)
        start = cs[:, POOL_BUF + 1 - w:POOL_BUF + 1 - w + t, sl]
        count = jnp.minimum(w, pos + 1).astype(jnp.float32)[None, :, None]
        outs.append((end[..., sl] - start) / count - uf[..., sl])
    diff = jnp.stack(outs, axis=2)
    y = jnp.einsum('btgc,gcd->btgd', diff, w_pool.astype(jnp.float32)).reshape(b, t, c)
    y = y * pool_scale.astype(jnp.float32)
    return y.astype(u.dtype), u_ext[:, -POOL_BUF:]


def mla_attention_prompt(q_lat, q_rope, latent, k_rope):
    b, s, h, r = q_lat.shape
    n_blk = s // Q_BLOCK
    scale = (QK_NOPE_DIM + QK_ROPE_DIM) ** -0.5
    lat_f = latent.astype(jnp.float32)
    kr_f = k_rope.astype(jnp.float32)
    key_pos = jnp.arange(s)

    def block(i):
        start = i * Q_BLOCK
        ql = lax.dynamic_slice_in_dim(q_lat, start, Q_BLOCK, axis=1).astype(jnp.float32)
        qr = lax.dynamic_slice_in_dim(q_rope, start, Q_BLOCK, axis=1).astype(jnp.float32)
        sc = (jnp.einsum('bqhr,bkr->bhqk', ql, lat_f) + jnp.einsum('bqhd,bkd->bhqk', qr, kr_f)) * scale
        q_pos = start + jnp.arange(Q_BLOCK)
        sc = jnp.where(key_pos[None, :] <= q_pos[:, None], sc, -jnp.inf)
        p = jax.nn.softmax(sc, axis=-1)
        return jnp.einsum('bhqk,bkr->bqhr', p, lat_f)

    o = lax.map(block, jnp.arange(n_blk))
    return o.transpose(1, 0, 2, 3, 4).reshape(b, s, h, r).astype(q_lat.dtype)


def mla_attention_sample(q_lat, q_rope, latent_new, k_rope_new, lat_past, kr_past):
    t = q_lat.shape[1]
    past_len = lat_past.shape[1]
    scale = (QK_NOPE_DIM + QK_ROPE_DIM) ** -0.5
    ql = q_lat.astype(jnp.float32)
    qr = q_rope.astype(jnp.float32)
    lp = lat_past.astype(jnp.float32)
    ln_ = latent_new.astype(jnp.float32)
    sp = jnp.einsum('bqhr,bkr->bhqk', ql, lp) + jnp.einsum('bqhd,bkd->bhqk', qr, kr_past.astype(jnp.float32))
    sn = jnp.einsum('bqhr,bkr->bhqk', ql, ln_) + jnp.einsum('bqhd,bkd->bhqk', qr, k_rope_new.astype(jnp.float32))
    causal = jnp.tril(jnp.ones((t, t), dtype=bool))
    sn = jnp.where(causal, sn, -jnp.inf)
    p = jax.nn.softmax(jnp.concatenate([sp, sn], axis=-1) * scale, axis=-1)
    o = jnp.einsum('bhqk,bkr->bqhr', p[..., :past_len], lp) + jnp.einsum('bhqk,bkr->bqhr', p[..., past_len:], ln_)
    return o.astype(q_lat.dtype)


def mixing_output(pool_y, o_lat, w_uv, w_out):
    b, t, _ = pool_y.shape
    mla_y = jnp.einsum('bthr,rhv->bthv', o_lat, w_uv).reshape(b, t, MLA_HEADS * V_HEAD_DIM)
    return jnp.concatenate([pool_y, mla_y.astype(pool_y.dtype)], axis=-1) @ w_out


def memory_kv(mem, w_mk, w_mv):
    b, m, _ = mem.shape
    k = (mem @ w_mk).reshape(b, m, MEM_HEADS, MEM_HEAD_DIM)
    v = (mem @ w_mv).reshape(b, m, MEM_HEADS, MEM_HEAD_DIM)
    return k, v


def memory_attention(x, mem_k, mem_v, w_mq, w_mo):
    b, t, _ = x.shape
    q = (x @ w_mq).reshape(b, t, MEM_HEADS, MEM_HEAD_DIM).astype(jnp.float32)
    sc = jnp.einsum('bthd,bmhd->bhtm', q, mem_k.astype(jnp.float32)) * (MEM_HEAD_DIM ** -0.5)
    p = jax.nn.softmax(sc, axis=-1)
    o = jnp.einsum('bhtm,bmhd->bthd', p, mem_v.astype(jnp.float32)).reshape(b, t, MEM_WIDTH)
    return o.astype(x.dtype) @ w_mo


def clamped_swiglu(h):
    gate = jnp.minimum(h[..., :D_EXPERT], SWIGLU_LIMIT)
    up = jnp.clip(h[..., D_EXPERT:], -SWIGLU_LIMIT, SWIGLU_LIMIT)
    return gate * jax.nn.sigmoid(SWIGLU_ALPHA * gate) * (up + 1.0)


def moe_ffn(x, w_router, b_router, w_gate_up, b_gate_up, w_down, b_down):
    b, t, d = x.shape
    n_tok = b * t
    xs = x.reshape(n_tok, d)
    logits = (xs @ w_router).astype(jnp.float32) + b_router.astype(jnp.float32)
    top_vals, top_idx = lax.top_k(logits, TOP_K)
    gates = jax.nn.softmax(top_vals, axis=-1)
    m = n_tok * TOP_K
    flat_e = top_idx.reshape(m)
    order = jnp.argsort(flat_e)
    sorted_e = flat_e[order]
    counts = jnp.bincount(flat_e, length=N_EXPERTS)
    padded = ((counts + EXPERT_BLOCK - 1) // EXPERT_BLOCK) * EXPERT_BLOCK
    start = jnp.cumsum(counts) - counts
    pend = jnp.cumsum(padded)
    pstart = pend - padded
    dest_sorted = (pstart[sorted_e] + jnp.arange(m) - start[sorted_e]).astype(jnp.int32)
    dest = jnp.zeros((m,), jnp.int32).at[order].set(dest_sorted)
    n_blk = (m + EXPERT_BLOCK - 1) // EXPERT_BLOCK + N_EXPERTS
    n_rows = n_blk * EXPERT_BLOCK
    row_tok = jnp.zeros((n_rows,), jnp.int32).at[dest].set((jnp.arange(m) // TOP_K).astype(jnp.int32))
    blk_exp = jnp.clip(jnp.searchsorted(pend, jnp.arange(n_blk) * EXPERT_BLOCK, side='right'), 0, N_EXPERTS - 1)
    x_rows = xs[row_tok].reshape(n_blk, EXPERT_BLOCK, d)

    def expert_block(args):
        xb, e = args
        h = xb @ w_gate_up[e] + b_gate_up[e]
        return clamped_swiglu(h) @ w_down[e] + b_down[e]

    y_rows = lax.map(expert_block, (x_rows, blk_exp)).reshape(n_rows, d)
    y = y_rows[dest].reshape(n_tok, TOP_K, d)
    out = jnp.einsum('tk,tkd->td', gates, y.astype(jnp.float32))
    return out.reshape(b, t, d).astype(x.dtype)


def finish_layer(x, mix_out, mem_k, mem_v, ln1_g, ln1_b, w_mq, w_mo, ln2_g, ln2_b,
                 w_router, b_router, w_gate_up, b_gate_up, w_down, b_down, ln3_g, ln3_b):
    x = layer_norm(DEEPNORM_ALPHA * x + mix_out, ln1_g, ln1_b)
    x = layer_norm(DEEPNORM_ALPHA * x + memory_attention(x, mem_k, mem_v, w_mq, w_mo), ln2_g, ln2_b)
    x = layer_norm(DEEPNORM_ALPHA * x + moe_ffn(x, w_router, b_router, w_gate_up, b_gate_up, w_down, b_down), ln3_g, ln3_b)
    return x


def setup_inputs(seed: int = 0) -> dict:
    key = jax.random.key(seed)
    ks = jax.random.split(key, 40)

    def nrm(k, shape, scale):
        return jax.random.normal(k, shape, jnp.float32) * scale

    n_pages = PAST_LEN // PAGE_SIZE
    n_phys = (DEC_BATCH * n_pages * 5) // 4
    L = DEPTH
    page_table = jax.random.permutation(ks[0], n_phys)[:DEC_BATCH * n_pages].reshape(DEC_BATCH, n_pages).astype(jnp.int32)
    return {
        "x_prompt": nrm(ks[1], (BATCH, SEQ, D_MODEL), 1.0),
        "x_sample": nrm(ks[2], (DEC_BATCH, DEC_SEQ, D_MODEL), 1.0),
        "mem_prompt": nrm(ks[3], (BATCH, MEM_TOKENS, D_MODEL), 1.0),
        "cache_kv_latent": nrm(ks[4], (L, n_phys, PAGE_SIZE, KV_RANK), 1.0),
        "cache_k_rope": nrm(ks[5], (L, n_phys, PAGE_SIZE, QK_ROPE_DIM), 1.0),
        "cache_mem_k": nrm(ks[6], (L, DEC_BATCH, MEM_TOKENS, MEM_HEADS, MEM_HEAD_DIM), 1.0),
        "cache_mem_v": nrm(ks[7], (L, DEC_BATCH, MEM_TOKENS, MEM_HEADS, MEM_HEAD_DIM), 1.0),
        "state_pool": nrm(ks[8], (L, DEC_BATCH, POOL_BUF, POOL_WIDTH), 1.0),
        "page_table": page_table,
        "w_in": nrm(ks[9], (L, D_MODEL, IN_WIDTH), D_MODEL ** -0.5),
        "w_pool": nrm(ks[10], (L, N_POOL_GROUPS, POOL_GROUP_DIM, POOL_GROUP_DIM), POOL_GROUP_DIM ** -0.5),
        "pool_scale": 1.0 + nrm(ks[11], (L, POOL_WIDTH), 0.1),
        "q_norm_g": 1.0 + nrm(ks[12], (L, Q_RANK), 0.05),
        "w_uq": nrm(ks[13], (L, Q_RANK, MLA_HEADS * (QK_NOPE_DIM + QK_ROPE_DIM)), Q_RANK ** -0.5),
        "kv_norm_g": 1.0 + nrm(ks[14], (L, KV_RANK), 0.05),
        "w_uk": nrm(ks[15], (L, KV_RANK, MLA_HEADS, QK_NOPE_DIM), KV_RANK ** -0.5),
        "w_uv": nrm(ks[16], (L, KV_RANK, MLA_HEADS, V_HEAD_DIM), KV_RANK ** -0.5),
        "w_out": nrm(ks[17], (L, MIX_WIDTH, D_MODEL), MIX_WIDTH ** -0.5 * DEEPNORM_BETA),
        "ln1_g": 1.0 + nrm(ks[18], (L, D_MODEL), 0.05),
        "ln1_b": nrm(ks[19], (L, D_MODEL), 0.02),
        "w_mq": nrm(ks[20], (L, D_MODEL, MEM_WIDTH), D_MODEL ** -0.5),
        "w_mk": nrm(ks[21], (L, D_MODEL, MEM_WIDTH), D_MODEL ** -0.5),
        "w_mv": nrm(ks[22], (L, D_MODEL, MEM_WIDTH), D_MODEL ** -0.5),
        "w_mo": nrm(ks[23], (L, MEM_WIDTH, D_MODEL), MEM_WIDTH ** -0.5 * DEEPNORM_BETA),
        "ln2_g": 1.0 + nrm(ks[24], (L, D_MODEL), 0.05),
        "ln2_b": nrm(ks[25], (L, D_MODEL), 0.02),
        "w_router": nrm(ks[26], (L, D_MODEL, N_EXPERTS), D_MODEL ** -0.5),
        "b_router": nrm(ks[27], (L, N_EXPERTS), 0.01),
        "w_gate_up": nrm(ks[28], (L, N_EXPERTS, D_MODEL, 2 * D_EXPERT), D_MODEL ** -0.5),
        "b_gate_up": nrm(ks[29], (L, N_EXPERTS, 2 * D_EXPERT), 0.02),
        "w_down": nrm(ks[30], (L, N_EXPERTS, D_EXPERT, D_MODEL), D_EXPERT ** -0.5 * DEEPNORM_BETA),
        "b_down": nrm(ks[31], (L, N_EXPERTS, D_MODEL), 0.02),
        "ln3_g": 1.0 + nrm(ks[32], (L, D_MODEL), 0.05),
        "ln3_b": nrm(ks[33], (L, D_MODEL), 0.02),
    }


def reference(x_prompt, x_sample, mem_prompt, cache_kv_latent, cache_k_rope, cache_mem_k, cache_mem_v,
              state_pool, page_table, w_in, w_pool, pool_scale, q_norm_g, w_uq, kv_norm_g, w_uk, w_uv,
              w_out, ln1_g, ln1_b, w_mq, w_mk, w_mv, w_mo, ln2_g, ln2_b, w_router, b_router,
              w_gate_up, b_gate_up, w_down, b_down, ln3_g, ln3_b):
    b_p, s_p, _ = x_prompt.shape
    b_s, s_s, _ = x_sample.shape
    past_len = page_table.shape[1] * cache_kv_latent.shape[2]
    pos_p = jnp.arange(s_p)
    pos_s = past_len + jnp.arange(s_s)
    xp = x_prompt
    xs = x_sample
    lat_p_l, kr_p_l, pool_p_l, mk_p_l, mv_p_l = [], [], [], [], []
    lat_s_l, kr_s_l, pool_s_l = [], [], []
    for l in range(DEPTH):
        u, ql, qr, lat, kr = mixer_projections(xp, pos_p, w_in[l], q_norm_g[l], w_uq[l], kv_norm_g[l], w_uk[l])
        pool_y, pool_state_p = pool_mixer(jnp.zeros((b_p, POOL_BUF, POOL_WIDTH), u.dtype), u, 0, w_pool[l], pool_scale[l])
        o_lat = mla_attention_prompt(ql, qr, lat, kr)
        mix = mixing_output(pool_y, o_lat, w_uv[l], w_out[l])
        mk, mv = memory_kv(mem_prompt, w_mk[l], w_mv[l])
        xp = finish_layer(xp, mix, mk, mv, ln1_g[l], ln1_b[l], w_mq[l], w_mo[l], ln2_g[l], ln2_b[l],
                          w_router[l], b_router[l], w_gate_up[l], b_gate_up[l], w_down[l], b_down[l], ln3_g[l], ln3_b[l])
        lat_p_l.append(lat)
        kr_p_l.append(kr)
        pool_p_l.append(pool_state_p)
        mk_p_l.append(mk)
        mv_p_l.append(mv)
        u, ql, qr, lat, kr = mixer_projections(xs, pos_s, w_in[l], q_norm_g[l], w_uq[l], kv_norm_g[l], w_uk[l])
        pool_y, pool_state_s = pool_mixer(state_pool[l], u, past_len, w_pool[l], pool_scale[l])
        lat_past = cache_kv_latent[l][page_table].reshape(b_s, past_len, KV_RANK)
        kr_past = cache_k_rope[l][page_table].reshape(b_s, past_len, QK_ROPE_DIM)
        o_lat = mla_attention_sample(ql, qr, lat, kr, lat_past, kr_past)
        mix = mixing_output(pool_y, o_lat, w_uv[l], w_out[l])
        xs = finish_layer(xs, mix, cache_mem_k[l], cache_mem_v[l], ln1_g[l], ln1_b[l], w_mq[l], w_mo[l], ln2_g[l], ln2_b[l],
                          w_router[l], b_router[l], w_gate_up[l], b_gate_up[l], w_down[l], b_down[l], ln3_g[l], ln3_b[l])
        lat_s_l.append(lat)
        kr_s_l.append(kr)
        pool_s_l.append(pool_state_s)
    kv_latent_prompt = jnp.stack(lat_p_l)
    k_rope_prompt = jnp.stack(kr_p_l)
    pool_state_prompt = jnp.stack(pool_p_l)
    mem_k_prompt = jnp.stack(mk_p_l)
    mem_v_prompt = jnp.stack(mv_p_l)
    kv_latent_sample = jnp.stack(lat_s_l)
    k_rope_sample = jnp.stack(kr_s_l)
    pool_state_sample = jnp.stack(pool_s_l)
    return (xp, xs, kv_latent_prompt, k_rope_prompt, pool_state_prompt, mem_k_prompt, mem_v_prompt,
            kv_latent_sample, k_rope_sample, pool_state_sample)
```

```python
import functools

import numpy as np
import jax
import jax.numpy as jnp
from jax import lax
from jax.experimental import pallas as pl
from jax.experimental.pallas import tpu as pltpu

F32 = jnp.float32
BF16 = jnp.bfloat16
I32 = jnp.int32

D_MODEL = 1024
POOL_WIDTH = 512
POOL_WINDOWS = (2, 4, 8, 16)
POOL_GROUP_DIM = 128
POOL_BUF = 15
POOL_HALO = 16
MLA_HEADS = 4
QK_NOPE_DIM = 128
QK_ROPE_DIM = 64
ROPE_HALF = 32
V_HEAD_DIM = 128
Q_RANK = 256
KV_RANK = 128
ROPE_THETA = 10000.0
QK_PAD = 256
MEM_HEADS = 4
MEM_HEAD_DIM = 128
MEM_WIDTH = 512
N_EXPERTS = 32
TOP_K = 4
D_EXPERT = 1024
SWIGLU_LIMIT = 7.0
SWIGLU_ALPHA = 1.702
LN_EPS = 1e-5
RMS_EPS = 1e-6
DEPTH = 1
DEEPNORM_ALPHA = (2.0 * DEPTH) ** 0.25
ATTN_SCALE = (QK_NOPE_DIM + QK_ROPE_DIM) ** -0.5
MEM_SCALE = MEM_HEAD_DIM ** -0.5

LANES = 128
VMEM_LIMIT = 48 * 1024 * 1024

TOKEN_BLOCK = 256
ATTN_Q_BLOCK = 128
ATTN_K_BLOCK = 256
PAGES_PER_STEP = 16
MEM_BATCH_BLOCK = 8
EXPERT_ROWS = 256
EXPERT_CHUNK = 512


def _dot(a, b):
    return jnp.dot(a, b, preferred_element_type=F32)


def _dot_nt(a, b):
    return lax.dot_general(a, b, (((1,), (1,)), ((), ())), preferred_element_type=F32)


def _rms(x, g):
    return x * lax.rsqrt(jnp.mean(x * x, axis=-1, keepdims=True) + RMS_EPS) * g


def _layer_norm(x, g, b):
    mu = jnp.mean(x, axis=-1, keepdims=True)
    xc = x - mu
    var = jnp.mean(xc * xc, axis=-1, keepdims=True)
    return xc * lax.rsqrt(var + LN_EPS) * g + b


def _params(*semantics):
    return pltpu.CompilerParams(dimension_semantics=semantics, vmem_limit_bytes=VMEM_LIMIT)


def _full(shape):
    n = len(shape)
    return pl.BlockSpec(shape, lambda *_: (0,) * n)


def _project(x_bf, w_in_ref, qg_ref, w_uq_ref, kvg_ref, w_ukt_ref, cos, sin):
    h = _dot(x_bf, w_in_ref[...])
    u = h[:, :POOL_WIDTH]
    qn = _rms(h[:, POOL_WIDTH:POOL_WIDTH + Q_RANK], qg_ref[...])
    q = _dot(qn.astype(BF16), w_uq_ref[...])
    q_lat, q_rope = [], []
    for hd in range(MLA_HEADS):
        nope = q[:, QK_PAD * hd:QK_PAD * hd + QK_NOPE_DIM]
        rp = q[:, QK_PAD * hd + QK_NOPE_DIM:QK_PAD * (hd + 1)]
        q_lat.append(_dot(nope.astype(BF16), w_ukt_ref[hd]) * ATTN_SCALE)
        q_rope.append((rp * cos + pltpu.roll(rp, 64, 1) * sin) * ATTN_SCALE)
    o2 = POOL_WIDTH + Q_RANK
    lat = _rms(h[:, o2:o2 + KV_RANK], kvg_ref[...])
    kp = h[:, o2 + KV_RANK:]
    k_rope = kp * cos + pltpu.roll(kp, 64, 1) * sin
    return u, q_lat, q_rope, lat, k_rope


def _compact_rope(r):
    lane = lax.broadcasted_iota(I32, r.shape, 1)
    moved = pltpu.roll(r, 96, 1)
    return jnp.where(lane < ROPE_HALF, r, jnp.where(lane < QK_ROPE_DIM, moved, 0.0))


def _pool_group_out(diff, g, w_pool_ref, pscale_ref):
    cols = slice(POOL_GROUP_DIM * g, POOL_GROUP_DIM * (g + 1))
    return _dot(diff.astype(BF16), w_pool_ref[g]) * pscale_ref[:, cols]


def _proj_prompt_kernel(x_ref, w_in_ref, qg_ref, w_uq_ref, kvg_ref, w_ukt_ref, w_pool_ref, pscale_ref,
                        cos_ref, sin_ref,
                        qcat_ref, kcat_ref, lat_ref, kr_ref, pooly_ref, pstate_ref, ubuf):
    j = pl.program_id(1)
    tm = x_ref.shape[0]

    @pl.when(j == 0)
    def _():
        ubuf[0:POOL_HALO, :] = jnp.zeros((POOL_HALO, POOL_WIDTH), F32)

    u, q_lat, q_rope, lat, k_rope = _project(
        x_ref[...].astype(BF16), w_in_ref, qg_ref, w_uq_ref, kvg_ref, w_ukt_ref, cos_ref[...], sin_ref[...])
    for hd in range(MLA_HEADS):
        qcat_ref[hd, :, 0:KV_RANK] = q_lat[hd].astype(BF16)
        qcat_ref[hd, :, KV_RANK:QK_PAD] = q_rope[hd].astype(BF16)
    lat_ref[...] = lat
    kcat_ref[:, 0:KV_RANK] = lat.astype(BF16)
    kcat_ref[:, KV_RANK:QK_PAD] = k_rope.astype(BF16)
    kr_ref[...] = _compact_rope(k_rope)[:, :QK_ROPE_DIM]

    ubuf[POOL_HALO:POOL_HALO + tm, :] = u
    pos = j * tm + lax.broadcasted_iota(I32, (tm, 1), 0)
    for g, w in enumerate(POOL_WINDOWS):
        cols = slice(POOL_GROUP_DIM * g, POOL_GROUP_DIM * (g + 1))
        ug = u[:, cols]
        ssum = ug
        for k in range(1, w):
            ssum = ssum + ubuf[POOL_HALO - k:POOL_HALO - k + tm, cols]
        count = jnp.minimum(w, pos + 1).astype(F32)
        pooly_ref[:, cols] = _pool_group_out(ssum / count - ug, g, w_pool_ref, pscale_ref).astype(BF16)
    tail = ubuf[tm:tm + POOL_HALO, :]
    ubuf[0:POOL_HALO, :] = tail

    @pl.when(j == pl.num_programs(1) - 1)
    def _():
        pstate_ref[...] = tail


def _proj_sample_kernel(x_ref, state_ref, w_in_ref, qg_ref, w_uq_ref, kvg_ref, w_ukt_ref, w_pool_ref, pscale_ref,
                        cos_ref, sin_ref,
                        qcat_ref, kcat_ref, lat_ref, kr_ref, pooly_ref, pstate_ref, *, past_len):
    n_b = state_ref.shape[1]
    n_t = x_ref.shape[0] // n_b
    u, q_lat, q_rope, lat, k_rope = _project(
        x_ref[...].astype(BF16), w_in_ref, qg_ref, w_uq_ref, kvg_ref, w_ukt_ref, cos_ref[...], sin_ref[...])
    for hd in range(MLA_HEADS):
        qcat_ref[hd, :, 0:KV_RANK] = q_lat[hd].astype(BF16)
        qcat_ref[hd, :, KV_RANK:QK_PAD] = _compact_rope(q_rope[hd]).astype(BF16)
    lat_ref[...] = lat
    kc = _compact_rope(k_rope)
    kcat_ref[:, 0:KV_RANK] = lat.astype(BF16)
    kcat_ref[:, KV_RANK:QK_PAD] = kc.astype(BF16)
    kr_ref[...] = kc[:, :QK_ROPE_DIM]

    def ext(jj):
        if jj < POOL_BUF:
            return state_ref[jj]
        return u[(jj - POOL_BUF) * n_b:(jj - POOL_BUF + 1) * n_b, :]

    for t in range(n_t):
        ut = ext(POOL_BUF + t)
        for g, w in enumerate(POOL_WINDOWS):
            cols = slice(POOL_GROUP_DIM * g, POOL_GROUP_DIM * (g + 1))
            ssum = ut[:, cols]
            for k in range(1, w):
                ssum = ssum + ext(POOL_BUF + t - k)[:, cols]
            count = float(min(w, past_len + t + 1))
            y = _pool_group_out(ssum / count - ut[:, cols], g, w_pool_ref, pscale_ref)
            pooly_ref[t * n_b:(t + 1) * n_b, cols] = y.astype(BF16)
    for jj in range(POOL_BUF):
        pstate_ref[jj] = ext(n_t + jj)


def _proj_weight_specs():
    return [
        _full((D_MODEL, D_MODEL)),
        _full((1, Q_RANK)),
        _full((Q_RANK, MLA_HEADS * QK_PAD)),
        _full((1, KV_RANK)),
        _full((MLA_HEADS, QK_NOPE_DIM, KV_RANK)),
        _full((len(POOL_WINDOWS), POOL_GROUP_DIM, POOL_GROUP_DIM)),
        _full((1, POOL_WIDTH)),
    ]


def _proj_prompt(x2d, weights, cos_t, sin_t, n_batch, seq):
    tm = TOKEN_BLOCK
    n = n_batch * seq
    nj = seq // tm
    tok = lambda b, j: (b * nj + j, 0)
    return pl.pallas_call(
        _proj_prompt_kernel,
        grid=(n_batch, nj),
        in_specs=[pl.BlockSpec((tm, D_MODEL), tok)] + _proj_weight_specs() + [
            pl.BlockSpec((tm, LANES), lambda b, j: (j, 0)),
            pl.BlockSpec((tm, LANES), lambda b, j: (j, 0)),
        ],
        out_specs=[
            pl.BlockSpec((MLA_HEADS, tm, QK_PAD), lambda b, j: (0, b * nj + j, 0)),
            pl.BlockSpec((tm, QK_PAD), tok),
            pl.BlockSpec((tm, KV_RANK), tok),
            pl.BlockSpec((tm, QK_ROPE_DIM), tok),
            pl.BlockSpec((tm, POOL_WIDTH), tok),
            pl.BlockSpec((None, POOL_HALO, POOL_WIDTH), lambda b, j: (b, 0, 0)),
        ],
        out_shape=[
            jax.ShapeDtypeStruct((MLA_HEADS, n, QK_PAD), BF16),
            jax.ShapeDtypeStruct((n, QK_PAD), BF16),
            jax.ShapeDtypeStruct((n, KV_RANK), F32),
            jax.ShapeDtypeStruct((n, QK_ROPE_DIM), F32),
            jax.ShapeDtypeStruct((n, POOL_WIDTH), BF16),
            jax.ShapeDtypeStruct((n_batch, POOL_HALO, POOL_WIDTH), F32),
        ],
        scratch_shapes=[pltpu.VMEM((POOL_HALO + tm, POOL_WIDTH), F32)],
        compiler_params=_params("arbitrary", "arbitrary"),
        name="proj_prompt",
    )(x2d, *weights, cos_t, sin_t)


def _proj_sample(x2d, state_t, weights, cos_t, sin_t, past_len):
    n = x2d.shape[0]
    n_b = state_t.shape[1]
    return pl.pallas_call(
        functools.partial(_proj_sample_kernel, past_len=past_len),
        grid=(1,),
        in_specs=[_full((n, D_MODEL)), _full((POOL_BUF, n_b, POOL_WIDTH))] + _proj_weight_specs() + [
            _full((n, LANES)), _full((n, LANES))],
        out_specs=[
            _full((MLA_HEADS, n, QK_PAD)), _full((n, QK_PAD)), _full((n, KV_RANK)), _full((n, QK_ROPE_DIM)),
            _full((n, POOL_WIDTH)), _full((POOL_BUF, n_b, POOL_WIDTH)),
        ],
        out_shape=[
            jax.ShapeDtypeStruct((MLA_HEADS, n, QK_PAD), BF16),
            jax.ShapeDtypeStruct((n, QK_PAD), BF16),
            jax.ShapeDtypeStruct((n, KV_RANK), F32),
            jax.ShapeDtypeStruct((n, QK_ROPE_DIM), F32),
            jax.ShapeDtypeStruct((n, POOL_WIDTH), BF16),
            jax.ShapeDtypeStruct((POOL_BUF, n_b, POOL_WIDTH), F32),
        ],
        compiler_params=_params("arbitrary"),
        name="proj_sample",
    )(x2d, state_t, *weights, cos_t, sin_t)


def _softmax_step(s, k_val, m_ref, l_ref, acc_ref):
    m_old = m_ref[...]
    m_new = jnp.maximum(m_old, jnp.max(s, axis=1, keepdims=True))
    alpha = jnp.exp(m_old - m_new)
    p = jnp.exp(s - m_new)
    l_ref[...] = alpha * l_ref[...] + jnp.sum(p, axis=1, keepdims=True)
    acc_ref[...] = alpha * acc_ref[...] + _dot(p.astype(BF16), k_val)
    m_ref[...] = m_new


def _attn_prompt_kernel(q_ref, k_ref, w_uv_ref, out_ref, m_ref, l_ref, acc_ref):
    i = pl.program_id(1)
    tq = q_ref.shape[1]
    rows = MLA_HEADS * tq
    tk = ATTN_K_BLOCK
    q = q_ref[...].reshape(rows, QK_PAD)
    m_ref[...] = jnp.full((rows, 1), -jnp.inf, F32)
    l_ref[...] = jnp.zeros((rows, 1), F32)
    acc_ref[...] = jnp.zeros((rows, KV_RANK), F32)

    def block(jk, masked):
        start = pl.multiple_of(jk * tk, tk)
        k = k_ref[pl.ds(start, tk), :]
        s = _dot_nt(q, k)
        if masked:
            q_pos = i * tq + lax.broadcasted_iota(I32, (rows, tk), 0) % tq
            k_pos = jk * tk + lax.broadcasted_iota(I32, (rows, tk), 1)
            s = jnp.where(k_pos <= q_pos, s, -jnp.inf)
        _softmax_step(s, k[:, :KV_RANK], m_ref, l_ref, acc_ref)

    n_full = (i * tq + 1) // tk
    n_kv = (i * tq + tq - 1) // tk + 1

    def full_body(jk, c):
        block(jk, False)
        return c

    def diag_body(jk, c):
        block(jk, True)
        return c

    lax.fori_loop(0, n_full, full_body, 0)
    lax.fori_loop(n_full, n_kv, diag_body, 0)

    o = acc_ref[...] / l_ref[...]
    for hd in range(MLA_HEADS):
        oh = o[hd * tq:(hd + 1) * tq, :].astype(BF16)
        out_ref[:, V_HEAD_DIM * hd:V_HEAD_DIM * (hd + 1)] = _dot(oh, w_uv_ref[hd]).astype(BF16)


def _attn_prompt(qcat, kcat, w_uv_h, n_batch, seq):
    tq = ATTN_Q_BLOCK
    nq = seq // tq
    n = n_batch * seq
    rows = MLA_HEADS * tq
    return pl.pallas_call(
        _attn_prompt_kernel,
        grid=(n_batch, nq),
        in_specs=[
            pl.BlockSpec((MLA_HEADS, tq, QK_PAD), lambda b, i: (0, b * nq + i, 0)),
            pl.BlockSpec((seq, QK_PAD), lambda b, i: (b, 0)),
            _full((MLA_HEADS, KV_RANK, V_HEAD_DIM)),
        ],
        out_specs=pl.BlockSpec((tq, MLA_HEADS * V_HEAD_DIM), lambda b, i: (b * nq + i, 0)),
        out_shape=jax.ShapeDtypeStruct((n, MLA_HEADS * V_HEAD_DIM), BF16),
        scratch_shapes=[pltpu.VMEM((rows, 1), F32), pltpu.VMEM((rows, 1), F32), pltpu.VMEM((rows, KV_RANK), F32)],
        compiler_params=_params("arbitrary", "arbitrary"),
        name="attn_prompt",
    )(qcat, kcat, w_uv_h)


def _attn_sample_kernel(pt_ref, q_ref, knew_ref, w_uv_ref, *rest, n_pages_step, n_new):
    lat_refs = rest[:n_pages_step]
    rope_refs = rest[n_pages_step:2 * n_pages_step]
    out_ref, kbuf, m_ref, l_ref, acc_ref, res_ref = rest[2 * n_pages_step:]
    del pt_ref
    s_idx = pl.program_id(1)
    page = lat_refs[0].shape[0]
    rows = q_ref.shape[0]

    @pl.when(s_idx == 0)
    def _():
        m_ref[...] = jnp.full((rows, 1), -jnp.inf, F32)
        l_ref[...] = jnp.zeros((rows, 1), F32)
        acc_ref[...] = jnp.zeros((rows, KV_RANK), F32)
        kbuf[:, KV_RANK:QK_PAD] = jnp.zeros((kbuf.shape[0], QK_PAD - KV_RANK), BF16)

    for p in range(n_pages_step):
        kbuf[p * page:(p + 1) * page, 0:KV_RANK] = lat_refs[p][...].astype(BF16)
        kbuf[p * page:(p + 1) * page, KV_RANK:KV_RANK + QK_ROPE_DIM] = rope_refs[p][...].astype(BF16)
    q = q_ref[...]
    k = kbuf[...]
    _softmax_step(_dot_nt(q, k), k[:, :KV_RANK], m_ref, l_ref, acc_ref)

    @pl.when(s_idx == pl.num_programs(1) - 1)
    def _():
        kn = knew_ref[...]
        s = _dot_nt(q, kn)
        q_t = lax.broadcasted_iota(I32, s.shape, 0) % n_new
        k_t = lax.broadcasted_iota(I32, s.shape, 1)
        s = jnp.where(k_t <= q_t, s, -jnp.inf)
        _softmax_step(s, kn[:, :KV_RANK], m_ref, l_ref, acc_ref)
        o = (acc_ref[...] / l_ref[...]).astype(BF16)
        for hd in range(MLA_HEADS):
            res_ref[...] = _dot(o, w_uv_ref[hd])
            out_ref[:, V_HEAD_DIM * hd:V_HEAD_DIM * (hd + 1)] = res_ref[hd * n_new:(hd + 1) * n_new, :]


def _attn_sample(page_table, q_b, knew_b, w_uv_h, cache_lat, cache_rope):
    n_b, n_pages = page_table.shape
    page = cache_lat.shape[1]
    pps = min(PAGES_PER_STEP, n_pages)
    assert n_pages % pps == 0
    n_steps = n_pages // pps
    rows = q_b.shape[1]
    n_new = rows // MLA_HEADS
    t_pad = knew_b.shape[1]

    def page_map(p):
        return lambda b, s, pt: (pt[b * n_pages + s * pps + p], 0, 0)

    lat_specs = [pl.BlockSpec((None, page, KV_RANK), page_map(p)) for p in range(pps)]
    rope_specs = [pl.BlockSpec((None, page, QK_ROPE_DIM), page_map(p)) for p in range(pps)]
    grid_spec = pltpu.PrefetchScalarGridSpec(
        num_scalar_prefetch=1,
        grid=(n_b, n_steps),
        in_specs=[
            pl.BlockSpec((None, rows, QK_PAD), lambda b, s, pt: (b, 0, 0)),
            pl.BlockSpec((None, t_pad, QK_PAD), lambda b, s, pt: (b, 0, 0)),
            pl.BlockSpec((MLA_HEADS, KV_RANK, V_HEAD_DIM), lambda b, s, pt: (0, 0, 0)),
        ] + lat_specs + rope_specs,
        out_specs=pl.BlockSpec((None, n_new, MLA_HEADS * V_HEAD_DIM), lambda b, s, pt: (b, 0, 0)),
        scratch_shapes=[
            pltpu.VMEM((pps * page, QK_PAD), BF16),
            pltpu.VMEM((rows, 1), F32), pltpu.VMEM((rows, 1), F32), pltpu.VMEM((rows, KV_RANK), F32),
            pltpu.VMEM((rows, V_HEAD_DIM), F32),
        ],
    )
    return pl.pallas_call(
        functools.partial(_attn_sample_kernel, n_pages_step=pps, n_new=n_new),
        grid_spec=grid_spec,
        out_shape=jax.ShapeDtypeStruct((n_b, n_new, MLA_HEADS * V_HEAD_DIM), F32),
        compiler_params=_params("arbitrary", "arbitrary"),
        name="attn_sample",
    )(page_table.reshape(-1), q_b, knew_b, w_uv_h, *([cache_lat] * pps), *([cache_rope] * pps))


def _mem_kv_kernel(mem_ref, w_mk_ref, w_mv_ref, k_ref, v_ref, kb_ref, vb_ref):
    m = mem_ref[...].astype(BF16)
    k = _dot(m, w_mk_ref[...])
    v = _dot(m, w_mv_ref[...])
    k_ref[...] = k
    v_ref[...] = v
    kb_ref[...] = k.astype(BF16)
    vb_ref[...] = v.astype(BF16)


def _mem_kv(mem2d, w_mk, w_mv, n_batch, n_mem):
    blk = pl.BlockSpec((n_mem, MEM_WIDTH), lambda b: (b, 0))
    n = n_batch * n_mem
    return pl.pallas_call(
        _mem_kv_kernel,
        grid=(n_batch,),
        in_specs=[pl.BlockSpec((n_mem, D_MODEL), lambda b: (b, 0)), _full((D_MODEL, MEM_WIDTH)),
                  _full((D_MODEL, MEM_WIDTH))],
        out_specs=[blk, blk, blk, blk],
        out_shape=[jax.ShapeDtypeStruct((n, MEM_WIDTH), F32), jax.ShapeDtypeStruct((n, MEM_WIDTH), F32),
                   jax.ShapeDtypeStruct((n, MEM_WIDTH), BF16), jax.ShapeDtypeStruct((n, MEM_WIDTH), BF16)],
        compiler_params=_params("arbitrary"),
        name="mem_kv",
    )(mem2d, w_mk, w_mv)


def _mix_ln1(x, pool_y, mla_y, w_out_a_ref, w_out_b_ref, g_ref, b_ref):
    mix = _dot(pool_y, w_out_a_ref[...]) + _dot(mla_y, w_out_b_ref[...])
    return _layer_norm(DEEPNORM_ALPHA * x + mix, g_ref[...], b_ref[...])


def _route(x2, w_r_ref, b_r_ref, carry):
    m = x2.shape[0]
    logits = jnp.dot(x2, w_r_ref[...], preferred_element_type=F32, precision=lax.Precision.HIGHEST) + b_r_ref[...]
    e_idx = lax.broadcasted_iota(I32, (m, N_EXPERTS), 1).astype(F32)
    work = logits
    vals, picks = [], []
    for _ in range(TOP_K):
        mx = jnp.max(work, axis=1, keepdims=True)
        pick = jnp.min(jnp.where(work == mx, e_idx, float(N_EXPERTS)), axis=1, keepdims=True)
        vals.append(mx)
        picks.append(pick)
        work = jnp.where(e_idx == pick, -jnp.inf, work)
    hot = jnp.where(work == -jnp.inf, 1.0, 0.0)
    exps = [jnp.exp(v - vals[0]) for v in vals]
    denom = exps[0] + exps[1] + exps[2] + exps[3]
    r_i = lax.broadcasted_iota(I32, (m, m), 0)
    c_i = lax.broadcasted_iota(I32, (m, m), 1)
    strict_lower = jnp.where(c_i < r_i, 1.0, 0.0).astype(BF16)
    before = _dot(strict_lower, hot.astype(BF16)) + carry
    lane4 = lax.broadcasted_iota(I32, (m, TOP_K), 1)
    idx = jnp.zeros((m, TOP_K), I32)
    gates = jnp.zeros((m, TOP_K), F32)
    rank = jnp.zeros((m, TOP_K), I32)
    for k in range(TOP_K):
        rk = jnp.sum(jnp.where(e_idx == picks[k], before, 0.0), axis=1, keepdims=True).astype(I32)
        idx = jnp.where(lane4 == k, picks[k].astype(I32), idx)
        gates = jnp.where(lane4 == k, exps[k] / denom, gates)
        rank = jnp.where(lane4 == k, rk, rank)
    return idx, gates, rank, carry + jnp.sum(hot, axis=0, keepdims=True)


def _mo_ln2_route(x1, o_bf, w_mo_ref, g_ref, b_ref, w_r_ref, b_r_ref, carry):
    x2 = _layer_norm(DEEPNORM_ALPHA * x1 + _dot(o_bf, w_mo_ref[...]), g_ref[...], b_ref[...])
    return (x2,) + _route(x2, w_r_ref, b_r_ref, carry)


def _post_prompt_kernel(x_ref, pooly_ref, mlay_ref, mk_ref, mv_ref,
                        w_out_a_ref, w_out_b_ref, g1_ref, b1_ref, w_mq_ref, w_mo_ref, g2_ref, b2_ref,
                        w_r_ref, b_r_ref,
                        x2_ref, idx_ref, gate_ref, rank_ref, cnt_ref, carry_ref):
    first = jnp.logical_and(pl.program_id(0) == 0, pl.program_id(1) == 0)

    @pl.when(first)
    def _():
        carry_ref[...] = jnp.zeros(carry_ref.shape, F32)

    x1 = _mix_ln1(x_ref[...], pooly_ref[...], mlay_ref[...], w_out_a_ref, w_out_b_ref, g1_ref, b1_ref)
    qm = (_dot(x1.astype(BF16), w_mq_ref[...]) * MEM_SCALE).astype(BF16)
    outs = []
    for hd in range(MEM_HEADS):
        cols = slice(MEM_HEAD_DIM * hd, MEM_HEAD_DIM * (hd + 1))
        s = _dot_nt(qm[:, cols], mk_ref[:, cols])
        p = jnp.exp(s - jnp.max(s, axis=1, keepdims=True))
        o = _dot(p.astype(BF16), mv_ref[:, cols]) / jnp.sum(p, axis=1, keepdims=True)
        outs.append(o.astype(BF16))
    o_all = jnp.concatenate(outs, axis=1)
    x2, idx, gates, rank, carry = _mo_ln2_route(x1, o_all, w_mo_ref, g2_ref, b2_ref, w_r_ref, b_r_ref,
                                                carry_ref[...])
    x2_ref[...] = x2
    idx_ref[...] = idx
    gate_ref[...] = gates
    rank_ref[...] = rank
    carry_ref[...] = carry
    cnt_ref[...] = carry


def _post_weight_specs():
    half = (POOL_WIDTH, D_MODEL)
    return [
        _full(half), _full(half), _full((1, D_MODEL)), _full((1, D_MODEL)),
        _full((D_MODEL, MEM_WIDTH)), _full((MEM_WIDTH, D_MODEL)), _full((1, D_MODEL)), _full((1, D_MODEL)),
        _full((D_MODEL, N_EXPERTS)), _full((1, N_EXPERTS)),
    ]


def _post_prompt(x2d, pool_y, mla_y, mk_b, mv_b, weights, n_batch, seq, n_mem, n_all):
    tm = TOKEN_BLOCK
    nj = seq // tm
    tok = lambda b, j: (b * nj + j, 0)
    return pl.pallas_call(
        _post_prompt_kernel,
        grid=(n_batch, nj),
        in_specs=[
            pl.BlockSpec((tm, D_MODEL), tok), pl.BlockSpec((tm, POOL_WIDTH), tok), pl.BlockSpec((tm, POOL_WIDTH), tok),
            pl.BlockSpec((n_mem, MEM_WIDTH), lambda b, j: (b, 0)), pl.BlockSpec((n_mem, MEM_WIDTH), lambda b, j: (b, 0)),
        ] + _post_weight_specs(),
        out_specs=[
            pl.BlockSpec((tm, D_MODEL), tok), pl.BlockSpec((tm, TOP_K), tok), pl.BlockSpec((tm, TOP_K), tok),
            pl.BlockSpec((tm, TOP_K), tok), _full((1, N_EXPERTS)),
        ],
        out_shape=[
            jax.ShapeDtypeStruct((n_all, D_MODEL), F32), jax.ShapeDtypeStruct((n_all, TOP_K), I32),
            jax.ShapeDtypeStruct((n_all, TOP_K), F32), jax.ShapeDtypeStruct((n_all, TOP_K), I32),
            jax.ShapeDtypeStruct((1, N_EXPERTS), F32),
        ],
        scratch_shapes=[pltpu.VMEM((1, N_EXPERTS), F32)],
        compiler_params=_params("arbitrary", "arbitrary"),
        name="post_prompt",
    )(x2d, pool_y, mla_y, mk_b, mv_b, *weights)


def _mix_sample_kernel(x_ref, pooly_ref, mlay_ref, w_out_a_ref, w_out_b_ref, g1_ref, b1_ref, w_mq_ref,
                       x1_ref, qm_ref):
    x1 = _mix_ln1(x_ref[...], pooly_ref[...], mlay_ref[...], w_out_a_ref, w_out_b_ref, g1_ref, b1_ref)
    x1_ref[...] = x1
    qm_ref[...] = (_dot(x1.astype(BF16), w_mq_ref[...]) * MEM_SCALE).astype(BF16)


def _mix_sample(x2d, pool_y, mla_y, weights):
    n = x2d.shape[0]
    w_out_a, w_out_b, g1, b1, w_mq = weights[:5]
    return pl.pallas_call(
        _mix_sample_kernel,
        grid=(1,),
        in_specs=[_full((n, D_MODEL)), _full((n, POOL_WIDTH)), _full((n, POOL_WIDTH))] + _post_weight_specs()[:5],
        out_specs=[_full((n, D_MODEL)), _full((n, MEM_WIDTH))],
        out_shape=[jax.ShapeDtypeStruct((n, D_MODEL), F32), jax.ShapeDtypeStruct((n, MEM_WIDTH), BF16)],
        compiler_params=_params("arbitrary"),
        name="mix_sample",
    )(x2d, pool_y, mla_y, w_out_a, w_out_b, g1, b1, w_mq)


def _mem_attn_sample_kernel(q_ref, k_ref, v_ref, out_ref, res_ref, *, n_new):
    for b in range(q_ref.shape[0]):
        s = _dot_nt(q_ref[b], k_ref[b].astype(BF16))
        p = jnp.exp(s - jnp.max(s, axis=1, keepdims=True))
        res_ref[...] = _dot(p.astype(BF16), v_ref[b].astype(BF16)) / jnp.sum(p, axis=1, keepdims=True)
        for hd in range(MEM_HEADS):
            cols = slice(MEM_HEAD_DIM * hd, MEM_HEAD_DIM * (hd + 1))
            out_ref[b, :, cols] = res_ref[hd * n_new:(hd + 1) * n_new, cols].astype(BF16)


def _mem_attn_sample(q_bd, mem_k, mem_v, n_new):
    n_b, rows, _ = q_bd.shape
    n_mem = mem_k.shape[1]
    gb = min(MEM_BATCH_BLOCK, n_b)
    assert n_b % gb == 0
    blk = lambda i: (i, 0, 0)
    return pl.pallas_call(
        functools.partial(_mem_attn_sample_kernel, n_new=n_new),
        grid=(n_b // gb,),
        in_specs=[pl.BlockSpec((gb, rows, MEM_WIDTH), blk), pl.BlockSpec((gb, n_mem, MEM_WIDTH), blk),
                  pl.BlockSpec((gb, n_mem, MEM_WIDTH), blk)],
        out_specs=pl.BlockSpec((gb, n_new, MEM_WIDTH), blk),
        out_shape=jax.ShapeDtypeStruct((n_b, n_new, MEM_WIDTH), BF16),
        scratch_shapes=[pltpu.VMEM((rows, MEM_WIDTH), F32)],
        compiler_params=_params("arbitrary"),
        name="mem_attn_sample",
    )(q_bd, mem_k, mem_v)


def _route_sample_kernel(x1_ref, o_ref, cnt_in_ref, w_mo_ref, g2_ref, b2_ref, w_r_ref, b_r_ref,
                         x2_in, idx_in, gate_in, rank_in,
                         x2_ref, idx_ref, gate_ref, rank_ref, cnt_ref):
    del x2_in, idx_in, gate_in, rank_in
    x2, idx, gates, rank, carry = _mo_ln2_route(x1_ref[...], o_ref[...], w_mo_ref, g2_ref, b2_ref, w_r_ref, b_r_ref,
                                                cnt_in_ref[...])
    x2_ref[...] = x2
    idx_ref[...] = idx
    gate_ref[...] = gates
    rank_ref[...] = rank
    cnt_ref[...] = carry


def _route_sample(x1, o_bf, cnt_in, weights, x2_all, idx_all, gate_all, rank_all, n_prompt):
    n = x1.shape[0]
    assert n_prompt % n == 0
    tail = lambda i: (n_prompt // n, 0)
    anyspec = pl.BlockSpec(memory_space=pl.ANY)
    w_mo, g2, b2, w_r, b_r = weights[5:]
    return pl.pallas_call(
        _route_sample_kernel,
        grid=(1,),
        in_specs=[_full((n, D_MODEL)), _full((n, MEM_WIDTH)), _full((1, N_EXPERTS))] + _post_weight_specs()[5:] + [
            anyspec, anyspec, anyspec, anyspec],
        out_specs=[pl.BlockSpec((n, D_MODEL), tail), pl.BlockSpec((n, TOP_K), tail), pl.BlockSpec((n, TOP_K), tail),
                   pl.BlockSpec((n, TOP_K), tail), _full((1, N_EXPERTS))],
        out_shape=[jax.ShapeDtypeStruct(x2_all.shape, F32), jax.ShapeDtypeStruct(idx_all.shape, I32),
                   jax.ShapeDtypeStruct(gate_all.shape, F32), jax.ShapeDtypeStruct(rank_all.shape, I32),
                   jax.ShapeDtypeStruct((1, N_EXPERTS), F32)],
        input_output_aliases={8: 0, 9: 1, 10: 2, 11: 3},
        compiler_params=_params("arbitrary"),
        name="route_sample",
    )(x1, o_bf, cnt_in, w_mo, g2, b2, w_r, b_r, x2_all, idx_all, gate_all, rank_all)


def _moe_kernel(blk_exp_ref, n_used_ref, src_hbm, x_hbm, wgu_ref, bgu_ref, wd_ref, bd_ref, y_hbm,
                src_smem, xbuf, ybuf, src_sem, x_sem, y_sem, *, n_assign):
    del blk_exp_ref
    i = pl.program_id(0)
    n_used = n_used_ref[0]
    tr = xbuf.shape[1]
    slot = i % 2
    nxt = 1 - slot

    def src_copy(blk, s):
        return pltpu.make_async_copy(src_hbm.at[blk], src_smem.at[s], src_sem.at[s])

    def start_gather(s):
        def body(r, c):
            a = src_smem[s, r]
            tok = lax.shift_right_logical(jnp.maximum(a, 0), TOP_K.bit_length() - 1)
            pltpu.make_async_copy(x_hbm.at[pl.ds(tok, 1)], xbuf.at[s, pl.ds(r, 1)], x_sem.at[s]).start()
            return c
        lax.fori_loop(0, tr, body, 0, unroll=8)

    def wait_rows(buf, sem, s):
        pltpu.make_async_copy(buf.at[s], buf.at[s], sem.at[s]).wait()

    @pl.when(jnp.logical_and(i == 0, n_used > 0))
    def _():
        src_copy(0, 0).start()
        src_copy(0, 0).wait()
        start_gather(0)

    @pl.when(i + 1 < n_used)
    def _():
        src_copy(i + 1, nxt).start()

    @pl.when(i < n_used)
    def _():
        @pl.when(i >= 2)
        def _():
            wait_rows(ybuf, y_sem, slot)

        wait_rows(xbuf, x_sem, slot)
        xb = xbuf[slot].astype(BF16)
        acc = jnp.zeros((tr, D_MODEL), F32) + bd_ref[...]
        for c in range(D_EXPERT // EXPERT_CHUNK):
            cg = slice(c * EXPERT_CHUNK, (c + 1) * EXPERT_CHUNK)
            cu = slice(D_EXPERT + c * EXPERT_CHUNK, D_EXPERT + (c + 1) * EXPERT_CHUNK)
            gate = jnp.minimum(_dot(xb, wgu_ref[:, cg]) + bgu_ref[:, cg], SWIGLU_LIMIT)
            up = jnp.clip(_dot(xb, wgu_ref[:, cu]) + bgu_ref[:, cu], -SWIGLU_LIMIT, SWIGLU_LIMIT)
            act = gate * (1.0 / (1.0 + jnp.exp(-SWIGLU_ALPHA * gate))) * (up + 1.0)
            acc = acc + _dot(act.astype(BF16), wd_ref[cg, :])
        ybuf[slot] = acc

        def scatter(r, c):
            a = src_smem[slot, r]
            dst = jnp.where(a >= 0, a, n_assign + r)
            pltpu.make_async_copy(ybuf.at[slot, pl.ds(r, 1)], y_hbm.at[pl.ds(dst, 1)], y_sem.at[slot]).start()
            return c
        lax.fori_loop(0, tr, scatter, 0, unroll=8)

        @pl.when(i == n_used - 1)
        def _():
            @pl.when(i >= 1)
            def _():
                wait_rows(ybuf, y_sem, nxt)

            wait_rows(ybuf, y_sem, slot)

    @pl.when(i + 1 < n_used)
    def _():
        src_copy(i + 1, nxt).wait()
        start_gather(nxt)


def _moe(blk_exp, n_used, row_src, x_all, w_gu, b_gu, w_d, b_d, n_assign):
    tr = EXPERT_ROWS
    n_blocks = blk_exp.shape[0]
    assert TOP_K & (TOP_K - 1) == 0
    anyspec = pl.BlockSpec(memory_space=pl.ANY)
    emap3 = lambda i, be, nu: (be[i], 0, 0)
    grid_spec = pltpu.PrefetchScalarGridSpec(
        num_scalar_prefetch=2,
        grid=(n_blocks,),
        in_specs=[
            anyspec, anyspec,
            pl.BlockSpec((None, D_MODEL, 2 * D_EXPERT), emap3), pl.BlockSpec((None, 1, 2 * D_EXPERT), emap3),
            pl.BlockSpec((None, D_EXPERT, D_MODEL), emap3), pl.BlockSpec((None, 1, D_MODEL), emap3),
        ],
        out_specs=anyspec,
        scratch_shapes=[
            pltpu.SMEM((2, tr), I32),
            pltpu.VMEM((2, tr, D_MODEL), F32), pltpu.VMEM((2, tr, D_MODEL), F32),
            pltpu.SemaphoreType.DMA((2,)), pltpu.SemaphoreType.DMA((2,)), pltpu.SemaphoreType.DMA((2,)),
        ],
    )
    return pl.pallas_call(
        functools.partial(_moe_kernel, n_assign=n_assign),
        grid_spec=grid_spec,
        out_shape=jax.ShapeDtypeStruct((n_assign + tr, D_MODEL), F32),
        compiler_params=_params("arbitrary"),
        name="moe_experts",
    )(blk_exp, n_used, row_src.reshape(n_blocks, tr), x_all, w_gu, b_gu, w_d, b_d)


def _combine_kernel(x2_ref, y_ref, gate_ref, g3_ref, b3_ref, out_ref):
    gates = gate_ref[...]
    moe = jnp.zeros(x2_ref.shape, F32)
    for k in range(TOP_K):
        moe = moe + gates[:, k:k + 1] * y_ref[:, D_MODEL * k:D_MODEL * (k + 1)]
    out_ref[...] = _layer_norm(DEEPNORM_ALPHA * x2_ref[...] + moe, g3_ref[...], b3_ref[...])


def _combine(x2_all, y_tok, gates_all, g3, b3, row0, n_rows, tm):
    assert row0 % tm == 0 and n_rows % tm == 0
    off = row0 // tm
    tok = lambda i: (off + i, 0)
    return pl.pallas_call(
        _combine_kernel,
        grid=(n_rows // tm,),
        in_specs=[pl.BlockSpec((tm, D_MODEL), tok), pl.BlockSpec((tm, TOP_K * D_MODEL), tok),
                  pl.BlockSpec((tm, TOP_K), tok), _full((1, D_MODEL)), _full((1, D_MODEL))],
        out_specs=pl.BlockSpec((tm, D_MODEL), lambda i: (i, 0)),
        out_shape=jax.ShapeDtypeStruct((n_rows, D_MODEL), F32),
        compiler_params=_params("arbitrary"),
        name="combine_ln3",
    )(x2_all, y_tok, gates_all, g3, b3)


def _rope_tables(pos):
    inv_freq = ROPE_THETA ** (-jnp.arange(ROPE_HALF, dtype=F32) / ROPE_HALF)
    ang = pos.astype(F32)[:, None] * inv_freq[None, :]
    c, s, z = jnp.cos(ang), jnp.sin(ang), jnp.zeros_like(ang)
    return jnp.concatenate([c, z, c, z], axis=1), jnp.concatenate([-s, z, s, z], axis=1)


def _pad_rope_cols(w):
    z = jnp.zeros(w.shape[:-1] + (ROPE_HALF,), w.dtype)
    return jnp.concatenate([w[..., :ROPE_HALF], z, w[..., ROPE_HALF:], z], axis=-1)


def _layer_weights(l, w_in, w_pool, pool_scale, q_norm_g, w_uq, kv_norm_g, w_uk, w_uv, w_out, ln1_g, ln1_b,
                   w_mq, w_mo, ln2_g, ln2_b, w_router, b_router):
    o3 = POOL_WIDTH + Q_RANK + KV_RANK
    w_in_p = jnp.concatenate([w_in[l][:, :o3], _pad_rope_cols(w_in[l][:, o3:])], axis=1).astype(BF16)
    uq = w_uq[l].reshape(Q_RANK, MLA_HEADS, QK_NOPE_DIM + QK_ROPE_DIM)
    uq_p = jnp.concatenate([uq[..., :QK_NOPE_DIM], _pad_rope_cols(uq[..., QK_NOPE_DIM:])], axis=-1)
    uq_p = uq_p.reshape(Q_RANK, MLA_HEADS * QK_PAD).astype(BF16)
    w_ukt = jnp.transpose(w_uk[l], (1, 2, 0)).astype(BF16)
    w_uv_h = jnp.transpose(w_uv[l], (1, 0, 2)).astype(BF16)
    row = lambda v: v[l].reshape(1, -1)
    proj = [w_in_p, row(q_norm_g), uq_p, row(kv_norm_g), w_ukt, w_pool[l].astype(BF16), row(pool_scale)]
    post = [w_out[l][:POOL_WIDTH].astype(BF16), w_out[l][POOL_WIDTH:].astype(BF16), row(ln1_g), row(ln1_b),
            w_mq[l].astype(BF16), w_mo[l].astype(BF16), row(ln2_g), row(ln2_b), w_router[l], row(b_router)]
    return proj, w_uv_h, post


def _routing_tables(idx_all, rank_all, counts, n_blocks):
    tr = EXPERT_ROWS
    cnt = counts.reshape(-1).astype(I32)
    padded = ((cnt + tr - 1) // tr) * tr
    pend = jnp.cumsum(padded)
    pstart = pend - padded
    dest = (pstart[idx_all] + rank_all).reshape(-1)
    m = dest.shape[0]
    row_src = jnp.full((n_blocks * tr,), -1, I32).at[dest].set(jnp.arange(m, dtype=I32))
    n_used = (pend[-1] // tr).astype(I32).reshape(1)
    blk = jnp.arange(n_blocks, dtype=I32)
    blk_exp = jnp.searchsorted(pend, jnp.minimum(blk, n_used[0] - 1) * tr, side="right").astype(I32)
    blk_exp = jnp.clip(blk_exp, 0, N_EXPERTS - 1)
    return blk_exp, n_used, row_src


def kernel(x_prompt, x_sample, mem_prompt, cache_kv_latent, cache_k_rope, cache_mem_k, cache_mem_v, state_pool,
           page_table, w_in, w_pool, pool_scale, q_norm_g, w_uq, kv_norm_g, w_uk, w_uv, w_out, ln1_g, ln1_b,
           w_mq, w_mk, w_mv, w_mo, ln2_g, ln2_b, w_router, b_router, w_gate_up, b_gate_up, w_down, b_down,
           ln3_g, ln3_b):
    n_bp, seq, _ = x_prompt.shape
    n_bs, n_new, _ = x_sample.shape
    n_mem = mem_prompt.shape[1]
    page = cache_kv_latent.shape[2]
    past_len = page_table.shape[1] * page
    n_p = n_bp * seq
    n_s = n_bs * n_new
    n_all = n_p + n_s
    n_assign = n_all * TOP_K
    assert w_in.shape[0] == DEPTH == 1
    assert seq % TOKEN_BLOCK == 0 and seq % ATTN_K_BLOCK == 0 and seq >= POOL_HALO and n_p % n_s == 0
    assert n_s % TOKEN_BLOCK == 0 or TOKEN_BLOCK % n_s == 0
    l = 0

    proj_w, w_uv_h, post_w = _layer_weights(l, w_in, w_pool, pool_scale, q_norm_g, w_uq, kv_norm_g, w_uk, w_uv,
                                            w_out, ln1_g, ln1_b, w_mq, w_mo, ln2_g, ln2_b, w_router, b_router)

    xp = x_prompt.reshape(n_p, D_MODEL)
    cos_p, sin_p = _rope_tables(jnp.arange(seq))
    qcat, kcat, lat_p, kr_p, pooly_p, pstate_p = _proj_prompt(xp, proj_w, cos_p, sin_p, n_bp, seq)
    mlay_p = _attn_prompt(qcat, kcat, w_uv_h, n_bp, seq)
    mk, mv, mk_b, mv_b = _mem_kv(mem_prompt.reshape(n_bp * n_mem, D_MODEL), w_mk[l].astype(BF16),
                                 w_mv[l].astype(BF16), n_bp, n_mem)
    x2_all, idx_all, gate_all, rank_all, cnt_p = _post_prompt(xp, pooly_p, mlay_p, mk_b, mv_b, post_w,
                                                              n_bp, seq, n_mem, n_all)

    xs = jnp.transpose(x_sample, (1, 0, 2)).reshape(n_s, D_MODEL)
    state_t = jnp.transpose(state_pool[l], (1, 0, 2))
    cos_s, sin_s = _rope_tables(jnp.repeat(past_len + jnp.arange(n_new), n_bs))
    qcat_s, kcat_s, lat_s, kr_s, pooly_s, pstate_s = _proj_sample(xs, state_t, proj_w, cos_s, sin_s, past_len)
    q_b = jnp.transpose(qcat_s.reshape(MLA_HEADS, n_new, n_bs, QK_PAD), (2, 0, 1, 3)).reshape(
        n_bs, MLA_HEADS * n_new, QK_PAD)
    knew_b = jnp.transpose(kcat_s.reshape(n_new, n_bs, QK_PAD), (1, 0, 2))
    knew_b = jnp.pad(knew_b, ((0, 0), (0, LANES - n_new), (0, 0)))
    mlay_b = _attn_sample(page_table, q_b, knew_b, w_uv_h, cache_kv_latent[l], cache_k_rope[l])
    mlay_s = jnp.transpose(mlay_b, (1, 0, 2)).reshape(n_s, POOL_WIDTH).astype(BF16)
    x1_s, qm_s = _mix_sample(xs, pooly_s, mlay_s, post_w)
    qm_b = jnp.transpose(qm_s.reshape(n_new, n_bs, MEM_WIDTH), (1, 0, 2))
    head_of_col = jnp.arange(MEM_WIDTH) // MEM_HEAD_DIM
    head_mask = (head_of_col[None, :] == jnp.arange(MEM_HEADS)[:, None]).astype(BF16)
    q_bd = (qm_b[:, None, :, :] * head_mask[None, :, None, :]).reshape(n_bs, MEM_HEADS * n_new, MEM_WIDTH)
    o_b = _mem_attn_sample(q_bd, cache_mem_k[l].reshape(n_bs, n_mem, MEM_WIDTH),
                           cache_mem_v[l].reshape(n_bs, n_mem, MEM_WIDTH), n_new)
    o_s = jnp.transpose(o_b, (1, 0, 2)).reshape(n_s, MEM_WIDTH)
    x2_all, idx_all, gate_all, rank_all, counts = _route_sample(x1_s, o_s, cnt_p, post_w, x2_all, idx_all, gate_all,
                                                               rank_all, n_p)

    n_blocks = -(-n_assign // EXPERT_ROWS) + N_EXPERTS
    blk_exp, n_used, row_src = _routing_tables(idx_all, rank_all, counts, n_blocks)
    y_tok = _moe(blk_exp, n_used, row_src, x2_all, w_gate_up[l].astype(BF16), b_gate_up[l][:, None, :],
                 w_down[l].astype(BF16), b_down[l][:, None, :], n_assign)
    y_tok = y_tok.reshape(-1, TOP_K * D_MODEL)
    g3, b3 = ln3_g[l].reshape(1, -1), ln3_b[l].reshape(1, -1)
    y_p = _combine(x2_all, y_tok, gate_all, g3, b3, 0, n_p, TOKEN_BLOCK)
    y_s = _combine(x2_all, y_tok, gate_all, g3, b3, n_p, n_s, min(TOKEN_BLOCK, n_s))

    def from_steps(a, width):
        return jnp.transpose(a.reshape(n_new, n_bs, width), (1, 0, 2))[None]

    return (
        y_p.reshape(n_bp, seq, D_MODEL),
        from_steps(y_s, D_MODEL)[0],
        lat_p.reshape(1, n_bp, seq, KV_RANK),
        kr_p.reshape(1, n_bp, seq, QK_ROPE_DIM),
        pstate_p[None, :, POOL_HALO - POOL_BUF:, :],
        mk.reshape(1, n_bp, n_mem, MEM_HEADS, MEM_HEAD_DIM),
        mv.reshape(1, n_bp, n_mem, MEM_HEADS, MEM_HEAD_DIM),
        from_steps(lat_s, KV_RANK),
        from_steps(kr_s, QK_ROPE_DIM),
        jnp.transpose(pstate_s, (1, 0, 2))[None],
    )
```

```python
import functools

import jax
import jax.numpy as jnp
from jax import lax
from jax.experimental import pallas as pl
from jax.experimental.pallas import tpu as pltpu

F32 = jnp.float32
BF16 = jnp.bfloat16
I32 = jnp.int32

D_MODEL = 1024
POOL_WIDTH = 512
POOL_WINDOWS = (2, 4, 8, 16)
POOL_GROUP_DIM = 128
POOL_BUF = 15
POOL_HALO = 16
MLA_HEADS = 4
QK_NOPE_DIM = 128
QK_ROPE_DIM = 64
ROPE_HALF = 32
V_HEAD_DIM = 128
Q_RANK = 256
KV_RANK = 128
ROPE_THETA = 10000.0
QK_PAD = 256
ONES_LANE = QK_PAD - 1
MEM_HEADS = 4
MEM_HEAD_DIM = 128
MEM_WIDTH = 512
N_EXPERTS = 32
TOP_K = 4
D_EXPERT = 1024
SWIGLU_LIMIT = 7.0
SWIGLU_ALPHA = 1.702
LN_EPS = 1e-5
RMS_EPS = 1e-6
DEPTH = 1
DEEPNORM_ALPHA = (2.0 * DEPTH) ** 0.25
ATTN_SCALE = (QK_NOPE_DIM + QK_ROPE_DIM) ** -0.5
MEM_SCALE = MEM_HEAD_DIM ** -0.5

LANES = 128
VMEM_LIMIT = 48 * 1024 * 1024

TOKEN_BLOCK = 256
ATTN_Q_BLOCK = 256
ATTN_K_BLOCK = 512
ATTN_GROUPS = 2
PAGES_PER_STEP = 16
MEM_BATCH_BLOCK = 8
EXPERT_ROWS = 256
EXPERT_CHUNK = 512


def _dot(a, b):
    return jnp.dot(a, b, preferred_element_type=F32)


def _dot_nt(a, b):
    return lax.dot_general(a, b, (((1,), (1,)), ((), ())), preferred_element_type=F32)


def _rms(x, g):
    return x * lax.rsqrt(jnp.mean(x * x, axis=-1, keepdims=True) + RMS_EPS) * g


def _layer_norm(x, g, b):
    mu = jnp.mean(x, axis=-1, keepdims=True)
    xc = x - mu
    var = jnp.mean(xc * xc, axis=-1, keepdims=True)
    return xc * lax.rsqrt(var + LN_EPS) * g + b


def _params(*semantics):
    return pltpu.CompilerParams(dimension_semantics=semantics, vmem_limit_bytes=VMEM_LIMIT)


def _full(shape):
    n = len(shape)
    return pl.BlockSpec(shape, lambda *_: (0,) * n)


def _project(x_bf, w_in_ref, qg_ref, w_uq_ref, kvg_ref, w_ukt_ref, cos, sin):
    h = _dot(x_bf, w_in_ref[...])
    u = h[:, :POOL_WIDTH]
    qn = _rms(h[:, POOL_WIDTH:POOL_WIDTH + Q_RANK], qg_ref[...])
    q = _dot(qn.astype(BF16), w_uq_ref[...])
    q_lat, q_rope = [], []
    for hd in range(MLA_HEADS):
        nope = q[:, QK_PAD * hd:QK_PAD * hd + QK_NOPE_DIM]
        rp = q[:, QK_PAD * hd + QK_NOPE_DIM:QK_PAD * (hd + 1)]
        q_lat.append(_dot(nope.astype(BF16), w_ukt_ref[hd]) * ATTN_SCALE)
        q_rope.append((rp * cos + pltpu.roll(rp, 64, 1) * sin) * ATTN_SCALE)
    o2 = POOL_WIDTH + Q_RANK
    lat = _rms(h[:, o2:o2 + KV_RANK], kvg_ref[...])
    kp = h[:, o2 + KV_RANK:]
    k_rope = kp * cos + pltpu.roll(kp, 64, 1) * sin
    return u, q_lat, q_rope, lat, k_rope


def _compact_rope(r):
    lane = lax.broadcasted_iota(I32, r.shape, 1)
    moved = pltpu.roll(r, 96, 1)
    return jnp.where(lane < ROPE_HALF, r, jnp.where(lane < QK_ROPE_DIM, moved, 0.0))


def _with_ones_lane(r):
    lane = lax.broadcasted_iota(I32, r.shape, 1)
    return jnp.where(lane == ONES_LANE - KV_RANK, 1.0, r)


def _pool_group_out(diff, g, w_pool_ref, pscale_ref):
    cols = slice(POOL_GROUP_DIM * g, POOL_GROUP_DIM * (g + 1))
    return _dot(diff.astype(BF16), w_pool_ref[g]) * pscale_ref[:, cols]


def _proj_prompt_kernel(x_ref, w_in_ref, qg_ref, w_uq_ref, kvg_ref, w_ukt_ref, w_pool_ref, pscale_ref,
                        cos_ref, sin_ref,
                        qcat_ref, kcat_ref, lat_ref, kr_ref, pooly_ref, pstate_ref, ubuf):
    j = pl.program_id(1)
    tm = x_ref.shape[0]

    @pl.when(j == 0)
    def _():
        ubuf[0:POOL_HALO, :] = jnp.zeros((POOL_HALO, POOL_WIDTH), F32)

    u, q_lat, q_rope, lat, k_rope = _project(
        x_ref[...].astype(BF16), w_in_ref, qg_ref, w_uq_ref, kvg_ref, w_ukt_ref, cos_ref[...], sin_ref[...])
    for hd in range(MLA_HEADS):
        qcat_ref[hd, :, 0:KV_RANK] = q_lat[hd].astype(BF16)
        qcat_ref[hd, :, KV_RANK:QK_PAD] = q_rope[hd].astype(BF16)
    lat_ref[...] = lat
    kcat_ref[:, 0:KV_RANK] = lat.astype(BF16)
    kcat_ref[:, KV_RANK:QK_PAD] = _with_ones_lane(k_rope).astype(BF16)
    kr_ref[...] = _compact_rope(k_rope)[:, :QK_ROPE_DIM]

    ubuf[POOL_HALO:POOL_HALO + tm, :] = u
    pos = j * tm + lax.broadcasted_iota(I32, (tm, 1), 0)
    for g, w in enumerate(POOL_WINDOWS):
        cols = slice(POOL_GROUP_DIM * g, POOL_GROUP_DIM * (g + 1))
        ug = u[:, cols]
        ssum = ug
        for k in range(1, w):
            ssum = ssum + ubuf[POOL_HALO - k:POOL_HALO - k + tm, cols]
        count = jnp.minimum(w, pos + 1).astype(F32)
        pooly_ref[:, cols] = _pool_group_out(ssum / count - ug, g, w_pool_ref, pscale_ref).astype(BF16)
    tail = ubuf[tm:tm + POOL_HALO, :]
    ubuf[0:POOL_HALO, :] = tail

    @pl.when(j == pl.num_programs(1) - 1)
    def _():
        pstate_ref[...] = tail


def _proj_sample_kernel(x_ref, state_ref, w_in_ref, qg_ref, w_uq_ref, kvg_ref, w_ukt_ref, w_pool_ref, pscale_ref,
                        cos_ref, sin_ref,
                        qcat_ref, kcat_ref, lat_ref, kr_ref, pooly_ref, pstate_ref, *, past_len):
    n_b = state_ref.shape[1]
    n_t = x_ref.shape[0] // n_b
    u, q_lat, q_rope, lat, k_rope = _project(
        x_ref[...].astype(BF16), w_in_ref, qg_ref, w_uq_ref, kvg_ref, w_ukt_ref, cos_ref[...], sin_ref[...])
    for hd in range(MLA_HEADS):
        qcat_ref[hd, :, 0:KV_RANK] = q_lat[hd].astype(BF16)
        qcat_ref[hd, :, KV_RANK:QK_PAD] = _compact_rope(q_rope[hd]).astype(BF16)
    lat_ref[...] = lat
    kc = _compact_rope(k_rope)
    kcat_ref[:, 0:KV_RANK] = lat.astype(BF16)
    kcat_ref[:, KV_RANK:QK_PAD] = _with_ones_lane(kc).astype(BF16)
    kr_ref[...] = kc[:, :QK_ROPE_DIM]

    def ext(jj):
        if jj < POOL_BUF:
            return state_ref[jj]
        return u[(jj - POOL_BUF) * n_b:(jj - POOL_BUF + 1) * n_b, :]

    for t in range(n_t):
        ut = ext(POOL_BUF + t)
        for g, w in enumerate(POOL_WINDOWS):
            cols = slice(POOL_GROUP_DIM * g, POOL_GROUP_DIM * (g + 1))
            ssum = ut[:, cols]
            for k in range(1, w):
                ssum = ssum + ext(POOL_BUF + t - k)[:, cols]
            count = float(min(w, past_len + t + 1))
            y = _pool_group_out(ssum / count - ut[:, cols], g, w_pool_ref, pscale_ref)
            pooly_ref[t * n_b:(t + 1) * n_b, cols] = y.astype(BF16)
    for jj in range(POOL_BUF):
        pstate_ref[jj] = ext(n_t + jj)


def _proj_weight_specs():
    return [
        _full((D_MODEL, D_MODEL)),
        _full((1, Q_RANK)),
        _full((Q_RANK, MLA_HEADS * QK_PAD)),
        _full((1, KV_RANK)),
        _full((MLA_HEADS, QK_NOPE_DIM, KV_RANK)),
        _full((len(POOL_WINDOWS), POOL_GROUP_DIM, POOL_GROUP_DIM)),
        _full((1, POOL_WIDTH)),
    ]


def _proj_prompt(x2d, weights, cos_t, sin_t, n_batch, seq):
    tm = TOKEN_BLOCK
    n = n_batch * seq
    nj = seq // tm
    tok = lambda b, j: (b * nj + j, 0)
    return pl.pallas_call(
        _proj_prompt_kernel,
        grid=(n_batch, nj),
        in_specs=[pl.BlockSpec((tm, D_MODEL), tok)] + _proj_weight_specs() + [
            pl.BlockSpec((tm, LANES), lambda b, j: (j, 0)),
            pl.BlockSpec((tm, LANES), lambda b, j: (j, 0)),
        ],
        out_specs=[
            pl.BlockSpec((MLA_HEADS, tm, QK_PAD), lambda b, j: (0, b * nj + j, 0)),
            pl.BlockSpec((tm, QK_PAD), tok),
            pl.BlockSpec((tm, KV_RANK), tok),
            pl.BlockSpec((tm, QK_ROPE_DIM), tok),
            pl.BlockSpec((tm, POOL_WIDTH), tok),
            pl.BlockSpec((None, POOL_HALO, POOL_WIDTH), lambda b, j: (b, 0, 0)),
        ],
        out_shape=[
            jax.ShapeDtypeStruct((MLA_HEADS, n, QK_PAD), BF16),
            jax.ShapeDtypeStruct((n, QK_PAD), BF16),
            jax.ShapeDtypeStruct((n, KV_RANK), F32),
            jax.ShapeDtypeStruct((n, QK_ROPE_DIM), F32),
            jax.ShapeDtypeStruct((n, POOL_WIDTH), BF16),
            jax.ShapeDtypeStruct((n_batch, POOL_HALO, POOL_WIDTH), F32),
        ],
        scratch_shapes=[pltpu.VMEM((POOL_HALO + tm, POOL_WIDTH), F32)],
        compiler_params=_params("arbitrary", "arbitrary"),
        name="proj_prompt",
    )(x2d, *weights, cos_t, sin_t)


def _proj_sample(x2d, state_t, weights, cos_t, sin_t, past_len):
    n = x2d.shape[0]
    n_b = state_t.shape[1]
    return pl.pallas_call(
        functools.partial(_proj_sample_kernel, past_len=past_len),
        grid=(1,),
        in_specs=[_full((n, D_MODEL)), _full((POOL_BUF, n_b, POOL_WIDTH))] + _proj_weight_specs() + [
            _full((n, LANES)), _full((n, LANES))],
        out_specs=[
            _full((MLA_HEADS, n, QK_PAD)), _full((n, QK_PAD)), _full((n, KV_RANK)), _full((n, QK_ROPE_DIM)),
            _full((n, POOL_WIDTH)), _full((POOL_BUF, n_b, POOL_WIDTH)),
        ],
        out_shape=[
            jax.ShapeDtypeStruct((MLA_HEADS, n, QK_PAD), BF16),
            jax.ShapeDtypeStruct((n, QK_PAD), BF16),
            jax.ShapeDtypeStruct((n, KV_RANK), F32),
            jax.ShapeDtypeStruct((n, QK_ROPE_DIM), F32),
            jax.ShapeDtypeStruct((n, POOL_WIDTH), BF16),
            jax.ShapeDtypeStruct((POOL_BUF, n_b, POOL_WIDTH), F32),
        ],
        compiler_params=_params("arbitrary"),
        name="proj_sample",
    )(x2d, state_t, *weights, cos_t, sin_t)


def _attn_block(q, k, mask, m_ref, acc_ref):
    s = _dot_nt(q, k)
    if mask is not None:
        s = jnp.where(mask, s, -jnp.inf)
    tiles = [s[:, c * LANES:(c + 1) * LANES] for c in range(s.shape[1] // LANES)]
    m_old = m_ref[...]
    m_new = jnp.maximum(m_old, jnp.max(functools.reduce(jnp.maximum, tiles), axis=1, keepdims=True))
    alpha = jnp.exp(m_old - m_new)
    p = jnp.concatenate([jnp.exp(t - m_new) for t in tiles], axis=1).astype(BF16)
    acc_ref[...] = jnp.concatenate([alpha, alpha], axis=1) * acc_ref[...] + _dot(p, k)
    m_ref[...] = m_new


def _attn_output(acc):
    return acc[:, :KV_RANK] / acc[:, ONES_LANE:ONES_LANE + 1]


def _attn_prompt_kernel(q_ref, k_ref, w_uv_ref, out_ref, m_ref, acc_ref):
    i = pl.program_id(1)
    tq = q_ref.shape[1]
    tk = ATTN_K_BLOCK
    hpg = MLA_HEADS // ATTN_GROUPS
    rows = hpg * tq
    m_ref[...] = jnp.full(m_ref.shape, -jnp.inf, F32)
    acc_ref[...] = jnp.zeros(acc_ref.shape, F32)

    def block(jk, masked):
        k = k_ref[pl.ds(pl.multiple_of(jk * tk, tk), tk), :]
        mask = None
        if masked:
            q_pos = i * tq + lax.broadcasted_iota(I32, (rows, tk), 0) % tq
            k_pos = jk * tk + lax.broadcasted_iota(I32, (rows, tk), 1)
            mask = k_pos <= q_pos
        for g in range(ATTN_GROUPS):
            q = q_ref[hpg * g:hpg * (g + 1)].reshape(rows, QK_PAD)
            _attn_block(q, k, mask, m_ref.at[g], acc_ref.at[g])

    n_full = (i * tq + 1) // tk
    n_kv = (i * tq + tq - 1) // tk + 1

    def full_body(jk, c):
        block(jk, False)
        return c

    def diag_body(jk, c):
        block(jk, True)
        return c

    lax.fori_loop(0, n_full, full_body, 0)
    lax.fori_loop(n_full, n_kv, diag_body, 0)

    for g in range(ATTN_GROUPS):
        o = _attn_output(acc_ref[g])
        for hl in range(hpg):
            hd = hpg * g + hl
            oh = o[hl * tq:(hl + 1) * tq, :].astype(BF16)
            out_ref[:, V_HEAD_DIM * hd:V_HEAD_DIM * (hd + 1)] = _dot(oh, w_uv_ref[hd]).astype(BF16)


def _attn_prompt(qcat, kcat, w_uv_h, n_batch, seq):
    tq = ATTN_Q_BLOCK
    nq = seq // tq
    n = n_batch * seq
    rows = (MLA_HEADS // ATTN_GROUPS) * tq
    return pl.pallas_call(
        _attn_prompt_kernel,
        grid=(n_batch, nq),
        in_specs=[
            pl.BlockSpec((MLA_HEADS, tq, QK_PAD), lambda b, i: (0, b * nq + i, 0)),
            pl.BlockSpec((seq, QK_PAD), lambda b, i: (b, 0)),
            _full((MLA_HEADS, KV_RANK, V_HEAD_DIM)),
        ],
        out_specs=pl.BlockSpec((tq, MLA_HEADS * V_HEAD_DIM), lambda b, i: (b * nq + i, 0)),
        out_shape=jax.ShapeDtypeStruct((n, MLA_HEADS * V_HEAD_DIM), BF16),
        scratch_shapes=[pltpu.VMEM((ATTN_GROUPS, rows, LANES), F32), pltpu.VMEM((ATTN_GROUPS, rows, QK_PAD), F32)],
        compiler_params=_params("arbitrary", "arbitrary"),
        name="attn_prompt",
    )(qcat, kcat, w_uv_h)


def _attn_sample_kernel(pt_ref, q_ref, knew_ref, w_uv_ref, *rest, n_pages_step, n_new):
    lat_refs = rest[:n_pages_step]
    rope_refs = rest[n_pages_step:2 * n_pages_step]
    out_ref, kbuf, m_ref, acc_ref, res_ref = rest[2 * n_pages_step:]
    del pt_ref
    s_idx = pl.program_id(1)
    page = lat_refs[0].shape[0]

    @pl.when(s_idx == 0)
    def _():
        m_ref[...] = jnp.full(m_ref.shape, -jnp.inf, F32)
        acc_ref[...] = jnp.zeros(acc_ref.shape, F32)
        pad = (kbuf.shape[0], QK_PAD - KV_RANK - QK_ROPE_DIM)
        lane = lax.broadcasted_iota(I32, pad, 1)
        kbuf[:, KV_RANK + QK_ROPE_DIM:QK_PAD] = jnp.where(lane == pad[1] - 1, 1.0, 0.0).astype(BF16)

    for p in range(n_pages_step):
        kbuf[p * page:(p + 1) * page, 0:KV_RANK] = lat_refs[p][...].astype(BF16)
        kbuf[p * page:(p + 1) * page, KV_RANK:KV_RANK + QK_ROPE_DIM] = rope_refs[p][...].astype(BF16)
    q = q_ref[...]
    _attn_block(q, kbuf[...], None, m_ref, acc_ref)

    @pl.when(s_idx == pl.num_programs(1) - 1)
    def _():
        kn = knew_ref[...]
        q_t = lax.broadcasted_iota(I32, (q.shape[0], kn.shape[0]), 0) % n_new
        k_t = lax.broadcasted_iota(I32, (q.shape[0], kn.shape[0]), 1)
        _attn_block(q, kn, k_t <= q_t, m_ref, acc_ref)
        o = _attn_output(acc_ref[...]).astype(BF16)
        for hd in range(MLA_HEADS):
            res_ref[...] = _dot(o, w_uv_ref[hd])
            out_ref[:, V_HEAD_DIM * hd:V_HEAD_DIM * (hd + 1)] = res_ref[hd * n_new:(hd + 1) * n_new, :]


def _attn_sample(page_table, q_b, knew_b, w_uv_h, cache_lat, cache_rope, layer):
    n_b, n_pages = page_table.shape
    page = cache_lat.shape[2]
    pps = min(PAGES_PER_STEP, n_pages)
    assert n_pages % pps == 0
    n_steps = n_pages // pps
    rows = q_b.shape[1]
    n_new = rows // MLA_HEADS
    t_pad = knew_b.shape[1]

    def page_map(p):
        return lambda b, s, pt: (layer, pt[b * n_pages + s * pps + p], 0, 0)

    lat_specs = [pl.BlockSpec((None, None, page, KV_RANK), page_map(p)) for p in range(pps)]
    rope_specs = [pl.BlockSpec((None, None, page, QK_ROPE_DIM), page_map(p)) for p in range(pps)]
    grid_spec = pltpu.PrefetchScalarGridSpec(
        num_scalar_prefetch=1,
        grid=(n_b, n_steps),
        in_specs=[
            pl.BlockSpec((None, rows, QK_PAD), lambda b, s, pt: (b, 0, 0)),
            pl.BlockSpec((None, t_pad, QK_PAD), lambda b, s, pt: (b, 0, 0)),
            pl.BlockSpec((MLA_HEADS, KV_RANK, V_HEAD_DIM), lambda b, s, pt: (0, 0, 0)),
        ] + lat_specs + rope_specs,
        out_specs=pl.BlockSpec((None, n_new, MLA_HEADS * V_HEAD_DIM), lambda b, s, pt: (b, 0, 0)),
        scratch_shapes=[
            pltpu.VMEM((pps * page, QK_PAD), BF16),
            pltpu.VMEM((rows, LANES), F32), pltpu.VMEM((rows, QK_PAD), F32),
            pltpu.VMEM((rows, V_HEAD_DIM), F32),
        ],
    )
    return pl.pallas_call(
        functools.partial(_attn_sample_kernel, n_pages_step=pps, n_new=n_new),
        grid_spec=grid_spec,
        out_shape=jax.ShapeDtypeStruct((n_b, n_new, MLA_HEADS * V_HEAD_DIM), F32),
        compiler_params=_params("arbitrary", "arbitrary"),
        name="attn_sample",
    )(page_table.reshape(-1), q_b, knew_b, w_uv_h, *([cache_lat] * pps), *([cache_rope] * pps))


def _mem_kv_kernel(mem_ref, w_mk_ref, w_mv_ref, k_ref, v_ref, kb_ref, vb_ref):
    m = mem_ref[...].astype(BF16)
    k = _dot(m, w_mk_ref[...])
    v = _dot(m, w_mv_ref[...])
    k_ref[...] = k
    v_ref[...] = v
    kb_ref[...] = k.astype(BF16)
    vb_ref[...] = v.astype(BF16)


def _mem_kv(mem2d, w_mk, w_mv, n_batch, n_mem):
    blk = pl.BlockSpec((n_mem, MEM_WIDTH), lambda b: (b, 0))
    n = n_batch * n_mem
    return pl.pallas_call(
        _mem_kv_kernel,
        grid=(n_batch,),
        in_specs=[pl.BlockSpec((n_mem, D_MODEL), lambda b: (b, 0)), _full((D_MODEL, MEM_WIDTH)),
                  _full((D_MODEL, MEM_WIDTH))],
        out_specs=[blk, blk, blk, blk],
        out_shape=[jax.ShapeDtypeStruct((n, MEM_WIDTH), F32), jax.ShapeDtypeStruct((n, MEM_WIDTH), F32),
                   jax.ShapeDtypeStruct((n, MEM_WIDTH), BF16), jax.ShapeDtypeStruct((n, MEM_WIDTH), BF16)],
        compiler_params=_params("arbitrary"),
        name="mem_kv",
    )(mem2d, w_mk, w_mv)


def _mix_ln1(x, pool_y, mla_y, w_out_a_ref, w_out_b_ref, g_ref, b_ref):
    mix = _dot(pool_y, w_out_a_ref[...]) + _dot(mla_y, w_out_b_ref[...])
    return _layer_norm(DEEPNORM_ALPHA * x + mix, g_ref[...], b_ref[...])


def _route(x2, w_r_ref, b_r_ref, carry):
    m = x2.shape[0]
    logits = jnp.dot(x2, w_r_ref[...], preferred_element_type=F32, precision=lax.Precision.HIGHEST) + b_r_ref[...]
    e_idx = lax.broadcasted_iota(I32, (m, N_EXPERTS), 1).astype(F32)
    work = logits
    vals, picks = [], []
    for _ in range(TOP_K):
        mx = jnp.max(work, axis=1, keepdims=True)
        pick = jnp.min(jnp.where(work == mx, e_idx, float(N_EXPERTS)), axis=1, keepdims=True)
        vals.append(mx)
        picks.append(pick)
        work = jnp.where(e_idx == pick, -jnp.inf, work)
    hot = jnp.where(work == -jnp.inf, 1.0, 0.0)
    exps = [jnp.exp(v - vals[0]) for v in vals]
    denom = exps[0] + exps[1] + exps[2] + exps[3]
    r_i = lax.broadcasted_iota(I32, (m, m), 0)
    c_i = lax.broadcasted_iota(I32, (m, m), 1)
    strict_lower = jnp.where(c_i < r_i, 1.0, 0.0).astype(BF16)
    before = _dot(strict_lower, hot.astype(BF16)) + carry
    lane4 = lax.broadcasted_iota(I32, (m, TOP_K), 1)
    idx = jnp.zeros((m, TOP_K), I32)
    gates = jnp.zeros((m, TOP_K), F32)
    rank = jnp.zeros((m, TOP_K), I32)
    for k in range(TOP_K):
        rk = jnp.sum(jnp.where(e_idx == picks[k], before, 0.0), axis=1, keepdims=True).astype(I32)
        idx = jnp.where(lane4 == k, picks[k].astype(I32), idx)
        gates = jnp.where(lane4 == k, exps[k] / denom, gates)
        rank = jnp.where(lane4 == k, rk, rank)
    return idx, gates, rank, carry + jnp.sum(hot, axis=0, keepdims=True)


def _mo_ln2_route(x1, o_bf, w_mo_ref, g_ref, b_ref, w_r_ref, b_r_ref, carry):
    x2 = _layer_norm(DEEPNORM_ALPHA * x1 + _dot(o_bf, w_mo_ref[...]), g_ref[...], b_ref[...])
    return (x2,) + _route(x2, w_r_ref, b_r_ref, carry)


def _post_prompt_kernel(x_ref, pooly_ref, mlay_ref, mk_ref, mv_ref,
                        w_out_a_ref, w_out_b_ref, g1_ref, b1_ref, w_mq_ref, w_mo_ref, g2_ref, b2_ref,
                        w_r_ref, b_r_ref,
                        x2_ref, idx_ref, gate_ref, rank_ref, cnt_ref, carry_ref):
    first = jnp.logical_and(pl.program_id(0) == 0, pl.program_id(1) == 0)

    @pl.when(first)
    def _():
        carry_ref[...] = jnp.zeros(carry_ref.shape, F32)

    x1 = _mix_ln1(x_ref[...], pooly_ref[...], mlay_ref[...], w_out_a_ref, w_out_b_ref, g1_ref, b1_ref)
    qm = (_dot(x1.astype(BF16), w_mq_ref[...]) * MEM_SCALE).astype(BF16)
    outs = []
    for hd in range(MEM_HEADS):
        cols = slice(MEM_HEAD_DIM * hd, MEM_HEAD_DIM * (hd + 1))
        s = _dot_nt(qm[:, cols], mk_ref[:, cols])
        p = jnp.exp(s - jnp.max(s, axis=1, keepdims=True))
        o = _dot(p.astype(BF16), mv_ref[:, cols]) / jnp.sum(p, axis=1, keepdims=True)
        outs.append(o.astype(BF16))
    o_all = jnp.concatenate(outs, axis=1)
    x2, idx, gates, rank, carry = _mo_ln2_route(x1, o_all, w_mo_ref, g2_ref, b2_ref, w_r_ref, b_r_ref,
                                                carry_ref[...])
    x2_ref[...] = x2
    idx_ref[...] = idx
    gate_ref[...] = gates
    rank_ref[...] = rank
    carry_ref[...] = carry
    cnt_ref[...] = carry


def _post_weight_specs():
    half = (POOL_WIDTH, D_MODEL)
    return [
        _full(half), _full(half), _full((1, D_MODEL)), _full((1, D_MODEL)),
        _full((D_MODEL, MEM_WIDTH)), _full((MEM_WIDTH, D_MODEL)), _full((1, D_MODEL)), _full((1, D_MODEL)),
        _full((D_MODEL, N_EXPERTS)), _full((1, N_EXPERTS)),
    ]


def _post_prompt(x2d, pool_y, mla_y, mk_b, mv_b, weights, n_batch, seq, n_mem, n_all):
    tm = TOKEN_BLOCK
    nj = seq // tm
    tok = lambda b, j: (b * nj + j, 0)
    return pl.pallas_call(
        _post_prompt_kernel,
        grid=(n_batch, nj),
        in_specs=[
            pl.BlockSpec((tm, D_MODEL), tok), pl.BlockSpec((tm, POOL_WIDTH), tok), pl.BlockSpec((tm, POOL_WIDTH), tok),
            pl.BlockSpec((n_mem, MEM_WIDTH), lambda b, j: (b, 0)), pl.BlockSpec((n_mem, MEM_WIDTH), lambda b, j: (b, 0)),
        ] + _post_weight_specs(),
        out_specs=[
            pl.BlockSpec((tm, D_MODEL), tok), pl.BlockSpec((tm, TOP_K), tok), pl.BlockSpec((tm, TOP_K), tok),
            pl.BlockSpec((tm, TOP_K), tok), _full((1, N_EXPERTS)),
        ],
        out_shape=[
            jax.ShapeDtypeStruct((n_all, D_MODEL), F32), jax.ShapeDtypeStruct((n_all, TOP_K), I32),
            jax.ShapeDtypeStruct((n_all, TOP_K), F32), jax.ShapeDtypeStruct((n_all, TOP_K), I32),
            jax.ShapeDtypeStruct((1, N_EXPERTS), F32),
        ],
        scratch_shapes=[pltpu.VMEM((1, N_EXPERTS), F32)],
        compiler_params=_params("arbitrary", "arbitrary"),
        name="post_prompt",
    )(x2d, pool_y, mla_y, mk_b, mv_b, *weights)


def _mix_sample_kernel(x_ref, pooly_ref, mlay_ref, w_out_a_ref, w_out_b_ref, g1_ref, b1_ref, w_mq_ref,
                       x1_ref, qm_ref):
    x1 = _mix_ln1(x_ref[...], pooly_ref[...], mlay_ref[...], w_out_a_ref, w_out_b_ref, g1_ref, b1_ref)
    x1_ref[...] = x1
    qm_ref[...] = (_dot(x1.astype(BF16), w_mq_ref[...]) * MEM_SCALE).astype(BF16)


def _mix_sample(x2d, pool_y, mla_y, weights):
    n = x2d.shape[0]
    w_out_a, w_out_b, g1, b1, w_mq = weights[:5]
    return pl.pallas_call(
        _mix_sample_kernel,
        grid=(1,),
        in_specs=[_full((n, D_MODEL)), _full((n, POOL_WIDTH)), _full((n, POOL_WIDTH))] + _post_weight_specs()[:5],
        out_specs=[_full((n, D_MODEL)), _full((n, MEM_WIDTH))],
        out_shape=[jax.ShapeDtypeStruct((n, D_MODEL), F32), jax.ShapeDtypeStruct((n, MEM_WIDTH), BF16)],
        compiler_params=_params("arbitrary"),
        name="mix_sample",
    )(x2d, pool_y, mla_y, w_out_a, w_out_b, g1, b1, w_mq)


def _mem_attn_sample_kernel(q_ref, k_ref, v_ref, out_ref, res_ref, *, n_new):
    for b in range(q_ref.shape[0]):
        s = _dot_nt(q_ref[b], k_ref[b].astype(BF16))
        p = jnp.exp(s - jnp.max(s, axis=1, keepdims=True))
        res_ref[...] = _dot(p.astype(BF16), v_ref[b].astype(BF16)) / jnp.sum(p, axis=1, keepdims=True)
        for hd in range(MEM_HEADS):
            cols = slice(MEM_HEAD_DIM * hd, MEM_HEAD_DIM * (hd + 1))
            out_ref[b, :, cols] = res_ref[hd * n_new:(hd + 1) * n_new, cols].astype(BF16)


def _mem_attn_sample(q_bd, mem_k, mem_v, n_new):
    n_b, rows, _ = q_bd.shape
    n_mem = mem_k.shape[1]
    gb = min(MEM_BATCH_BLOCK, n_b)
    assert n_b % gb == 0
    blk = lambda i: (i, 0, 0)
    return pl.pallas_call(
        functools.partial(_mem_attn_sample_kernel, n_new=n_new),
        grid=(n_b // gb,),
        in_specs=[pl.BlockSpec((gb, rows, MEM_WIDTH), blk), pl.BlockSpec((gb, n_mem, MEM_WIDTH), blk),
                  pl.BlockSpec((gb, n_mem, MEM_WIDTH), blk)],
        out_specs=pl.BlockSpec((gb, n_new, MEM_WIDTH), blk),
        out_shape=jax.ShapeDtypeStruct((n_b, n_new, MEM_WIDTH), BF16),
        scratch_shapes=[pltpu.VMEM((rows, MEM_WIDTH), F32)],
        compiler_params=_params("arbitrary"),
        name="mem_attn_sample",
    )(q_bd, mem_k, mem_v)


def _route_sample_kernel(x1_ref, o_ref, cnt_in_ref, w_mo_ref, g2_ref, b2_ref, w_r_ref, b_r_ref,
                         x2_in, idx_in, gate_in, rank_in,
                         x2_ref, idx_ref, gate_ref, rank_ref, cnt_ref):
    del x2_in, idx_in, gate_in, rank_in
    x2, idx, gates, rank, carry = _mo_ln2_route(x1_ref[...], o_ref[...], w_mo_ref, g2_ref, b2_ref, w_r_ref, b_r_ref,
                                                cnt_in_ref[...])
    x2_ref[...] = x2
    idx_ref[...] = idx
    gate_ref[...] = gates
    rank_ref[...] = rank
    cnt_ref[...] = carry


def _route_sample(x1, o_bf, cnt_in, weights, x2_all, idx_all, gate_all, rank_all, n_prompt):
    n = x1.shape[0]
    assert n_prompt % n == 0
    tail = lambda i: (n_prompt // n, 0)
    anyspec = pl.BlockSpec(memory_space=pl.ANY)
    w_mo, g2, b2, w_r, b_r = weights[5:]
    return pl.pallas_call(
        _route_sample_kernel,
        grid=(1,),
        in_specs=[_full((n, D_MODEL)), _full((n, MEM_WIDTH)), _full((1, N_EXPERTS))] + _post_weight_specs()[5:] + [
            anyspec, anyspec, anyspec, anyspec],
        out_specs=[pl.BlockSpec((n, D_MODEL), tail), pl.BlockSpec((n, TOP_K), tail), pl.BlockSpec((n, TOP_K), tail),
                   pl.BlockSpec((n, TOP_K), tail), _full((1, N_EXPERTS))],
        out_shape=[jax.ShapeDtypeStruct(x2_all.shape, F32), jax.ShapeDtypeStruct(idx_all.shape, I32),
                   jax.ShapeDtypeStruct(gate_all.shape, F32), jax.ShapeDtypeStruct(rank_all.shape, I32),
                   jax.ShapeDtypeStruct((1, N_EXPERTS), F32)],
        input_output_aliases={8: 0, 9: 1, 10: 2, 11: 3},
        compiler_params=_params("arbitrary"),
        name="route_sample",
    )(x1, o_bf, cnt_in, w_mo, g2, b2, w_r, b_r, x2_all, idx_all, gate_all, rank_all)


def _moe_kernel(blk_exp_ref, idx_hbm, x_hbm, wgu_ref, bgu_ref, wd_ref, bd_ref, y_hbm,
                idx_smem, xbuf, ybuf, idx_sem, x_sem, y_sem):
    del blk_exp_ref
    i = pl.program_id(0)
    last = pl.num_programs(0) - 1
    tr = xbuf.shape[1]
    slot = i % 2
    nxt = 1 - slot

    def idx_copy(row, s):
        return pltpu.make_async_copy(idx_hbm.at[row], idx_smem.at[s], idx_sem.at[s])

    def gather(s_idx, xs):
        for r in range(tr):
            tok = idx_smem[s_idx, tr + r]
            pltpu.make_async_copy(x_hbm.at[pl.ds(tok, 1)], xbuf.at[xs, pl.ds(r, 1)], x_sem.at[xs]).start()

    def scatter(s_idx, ys):
        for r in range(tr):
            dst = idx_smem[s_idx, r]
            pltpu.make_async_copy(ybuf.at[ys, pl.ds(r, 1)], y_hbm.at[pl.ds(dst, 1)], y_sem.at[ys]).start()

    def wait_rows(buf, sem, s):
        pltpu.make_async_copy(buf.at[s], buf.at[s], sem.at[s]).wait()

    @pl.when(i == 0)
    def _():
        idx_copy(0, 1).start()
        idx_copy(0, 1).wait()
        gather(1, 0)
        idx_copy(1, 0).start()

    @pl.when(i >= 2)
    def _():
        wait_rows(ybuf, y_sem, slot)

    idx_copy(i + 1, slot).wait()
    idx_copy(i + 2, nxt).start()
    gather(slot, nxt)
    wait_rows(xbuf, x_sem, slot)
    xb = xbuf[slot].astype(BF16)
    acc = jnp.zeros((tr, D_MODEL), F32) + bd_ref[...]
    for c in range(D_EXPERT // EXPERT_CHUNK):
        cg = slice(c * EXPERT_CHUNK, (c + 1) * EXPERT_CHUNK)
        cu = slice(D_EXPERT + c * EXPERT_CHUNK, D_EXPERT + (c + 1) * EXPERT_CHUNK)
        gate = jnp.minimum(_dot(xb, wgu_ref[:, cg]) + bgu_ref[:, cg], SWIGLU_LIMIT)
        up = jnp.clip(_dot(xb, wgu_ref[:, cu]) + bgu_ref[:, cu], -SWIGLU_LIMIT, SWIGLU_LIMIT)
        act = gate * (1.0 / (1.0 + jnp.exp(-SWIGLU_ALPHA * gate))) * (up + 1.0)
        acc = acc + _dot(act.astype(BF16), wd_ref[cg, :])
    ybuf[slot] = acc
    scatter(slot, slot)

    @pl.when(i == last)
    def _():
        idx_copy(i + 2, nxt).wait()
        wait_rows(xbuf, x_sem, nxt)
        wait_rows(ybuf, y_sem, nxt)
        wait_rows(ybuf, y_sem, slot)


def _moe(blk_exp, idx_rows, x_all, w_gu, b_gu, w_d, b_d, n_assign):
    tr = EXPERT_ROWS
    n_blocks = blk_exp.shape[0]
    assert n_blocks >= 2 and idx_rows.shape == (n_blocks + 2, 2 * tr)
    anyspec = pl.BlockSpec(memory_space=pl.ANY)
    emap3 = lambda i, be: (be[i], 0, 0)
    grid_spec = pltpu.PrefetchScalarGridSpec(
        num_scalar_prefetch=1,
        grid=(n_blocks,),
        in_specs=[
            anyspec, anyspec,
            pl.BlockSpec((None, D_MODEL, 2 * D_EXPERT), emap3), pl.BlockSpec((None, 1, 2 * D_EXPERT), emap3),
            pl.BlockSpec((None, D_EXPERT, D_MODEL), emap3), pl.BlockSpec((None, 1, D_MODEL), emap3),
        ],
        out_specs=anyspec,
        scratch_shapes=[
            pltpu.SMEM((2, 2 * tr), I32),
            pltpu.VMEM((2, tr, D_MODEL), F32), pltpu.VMEM((2, tr, D_MODEL), F32),
            pltpu.SemaphoreType.DMA((2,)), pltpu.SemaphoreType.DMA((2,)), pltpu.SemaphoreType.DMA((2,)),
        ],
    )
    return pl.pallas_call(
        _moe_kernel,
        grid_spec=grid_spec,
        out_shape=jax.ShapeDtypeStruct((n_assign + 2 * tr, D_MODEL), F32),
        compiler_params=_params("arbitrary"),
        name="moe_experts",
    )(blk_exp, idx_rows, x_all, w_gu, b_gu, w_d, b_d)


def _combine_kernel(x2_ref, y_ref, gate_ref, g3_ref, b3_ref, out_ref):
    gates = gate_ref[...]
    moe = jnp.zeros(x2_ref.shape, F32)
    for k in range(TOP_K):
        moe = moe + gates[:, k:k + 1] * y_ref[:, D_MODEL * k:D_MODEL * (k + 1)]
    out_ref[...] = _layer_norm(DEEPNORM_ALPHA * x2_ref[...] + moe, g3_ref[...], b3_ref[...])


def _combine(x2_all, y_tok, gates_all, g3, b3, row0, n_rows, tm):
    assert row0 % tm == 0 and n_rows % tm == 0
    off = row0 // tm
    tok = lambda i: (off + i, 0)
    return pl.pallas_call(
        _combine_kernel,
        grid=(n_rows // tm,),
        in_specs=[pl.BlockSpec((tm, D_MODEL), tok), pl.BlockSpec((tm, TOP_K * D_MODEL), tok),
                  pl.BlockSpec((tm, TOP_K), tok), _full((1, D_MODEL)), _full((1, D_MODEL))],
        out_specs=pl.BlockSpec((tm, D_MODEL), lambda i: (i, 0)),
        out_shape=jax.ShapeDtypeStruct((n_rows, D_MODEL), F32),
        compiler_params=_params("arbitrary"),
        name="combine_ln3",
    )(x2_all, y_tok, gates_all, g3, b3)


def _rope_tables(pos):
    inv_freq = ROPE_THETA ** (-jnp.arange(ROPE_HALF, dtype=F32) / ROPE_HALF)
    ang = pos.astype(F32)[:, None] * inv_freq[None, :]
    c, s, z = jnp.cos(ang), jnp.sin(ang), jnp.zeros_like(ang)
    return jnp.concatenate([c, z, c, z], axis=1), jnp.concatenate([-s, z, s, z], axis=1)


def _pad_rope_cols(w):
    z = jnp.zeros(w.shape[:-1] + (ROPE_HALF,), w.dtype)
    return jnp.concatenate([w[..., :ROPE_HALF], z, w[..., ROPE_HALF:], z], axis=-1)


def _layer_weights(l, w_in, w_pool, pool_scale, q_norm_g, w_uq, kv_norm_g, w_uk, w_uv, w_out, ln1_g, ln1_b,
                   w_mq, w_mo, ln2_g, ln2_b, w_router, b_router):
    o3 = POOL_WIDTH + Q_RANK + KV_RANK
    w_in_p = jnp.concatenate([w_in[l][:, :o3], _pad_rope_cols(w_in[l][:, o3:])], axis=1).astype(BF16)
    uq = w_uq[l].reshape(Q_RANK, MLA_HEADS, QK_NOPE_DIM + QK_ROPE_DIM)
    uq_p = jnp.concatenate([uq[..., :QK_NOPE_DIM], _pad_rope_cols(uq[..., QK_NOPE_DIM:])], axis=-1)
    uq_p = uq_p.reshape(Q_RANK, MLA_HEADS * QK_PAD).astype(BF16)
    w_ukt = jnp.transpose(w_uk[l], (1, 2, 0)).astype(BF16)
    w_uv_h = jnp.transpose(w_uv[l], (1, 0, 2)).astype(BF16)
    row = lambda v: v[l].reshape(1, -1)
    proj = [w_in_p, row(q_norm_g), uq_p, row(kv_norm_g), w_ukt, w_pool[l].astype(BF16), row(pool_scale)]
    post = [w_out[l][:POOL_WIDTH].astype(BF16), w_out[l][POOL_WIDTH:].astype(BF16), row(ln1_g), row(ln1_b),
            w_mq[l].astype(BF16), w_mo[l].astype(BF16), row(ln2_g), row(ln2_b), w_router[l], row(b_router)]
    return proj, w_uv_h, post


def _routing_tables(idx_all, rank_all, counts, n_blocks):
    tr = EXPERT_ROWS
    cnt = counts.reshape(-1).astype(I32)
    padded = ((cnt + tr - 1) // tr) * tr
    pend = jnp.cumsum(padded)
    pstart = pend - padded
    dest = (pstart[idx_all] + rank_all).reshape(-1)
    m = dest.shape[0]
    n_rows = n_blocks * tr
    row_src = jnp.full((n_rows,), -1, I32).at[dest].set(jnp.arange(m, dtype=I32), unique_indices=True)
    spare = m + jnp.arange(n_rows, dtype=I32) % (2 * tr)
    dst_rows = jnp.where(row_src >= 0, row_src, spare).reshape(n_blocks, tr)
    src_tok = (jnp.maximum(row_src, 0) // TOP_K).reshape(n_blocks, tr)
    spare_row = spare[:tr][None]
    zero_rows = jnp.zeros((2, tr), I32)
    idx_rows = jnp.concatenate([jnp.concatenate([spare_row, dst_rows, spare_row], axis=0),
                                jnp.concatenate([src_tok, zero_rows], axis=0)], axis=1)
    n_used = pend[-1] // tr
    blk = jnp.minimum(jnp.arange(n_blocks, dtype=I32), n_used - 1)
    blk_exp = jnp.sum((pend[None, :] <= (blk * tr)[:, None]).astype(I32), axis=1)
    return jnp.clip(blk_exp, 0, N_EXPERTS - 1), idx_rows


def kernel(x_prompt, x_sample, mem_prompt, cache_kv_latent, cache_k_rope, cache_mem_k, cache_mem_v, state_pool,
           page_table, w_in, w_pool, pool_scale, q_norm_g, w_uq, kv_norm_g, w_uk, w_uv, w_out, ln1_g, ln1_b,
           w_mq, w_mk, w_mv, w_mo, ln2_g, ln2_b, w_router, b_router, w_gate_up, b_gate_up, w_down, b_down,
           ln3_g, ln3_b):
    n_bp, seq, _ = x_prompt.shape
    n_bs, n_new, _ = x_sample.shape
    n_mem = mem_prompt.shape[1]
    page = cache_kv_latent.shape[2]
    past_len = page_table.shape[1] * page
    n_p = n_bp * seq
    n_s = n_bs * n_new
    n_all = n_p + n_s
    n_assign = n_all * TOP_K
    assert w_in.shape[0] == DEPTH == 1
    assert seq % TOKEN_BLOCK == 0 and seq % ATTN_K_BLOCK == 0 and seq % ATTN_Q_BLOCK == 0 and seq >= POOL_HALO
    assert n_p % n_s == 0 and (n_s % TOKEN_BLOCK == 0 or TOKEN_BLOCK % n_s == 0)
    l = 0

    proj_w, w_uv_h, post_w = _layer_weights(l, w_in, w_pool, pool_scale, q_norm_g, w_uq, kv_norm_g, w_uk, w_uv,
                                            w_out, ln1_g, ln1_b, w_mq, w_mo, ln2_g, ln2_b, w_router, b_router)

    xp = x_prompt.reshape(n_p, D_MODEL)
    cos_p, sin_p = _rope_tables(jnp.arange(seq))
    qcat, kcat, lat_p, kr_p, pooly_p, pstate_p = _proj_prompt(xp, proj_w, cos_p, sin_p, n_bp, seq)
    mlay_p = _attn_prompt(qcat, kcat, w_uv_h, n_bp, seq)
    mk, mv, mk_b, mv_b = _mem_kv(mem_prompt.reshape(n_bp * n_mem, D_MODEL), w_mk[l].astype(BF16),
                                 w_mv[l].astype(BF16), n_bp, n_mem)
    x2_all, idx_all, gate_all, rank_all, cnt_p = _post_prompt(xp, pooly_p, mlay_p, mk_b, mv_b, post_w,
                                                              n_bp, seq, n_mem, n_all)

    xs = jnp.transpose(x_sample, (1, 0, 2)).reshape(n_s, D_MODEL)
    state_t = jnp.transpose(state_pool[l], (1, 0, 2))
    cos_s, sin_s = _rope_tables(jnp.repeat(past_len + jnp.arange(n_new), n_bs))
    qcat_s, kcat_s, lat_s, kr_s, pooly_s, pstate_s = _proj_sample(xs, state_t, proj_w, cos_s, sin_s, past_len)
    q_b = jnp.transpose(qcat_s.reshape(MLA_HEADS, n_new, n_bs, QK_PAD), (2, 0, 1, 3)).reshape(
        n_bs, MLA_HEADS * n_new, QK_PAD)
    knew_b = jnp.transpose(kcat_s.reshape(n_new, n_bs, QK_PAD), (1, 0, 2))
    knew_b = jnp.pad(knew_b, ((0, 0), (0, LANES - n_new), (0, 0)))
    mlay_b = _attn_sample(page_table, q_b, knew_b, w_uv_h, cache_kv_latent, cache_k_rope, l)
    mlay_s = jnp.transpose(mlay_b, (1, 0, 2)).reshape(n_s, POOL_WIDTH).astype(BF16)
    x1_s, qm_s = _mix_sample(xs, pooly_s, mlay_s, post_w)
    qm_b = jnp.transpose(qm_s.reshape(n_new, n_bs, MEM_WIDTH), (1, 0, 2))
    head_of_col = jnp.arange(MEM_WIDTH) // MEM_HEAD_DIM
    head_mask = (head_of_col[None, :] == jnp.arange(MEM_HEADS)[:, None]).astype(BF16)
    q_bd = (qm_b[:, None, :, :] * head_mask[None, :, None, :]).reshape(n_bs, MEM_HEADS * n_new, MEM_WIDTH)
    o_b = _mem_attn_sample(q_bd, cache_mem_k[l].reshape(n_bs, n_mem, MEM_WIDTH),
                           cache_mem_v[l].reshape(n_bs, n_mem, MEM_WIDTH), n_new)
    o_s = jnp.transpose(o_b, (1, 0, 2)).reshape(n_s, MEM_WIDTH)
    x2_all, idx_all, gate_all, rank_all, counts = _route_sample(x1_s, o_s, cnt_p, post_w, x2_all, idx_all, gate_all,
                                                               rank_all, n_p)

    n_blocks = -(-n_assign // EXPERT_ROWS) + N_EXPERTS + 1
    blk_exp, idx_rows = _routing_tables(idx_all, rank_all, counts, n_blocks)
    y_tok = _moe(blk_exp, idx_rows, x2_all, w_gate_up[l].astype(BF16), b_gate_up[l][:, None, :],
                 w_down[l].astype(BF16), b_down[l][:, None, :], n_assign)
    y_tok = y_tok.reshape(-1, TOP_K * D_MODEL)
    g3, b3 = ln3_g[l].reshape(1, -1), ln3_b[l].reshape(1, -1)
    y_p = _combine(x2_all, y_tok, gate_all, g3, b3, 0, n_p, TOKEN_BLOCK)
    y_s = _combine(x2_all, y_tok, gate_all, g3, b3, n_p, n_s, min(TOKEN_BLOCK, n_s))

    def from_steps(a, width):
        return jnp.transpose(a.reshape(n_new, n_bs, width), (1, 0, 2))[None]

    return (
        y_p.reshape(n_bp, seq, D_MODEL),
        from_steps(y_s, D_MODEL)[0],
        lat_p.reshape(1, n_bp, seq, KV_RANK),
        kr_p.reshape(1, n_bp, seq, QK_ROPE_DIM),
        pstate_p[None, :, POOL_HALO - POOL_BUF:, :],
        mk.reshape(1, n_bp, n_mem, MEM_HEADS, MEM_HEAD_DIM),
        mv.reshape(1, n_bp, n_mem, MEM_HEADS, MEM_HEAD_DIM),
        from_steps(lat_s, KV_RANK),
        from_steps(kr_s, QK_ROPE_DIM),
        jnp.transpose(pstate_s, (1, 0, 2))[None],
    )
```

```python
import functools

import jax
import jax.numpy as jnp
from jax import lax
from jax.experimental import pallas as pl
from jax.experimental.pallas import tpu as pltpu

F32 = jnp.float32
BF16 = jnp.bfloat16
I32 = jnp.int32

D_MODEL = 1024
POOL_WIDTH = 512
POOL_WINDOWS = (2, 4, 8, 16)
POOL_GROUP_DIM = 128
POOL_BUF = 15
POOL_HALO = 16
MLA_HEADS = 4
QK_NOPE_DIM = 128
QK_ROPE_DIM = 64
ROPE_HALF = 32
V_HEAD_DIM = 128
Q_RANK = 256
KV_RANK = 128
ROPE_THETA = 10000.0
QK_PAD = 256
ONES_LANE = QK_PAD - 1
MEM_HEADS = 4
MEM_HEAD_DIM = 128
MEM_WIDTH = 512
N_EXPERTS = 32
TOP_K = 4
D_EXPERT = 1024
SWIGLU_LIMIT = 7.0
SWIGLU_ALPHA = 1.702
LN_EPS = 1e-5
RMS_EPS = 1e-6
DEPTH = 1
DEEPNORM_ALPHA = (2.0 * DEPTH) ** 0.25
ATTN_SCALE = (QK_NOPE_DIM + QK_ROPE_DIM) ** -0.5
MEM_SCALE = MEM_HEAD_DIM ** -0.5

LANES = 128
VMEM_LIMIT = 48 * 1024 * 1024

TOKEN_BLOCK = 256
ATTN_Q_BLOCK = 256
ATTN_K_BLOCK = 512
ATTN_GROUPS = 2
PAGES_PER_STEP = 16
MEM_BATCH_BLOCK = 8
EXPERT_ROWS = 256
EXPERT_CHUNK = 256


def _dot(a, b):
    return jnp.dot(a, b, preferred_element_type=F32)


def _dot_nt(a, b):
    return lax.dot_general(a, b, (((1,), (1,)), ((), ())), preferred_element_type=F32)


def _rms(x, g):
    return x * lax.rsqrt(jnp.mean(x * x, axis=-1, keepdims=True) + RMS_EPS) * g


def _layer_norm(x, g, b):
    mu = jnp.mean(x, axis=-1, keepdims=True)
    xc = x - mu
    var = jnp.mean(xc * xc, axis=-1, keepdims=True)
    return xc * lax.rsqrt(var + LN_EPS) * g + b


ROW_TILE = D_MODEL // LANES


def _store_row_tiles(ref, lead, val):
    m = val.shape[0]
    for j in range(ROW_TILE):
        ref[(*lead, pl.ds(j, m, stride=ROW_TILE), slice(None))] = val[:, LANES * j:LANES * (j + 1)]


def _load_row_tiles(ref, lead, m):
    return jnp.concatenate([ref[(*lead, pl.ds(j, m, stride=ROW_TILE), slice(None))] for j in range(ROW_TILE)], axis=1)


def _params(*semantics):
    return pltpu.CompilerParams(dimension_semantics=semantics, vmem_limit_bytes=VMEM_LIMIT)


def _full(shape):
    n = len(shape)
    return pl.BlockSpec(shape, lambda *_: (0,) * n)


def _project(x_bf, w_in_ref, qg_ref, w_uq_ref, kvg_ref, w_ukt_ref, cos, sin):
    h = _dot(x_bf, w_in_ref[...])
    u = h[:, :POOL_WIDTH]
    qn = _rms(h[:, POOL_WIDTH:POOL_WIDTH + Q_RANK], qg_ref[...])
    q = _dot(qn.astype(BF16), w_uq_ref[...])
    q_lat, q_rope = [], []
    for hd in range(MLA_HEADS):
        nope = q[:, QK_PAD * hd:QK_PAD * hd + QK_NOPE_DIM]
        rp = q[:, QK_PAD * hd + QK_NOPE_DIM:QK_PAD * (hd + 1)]
        q_lat.append(_dot(nope.astype(BF16), w_ukt_ref[hd]) * ATTN_SCALE)
        q_rope.append((rp * cos + pltpu.roll(rp, 64, 1) * sin) * ATTN_SCALE)
    o2 = POOL_WIDTH + Q_RANK
    lat = _rms(h[:, o2:o2 + KV_RANK], kvg_ref[...])
    kp = h[:, o2 + KV_RANK:]
    k_rope = kp * cos + pltpu.roll(kp, 64, 1) * sin
    return u, q_lat, q_rope, lat, k_rope


def _compact_rope(r):
    lane = lax.broadcasted_iota(I32, r.shape, 1)
    moved = pltpu.roll(r, 96, 1)
    return jnp.where(lane < ROPE_HALF, r, jnp.where(lane < QK_ROPE_DIM, moved, 0.0))


def _with_ones_lane(r):
    lane = lax.broadcasted_iota(I32, r.shape, 1)
    return jnp.where(lane == ONES_LANE - KV_RANK, 1.0, r)


def _pool_group_out(diff, g, w_pool_ref, pscale_ref):
    cols = slice(POOL_GROUP_DIM * g, POOL_GROUP_DIM * (g + 1))
    return _dot(diff.astype(BF16), w_pool_ref[g]) * pscale_ref[:, cols]


def _proj_prompt_kernel(x_ref, w_in_ref, qg_ref, w_uq_ref, kvg_ref, w_ukt_ref, w_pool_ref, pscale_ref,
                        cos_ref, sin_ref,
                        qcat_ref, kcat_ref, lat_ref, kr_ref, pooly_ref, pstate_ref, ubuf):
    j = pl.program_id(1)
    tm = x_ref.shape[0]

    @pl.when(j == 0)
    def _():
        ubuf[0:POOL_HALO, :] = jnp.zeros((POOL_HALO, POOL_WIDTH), F32)

    u, q_lat, q_rope, lat, k_rope = _project(
        x_ref[...].astype(BF16), w_in_ref, qg_ref, w_uq_ref, kvg_ref, w_ukt_ref, cos_ref[...], sin_ref[...])
    for hd in range(MLA_HEADS):
        qcat_ref[hd, :, 0:KV_RANK] = q_lat[hd].astype(BF16)
        qcat_ref[hd, :, KV_RANK:QK_PAD] = q_rope[hd].astype(BF16)
    lat_ref[...] = lat
    kcat_ref[:, 0:KV_RANK] = lat.astype(BF16)
    kcat_ref[:, KV_RANK:QK_PAD] = _with_ones_lane(k_rope).astype(BF16)
    kr_ref[...] = _compact_rope(k_rope)[:, :QK_ROPE_DIM]

    ubuf[POOL_HALO:POOL_HALO + tm, :] = u
    pos = j * tm + lax.broadcasted_iota(I32, (tm, 1), 0)
    for g, w in enumerate(POOL_WINDOWS):
        cols = slice(POOL_GROUP_DIM * g, POOL_GROUP_DIM * (g + 1))
        ug = u[:, cols]
        ssum = ug
        for k in range(1, w):
            ssum = ssum + ubuf[POOL_HALO - k:POOL_HALO - k + tm, cols]
        count = jnp.minimum(w, pos + 1).astype(F32)
        pooly_ref[:, cols] = _pool_group_out(ssum / count - ug, g, w_pool_ref, pscale_ref).astype(BF16)
    tail = ubuf[tm:tm + POOL_HALO, :]
    ubuf[0:POOL_HALO, :] = tail

    @pl.when(j == pl.num_programs(1) - 1)
    def _():
        pstate_ref[...] = tail


def _proj_sample_kernel(x_ref, state_ref, w_in_ref, qg_ref, w_uq_ref, kvg_ref, w_ukt_ref, w_pool_ref, pscale_ref,
                        cos_ref, sin_ref,
                        qcat_ref, kcat_ref, lat_ref, kr_ref, pooly_ref, pstate_ref, *, past_len):
    n_b = state_ref.shape[1]
    n_t = x_ref.shape[0] // n_b
    u, q_lat, q_rope, lat, k_rope = _project(
        x_ref[...].astype(BF16), w_in_ref, qg_ref, w_uq_ref, kvg_ref, w_ukt_ref, cos_ref[...], sin_ref[...])
    for hd in range(MLA_HEADS):
        qcat_ref[hd, :, 0:KV_RANK] = q_lat[hd].astype(BF16)
        qcat_ref[hd, :, KV_RANK:QK_PAD] = _compact_rope(q_rope[hd]).astype(BF16)
    lat_ref[...] = lat
    kc = _compact_rope(k_rope)
    kcat_ref[:, 0:KV_RANK] = lat.astype(BF16)
    kcat_ref[:, KV_RANK:QK_PAD] = _with_ones_lane(kc).astype(BF16)
    kr_ref[...] = kc[:, :QK_ROPE_DIM]

    def ext(jj):
        if jj < POOL_BUF:
            return state_ref[jj]
        return u[(jj - POOL_BUF) * n_b:(jj - POOL_BUF + 1) * n_b, :]

    for t in range(n_t):
        ut = ext(POOL_BUF + t)
        for g, w in enumerate(POOL_WINDOWS):
            cols = slice(POOL_GROUP_DIM * g, POOL_GROUP_DIM * (g + 1))
            ssum = ut[:, cols]
            for k in range(1, w):
                ssum = ssum + ext(POOL_BUF + t - k)[:, cols]
            count = float(min(w, past_len + t + 1))
            y = _pool_group_out(ssum / count - ut[:, cols], g, w_pool_ref, pscale_ref)
            pooly_ref[t * n_b:(t + 1) * n_b, cols] = y.astype(BF16)
    for jj in range(POOL_BUF):
        pstate_ref[jj] = ext(n_t + jj)


def _proj_weight_specs():
    return [
        _full((D_MODEL, D_MODEL)),
        _full((1, Q_RANK)),
        _full((Q_RANK, MLA_HEADS * QK_PAD)),
        _full((1, KV_RANK)),
        _full((MLA_HEADS, QK_NOPE_DIM, KV_RANK)),
        _full((len(POOL_WINDOWS), POOL_GROUP_DIM, POOL_GROUP_DIM)),
        _full((1, POOL_WIDTH)),
    ]


def _proj_prompt(x2d, weights, cos_t, sin_t, n_batch, seq):
    tm = TOKEN_BLOCK
    n = n_batch * seq
    nj = seq // tm
    tok = lambda b, j: (b * nj + j, 0)
    return pl.pallas_call(
        _proj_prompt_kernel,
        grid=(n_batch, nj),
        in_specs=[pl.BlockSpec((tm, D_MODEL), tok)] + _proj_weight_specs() + [
            pl.BlockSpec((tm, LANES), lambda b, j: (j, 0)),
            pl.BlockSpec((tm, LANES), lambda b, j: (j, 0)),
        ],
        out_specs=[
            pl.BlockSpec((MLA_HEADS, tm, QK_PAD), lambda b, j: (0, b * nj + j, 0)),
            pl.BlockSpec((tm, QK_PAD), tok),
            pl.BlockSpec((tm, KV_RANK), tok),
            pl.BlockSpec((tm, QK_ROPE_DIM), tok),
            pl.BlockSpec((tm, POOL_WIDTH), tok),
            pl.BlockSpec((None, POOL_HALO, POOL_WIDTH), lambda b, j: (b, 0, 0)),
        ],
        out_shape=[
            jax.ShapeDtypeStruct((MLA_HEADS, n, QK_PAD), BF16),
            jax.ShapeDtypeStruct((n, QK_PAD), BF16),
            jax.ShapeDtypeStruct((n, KV_RANK), F32),
            jax.ShapeDtypeStruct((n, QK_ROPE_DIM), F32),
            jax.ShapeDtypeStruct((n, POOL_WIDTH), BF16),
            jax.ShapeDtypeStruct((n_batch, POOL_HALO, POOL_WIDTH), F32),
        ],
        scratch_shapes=[pltpu.VMEM((POOL_HALO + tm, POOL_WIDTH), F32)],
        compiler_params=_params("arbitrary", "arbitrary"),
        name="proj_prompt",
    )(x2d, *weights, cos_t, sin_t)


def _proj_sample(x2d, state_t, weights, cos_t, sin_t, past_len):
    n = x2d.shape[0]
    n_b = state_t.shape[1]
    return pl.pallas_call(
        functools.partial(_proj_sample_kernel, past_len=past_len),
        grid=(1,),
        in_specs=[_full((n, D_MODEL)), _full((POOL_BUF, n_b, POOL_WIDTH))] + _proj_weight_specs() + [
            _full((n, LANES)), _full((n, LANES))],
        out_specs=[
            _full((MLA_HEADS, n, QK_PAD)), _full((n, QK_PAD)), _full((n, KV_RANK)), _full((n, QK_ROPE_DIM)),
            _full((n, POOL_WIDTH)), _full((POOL_BUF, n_b, POOL_WIDTH)),
        ],
        out_shape=[
            jax.ShapeDtypeStruct((MLA_HEADS, n, QK_PAD), BF16),
            jax.ShapeDtypeStruct((n, QK_PAD), BF16),
            jax.ShapeDtypeStruct((n, KV_RANK), F32),
            jax.ShapeDtypeStruct((n, QK_ROPE_DIM), F32),
            jax.ShapeDtypeStruct((n, POOL_WIDTH), BF16),
            jax.ShapeDtypeStruct((POOL_BUF, n_b, POOL_WIDTH), F32),
        ],
        compiler_params=_params("arbitrary"),
        name="proj_sample",
    )(x2d, state_t, *weights, cos_t, sin_t)


def _attn_block(q, k, mask, m_ref, acc_ref):
    s = _dot_nt(q, k)
    if mask is not None:
        s = jnp.where(mask, s, -jnp.inf)
    tiles = [s[:, c * LANES:(c + 1) * LANES] for c in range(s.shape[1] // LANES)]
    m_old = m_ref[...]
    m_new = jnp.maximum(m_old, jnp.max(functools.reduce(jnp.maximum, tiles), axis=1, keepdims=True))
    alpha = jnp.exp(m_old - m_new)
    p = jnp.concatenate([jnp.exp(t - m_new) for t in tiles], axis=1).astype(BF16)
    acc_ref[...] = jnp.concatenate([alpha, alpha], axis=1) * acc_ref[...] + _dot(p, k)
    m_ref[...] = m_new


def _attn_output(acc):
    return acc[:, :KV_RANK] / acc[:, ONES_LANE:ONES_LANE + 1]


def _attn_prompt_kernel(q_ref, k_ref, w_uv_ref, out_ref, m_ref, acc_ref):
    i = pl.program_id(1)
    tq = q_ref.shape[1]
    tk = ATTN_K_BLOCK
    hpg = MLA_HEADS // ATTN_GROUPS
    rows = hpg * tq
    m_ref[...] = jnp.full(m_ref.shape, -jnp.inf, F32)
    acc_ref[...] = jnp.zeros(acc_ref.shape, F32)

    def block(jk, masked):
        k = k_ref[pl.ds(pl.multiple_of(jk * tk, tk), tk), :]
        mask = None
        if masked:
            q_pos = i * tq + lax.broadcasted_iota(I32, (rows, tk), 0) % tq
            k_pos = jk * tk + lax.broadcasted_iota(I32, (rows, tk), 1)
            mask = k_pos <= q_pos
        for g in range(ATTN_GROUPS):
            q = q_ref[hpg * g:hpg * (g + 1)].reshape(rows, QK_PAD)
            _attn_block(q, k, mask, m_ref.at[g], acc_ref.at[g])

    n_full = (i * tq + 1) // tk
    n_kv = (i * tq + tq - 1) // tk + 1

    def full_body(jk, c):
        block(jk, False)
        return c

    def diag_body(jk, c):
        block(jk, True)
        return c

    lax.fori_loop(0, n_full, full_body, 0)
    lax.fori_loop(n_full, n_kv, diag_body, 0)

    for g in range(ATTN_GROUPS):
        o = _attn_output(acc_ref[g])
        for hl in range(hpg):
            hd = hpg * g + hl
            oh = o[hl * tq:(hl + 1) * tq, :].astype(BF16)
            out_ref[:, V_HEAD_DIM * hd:V_HEAD_DIM * (hd + 1)] = _dot(oh, w_uv_ref[hd]).astype(BF16)


def _attn_prompt(qcat, kcat, w_uv_h, n_batch, seq):
    tq = ATTN_Q_BLOCK
    nq = seq // tq
    n = n_batch * seq
    rows = (MLA_HEADS // ATTN_GROUPS) * tq
    return pl.pallas_call(
        _attn_prompt_kernel,
        grid=(n_batch, nq),
        in_specs=[
            pl.BlockSpec((MLA_HEADS, tq, QK_PAD), lambda b, i: (0, b * nq + i, 0)),
            pl.BlockSpec((seq, QK_PAD), lambda b, i: (b, 0)),
            _full((MLA_HEADS, KV_RANK, V_HEAD_DIM)),
        ],
        out_specs=pl.BlockSpec((tq, MLA_HEADS * V_HEAD_DIM), lambda b, i: (b * nq + i, 0)),
        out_shape=jax.ShapeDtypeStruct((n, MLA_HEADS * V_HEAD_DIM), BF16),
        scratch_shapes=[pltpu.VMEM((ATTN_GROUPS, rows, LANES), F32), pltpu.VMEM((ATTN_GROUPS, rows, QK_PAD), F32)],
        compiler_params=_params("arbitrary", "arbitrary"),
        name="attn_prompt",
    )(qcat, kcat, w_uv_h)


def _attn_sample_kernel(pt_ref, q_ref, knew_ref, w_uv_ref, *rest, n_pages_step, n_new):
    lat_refs = rest[:n_pages_step]
    rope_refs = rest[n_pages_step:2 * n_pages_step]
    out_ref, m_ref, l_ref, acc_ref, res_ref = rest[2 * n_pages_step:]
    del pt_ref
    s_idx = pl.program_id(1)

    @pl.when(s_idx == 0)
    def _():
        m_ref[...] = jnp.full(m_ref.shape, -jnp.inf, F32)
        l_ref[...] = jnp.zeros(l_ref.shape, F32)
        acc_ref[...] = jnp.zeros(acc_ref.shape, F32)

    def update(s_tiles, values):
        m_old = m_ref[...]
        m_new = jnp.maximum(m_old, jnp.max(functools.reduce(jnp.maximum, s_tiles), axis=1, keepdims=True))
        alpha = jnp.exp(m_old - m_new)
        p_tiles = [jnp.exp(t - m_new) for t in s_tiles]
        row_sum = jnp.sum(functools.reduce(jnp.add, p_tiles), axis=1, keepdims=True)
        pv = functools.reduce(jnp.add, [_dot(p.astype(BF16), v) for p, v in zip(p_tiles, values)])
        l_ref[...] = alpha * l_ref[...] + row_sum
        acc_ref[...] = alpha * acc_ref[...] + pv
        m_ref[...] = m_new

    q = q_ref[...]
    q_lat = q[:, :KV_RANK]
    q_rope = q[:, KV_RANK:KV_RANK + QK_ROPE_DIM]
    lats = [lat_refs[p][...].astype(BF16) for p in range(n_pages_step)]
    update([_dot_nt(q_lat, lats[p]) + _dot(q_rope, rope_refs[p][...].astype(BF16)) for p in range(n_pages_step)],
           lats)

    @pl.when(s_idx == pl.num_programs(1) - 1)
    def _():
        kn = knew_ref[...]
        q_t = lax.broadcasted_iota(I32, (q.shape[0], kn.shape[0]), 0) % n_new
        k_t = lax.broadcasted_iota(I32, (q.shape[0], kn.shape[0]), 1)
        update([jnp.where(k_t <= q_t, _dot_nt(q, kn), -jnp.inf)], [kn[:, :KV_RANK]])
        o = (acc_ref[...] / l_ref[...]).astype(BF16)
        for hd in range(MLA_HEADS):
            res_ref[...] = _dot(o, w_uv_ref[hd])
            out_ref[:, V_HEAD_DIM * hd:V_HEAD_DIM * (hd + 1)] = res_ref[hd * n_new:(hd + 1) * n_new, :]


def _attn_sample(page_table, q_b, knew_b, w_uv_h, cache_lat, cache_rope, layer):
    n_b, n_pages = page_table.shape
    page = cache_lat.shape[2]
    pps = min(PAGES_PER_STEP, n_pages)
    assert n_pages % pps == 0
    n_steps = n_pages // pps
    rows = q_b.shape[1]
    n_new = rows // MLA_HEADS
    t_pad = knew_b.shape[1]

    def page_map(p):
        return lambda b, s, pt: (layer, pt[b * n_pages + s * pps + p], 0, 0)

    lat_specs = [pl.BlockSpec((None, None, page, KV_RANK), page_map(p)) for p in range(pps)]
    rope_specs = [pl.BlockSpec((None, None, QK_ROPE_DIM, page), page_map(p)) for p in range(pps)]
    grid_spec = pltpu.PrefetchScalarGridSpec(
        num_scalar_prefetch=1,
        grid=(n_b, n_steps),
        in_specs=[
            pl.BlockSpec((None, rows, QK_PAD), lambda b, s, pt: (b, 0, 0)),
            pl.BlockSpec((None, t_pad, QK_PAD), lambda b, s, pt: (b, 0, 0)),
            pl.BlockSpec((MLA_HEADS, KV_RANK, V_HEAD_DIM), lambda b, s, pt: (0, 0, 0)),
        ] + lat_specs + rope_specs,
        out_specs=pl.BlockSpec((None, n_new, MLA_HEADS * V_HEAD_DIM), lambda b, s, pt: (b, 0, 0)),
        scratch_shapes=[
            pltpu.VMEM((rows, LANES), F32), pltpu.VMEM((rows, LANES), F32), pltpu.VMEM((rows, KV_RANK), F32),
            pltpu.VMEM((rows, V_HEAD_DIM), F32),
        ],
    )
    return pl.pallas_call(
        functools.partial(_attn_sample_kernel, n_pages_step=pps, n_new=n_new),
        grid_spec=grid_spec,
        out_shape=jax.ShapeDtypeStruct((n_b, n_new, MLA_HEADS * V_HEAD_DIM), F32),
        compiler_params=_params("arbitrary", "arbitrary"),
        name="attn_sample",
    )(page_table.reshape(-1), q_b, knew_b, w_uv_h, *([cache_lat] * pps), *([cache_rope] * pps))


def _mem_kv_kernel(mem_ref, w_mk_ref, w_mv_ref, k_ref, v_ref, kb_ref, vb_ref):
    m = mem_ref[...].astype(BF16)
    k = _dot(m, w_mk_ref[...])
    v = _dot(m, w_mv_ref[...])
    k_ref[...] = k
    v_ref[...] = v
    kb_ref[...] = k.astype(BF16)
    vb_ref[...] = v.astype(BF16)


def _mem_kv(mem2d, w_mk, w_mv, n_batch, n_mem):
    blk = pl.BlockSpec((n_mem, MEM_WIDTH), lambda b: (b, 0))
    n = n_batch * n_mem
    return pl.pallas_call(
        _mem_kv_kernel,
        grid=(n_batch,),
        in_specs=[pl.BlockSpec((n_mem, D_MODEL), lambda b: (b, 0)), _full((D_MODEL, MEM_WIDTH)),
                  _full((D_MODEL, MEM_WIDTH))],
        out_specs=[blk, blk, blk, blk],
        out_shape=[jax.ShapeDtypeStruct((n, MEM_WIDTH), F32), jax.ShapeDtypeStruct((n, MEM_WIDTH), F32),
                   jax.ShapeDtypeStruct((n, MEM_WIDTH), BF16), jax.ShapeDtypeStruct((n, MEM_WIDTH), BF16)],
        compiler_params=_params("arbitrary"),
        name="mem_kv",
    )(mem2d, w_mk, w_mv)


def _mix_ln1(x, pool_y, mla_y, w_out_a_ref, w_out_b_ref, g_ref, b_ref):
    mix = _dot(pool_y, w_out_a_ref[...]) + _dot(mla_y, w_out_b_ref[...])
    return _layer_norm(DEEPNORM_ALPHA * x + mix, g_ref[...], b_ref[...])


def _route(x2, w_r_ref, b_r_ref, carry):
    m = x2.shape[0]
    logits = jnp.dot(x2, w_r_ref[...], preferred_element_type=F32, precision=lax.Precision.HIGHEST) + b_r_ref[...]
    e_idx = lax.broadcasted_iota(I32, (m, N_EXPERTS), 1).astype(F32)
    work = logits
    vals, picks = [], []
    for _ in range(TOP_K):
        mx = jnp.max(work, axis=1, keepdims=True)
        pick = jnp.min(jnp.where(work == mx, e_idx, float(N_EXPERTS)), axis=1, keepdims=True)
        vals.append(mx)
        picks.append(pick)
        work = jnp.where(e_idx == pick, -jnp.inf, work)
    hot = jnp.where(work == -jnp.inf, 1.0, 0.0)
    exps = [jnp.exp(v - vals[0]) for v in vals]
    denom = exps[0] + exps[1] + exps[2] + exps[3]
    r_i = lax.broadcasted_iota(I32, (m, m), 0)
    c_i = lax.broadcasted_iota(I32, (m, m), 1)
    strict_lower = jnp.where(c_i < r_i, 1.0, 0.0).astype(BF16)
    before = _dot(strict_lower, hot.astype(BF16)) + carry
    lane4 = lax.broadcasted_iota(I32, (m, TOP_K), 1)
    idx = jnp.zeros((m, TOP_K), I32)
    gates = jnp.zeros((m, TOP_K), F32)
    rank = jnp.zeros((m, TOP_K), I32)
    for k in range(TOP_K):
        rk = jnp.sum(jnp.where(e_idx == picks[k], before, 0.0), axis=1, keepdims=True).astype(I32)
        idx = jnp.where(lane4 == k, picks[k].astype(I32), idx)
        gates = jnp.where(lane4 == k, exps[k] / denom, gates)
        rank = jnp.where(lane4 == k, rk, rank)
    return idx, gates, rank, carry + jnp.sum(hot, axis=0, keepdims=True)


def _mo_ln2_route(x1, o_bf, w_mo_ref, g_ref, b_ref, w_r_ref, b_r_ref, carry):
    x2 = _layer_norm(DEEPNORM_ALPHA * x1 + _dot(o_bf, w_mo_ref[...]), g_ref[...], b_ref[...])
    return (x2,) + _route(x2, w_r_ref, b_r_ref, carry)


def _post_prompt_kernel(x_ref, pooly_ref, mlay_ref, mk_ref, mv_ref,
                        w_out_a_ref, w_out_b_ref, g1_ref, b1_ref, w_mq_ref, w_mo_ref, g2_ref, b2_ref,
                        w_r_ref, b_r_ref,
                        x2_ref, idx_ref, gate_ref, rank_ref, cnt_ref, carry_ref):
    first = jnp.logical_and(pl.program_id(0) == 0, pl.program_id(1) == 0)

    @pl.when(first)
    def _():
        carry_ref[...] = jnp.zeros(carry_ref.shape, F32)

    x1 = _mix_ln1(x_ref[...], pooly_ref[...], mlay_ref[...], w_out_a_ref, w_out_b_ref, g1_ref, b1_ref)
    qm = (_dot(x1.astype(BF16), w_mq_ref[...]) * MEM_SCALE).astype(BF16)
    outs = []
    for hd in range(MEM_HEADS):
        cols = slice(MEM_HEAD_DIM * hd, MEM_HEAD_DIM * (hd + 1))
        s = _dot_nt(qm[:, cols], mk_ref[:, cols])
        p = jnp.exp(s - jnp.max(s, axis=1, keepdims=True))
        o = _dot(p.astype(BF16), mv_ref[:, cols]) / jnp.sum(p, axis=1, keepdims=True)
        outs.append(o.astype(BF16))
    o_all = jnp.concatenate(outs, axis=1)
    x2, idx, gates, rank, carry = _mo_ln2_route(x1, o_all, w_mo_ref, g2_ref, b2_ref, w_r_ref, b_r_ref,
                                                carry_ref[...])
    _store_row_tiles(x2_ref, (), x2)
    idx_ref[...] = idx
    gate_ref[...] = gates
    rank_ref[...] = rank
    carry_ref[...] = carry
    cnt_ref[...] = carry


def _post_weight_specs():
    half = (POOL_WIDTH, D_MODEL)
    return [
        _full(half), _full(half), _full((1, D_MODEL)), _full((1, D_MODEL)),
        _full((D_MODEL, MEM_WIDTH)), _full((MEM_WIDTH, D_MODEL)), _full((1, D_MODEL)), _full((1, D_MODEL)),
        _full((D_MODEL, N_EXPERTS)), _full((1, N_EXPERTS)),
    ]


def _post_prompt(x2d, pool_y, mla_y, mk_b, mv_b, weights, n_batch, seq, n_mem, n_all):
    tm = TOKEN_BLOCK
    nj = seq // tm
    tok = lambda b, j: (b * nj + j, 0)
    return pl.pallas_call(
        _post_prompt_kernel,
        grid=(n_batch, nj),
        in_specs=[
            pl.BlockSpec((tm, D_MODEL), tok), pl.BlockSpec((tm, POOL_WIDTH), tok), pl.BlockSpec((tm, POOL_WIDTH), tok),
            pl.BlockSpec((n_mem, MEM_WIDTH), lambda b, j: (b, 0)), pl.BlockSpec((n_mem, MEM_WIDTH), lambda b, j: (b, 0)),
        ] + _post_weight_specs(),
        out_specs=[
            pl.BlockSpec((tm * ROW_TILE, LANES), tok), pl.BlockSpec((tm, TOP_K), tok), pl.BlockSpec((tm, TOP_K), tok),
            pl.BlockSpec((tm, TOP_K), tok), _full((1, N_EXPERTS)),
        ],
        out_shape=[
            jax.ShapeDtypeStruct((n_all * ROW_TILE, LANES), F32), jax.ShapeDtypeStruct((n_all, TOP_K), I32),
            jax.ShapeDtypeStruct((n_all, TOP_K), F32), jax.ShapeDtypeStruct((n_all, TOP_K), I32),
            jax.ShapeDtypeStruct((1, N_EXPERTS), F32),
        ],
        scratch_shapes=[pltpu.VMEM((1, N_EXPERTS), F32)],
        compiler_params=_params("arbitrary", "arbitrary"),
        name="post_prompt",
    )(x2d, pool_y, mla_y, mk_b, mv_b, *weights)


def _mix_sample_kernel(x_ref, pooly_ref, mlay_ref, w_out_a_ref, w_out_b_ref, g1_ref, b1_ref, w_mq_ref,
                       x1_ref, qm_ref):
    x1 = _mix_ln1(x_ref[...], pooly_ref[...], mlay_ref[...], w_out_a_ref, w_out_b_ref, g1_ref, b1_ref)
    x1_ref[...] = x1
    qm_ref[...] = (_dot(x1.astype(BF16), w_mq_ref[...]) * MEM_SCALE).astype(BF16)


def _mix_sample(x2d, pool_y, mla_y, weights):
    n = x2d.shape[0]
    w_out_a, w_out_b, g1, b1, w_mq = weights[:5]
    return pl.pallas_call(
        _mix_sample_kernel,
        grid=(1,),
        in_specs=[_full((n, D_MODEL)), _full((n, POOL_WIDTH)), _full((n, POOL_WIDTH))] + _post_weight_specs()[:5],
        out_specs=[_full((n, D_MODEL)), _full((n, MEM_WIDTH))],
        out_shape=[jax.ShapeDtypeStruct((n, D_MODEL), F32), jax.ShapeDtypeStruct((n, MEM_WIDTH), BF16)],
        compiler_params=_params("arbitrary"),
        name="mix_sample",
    )(x2d, pool_y, mla_y, w_out_a, w_out_b, g1, b1, w_mq)


def _mem_attn_sample_kernel(q_ref, k_ref, v_ref, out_ref, res_ref, *, n_new):
    for b in range(q_ref.shape[0]):
        s = _dot_nt(q_ref[b], k_ref[b].astype(BF16))
        p = jnp.exp(s - jnp.max(s, axis=1, keepdims=True))
        res_ref[...] = _dot(p.astype(BF16), v_ref[b].astype(BF16)) / jnp.sum(p, axis=1, keepdims=True)
        for hd in range(MEM_HEADS):
            cols = slice(MEM_HEAD_DIM * hd, MEM_HEAD_DIM * (hd + 1))
            out_ref[b, :, cols] = res_ref[hd * n_new:(hd + 1) * n_new, cols].astype(BF16)


def _mem_attn_sample(q_bd, mem_k, mem_v, n_new):
    n_b, rows, _ = q_bd.shape
    n_mem = mem_k.shape[1]
    gb = min(MEM_BATCH_BLOCK, n_b)
    assert n_b % gb == 0
    blk = lambda i: (i, 0, 0)
    return pl.pallas_call(
        functools.partial(_mem_attn_sample_kernel, n_new=n_new),
        grid=(n_b // gb,),
        in_specs=[pl.BlockSpec((gb, rows, MEM_WIDTH), blk), pl.BlockSpec((gb, n_mem, MEM_WIDTH), blk),
                  pl.BlockSpec((gb, n_mem, MEM_WIDTH), blk)],
        out_specs=pl.BlockSpec((gb, n_new, MEM_WIDTH), blk),
        out_shape=jax.ShapeDtypeStruct((n_b, n_new, MEM_WIDTH), BF16),
        scratch_shapes=[pltpu.VMEM((rows, MEM_WIDTH), F32)],
        compiler_params=_params("arbitrary"),
        name="mem_attn_sample",
    )(q_bd, mem_k, mem_v)


def _route_sample_kernel(x1_ref, o_ref, cnt_in_ref, w_mo_ref, g2_ref, b2_ref, w_r_ref, b_r_ref,
                         x2_in, idx_in, gate_in, rank_in,
                         x2_ref, idx_ref, gate_ref, rank_ref, cnt_ref):
    del x2_in, idx_in, gate_in, rank_in
    x2, idx, gates, rank, carry = _mo_ln2_route(x1_ref[...], o_ref[...], w_mo_ref, g2_ref, b2_ref, w_r_ref, b_r_ref,
                                                cnt_in_ref[...])
    _store_row_tiles(x2_ref, (), x2)
    idx_ref[...] = idx
    gate_ref[...] = gates
    rank_ref[...] = rank
    cnt_ref[...] = carry


def _route_sample(x1, o_bf, cnt_in, weights, x2_all, idx_all, gate_all, rank_all, n_prompt):
    n = x1.shape[0]
    assert n_prompt % n == 0
    tail = lambda i: (n_prompt // n, 0)
    anyspec = pl.BlockSpec(memory_space=pl.ANY)
    w_mo, g2, b2, w_r, b_r = weights[5:]
    return pl.pallas_call(
        _route_sample_kernel,
        grid=(1,),
        in_specs=[_full((n, D_MODEL)), _full((n, MEM_WIDTH)), _full((1, N_EXPERTS))] + _post_weight_specs()[5:] + [
            anyspec, anyspec, anyspec, anyspec],
        out_specs=[pl.BlockSpec((n * ROW_TILE, LANES), tail), pl.BlockSpec((n, TOP_K), tail), pl.BlockSpec((n, TOP_K), tail),
                   pl.BlockSpec((n, TOP_K), tail), _full((1, N_EXPERTS))],
        out_shape=[jax.ShapeDtypeStruct(x2_all.shape, F32), jax.ShapeDtypeStruct(idx_all.shape, I32),
                   jax.ShapeDtypeStruct(gate_all.shape, F32), jax.ShapeDtypeStruct(rank_all.shape, I32),
                   jax.ShapeDtypeStruct((1, N_EXPERTS), F32)],
        input_output_aliases={8: 0, 9: 1, 10: 2, 11: 3},
        compiler_params=_params("arbitrary"),
        name="route_sample",
    )(x1, o_bf, cnt_in, w_mo, g2, b2, w_r, b_r, x2_all, idx_all, gate_all, rank_all)


def _moe_kernel(blk_exp_ref, idx_hbm, x_hbm, wgu_ref, bgu_ref, wd_ref, bd_ref, y_hbm,
                idx_smem, xbuf, ybuf, idx_sem, x_sem, y_sem):
    del blk_exp_ref
    i = pl.program_id(0)
    last = pl.num_programs(0) - 1
    tr = xbuf.shape[1] // ROW_TILE
    n_chunks = D_EXPERT // EXPERT_CHUNK
    rows_per_chunk = tr // n_chunks

    def idx_copy(row, s):
        return pltpu.make_async_copy(idx_hbm.at[row], idx_smem.at[s], idx_sem.at[s])

    def gather(s_idx, xs, rows):
        for r in rows:
            src = pl.multiple_of(idx_smem[s_idx, tr + r], ROW_TILE)
            pltpu.make_async_copy(x_hbm.at[pl.ds(src, ROW_TILE)], xbuf.at[xs, pl.ds(ROW_TILE * r, ROW_TILE)],
                                  x_sem.at[xs]).start()

    def scatter(s_idx, ys, rows):
        for r in rows:
            dst = pl.multiple_of(idx_smem[s_idx, r], ROW_TILE)
            pltpu.make_async_copy(ybuf.at[ys, pl.ds(ROW_TILE * r, ROW_TILE)], y_hbm.at[pl.ds(dst, ROW_TILE)],
                                  y_sem.at[ys]).start()

    def wait_rows(buf, sem, s):
        pltpu.make_async_copy(buf.at[s], buf.at[s], sem.at[s]).wait()

    @pl.when(i == 0)
    def _():
        idx_copy(0, 1).start()
        idx_copy(0, 1).wait()
        gather(1, 0, range(tr))
        idx_copy(1, 0).start()
        ybuf[1] = jnp.zeros(ybuf.shape[1:], F32)

    def step(slot):
        nxt = 1 - slot

        @pl.when(i >= 1)
        def _():
            wait_rows(ybuf, y_sem, slot)

        idx_copy(i + 1, slot).wait()
        idx_copy(i + 2, nxt).start()
        wait_rows(xbuf, x_sem, slot)
        xb = _load_row_tiles(xbuf, (slot,), tr).astype(BF16)
        acc = jnp.zeros((tr, D_MODEL), F32) + bd_ref[...]
        for c in range(n_chunks):
            cg = slice(c * EXPERT_CHUNK, (c + 1) * EXPERT_CHUNK)
            cu = slice(D_EXPERT + c * EXPERT_CHUNK, D_EXPERT + (c + 1) * EXPERT_CHUNK)
            gate = jnp.minimum(_dot(xb, wgu_ref[:, cg]) + bgu_ref[:, cg], SWIGLU_LIMIT)
            up = jnp.clip(_dot(xb, wgu_ref[:, cu]) + bgu_ref[:, cu], -SWIGLU_LIMIT, SWIGLU_LIMIT)
            act = gate * (1.0 / (1.0 + jnp.exp(-SWIGLU_ALPHA * gate))) * (up + 1.0)
            acc = acc + _dot(act.astype(BF16), wd_ref[cg, :])
            rows = range(c * rows_per_chunk, (c + 1) * rows_per_chunk)
            gather(slot, nxt, rows)
            scatter(slot, nxt, rows)
        _store_row_tiles(ybuf, (slot,), acc)

        @pl.when(i == last)
        def _():
            idx_copy(i + 2, nxt).wait()
            wait_rows(xbuf, x_sem, nxt)
            wait_rows(ybuf, y_sem, nxt)

    for parity in range(2):
        pl.when(i % 2 == parity)(functools.partial(step, parity))


def _moe(blk_exp, idx_rows, x_all, w_gu, b_gu, w_d, b_d, n_assign):
    tr = EXPERT_ROWS
    n_blocks = blk_exp.shape[0]
    assert n_blocks >= 2 and idx_rows.shape == (n_blocks + 2, 2 * tr) and tr % (D_EXPERT // EXPERT_CHUNK) == 0
    anyspec = pl.BlockSpec(memory_space=pl.ANY)
    emap3 = lambda i, be: (be[i], 0, 0)
    grid_spec = pltpu.PrefetchScalarGridSpec(
        num_scalar_prefetch=1,
        grid=(n_blocks,),
        in_specs=[
            anyspec, anyspec,
            pl.BlockSpec((None, D_MODEL, 2 * D_EXPERT), emap3), pl.BlockSpec((None, 1, 2 * D_EXPERT), emap3),
            pl.BlockSpec((None, D_EXPERT, D_MODEL), emap3), pl.BlockSpec((None, 1, D_MODEL), emap3),
        ],
        out_specs=anyspec,
        scratch_shapes=[
            pltpu.SMEM((2, 2 * tr), I32),
            pltpu.VMEM((2, tr * ROW_TILE, LANES), F32), pltpu.VMEM((2, tr * ROW_TILE, LANES), F32),
            pltpu.SemaphoreType.DMA((2,)), pltpu.SemaphoreType.DMA((2,)), pltpu.SemaphoreType.DMA((2,)),
        ],
    )
    return pl.pallas_call(
        _moe_kernel,
        grid_spec=grid_spec,
        out_shape=jax.ShapeDtypeStruct(((n_assign + 2 * tr) * ROW_TILE, LANES), F32),
        compiler_params=_params("arbitrary"),
        name="moe_experts",
    )(blk_exp, idx_rows, x_all, w_gu, b_gu, w_d, b_d)


def _combine_kernel(x2_ref, gate_ref, g3_ref, b3_ref, *rest):
    y_refs, out_ref = rest[:TOP_K], rest[TOP_K]
    gates = gate_ref[...]
    tm = out_ref.shape[0]
    moe = jnp.zeros(out_ref.shape, F32)
    for k in range(TOP_K):
        moe = moe + gates[:, k:k + 1] * _load_row_tiles(y_refs[k], (), tm)
    out_ref[...] = _layer_norm(DEEPNORM_ALPHA * _load_row_tiles(x2_ref, (), tm) + moe, g3_ref[...], b3_ref[...])


def _combine(x2_all, y_tok, gates_all, g3, b3, row0, n_rows, tm):
    n_all = gates_all.shape[0]
    assert row0 % tm == 0 and n_rows % tm == 0 and n_all % tm == 0
    off = row0 // tm
    tok = lambda i: (off + i, 0)
    tiles = (tm * ROW_TILE, LANES)
    slot_specs = [pl.BlockSpec(tiles, functools.partial(lambda i, k: (k * (n_all // tm) + off + i, 0), k=k))
                  for k in range(TOP_K)]
    return pl.pallas_call(
        _combine_kernel,
        grid=(n_rows // tm,),
        in_specs=[pl.BlockSpec(tiles, tok), pl.BlockSpec((tm, TOP_K), tok), _full((1, D_MODEL)),
                  _full((1, D_MODEL))] + slot_specs,
        out_specs=pl.BlockSpec((tm, D_MODEL), lambda i: (i, 0)),
        out_shape=jax.ShapeDtypeStruct((n_rows, D_MODEL), F32),
        compiler_params=_params("arbitrary"),
        name="combine_ln3",
    )(x2_all, gates_all, g3, b3, *([y_tok] * TOP_K))


def _rope_tables(pos):
    inv_freq = ROPE_THETA ** (-jnp.arange(ROPE_HALF, dtype=F32) / ROPE_HALF)
    ang = pos.astype(F32)[:, None] * inv_freq[None, :]
    c, s, z = jnp.cos(ang), jnp.sin(ang), jnp.zeros_like(ang)
    return jnp.concatenate([c, z, c, z], axis=1), jnp.concatenate([-s, z, s, z], axis=1)


def _pad_rope_cols(w):
    z = jnp.zeros(w.shape[:-1] + (ROPE_HALF,), w.dtype)
    return jnp.concatenate([w[..., :ROPE_HALF], z, w[..., ROPE_HALF:], z], axis=-1)


def _layer_weights(l, w_in, w_pool, pool_scale, q_norm_g, w_uq, kv_norm_g, w_uk, w_uv, w_out, ln1_g, ln1_b,
                   w_mq, w_mo, ln2_g, ln2_b, w_router, b_router):
    o3 = POOL_WIDTH + Q_RANK + KV_RANK
    w_in_p = jnp.concatenate([w_in[l][:, :o3], _pad_rope_cols(w_in[l][:, o3:])], axis=1).astype(BF16)
    uq = w_uq[l].reshape(Q_RANK, MLA_HEADS, QK_NOPE_DIM + QK_ROPE_DIM)
    uq_p = jnp.concatenate([uq[..., :QK_NOPE_DIM], _pad_rope_cols(uq[..., QK_NOPE_DIM:])], axis=-1)
    uq_p = uq_p.reshape(Q_RANK, MLA_HEADS * QK_PAD).astype(BF16)
    w_ukt = jnp.transpose(w_uk[l], (1, 2, 0)).astype(BF16)
    w_uv_h = jnp.transpose(w_uv[l], (1, 0, 2)).astype(BF16)
    row = lambda v: v[l].reshape(1, -1)
    proj = [w_in_p, row(q_norm_g), uq_p, row(kv_norm_g), w_ukt, w_pool[l].astype(BF16), row(pool_scale)]
    post = [w_out[l][:POOL_WIDTH].astype(BF16), w_out[l][POOL_WIDTH:].astype(BF16), row(ln1_g), row(ln1_b),
            w_mq[l].astype(BF16), w_mo[l].astype(BF16), row(ln2_g), row(ln2_b), w_router[l], row(b_router)]
    return proj, w_uv_h, post


def _routing_tables(idx_all, rank_all, counts, n_blocks):
    tr = EXPERT_ROWS
    cnt = counts.reshape(-1).astype(I32)
    padded = ((cnt + tr - 1) // tr) * tr
    pend = jnp.cumsum(padded)
    pstart = pend - padded
    dest = (pstart[idx_all] + rank_all).reshape(-1)
    n_all = idx_all.shape[0]
    m = dest.shape[0]
    n_rows = n_blocks * tr
    assign = (jnp.arange(TOP_K, dtype=I32)[None, :] * n_all + jnp.arange(n_all, dtype=I32)[:, None]).reshape(-1)
    row_src = jnp.full((n_rows,), -1, I32).at[dest].set(assign, unique_indices=True)
    spare = m + jnp.arange(n_rows, dtype=I32) % (2 * tr)
    dst_rows = (jnp.where(row_src >= 0, row_src, spare) * ROW_TILE).reshape(n_blocks, tr)
    src_tok = ((jnp.maximum(row_src, 0) % n_all) * ROW_TILE).reshape(n_blocks, tr)
    spare_row = spare[:tr][None] * ROW_TILE
    idx_rows = jnp.concatenate([jnp.concatenate([spare_row, spare_row, dst_rows, spare_row], axis=0),
                                jnp.concatenate([src_tok, jnp.zeros((3, tr), I32)], axis=0)], axis=1)
    n_used = pend[-1] // tr
    blk = jnp.minimum(jnp.arange(n_blocks + 1, dtype=I32), n_used - 1)
    blk_exp = jnp.sum((pend[None, :] <= (blk * tr)[:, None]).astype(I32), axis=1)
    return jnp.clip(blk_exp, 0, N_EXPERTS - 1), idx_rows


def kernel(x_prompt, x_sample, mem_prompt, cache_kv_latent, cache_k_rope, cache_mem_k, cache_mem_v, state_pool,
           page_table, w_in, w_pool, pool_scale, q_norm_g, w_uq, kv_norm_g, w_uk, w_uv, w_out, ln1_g, ln1_b,
           w_mq, w_mk, w_mv, w_mo, ln2_g, ln2_b, w_router, b_router, w_gate_up, b_gate_up, w_down, b_down,
           ln3_g, ln3_b):
    n_bp, seq, _ = x_prompt.shape
    n_bs, n_new, _ = x_sample.shape
    n_mem = mem_prompt.shape[1]
    page = cache_kv_latent.shape[2]
    past_len = page_table.shape[1] * page
    n_p = n_bp * seq
    n_s = n_bs * n_new
    n_all = n_p + n_s
    n_assign = n_all * TOP_K
    assert w_in.shape[0] == DEPTH == 1
    assert seq % TOKEN_BLOCK == 0 and seq % ATTN_K_BLOCK == 0 and seq % ATTN_Q_BLOCK == 0 and seq >= POOL_HALO
    assert n_p % n_s == 0 and (n_s % TOKEN_BLOCK == 0 or TOKEN_BLOCK % n_s == 0)
    l = 0

    proj_w, w_uv_h, post_w = _layer_weights(l, w_in, w_pool, pool_scale, q_norm_g, w_uq, kv_norm_g, w_uk, w_uv,
                                            w_out, ln1_g, ln1_b, w_mq, w_mo, ln2_g, ln2_b, w_router, b_router)

    xp = x_prompt.reshape(n_p, D_MODEL)
    cos_p, sin_p = _rope_tables(jnp.arange(seq))
    qcat, kcat, lat_p, kr_p, pooly_p, pstate_p = _proj_prompt(xp, proj_w, cos_p, sin_p, n_bp, seq)
    mlay_p = _attn_prompt(qcat, kcat, w_uv_h, n_bp, seq)
    mk, mv, mk_b, mv_b = _mem_kv(mem_prompt.reshape(n_bp * n_mem, D_MODEL), w_mk[l].astype(BF16),
                                 w_mv[l].astype(BF16), n_bp, n_mem)
    x2_all, idx_all, gate_all, rank_all, cnt_p = _post_prompt(xp, pooly_p, mlay_p, mk_b, mv_b, post_w,
                                                              n_bp, seq, n_mem, n_all)

    xs = jnp.transpose(x_sample, (1, 0, 2)).reshape(n_s, D_MODEL)
    state_t = jnp.transpose(state_pool[l], (1, 0, 2))
    cos_s, sin_s = _rope_tables(jnp.repeat(past_len + jnp.arange(n_new), n_bs))
    qcat_s, kcat_s, lat_s, kr_s, pooly_s, pstate_s = _proj_sample(xs, state_t, proj_w, cos_s, sin_s, past_len)
    q_b = jnp.transpose(qcat_s.reshape(MLA_HEADS, n_new, n_bs, QK_PAD), (2, 0, 1, 3)).reshape(
        n_bs, MLA_HEADS * n_new, QK_PAD)
    knew_b = jnp.transpose(kcat_s.reshape(n_new, n_bs, QK_PAD), (1, 0, 2))
    knew_b = jnp.pad(knew_b, ((0, 0), (0, LANES - n_new), (0, 0)))
    mlay_b = _attn_sample(page_table, q_b, knew_b, w_uv_h, cache_kv_latent,
                          jnp.transpose(cache_k_rope, (0, 1, 3, 2)), l)
    mlay_s = jnp.transpose(mlay_b, (1, 0, 2)).reshape(n_s, POOL_WIDTH).astype(BF16)
    x1_s, qm_s = _mix_sample(xs, pooly_s, mlay_s, post_w)
    qm_b = jnp.transpose(qm_s.reshape(n_new, n_bs, MEM_WIDTH), (1, 0, 2))
    head_of_col = jnp.arange(MEM_WIDTH) // MEM_HEAD_DIM
    head_mask = (head_of_col[None, :] == jnp.arange(MEM_HEADS)[:, None]).astype(BF16)
    q_bd = (qm_b[:, None, :, :] * head_mask[None, :, None, :]).reshape(n_bs, MEM_HEADS * n_new, MEM_WIDTH)
    o_b = _mem_attn_sample(q_bd, cache_mem_k[l].reshape(n_bs, n_mem, MEM_WIDTH),
                           cache_mem_v[l].reshape(n_bs, n_mem, MEM_WIDTH), n_new)
    o_s = jnp.transpose(o_b, (1, 0, 2)).reshape(n_s, MEM_WIDTH)
    x2_all, idx_all, gate_all, rank_all, counts = _route_sample(x1_s, o_s, cnt_p, post_w, x2_all, idx_all, gate_all,
                                                               rank_all, n_p)

    n_blocks = -(-n_assign // EXPERT_ROWS) + N_EXPERTS + 1
    blk_exp, idx_rows = _routing_tables(idx_all, rank_all, counts, n_blocks)
    y_tok = _moe(blk_exp, idx_rows, x2_all, w_gate_up[l].astype(BF16), b_gate_up[l][:, None, :],
                 w_down[l].astype(BF16), b_down[l][:, None, :], n_assign)
    g3, b3 = ln3_g[l].reshape(1, -1), ln3_b[l].reshape(1, -1)
    y_p = _combine(x2_all, y_tok, gate_all, g3, b3, 0, n_p, TOKEN_BLOCK)
    y_s = _combine(x2_all, y_tok, gate_all, g3, b3, n_p, n_s, min(TOKEN_BLOCK, n_s))

    def from_steps(a, width):
        return jnp.transpose(a.reshape(n_new, n_bs, width), (1, 0, 2))[None]

    return (
        y_p.reshape(n_bp, seq, D_MODEL),
        from_steps(y_s, D_MODEL)[0],
        lat_p.reshape(1, n_bp, seq, KV_RANK),
        kr_p.reshape(1, n_bp, seq, QK_ROPE_DIM),
        pstate_p[None, :, POOL_HALO - POOL_BUF:, :],
        mk.reshape(1, n_bp, n_mem, MEM_HEADS, MEM_HEAD_DIM),
        mv.reshape(1, n_bp, n_mem, MEM_HEADS, MEM_HEAD_DIM),
        from_steps(lat_s, KV_RANK),
        from_steps(kr_s, QK_ROPE_DIM),
        jnp.transpose(pstate_s, (1, 0, 2))[None],
    )
```

```python
import functools

import jax
import jax.numpy as jnp
from jax import lax
from jax.experimental import pallas as pl
from jax.experimental.pallas import tpu as pltpu

F32 = jnp.float32
BF16 = jnp.bfloat16
I32 = jnp.int32

D_MODEL = 1024
POOL_WIDTH = 512
POOL_WINDOWS = (2, 4, 8, 16)
POOL_GROUP_DIM = 128
POOL_BUF = 15
POOL_HALO = 16
MLA_HEADS = 4
QK_NOPE_DIM = 128
QK_ROPE_DIM = 64
ROPE_HALF = 32
V_HEAD_DIM = 128
Q_RANK = 256
KV_RANK = 128
ROPE_THETA = 10000.0
QK_PAD = 256
ONES_LANE = QK_PAD - 1
MEM_HEADS = 4
MEM_HEAD_DIM = 128
MEM_WIDTH = 512
N_EXPERTS = 32
TOP_K = 4
D_EXPERT = 1024
SWIGLU_LIMIT = 7.0
SWIGLU_ALPHA = 1.702
LN_EPS = 1e-5
RMS_EPS = 1e-6
DEPTH = 1
DEEPNORM_ALPHA = (2.0 * DEPTH) ** 0.25
ATTN_SCALE = (QK_NOPE_DIM + QK_ROPE_DIM) ** -0.5
MEM_SCALE = MEM_HEAD_DIM ** -0.5

LANES = 128
VMEM_LIMIT = 48 * 1024 * 1024

TOKEN_BLOCK = 256
ATTN_Q_BLOCK = 256
ATTN_K_BLOCK = 512
ATTN_GROUPS = 2
PAGES_PER_STEP = 32
MEM_BATCH_BLOCK = 8
EXPERT_ROWS = 256
EXPERT_CHUNK = 256


def _dot(a, b):
    return jnp.dot(a, b, preferred_element_type=F32)


def _dot_nt(a, b):
    return lax.dot_general(a, b, (((1,), (1,)), ((), ())), preferred_element_type=F32)


def _rms(x, g):
    return x * lax.rsqrt(jnp.mean(x * x, axis=-1, keepdims=True) + RMS_EPS) * g


def _layer_norm(x, g, b):
    mu = jnp.mean(x, axis=-1, keepdims=True)
    xc = x - mu
    var = jnp.mean(xc * xc, axis=-1, keepdims=True)
    return xc * lax.rsqrt(var + LN_EPS) * g + b


ROW_TILE = D_MODEL // LANES


def _store_row_tiles(ref, lead, val):
    m = val.shape[0]
    for j in range(ROW_TILE):
        ref[(*lead, pl.ds(j, m, stride=ROW_TILE), slice(None))] = val[:, LANES * j:LANES * (j + 1)]


def _load_row_tiles(ref, lead, m):
    return jnp.concatenate([ref[(*lead, pl.ds(j, m, stride=ROW_TILE), slice(None))] for j in range(ROW_TILE)], axis=1)


def _params(*semantics):
    return pltpu.CompilerParams(dimension_semantics=semantics, vmem_limit_bytes=VMEM_LIMIT)


def _full(shape):
    n = len(shape)
    return pl.BlockSpec(shape, lambda *_: (0,) * n)


def _project(x_bf, w_in_ref, qg_ref, w_uq_ref, kvg_ref, w_ukt_ref, cos, sin):
    h = _dot(x_bf, w_in_ref[...])
    u = h[:, :POOL_WIDTH]
    qn = _rms(h[:, POOL_WIDTH:POOL_WIDTH + Q_RANK], qg_ref[...])
    q = _dot(qn.astype(BF16), w_uq_ref[...])
    q_lat, q_rope = [], []
    for hd in range(MLA_HEADS):
        nope = q[:, QK_PAD * hd:QK_PAD * hd + QK_NOPE_DIM]
        rp = q[:, QK_PAD * hd + QK_NOPE_DIM:QK_PAD * (hd + 1)]
        q_lat.append(_dot(nope.astype(BF16), w_ukt_ref[hd]) * ATTN_SCALE)
        q_rope.append((rp * cos + pltpu.roll(rp, 64, 1) * sin) * ATTN_SCALE)
    o2 = POOL_WIDTH + Q_RANK
    lat = _rms(h[:, o2:o2 + KV_RANK], kvg_ref[...])
    kp = h[:, o2 + KV_RANK:]
    k_rope = kp * cos + pltpu.roll(kp, 64, 1) * sin
    return u, q_lat, q_rope, lat, k_rope


def _compact_rope(r):
    lane = lax.broadcasted_iota(I32, r.shape, 1)
    moved = pltpu.roll(r, 96, 1)
    return jnp.where(lane < ROPE_HALF, r, jnp.where(lane < QK_ROPE_DIM, moved, 0.0))


def _with_ones_lane(r):
    lane = lax.broadcasted_iota(I32, r.shape, 1)
    return jnp.where(lane == ONES_LANE - KV_RANK, 1.0, r)


def _pool_group_out(diff, g, w_pool_ref, pscale_ref):
    cols = slice(POOL_GROUP_DIM * g, POOL_GROUP_DIM * (g + 1))
    return _dot(diff.astype(BF16), w_pool_ref[g]) * pscale_ref[:, cols]


def _proj_prompt_kernel(x_ref, w_in_ref, qg_ref, w_uq_ref, kvg_ref, w_ukt_ref, w_pool_ref, pscale_ref,
                        cos_ref, sin_ref,
                        qcat_ref, kcat_ref, lat_ref, kr_ref, pooly_ref, pstate_ref, ubuf):
    j = pl.program_id(1)
    tm = x_ref.shape[0]

    @pl.when(j == 0)
    def _():
        ubuf[0:POOL_HALO, :] = jnp.zeros((POOL_HALO, POOL_WIDTH), F32)

    u, q_lat, q_rope, lat, k_rope = _project(
        x_ref[...].astype(BF16), w_in_ref, qg_ref, w_uq_ref, kvg_ref, w_ukt_ref, cos_ref[...], sin_ref[...])
    for hd in range(MLA_HEADS):
        qcat_ref[hd, :, 0:KV_RANK] = q_lat[hd].astype(BF16)
        qcat_ref[hd, :, KV_RANK:QK_PAD] = q_rope[hd].astype(BF16)
    lat_ref[...] = lat
    kcat_ref[:, 0:KV_RANK] = lat.astype(BF16)
    kcat_ref[:, KV_RANK:QK_PAD] = _with_ones_lane(k_rope).astype(BF16)
    kr_ref[...] = _compact_rope(k_rope)[:, :QK_ROPE_DIM]

    ubuf[POOL_HALO:POOL_HALO + tm, :] = u
    pos = j * tm + lax.broadcasted_iota(I32, (tm, 1), 0)
    for g, w in enumerate(POOL_WINDOWS):
        cols = slice(POOL_GROUP_DIM * g, POOL_GROUP_DIM * (g + 1))
        ug = u[:, cols]
        ssum = ug
        for k in range(1, w):
            ssum = ssum + ubuf[POOL_HALO - k:POOL_HALO - k + tm, cols]
        count = jnp.minimum(w, pos + 1).astype(F32)
        pooly_ref[:, cols] = _pool_group_out(ssum / count - ug, g, w_pool_ref, pscale_ref).astype(BF16)
    tail = ubuf[tm:tm + POOL_HALO, :]
    ubuf[0:POOL_HALO, :] = tail

    @pl.when(j == pl.num_programs(1) - 1)
    def _():
        pstate_ref[...] = tail


def _proj_sample_kernel(x_ref, state_ref, w_in_ref, qg_ref, w_uq_ref, kvg_ref, w_ukt_ref, w_pool_ref, pscale_ref,
                        cos_ref, sin_ref,
                        qcat_ref, kcat_ref, lat_ref, kr_ref, pooly_ref, pstate_ref, *, past_len):
    n_b = state_ref.shape[1]
    n_t = x_ref.shape[0] // n_b
    u, q_lat, q_rope, lat, k_rope = _project(
        x_ref[...].astype(BF16), w_in_ref, qg_ref, w_uq_ref, kvg_ref, w_ukt_ref, cos_ref[...], sin_ref[...])
    for hd in range(MLA_HEADS):
        qcat_ref[hd, :, 0:KV_RANK] = q_lat[hd].astype(BF16)
        qcat_ref[hd, :, KV_RANK:QK_PAD] = _compact_rope(q_rope[hd]).astype(BF16)
    lat_ref[...] = lat
    kc = _compact_rope(k_rope)
    kcat_ref[:, 0:KV_RANK] = lat.astype(BF16)
    kcat_ref[:, KV_RANK:QK_PAD] = _with_ones_lane(kc).astype(BF16)
    kr_ref[...] = kc[:, :QK_ROPE_DIM]

    def ext(jj):
        if jj < POOL_BUF:
            return state_ref[jj]
        return u[(jj - POOL_BUF) * n_b:(jj - POOL_BUF + 1) * n_b, :]

    for t in range(n_t):
        ut = ext(POOL_BUF + t)
        for g, w in enumerate(POOL_WINDOWS):
            cols = slice(POOL_GROUP_DIM * g, POOL_GROUP_DIM * (g + 1))
            ssum = ut[:, cols]
            for k in range(1, w):
                ssum = ssum + ext(POOL_BUF + t - k)[:, cols]
            count = float(min(w, past_len + t + 1))
            y = _pool_group_out(ssum / count - ut[:, cols], g, w_pool_ref, pscale_ref)
            pooly_ref[t * n_b:(t + 1) * n_b, cols] = y.astype(BF16)
    for jj in range(POOL_BUF):
        pstate_ref[jj] = ext(n_t + jj)


def _proj_weight_specs():
    return [
        _full((D_MODEL, D_MODEL)),
        _full((1, Q_RANK)),
        _full((Q_RANK, MLA_HEADS * QK_PAD)),
        _full((1, KV_RANK)),
        _full((MLA_HEADS, QK_NOPE_DIM, KV_RANK)),
        _full((len(POOL_WINDOWS), POOL_GROUP_DIM, POOL_GROUP_DIM)),
        _full((1, POOL_WIDTH)),
    ]


def _proj_prompt(x2d, weights, cos_t, sin_t, n_batch, seq):
    tm = TOKEN_BLOCK
    n = n_batch * seq
    nj = seq // tm
    tok = lambda b, j: (b * nj + j, 0)
    return pl.pallas_call(
        _proj_prompt_kernel,
        grid=(n_batch, nj),
        in_specs=[pl.BlockSpec((tm, D_MODEL), tok)] + _proj_weight_specs() + [
            pl.BlockSpec((tm, LANES), lambda b, j: (j, 0)),
            pl.BlockSpec((tm, LANES), lambda b, j: (j, 0)),
        ],
        out_specs=[
            pl.BlockSpec((MLA_HEADS, tm, QK_PAD), lambda b, j: (0, b * nj + j, 0)),
            pl.BlockSpec((tm, QK_PAD), tok),
            pl.BlockSpec((tm, KV_RANK), tok),
            pl.BlockSpec((tm, QK_ROPE_DIM), tok),
            pl.BlockSpec((tm, POOL_WIDTH), tok),
            pl.BlockSpec((None, POOL_HALO, POOL_WIDTH), lambda b, j: (b, 0, 0)),
        ],
        out_shape=[
            jax.ShapeDtypeStruct((MLA_HEADS, n, QK_PAD), BF16),
            jax.ShapeDtypeStruct((n, QK_PAD), BF16),
            jax.ShapeDtypeStruct((n, KV_RANK), F32),
            jax.ShapeDtypeStruct((n, QK_ROPE_DIM), F32),
            jax.ShapeDtypeStruct((n, POOL_WIDTH), BF16),
            jax.ShapeDtypeStruct((n_batch, POOL_HALO, POOL_WIDTH), F32),
        ],
        scratch_shapes=[pltpu.VMEM((POOL_HALO + tm, POOL_WIDTH), F32)],
        compiler_params=_params("arbitrary", "arbitrary"),
        name="proj_prompt",
    )(x2d, *weights, cos_t, sin_t)


def _proj_sample(x2d, state_t, weights, cos_t, sin_t, past_len):
    n = x2d.shape[0]
    n_b = state_t.shape[1]
    return pl.pallas_call(
        functools.partial(_proj_sample_kernel, past_len=past_len),
        grid=(1,),
        in_specs=[_full((n, D_MODEL)), _full((POOL_BUF, n_b, POOL_WIDTH))] + _proj_weight_specs() + [
            _full((n, LANES)), _full((n, LANES))],
        out_specs=[
            _full((MLA_HEADS, n, QK_PAD)), _full((n, QK_PAD)), _full((n, KV_RANK)), _full((n, QK_ROPE_DIM)),
            _full((n, POOL_WIDTH)), _full((POOL_BUF, n_b, POOL_WIDTH)),
        ],
        out_shape=[
            jax.ShapeDtypeStruct((MLA_HEADS, n, QK_PAD), BF16),
            jax.ShapeDtypeStruct((n, QK_PAD), BF16),
            jax.ShapeDtypeStruct((n, KV_RANK), F32),
            jax.ShapeDtypeStruct((n, QK_ROPE_DIM), F32),
            jax.ShapeDtypeStruct((n, POOL_WIDTH), BF16),
            jax.ShapeDtypeStruct((POOL_BUF, n_b, POOL_WIDTH), F32),
        ],
        compiler_params=_params("arbitrary"),
        name="proj_sample",
    )(x2d, state_t, *weights, cos_t, sin_t)


def _attn_block(q, k, mask, m_ref, acc_ref):
    s = _dot_nt(q, k)
    if mask is not None:
        s = jnp.where(mask, s, -jnp.inf)
    tiles = [s[:, c * LANES:(c + 1) * LANES] for c in range(s.shape[1] // LANES)]
    m_old = m_ref[...]
    m_new = jnp.maximum(m_old, jnp.max(functools.reduce(jnp.maximum, tiles), axis=1, keepdims=True))
    alpha = jnp.exp(m_old - m_new)
    p = jnp.concatenate([jnp.exp(t - m_new) for t in tiles], axis=1).astype(BF16)
    acc_ref[...] = jnp.concatenate([alpha, alpha], axis=1) * acc_ref[...] + _dot(p, k)
    m_ref[...] = m_new


def _attn_output(acc):
    return acc[:, :KV_RANK] / acc[:, ONES_LANE:ONES_LANE + 1]


def _attn_prompt_kernel(q_ref, k_ref, w_uv_ref, out_ref, m_ref, acc_ref):
    i = pl.program_id(1)
    tq = q_ref.shape[1]
    tk = ATTN_K_BLOCK
    hpg = MLA_HEADS // ATTN_GROUPS
    rows = hpg * tq
    m_ref[...] = jnp.full(m_ref.shape, -jnp.inf, F32)
    acc_ref[...] = jnp.zeros(acc_ref.shape, F32)

    def block(jk, masked):
        k = k_ref[pl.ds(pl.multiple_of(jk * tk, tk), tk), :]
        mask = None
        if masked:
            q_pos = i * tq + lax.broadcasted_iota(I32, (rows, tk), 0) % tq
            k_pos = jk * tk + lax.broadcasted_iota(I32, (rows, tk), 1)
            mask = k_pos <= q_pos
        for g in range(ATTN_GROUPS):
            q = q_ref[hpg * g:hpg * (g + 1)].reshape(rows, QK_PAD)
            _attn_block(q, k, mask, m_ref.at[g], acc_ref.at[g])

    n_full = (i * tq + 1) // tk
    n_kv = (i * tq + tq - 1) // tk + 1

    def full_body(jk, c):
        block(jk, False)
        return c

    def diag_body(jk, c):
        block(jk, True)
        return c

    lax.fori_loop(0, n_full, full_body, 0)
    lax.fori_loop(n_full, n_kv, diag_body, 0)

    for g in range(ATTN_GROUPS):
        o = _attn_output(acc_ref[g])
        for hl in range(hpg):
            hd = hpg * g + hl
            oh = o[hl * tq:(hl + 1) * tq, :].astype(BF16)
            out_ref[:, V_HEAD_DIM * hd:V_HEAD_DIM * (hd + 1)] = _dot(oh, w_uv_ref[hd]).astype(BF16)


def _attn_prompt(qcat, kcat, w_uv_h, n_batch, seq):
    tq = ATTN_Q_BLOCK
    nq = seq // tq
    n = n_batch * seq
    rows = (MLA_HEADS // ATTN_GROUPS) * tq
    return pl.pallas_call(
        _attn_prompt_kernel,
        grid=(n_batch, nq),
        in_specs=[
            pl.BlockSpec((MLA_HEADS, tq, QK_PAD), lambda b, i: (0, b * nq + i, 0)),
            pl.BlockSpec((seq, QK_PAD), lambda b, i: (b, 0)),
            _full((MLA_HEADS, KV_RANK, V_HEAD_DIM)),
        ],
        out_specs=pl.BlockSpec((tq, MLA_HEADS * V_HEAD_DIM), lambda b, i: (b * nq + i, 0)),
        out_shape=jax.ShapeDtypeStruct((n, MLA_HEADS * V_HEAD_DIM), BF16),
        scratch_shapes=[pltpu.VMEM((ATTN_GROUPS, rows, LANES), F32), pltpu.VMEM((ATTN_GROUPS, rows, QK_PAD), F32)],
        compiler_params=_params("arbitrary", "arbitrary"),
        name="attn_prompt",
    )(qcat, kcat, w_uv_h)


def _attn_sample_kernel(pt_ref, q_ref, knew_ref, w_uv_ref, *rest, n_pages_step, n_new):
    lat_refs = rest[:n_pages_step]
    rope_refs = rest[n_pages_step:2 * n_pages_step]
    out_ref, m_ref, l_ref, acc_ref, res_ref = rest[2 * n_pages_step:]
    del pt_ref
    s_idx = pl.program_id(1)

    @pl.when(s_idx == 0)
    def _():
        m_ref[...] = jnp.full(m_ref.shape, -jnp.inf, F32)
        l_ref[...] = jnp.zeros(l_ref.shape, F32)
        acc_ref[...] = jnp.zeros(acc_ref.shape, F32)

    def update(s_tiles, values):
        m_old = m_ref[...]
        m_new = jnp.maximum(m_old, jnp.max(functools.reduce(jnp.maximum, s_tiles), axis=1, keepdims=True))
        alpha = jnp.exp(m_old - m_new)
        p_tiles = [jnp.exp(t - m_new) for t in s_tiles]
        row_sum = jnp.sum(functools.reduce(jnp.add, p_tiles), axis=1, keepdims=True)
        pv = functools.reduce(jnp.add, [_dot(p.astype(BF16), v) for p, v in zip(p_tiles, values)])
        l_ref[...] = alpha * l_ref[...] + row_sum
        acc_ref[...] = alpha * acc_ref[...] + pv
        m_ref[...] = m_new

    q = q_ref[...]
    q_lat = q[:, :KV_RANK]
    q_rope = q[:, KV_RANK:KV_RANK + QK_ROPE_DIM]
    lats = [lat_refs[p][...].astype(BF16) for p in range(n_pages_step)]
    update([_dot_nt(q_lat, lats[p]) + _dot(q_rope, rope_refs[p][...].astype(BF16)) for p in range(n_pages_step)],
           lats)

    @pl.when(s_idx == pl.num_programs(1) - 1)
    def _():
        kn = knew_ref[...]
        q_t = lax.broadcasted_iota(I32, (q.shape[0], kn.shape[0]), 0) % n_new
        k_t = lax.broadcasted_iota(I32, (q.shape[0], kn.shape[0]), 1)
        update([jnp.where(k_t <= q_t, _dot_nt(q, kn), -jnp.inf)], [kn[:, :KV_RANK]])
        o = (acc_ref[...] / l_ref[...]).astype(BF16)
        for hd in range(MLA_HEADS):
            res_ref[...] = _dot(o, w_uv_ref[hd])
            out_ref[:, V_HEAD_DIM * hd:V_HEAD_DIM * (hd + 1)] = res_ref[hd * n_new:(hd + 1) * n_new, :]


def _attn_sample(page_table, q_b, knew_b, w_uv_h, cache_lat, cache_rope, layer):
    n_b, n_pages = page_table.shape
    page = cache_lat.shape[2]
    pps = min(PAGES_PER_STEP, n_pages)
    assert n_pages % pps == 0
    n_steps = n_pages // pps
    rows = q_b.shape[1]
    n_new = rows // MLA_HEADS
    t_pad = knew_b.shape[1]

    def page_map(p):
        return lambda b, s, pt: (layer, pt[b * n_pages + s * pps + p], 0, 0)

    lat_specs = [pl.BlockSpec((None, None, page, KV_RANK), page_map(p)) for p in range(pps)]
    rope_specs = [pl.BlockSpec((None, None, QK_ROPE_DIM, page), page_map(p)) for p in range(pps)]
    grid_spec = pltpu.PrefetchScalarGridSpec(
        num_scalar_prefetch=1,
        grid=(n_b, n_steps),
        in_specs=[
            pl.BlockSpec((None, rows, QK_PAD), lambda b, s, pt: (b, 0, 0)),
            pl.BlockSpec((None, t_pad, QK_PAD), lambda b, s, pt: (b, 0, 0)),
            pl.BlockSpec((MLA_HEADS, KV_RANK, V_HEAD_DIM), lambda b, s, pt: (0, 0, 0)),
        ] + lat_specs + rope_specs,
        out_specs=pl.BlockSpec((None, n_new, MLA_HEADS * V_HEAD_DIM), lambda b, s, pt: (b, 0, 0)),
        scratch_shapes=[
            pltpu.VMEM((rows, LANES), F32), pltpu.VMEM((rows, LANES), F32), pltpu.VMEM((rows, KV_RANK), F32),
            pltpu.VMEM((rows, V_HEAD_DIM), F32),
        ],
    )
    return pl.pallas_call(
        functools.partial(_attn_sample_kernel, n_pages_step=pps, n_new=n_new),
        grid_spec=grid_spec,
        out_shape=jax.ShapeDtypeStruct((n_b, n_new, MLA_HEADS * V_HEAD_DIM), F32),
        compiler_params=_params("arbitrary", "arbitrary"),
        name="attn_sample",
    )(page_table.reshape(-1), q_b, knew_b, w_uv_h, *([cache_lat] * pps), *([cache_rope] * pps))


def _mem_kv_kernel(mem_ref, w_mk_ref, w_mv_ref, k_ref, v_ref, kb_ref, vb_ref):
    m = mem_ref[...].astype(BF16)
    k = _dot(m, w_mk_ref[...])
    v = _dot(m, w_mv_ref[...])
    k_ref[...] = k
    v_ref[...] = v
    kb_ref[...] = k.astype(BF16)
    vb_ref[...] = v.astype(BF16)


def _mem_kv(mem2d, w_mk, w_mv, n_batch, n_mem):
    blk = pl.BlockSpec((n_mem, MEM_WIDTH), lambda b: (b, 0))
    n = n_batch * n_mem
    return pl.pallas_call(
        _mem_kv_kernel,
        grid=(n_batch,),
        in_specs=[pl.BlockSpec((n_mem, D_MODEL), lambda b: (b, 0)), _full((D_MODEL, MEM_WIDTH)),
                  _full((D_MODEL, MEM_WIDTH))],
        out_specs=[blk, blk, blk, blk],
        out_shape=[jax.ShapeDtypeStruct((n, MEM_WIDTH), F32), jax.ShapeDtypeStruct((n, MEM_WIDTH), F32),
                   jax.ShapeDtypeStruct((n, MEM_WIDTH), BF16), jax.ShapeDtypeStruct((n, MEM_WIDTH), BF16)],
        compiler_params=_params("arbitrary"),
        name="mem_kv",
    )(mem2d, w_mk, w_mv)


def _mix_ln1(x, pool_y, mla_y, w_out_a_ref, w_out_b_ref, g_ref, b_ref):
    mix = _dot(pool_y, w_out_a_ref[...]) + _dot(mla_y, w_out_b_ref[...])
    return _layer_norm(DEEPNORM_ALPHA * x + mix, g_ref[...], b_ref[...])


def _route(x2, w_r_ref, b_r_ref, carry):
    m = x2.shape[0]
    logits = jnp.dot(x2, w_r_ref[...], preferred_element_type=F32, precision=lax.Precision.HIGHEST) + b_r_ref[...]
    e_idx = lax.broadcasted_iota(I32, (m, N_EXPERTS), 1).astype(F32)
    work = logits
    vals, picks = [], []
    for _ in range(TOP_K):
        mx = jnp.max(work, axis=1, keepdims=True)
        pick = jnp.min(jnp.where(work == mx, e_idx, float(N_EXPERTS)), axis=1, keepdims=True)
        vals.append(mx)
        picks.append(pick)
        work = jnp.where(e_idx == pick, -jnp.inf, work)
    hot = jnp.where(work == -jnp.inf, 1.0, 0.0)
    exps = [jnp.exp(v - vals[0]) for v in vals]
    denom = exps[0] + exps[1] + exps[2] + exps[3]
    r_i = lax.broadcasted_iota(I32, (m, m), 0)
    c_i = lax.broadcasted_iota(I32, (m, m), 1)
    strict_lower = jnp.where(c_i < r_i, 1.0, 0.0).astype(BF16)
    before = _dot(strict_lower, hot.astype(BF16)) + carry
    lane4 = lax.broadcasted_iota(I32, (m, TOP_K), 1)
    idx = jnp.zeros((m, TOP_K), I32)
    gates = jnp.zeros((m, TOP_K), F32)
    rank = jnp.zeros((m, TOP_K), I32)
    for k in range(TOP_K):
        rk = jnp.sum(jnp.where(e_idx == picks[k], before, 0.0), axis=1, keepdims=True).astype(I32)
        idx = jnp.where(lane4 == k, picks[k].astype(I32), idx)
        gates = jnp.where(lane4 == k, exps[k] / denom, gates)
        rank = jnp.where(lane4 == k, rk, rank)
    return idx, gates, rank, carry + jnp.sum(hot, axis=0, keepdims=True)


def _mo_ln2_route(x1, o_bf, w_mo_ref, g_ref, b_ref, w_r_ref, b_r_ref, carry):
    x2 = _layer_norm(DEEPNORM_ALPHA * x1 + _dot(o_bf, w_mo_ref[...]), g_ref[...], b_ref[...])
    return (x2,) + _route(x2, w_r_ref, b_r_ref, carry)


def _post_prompt_kernel(x_ref, pooly_ref, mlay_ref, mk_ref, mv_ref,
                        w_out_a_ref, w_out_b_ref, g1_ref, b1_ref, w_mq_ref, w_mo_ref, g2_ref, b2_ref,
                        w_r_ref, b_r_ref,
                        x2_ref, idx_ref, gate_ref, rank_ref, cnt_ref, carry_ref):
    first = jnp.logical_and(pl.program_id(0) == 0, pl.program_id(1) == 0)

    @pl.when(first)
    def _():
        carry_ref[...] = jnp.zeros(carry_ref.shape, F32)

    x1 = _mix_ln1(x_ref[...], pooly_ref[...], mlay_ref[...], w_out_a_ref, w_out_b_ref, g1_ref, b1_ref)
    qm = (_dot(x1.astype(BF16), w_mq_ref[...]) * MEM_SCALE).astype(BF16)
    outs = []
    for hd in range(MEM_HEADS):
        cols = slice(MEM_HEAD_DIM * hd, MEM_HEAD_DIM * (hd + 1))
        s = _dot_nt(qm[:, cols], mk_ref[:, cols])
        p = jnp.exp(s - jnp.max(s, axis=1, keepdims=True))
        o = _dot(p.astype(BF16), mv_ref[:, cols]) / jnp.sum(p, axis=1, keepdims=True)
        outs.append(o.astype(BF16))
    o_all = jnp.concatenate(outs, axis=1)
    x2, idx, gates, rank, carry = _mo_ln2_route(x1, o_all, w_mo_ref, g2_ref, b2_ref, w_r_ref, b_r_ref,
                                                carry_ref[...])
    _store_row_tiles(x2_ref, (), x2)
    idx_ref[...] = idx
    gate_ref[...] = gates
    rank_ref[...] = rank
    carry_ref[...] = carry
    cnt_ref[...] = carry


def _post_weight_specs():
    half = (POOL_WIDTH, D_MODEL)
    return [
        _full(half), _full(half), _full((1, D_MODEL)), _full((1, D_MODEL)),
        _full((D_MODEL, MEM_WIDTH)), _full((MEM_WIDTH, D_MODEL)), _full((1, D_MODEL)), _full((1, D_MODEL)),
        _full((D_MODEL, N_EXPERTS)), _full((1, N_EXPERTS)),
    ]


def _post_prompt(x2d, pool_y, mla_y, mk_b, mv_b, weights, n_batch, seq, n_mem, n_all):
    tm = TOKEN_BLOCK
    nj = seq // tm
    tok = lambda b, j: (b * nj + j, 0)
    return pl.pallas_call(
        _post_prompt_kernel,
        grid=(n_batch, nj),
        in_specs=[
            pl.BlockSpec((tm, D_MODEL), tok), pl.BlockSpec((tm, POOL_WIDTH), tok), pl.BlockSpec((tm, POOL_WIDTH), tok),
            pl.BlockSpec((n_mem, MEM_WIDTH), lambda b, j: (b, 0)), pl.BlockSpec((n_mem, MEM_WIDTH), lambda b, j: (b, 0)),
        ] + _post_weight_specs(),
        out_specs=[
            pl.BlockSpec((tm * ROW_TILE, LANES), tok), pl.BlockSpec((tm, TOP_K), tok), pl.BlockSpec((tm, TOP_K), tok),
            pl.BlockSpec((tm, TOP_K), tok), _full((1, N_EXPERTS)),
        ],
        out_shape=[
            jax.ShapeDtypeStruct((n_all * ROW_TILE, LANES), F32), jax.ShapeDtypeStruct((n_all, TOP_K), I32),
            jax.ShapeDtypeStruct((n_all, TOP_K), F32), jax.ShapeDtypeStruct((n_all, TOP_K), I32),
            jax.ShapeDtypeStruct((1, N_EXPERTS), F32),
        ],
        scratch_shapes=[pltpu.VMEM((1, N_EXPERTS), F32)],
        compiler_params=_params("arbitrary", "arbitrary"),
        name="post_prompt",
    )(x2d, pool_y, mla_y, mk_b, mv_b, *weights)


def _mix_sample_kernel(x_ref, pooly_ref, mlay_ref, w_out_a_ref, w_out_b_ref, g1_ref, b1_ref, w_mq_ref,
                       x1_ref, qm_ref):
    x1 = _mix_ln1(x_ref[...], pooly_ref[...], mlay_ref[...], w_out_a_ref, w_out_b_ref, g1_ref, b1_ref)
    x1_ref[...] = x1
    qm_ref[...] = (_dot(x1.astype(BF16), w_mq_ref[...]) * MEM_SCALE).astype(BF16)


def _mix_sample(x2d, pool_y, mla_y, weights):
    n = x2d.shape[0]
    w_out_a, w_out_b, g1, b1, w_mq = weights[:5]
    return pl.pallas_call(
        _mix_sample_kernel,
        grid=(1,),
        in_specs=[_full((n, D_MODEL)), _full((n, POOL_WIDTH)), _full((n, POOL_WIDTH))] + _post_weight_specs()[:5],
        out_specs=[_full((n, D_MODEL)), _full((n, MEM_WIDTH))],
        out_shape=[jax.ShapeDtypeStruct((n, D_MODEL), F32), jax.ShapeDtypeStruct((n, MEM_WIDTH), BF16)],
        compiler_params=_params("arbitrary"),
        name="mix_sample",
    )(x2d, pool_y, mla_y, w_out_a, w_out_b, g1, b1, w_mq)


def _mem_attn_sample_kernel(q_ref, k_ref, v_ref, out_ref, res_ref, *, n_new):
    for b in range(q_ref.shape[0]):
        s = _dot_nt(q_ref[b], k_ref[b].astype(BF16))
        p = jnp.exp(s - jnp.max(s, axis=1, keepdims=True))
        res_ref[...] = _dot(p.astype(BF16), v_ref[b].astype(BF16)) / jnp.sum(p, axis=1, keepdims=True)
        for hd in range(MEM_HEADS):
            cols = slice(MEM_HEAD_DIM * hd, MEM_HEAD_DIM * (hd + 1))
            out_ref[b, :, cols] = res_ref[hd * n_new:(hd + 1) * n_new, cols].astype(BF16)


def _mem_attn_sample(q_bd, mem_k, mem_v, n_new):
    n_b, rows, _ = q_bd.shape
    n_mem = mem_k.shape[1]
    gb = min(MEM_BATCH_BLOCK, n_b)
    assert n_b % gb == 0
    blk = lambda i: (i, 0, 0)
    return pl.pallas_call(
        functools.partial(_mem_attn_sample_kernel, n_new=n_new),
        grid=(n_b // gb,),
        in_specs=[pl.BlockSpec((gb, rows, MEM_WIDTH), blk), pl.BlockSpec((gb, n_mem, MEM_WIDTH), blk),
                  pl.BlockSpec((gb, n_mem, MEM_WIDTH), blk)],
        out_specs=pl.BlockSpec((gb, n_new, MEM_WIDTH), blk),
        out_shape=jax.ShapeDtypeStruct((n_b, n_new, MEM_WIDTH), BF16),
        scratch_shapes=[pltpu.VMEM((rows, MEM_WIDTH), F32)],
        compiler_params=_params("arbitrary"),
        name="mem_attn_sample",
    )(q_bd, mem_k, mem_v)


def _route_sample_kernel(x1_ref, o_ref, cnt_in_ref, w_mo_ref, g2_ref, b2_ref, w_r_ref, b_r_ref,
                         x2_in, idx_in, gate_in, rank_in,
                         x2_ref, idx_ref, gate_ref, rank_ref, cnt_ref):
    del x2_in, idx_in, gate_in, rank_in
    x2, idx, gates, rank, carry = _mo_ln2_route(x1_ref[...], o_ref[...], w_mo_ref, g2_ref, b2_ref, w_r_ref, b_r_ref,
                                                cnt_in_ref[...])
    _store_row_tiles(x2_ref, (), x2)
    idx_ref[...] = idx
    gate_ref[...] = gates
    rank_ref[...] = rank
    cnt_ref[...] = carry


def _route_sample(x1, o_bf, cnt_in, weights, x2_all, idx_all, gate_all, rank_all, n_prompt):
    n = x1.shape[0]
    assert n_prompt % n == 0
    tail = lambda i: (n_prompt // n, 0)
    anyspec = pl.BlockSpec(memory_space=pl.ANY)
    w_mo, g2, b2, w_r, b_r = weights[5:]
    return pl.pallas_call(
        _route_sample_kernel,
        grid=(1,),
        in_specs=[_full((n, D_MODEL)), _full((n, MEM_WIDTH)), _full((1, N_EXPERTS))] + _post_weight_specs()[5:] + [
            anyspec, anyspec, anyspec, anyspec],
        out_specs=[pl.BlockSpec((n * ROW_TILE, LANES), tail), pl.BlockSpec((n, TOP_K), tail), pl.BlockSpec((n, TOP_K), tail),
                   pl.BlockSpec((n, TOP_K), tail), _full((1, N_EXPERTS))],
        out_shape=[jax.ShapeDtypeStruct(x2_all.shape, F32), jax.ShapeDtypeStruct(idx_all.shape, I32),
                   jax.ShapeDtypeStruct(gate_all.shape, F32), jax.ShapeDtypeStruct(rank_all.shape, I32),
                   jax.ShapeDtypeStruct((1, N_EXPERTS), F32)],
        input_output_aliases={8: 0, 9: 1, 10: 2, 11: 3},
        compiler_params=_params("arbitrary"),
        name="route_sample",
    )(x1, o_bf, cnt_in, w_mo, g2, b2, w_r, b_r, x2_all, idx_all, gate_all, rank_all)


def _moe_kernel(blk_exp_ref, idx_hbm, x_hbm, wgu_ref, bgu_ref, wd_ref, bd_ref, y_hbm,
                idx_smem, xbuf, ybuf, idx_sem, x_sem, y_sem):
    del blk_exp_ref
    i = pl.program_id(0)
    last = pl.num_programs(0) - 1
    tr = xbuf.shape[1] // ROW_TILE
    n_chunks = D_EXPERT // EXPERT_CHUNK
    rows_per_chunk = tr // n_chunks

    def idx_copy(row, s):
        return pltpu.make_async_copy(idx_hbm.at[row], idx_smem.at[s], idx_sem.at[s])

    def gather(s_idx, xs, rows):
        for r in rows:
            src = pl.multiple_of(idx_smem[s_idx, tr + r], ROW_TILE)
            pltpu.make_async_copy(x_hbm.at[pl.ds(src, ROW_TILE)], xbuf.at[xs, pl.ds(ROW_TILE * r, ROW_TILE)],
                                  x_sem.at[xs]).start(priority=r % 2)

    def scatter(s_idx, ys, rows):
        for r in rows:
            dst = pl.multiple_of(idx_smem[s_idx, r], ROW_TILE)
            pltpu.make_async_copy(ybuf.at[ys, pl.ds(ROW_TILE * r, ROW_TILE)], y_hbm.at[pl.ds(dst, ROW_TILE)],
                                  y_sem.at[ys]).start(priority=r % 2)

    def wait_rows(buf, sem, s):
        pltpu.make_async_copy(buf.at[s], buf.at[s], sem.at[s]).wait()

    @pl.when(i == 0)
    def _():
        idx_copy(0, 1).start()
        idx_copy(0, 1).wait()
        gather(1, 0, range(tr))
        idx_copy(1, 0).start()
        ybuf[1] = jnp.zeros(ybuf.shape[1:], F32)

    def step(slot):
        nxt = 1 - slot
        idx_copy(i + 1, slot).wait()
        idx_copy(i + 2, nxt).start()
        wait_rows(xbuf, x_sem, slot)
        xb = _load_row_tiles(xbuf, (slot,), tr).astype(BF16)
        acc = jnp.zeros((tr, D_MODEL), F32) + bd_ref[...]
        for c in range(n_chunks):
            cg = slice(c * EXPERT_CHUNK, (c + 1) * EXPERT_CHUNK)
            cu = slice(D_EXPERT + c * EXPERT_CHUNK, D_EXPERT + (c + 1) * EXPERT_CHUNK)
            gate = jnp.minimum(_dot(xb, wgu_ref[:, cg]) + bgu_ref[:, cg], SWIGLU_LIMIT)
            up = jnp.clip(_dot(xb, wgu_ref[:, cu]) + bgu_ref[:, cu], -SWIGLU_LIMIT, SWIGLU_LIMIT)
            act = gate * (1.0 / (1.0 + jnp.exp(-SWIGLU_ALPHA * gate))) * (up + 1.0)
            acc = acc + _dot(act.astype(BF16), wd_ref[cg, :])
            half = n_chunks // 2
            rows = range((c % half) * 2 * rows_per_chunk, (c % half + 1) * 2 * rows_per_chunk)
            if c < half:
                gather(slot, nxt, rows)
            else:
                scatter(slot, nxt, rows)

        @pl.when(i >= 1)
        def _():
            wait_rows(ybuf, y_sem, slot)

        _store_row_tiles(ybuf, (slot,), acc)

        @pl.when(i == last)
        def _():
            idx_copy(i + 2, nxt).wait()
            wait_rows(xbuf, x_sem, nxt)
            wait_rows(ybuf, y_sem, nxt)

    for parity in range(2):
        pl.when(i % 2 == parity)(functools.partial(step, parity))


def _moe(blk_exp, idx_rows, x_all, w_gu, b_gu, w_d, b_d, n_assign):
    tr = EXPERT_ROWS
    n_blocks = blk_exp.shape[0]
    assert n_blocks >= 2 and idx_rows.shape == (n_blocks + 2, 2 * tr) and tr % (D_EXPERT // EXPERT_CHUNK) == 0
    anyspec = pl.BlockSpec(memory_space=pl.ANY)
    emap3 = lambda i, be: (be[i], 0, 0)
    grid_spec = pltpu.PrefetchScalarGridSpec(
        num_scalar_prefetch=1,
        grid=(n_blocks,),
        in_specs=[
            anyspec, anyspec,
            pl.BlockSpec((None, D_MODEL, 2 * D_EXPERT), emap3), pl.BlockSpec((None, 1, 2 * D_EXPERT), emap3),
            pl.BlockSpec((None, D_EXPERT, D_MODEL), emap3), pl.BlockSpec((None, 1, D_MODEL), emap3),
        ],
        out_specs=anyspec,
        scratch_shapes=[
            pltpu.SMEM((2, 2 * tr), I32),
            pltpu.VMEM((2, tr * ROW_TILE, LANES), F32), pltpu.VMEM((2, tr * ROW_TILE, LANES), F32),
            pltpu.SemaphoreType.DMA((2,)), pltpu.SemaphoreType.DMA((2,)), pltpu.SemaphoreType.DMA((2,)),
        ],
    )
    return pl.pallas_call(
        _moe_kernel,
        grid_spec=grid_spec,
        out_shape=jax.ShapeDtypeStruct(((n_assign + 2 * tr) * ROW_TILE, LANES), F32),
        compiler_params=_params("arbitrary"),
        name="moe_experts",
    )(blk_exp, idx_rows, x_all, w_gu, b_gu, w_d, b_d)


def _combine_kernel(x2_ref, gate_ref, g3_ref, b3_ref, *rest):
    y_refs, out_ref = rest[:TOP_K], rest[TOP_K]
    gates = gate_ref[...]
    tm = out_ref.shape[0]
    moe = jnp.zeros(out_ref.shape, F32)
    for k in range(TOP_K):
        moe = moe + gates[:, k:k + 1] * _load_row_tiles(y_refs[k], (), tm)
    out_ref[...] = _layer_norm(DEEPNORM_ALPHA * _load_row_tiles(x2_ref, (), tm) + moe, g3_ref[...], b3_ref[...])


def _combine(x2_all, y_tok, gates_all, g3, b3, row0, n_rows, tm):
    n_all = gates_all.shape[0]
    assert row0 % tm == 0 and n_rows % tm == 0 and n_all % tm == 0
    off = row0 // tm
    tok = lambda i: (off + i, 0)
    tiles = (tm * ROW_TILE, LANES)
    slot_specs = [pl.BlockSpec(tiles, functools.partial(lambda i, k: (k * (n_all // tm) + off + i, 0), k=k))
                  for k in range(TOP_K)]
    return pl.pallas_call(
        _combine_kernel,
        grid=(n_rows // tm,),
        in_specs=[pl.BlockSpec(tiles, tok), pl.BlockSpec((tm, TOP_K), tok), _full((1, D_MODEL)),
                  _full((1, D_MODEL))] + slot_specs,
        out_specs=pl.BlockSpec((tm, D_MODEL), lambda i: (i, 0)),
        out_shape=jax.ShapeDtypeStruct((n_rows, D_MODEL), F32),
        compiler_params=_params("arbitrary"),
        name="combine_ln3",
    )(x2_all, gates_all, g3, b3, *([y_tok] * TOP_K))


def _rope_tables(pos):
    inv_freq = ROPE_THETA ** (-jnp.arange(ROPE_HALF, dtype=F32) / ROPE_HALF)
    ang = pos.astype(F32)[:, None] * inv_freq[None, :]
    c, s, z = jnp.cos(ang), jnp.sin(ang), jnp.zeros_like(ang)
    return jnp.concatenate([c, z, c, z], axis=1), jnp.concatenate([-s, z, s, z], axis=1)


def _pad_rope_cols(w):
    z = jnp.zeros(w.shape[:-1] + (ROPE_HALF,), w.dtype)
    return jnp.concatenate([w[..., :ROPE_HALF], z, w[..., ROPE_HALF:], z], axis=-1)


def _layer_weights(l, w_in, w_pool, pool_scale, q_norm_g, w_uq, kv_norm_g, w_uk, w_uv, w_out, ln1_g, ln1_b,
                   w_mq, w_mo, ln2_g, ln2_b, w_router, b_router):
    o3 = POOL_WIDTH + Q_RANK + KV_RANK
    w_in_p = jnp.concatenate([w_in[l][:, :o3], _pad_rope_cols(w_in[l][:, o3:])], axis=1).astype(BF16)
    uq = w_uq[l].reshape(Q_RANK, MLA_HEADS, QK_NOPE_DIM + QK_ROPE_DIM)
    uq_p = jnp.concatenate([uq[..., :QK_NOPE_DIM], _pad_rope_cols(uq[..., QK_NOPE_DIM:])], axis=-1)
    uq_p = uq_p.reshape(Q_RANK, MLA_HEADS * QK_PAD).astype(BF16)
    w_ukt = jnp.transpose(w_uk[l], (1, 2, 0)).astype(BF16)
    w_uv_h = jnp.transpose(w_uv[l], (1, 0, 2)).astype(BF16)
    row = lambda v: v[l].reshape(1, -1)
    proj = [w_in_p, row(q_norm_g), uq_p, row(kv_norm_g), w_ukt, w_pool[l].astype(BF16), row(pool_scale)]
    post = [w_out[l][:POOL_WIDTH].astype(BF16), w_out[l][POOL_WIDTH:].astype(BF16), row(ln1_g), row(ln1_b),
            w_mq[l].astype(BF16), w_mo[l].astype(BF16), row(ln2_g), row(ln2_b), w_router[l], row(b_router)]
    return proj, w_uv_h, post


def _routing_tables(idx_all, rank_all, counts, n_blocks):
    tr = EXPERT_ROWS
    cnt = counts.reshape(-1).astype(I32)
    padded = ((cnt + tr - 1) // tr) * tr
    pend = jnp.cumsum(padded)
    pstart = pend - padded
    dest = (pstart[idx_all] + rank_all).reshape(-1)
    n_all = idx_all.shape[0]
    m = dest.shape[0]
    n_rows = n_blocks * tr
    assign = (jnp.arange(TOP_K, dtype=I32)[None, :] * n_all + jnp.arange(n_all, dtype=I32)[:, None]).reshape(-1)
    row_src = jnp.full((n_rows,), -1, I32).at[dest].set(assign, unique_indices=True)
    spare = m + jnp.arange(n_rows, dtype=I32) % (2 * tr)
    dst_rows = (jnp.where(row_src >= 0, row_src, spare) * ROW_TILE).reshape(n_blocks, tr)
    src_tok = ((jnp.maximum(row_src, 0) % n_all) * ROW_TILE).reshape(n_blocks, tr)
    spare_row = spare[:tr][None] * ROW_TILE
    idx_rows = jnp.concatenate([jnp.concatenate([spare_row, spare_row, dst_rows, spare_row], axis=0),
                                jnp.concatenate([src_tok, jnp.zeros((3, tr), I32)], axis=0)], axis=1)
    n_used = pend[-1] // tr
    blk = jnp.minimum(jnp.arange(n_blocks + 1, dtype=I32), n_used - 1)
    blk_exp = jnp.sum((pend[None, :] <= (blk * tr)[:, None]).astype(I32), axis=1)
    return jnp.clip(blk_exp, 0, N_EXPERTS - 1), idx_rows


def kernel(x_prompt, x_sample, mem_prompt, cache_kv_latent, cache_k_rope, cache_mem_k, cache_mem_v, state_pool,
           page_table, w_in, w_pool, pool_scale, q_norm_g, w_uq, kv_norm_g, w_uk, w_uv, w_out, ln1_g, ln1_b,
           w_mq, w_mk, w_mv, w_mo, ln2_g, ln2_b, w_router, b_router, w_gate_up, b_gate_up, w_down, b_down,
           ln3_g, ln3_b):
    n_bp, seq, _ = x_prompt.shape
    n_bs, n_new, _ = x_sample.shape
    n_mem = mem_prompt.shape[1]
    page = cache_kv_latent.shape[2]
    past_len = page_table.shape[1] * page
    n_p = n_bp * seq
    n_s = n_bs * n_new
    n_all = n_p + n_s
    n_assign = n_all * TOP_K
    assert w_in.shape[0] == DEPTH == 1
    assert seq % TOKEN_BLOCK == 0 and seq % ATTN_K_BLOCK == 0 and seq % ATTN_Q_BLOCK == 0 and seq >= POOL_HALO
    assert n_p % n_s == 0 and (n_s % TOKEN_BLOCK == 0 or TOKEN_BLOCK % n_s == 0)
    l = 0

    proj_w, w_uv_h, post_w = _layer_weights(l, w_in, w_pool, pool_scale, q_norm_g, w_uq, kv_norm_g, w_uk, w_uv,
                                            w_out, ln1_g, ln1_b, w_mq, w_mo, ln2_g, ln2_b, w_router, b_router)

    xp = x_prompt.reshape(n_p, D_MODEL)
    cos_p, sin_p = _rope_tables(jnp.arange(seq))
    qcat, kcat, lat_p, kr_p, pooly_p, pstate_p = _proj_prompt(xp, proj_w, cos_p, sin_p, n_bp, seq)
    mlay_p = _attn_prompt(qcat, kcat, w_uv_h, n_bp, seq)
    mk, mv, mk_b, mv_b = _mem_kv(mem_prompt.reshape(n_bp * n_mem, D_MODEL), w_mk[l].astype(BF16),
                                 w_mv[l].astype(BF16), n_bp, n_mem)
    x2_all, idx_all, gate_all, rank_all, cnt_p = _post_prompt(xp, pooly_p, mlay_p, mk_b, mv_b, post_w,
                                                              n_bp, seq, n_mem, n_all)

    xs = jnp.transpose(x_sample, (1, 0, 2)).reshape(n_s, D_MODEL)
    state_t = jnp.transpose(state_pool[l], (1, 0, 2))
    cos_s, sin_s = _rope_tables(jnp.repeat(past_len + jnp.arange(n_new), n_bs))
    qcat_s, kcat_s, lat_s, kr_s, pooly_s, pstate_s = _proj_sample(xs, state_t, proj_w, cos_s, sin_s, past_len)
    q_b = jnp.transpose(qcat_s.reshape(MLA_HEADS, n_new, n_bs, QK_PAD), (2, 0, 1, 3)).reshape(
        n_bs, MLA_HEADS * n_new, QK_PAD)
    knew_b = jnp.transpose(kcat_s.reshape(n_new, n_bs, QK_PAD), (1, 0, 2))
    knew_b = jnp.pad(knew_b, ((0, 0), (0, LANES - n_new), (0, 0)))
    mlay_b = _attn_sample(page_table, q_b, knew_b, w_uv_h, cache_kv_latent,
                          jnp.transpose(cache_k_rope, (0, 1, 3, 2)), l)
    mlay_s = jnp.transpose(mlay_b, (1, 0, 2)).reshape(n_s, POOL_WIDTH).astype(BF16)
    x1_s, qm_s = _mix_sample(xs, pooly_s, mlay_s, post_w)
    qm_b = jnp.transpose(qm_s.reshape(n_new, n_bs, MEM_WIDTH), (1, 0, 2))
    head_of_col = jnp.arange(MEM_WIDTH) // MEM_HEAD_DIM
    head_mask = (head_of_col[None, :] == jnp.arange(MEM_HEADS)[:, None]).astype(BF16)
    q_bd = (qm_b[:, None, :, :] * head_mask[None, :, None, :]).reshape(n_bs, MEM_HEADS * n_new, MEM_WIDTH)
    o_b = _mem_attn_sample(q_bd, cache_mem_k[l].reshape(n_bs, n_mem, MEM_WIDTH),
                           cache_mem_v[l].reshape(n_bs, n_mem, MEM_WIDTH), n_new)
    o_s = jnp.transpose(o_b, (1, 0, 2)).reshape(n_s, MEM_WIDTH)
    x2_all, idx_all, gate_all, rank_all, counts = _route_sample(x1_s, o_s, cnt_p, post_w, x2_all, idx_all, gate_all,
                                                               rank_all, n_p)

    n_blocks = -(-n_assign // EXPERT_ROWS) + N_EXPERTS + 1
    blk_exp, idx_rows = _routing_tables(idx_all, rank_all, counts, n_blocks)
    y_tok = _moe(blk_exp, idx_rows, x2_all, w_gate_up[l].astype(BF16), b_gate_up[l][:, None, :],
                 w_down[l].astype(BF16), b_down[l][:, None, :], n_assign)
    g3, b3 = ln3_g[l].reshape(1, -1), ln3_b[l].reshape(1, -1)
    y_p = _combine(x2_all, y_tok, gate_all, g3, b3, 0, n_p, TOKEN_BLOCK)
    y_s = _combine(x2_all, y_tok, gate_all, g3, b3, n_p, n_s, min(TOKEN_BLOCK, n_s))

    def from_steps(a, width):
        return jnp.transpose(a.reshape(n_new, n_bs, width), (1, 0, 2))[None]

    return (
        y_p.reshape(n_bp, seq, D_MODEL),
        from_steps(y_s, D_MODEL)[0],
        lat_p.reshape(1, n_bp, seq, KV_RANK),
        kr_p.reshape(1, n_bp, seq, QK_ROPE_DIM),
        pstate_p[None, :, POOL_HALO - POOL_BUF:, :],
        mk.reshape(1, n_bp, n_mem, MEM_HEADS, MEM_HEAD_DIM),
        mv.reshape(1, n_bp, n_mem, MEM_HEADS, MEM_HEAD_DIM),
        from_steps(lat_s, KV_RANK),
        from_steps(kr_s, QK_ROPE_DIM),
        jnp.transpose(pstate_s, (1, 0, 2))[None],
    )
```

```python
import functools

import jax
import jax.numpy as jnp
from jax import lax
from jax.experimental import pallas as pl
from jax.experimental.pallas import tpu as pltpu

F32 = jnp.float32
BF16 = jnp.bfloat16
I32 = jnp.int32

D_MODEL = 1024
POOL_WIDTH = 512
POOL_WINDOWS = (2, 4, 8, 16)
POOL_GROUP_DIM = 128
POOL_BUF = 15
POOL_HALO = 16
MLA_HEADS = 4
QK_NOPE_DIM = 128
QK_ROPE_DIM = 64
ROPE_HALF = 32
V_HEAD_DIM = 128
Q_RANK = 256
KV_RANK = 128
ROPE_THETA = 10000.0
QK_PAD = 256
ONES_LANE = QK_PAD - 1
MEM_HEADS = 4
MEM_HEAD_DIM = 128
MEM_WIDTH = 512
N_EXPERTS = 32
TOP_K = 4
D_EXPERT = 1024
SWIGLU_LIMIT = 7.0
SWIGLU_ALPHA = 1.702
LN_EPS = 1e-5
RMS_EPS = 1e-6
DEPTH = 1
DEEPNORM_ALPHA = (2.0 * DEPTH) ** 0.25
ATTN_SCALE = (QK_NOPE_DIM + QK_ROPE_DIM) ** -0.5
MEM_SCALE = MEM_HEAD_DIM ** -0.5

LANES = 128
VMEM_LIMIT = 48 * 1024 * 1024

TOKEN_BLOCK = 256
ATTN_Q_BLOCK = 256
ATTN_K_BLOCK = 512
ATTN_GROUPS = 2
PAGES_PER_STEP = 32
MEM_BATCH_BLOCK = 8
EXPERT_ROWS = 256
EXPERT_CHUNK = 256


def _dot(a, b):
    return jnp.dot(a, b, preferred_element_type=F32)


def _dot_nt(a, b):
    return lax.dot_general(a, b, (((1,), (1,)), ((), ())), preferred_element_type=F32)


def _rms(x, g):
    return x * lax.rsqrt(jnp.mean(x * x, axis=-1, keepdims=True) + RMS_EPS) * g


def _layer_norm(x, g, b):
    mu = jnp.mean(x, axis=-1, keepdims=True)
    xc = x - mu
    var = jnp.mean(xc * xc, axis=-1, keepdims=True)
    return xc * lax.rsqrt(var + LN_EPS) * g + b


ROW_TILE = D_MODEL // LANES


def _store_row_tiles(ref, lead, val):
    m = val.shape[0]
    for j in range(ROW_TILE):
        ref[(*lead, pl.ds(j, m, stride=ROW_TILE), slice(None))] = val[:, LANES * j:LANES * (j + 1)]


def _load_row_tiles(ref, lead, m):
    return jnp.concatenate([ref[(*lead, pl.ds(j, m, stride=ROW_TILE), slice(None))] for j in range(ROW_TILE)], axis=1)


def _params(*semantics):
    return pltpu.CompilerParams(dimension_semantics=semantics, vmem_limit_bytes=VMEM_LIMIT)


def _full(shape):
    n = len(shape)
    return pl.BlockSpec(shape, lambda *_: (0,) * n)


def _project(x_bf, w_in_ref, qg_ref, w_uq_ref, kvg_ref, w_ukt_ref, cos, sin):
    h = _dot(x_bf, w_in_ref[...])
    u = h[:, :POOL_WIDTH]
    qn = _rms(h[:, POOL_WIDTH:POOL_WIDTH + Q_RANK], qg_ref[...])
    q = _dot(qn.astype(BF16), w_uq_ref[...])
    q_lat, q_rope = [], []
    for hd in range(MLA_HEADS):
        nope = q[:, QK_PAD * hd:QK_PAD * hd + QK_NOPE_DIM]
        rp = q[:, QK_PAD * hd + QK_NOPE_DIM:QK_PAD * (hd + 1)]
        q_lat.append(_dot(nope.astype(BF16), w_ukt_ref[hd]) * ATTN_SCALE)
        q_rope.append((rp * cos + pltpu.roll(rp, 64, 1) * sin) * ATTN_SCALE)
    o2 = POOL_WIDTH + Q_RANK
    lat = _rms(h[:, o2:o2 + KV_RANK], kvg_ref[...])
    kp = h[:, o2 + KV_RANK:]
    k_rope = kp * cos + pltpu.roll(kp, 64, 1) * sin
    return u, q_lat, q_rope, lat, k_rope


def _compact_rope(r):
    lane = lax.broadcasted_iota(I32, r.shape, 1)
    moved = pltpu.roll(r, 96, 1)
    return jnp.where(lane < ROPE_HALF, r, jnp.where(lane < QK_ROPE_DIM, moved, 0.0))


def _with_ones_lane(r):
    lane = lax.broadcasted_iota(I32, r.shape, 1)
    return jnp.where(lane == ONES_LANE - KV_RANK, 1.0, r)


def _pool_group_out(diff, g, w_pool_ref, pscale_ref):
    cols = slice(POOL_GROUP_DIM * g, POOL_GROUP_DIM * (g + 1))
    return _dot(diff.astype(BF16), w_pool_ref[g]) * pscale_ref[:, cols]


def _proj_prompt_kernel(x_ref, w_in_ref, qg_ref, w_uq_ref, kvg_ref, w_ukt_ref, w_pool_ref, pscale_ref,
                        cos_ref, sin_ref,
                        qcat_ref, kcat_ref, lat_ref, kr_ref, pooly_ref, pstate_ref, ubuf):
    j = pl.program_id(1)
    tm = x_ref.shape[0]

    @pl.when(j == 0)
    def _():
        ubuf[0:POOL_HALO, :] = jnp.zeros((POOL_HALO, POOL_WIDTH), F32)

    u, q_lat, q_rope, lat, k_rope = _project(
        x_ref[...].astype(BF16), w_in_ref, qg_ref, w_uq_ref, kvg_ref, w_ukt_ref, cos_ref[...], sin_ref[...])
    for hd in range(MLA_HEADS):
        qcat_ref[hd, :, 0:KV_RANK] = q_lat[hd].astype(BF16)
        qcat_ref[hd, :, KV_RANK:QK_PAD] = q_rope[hd].astype(BF16)
    lat_ref[...] = lat
    kcat_ref[:, 0:KV_RANK] = lat.astype(BF16)
    kcat_ref[:, KV_RANK:QK_PAD] = _with_ones_lane(k_rope).astype(BF16)
    kr_ref[...] = _compact_rope(k_rope)[:, :QK_ROPE_DIM]

    ubuf[POOL_HALO:POOL_HALO + tm, :] = u
    pos = j * tm + lax.broadcasted_iota(I32, (tm, 1), 0)
    for g, w in enumerate(POOL_WINDOWS):
        cols = slice(POOL_GROUP_DIM * g, POOL_GROUP_DIM * (g + 1))
        ug = u[:, cols]
        ssum = ug
        for k in range(1, w):
            ssum = ssum + ubuf[POOL_HALO - k:POOL_HALO - k + tm, cols]
        count = jnp.minimum(w, pos + 1).astype(F32)
        pooly_ref[:, cols] = _pool_group_out(ssum / count - ug, g, w_pool_ref, pscale_ref).astype(BF16)
    tail = ubuf[tm:tm + POOL_HALO, :]
    ubuf[0:POOL_HALO, :] = tail

    @pl.when(j == pl.num_programs(1) - 1)
    def _():
        pstate_ref[...] = tail


def _proj_sample_kernel(x_ref, state_ref, w_in_ref, qg_ref, w_uq_ref, kvg_ref, w_ukt_ref, w_pool_ref, pscale_ref,
                        cos_ref, sin_ref,
                        qcat_ref, kcat_ref, lat_ref, kr_ref, pooly_ref, pstate_ref, *, past_len):
    n_b = state_ref.shape[1]
    n_t = x_ref.shape[0] // n_b
    u, q_lat, q_rope, lat, k_rope = _project(
        x_ref[...].astype(BF16), w_in_ref, qg_ref, w_uq_ref, kvg_ref, w_ukt_ref, cos_ref[...], sin_ref[...])
    for hd in range(MLA_HEADS):
        qcat_ref[hd, :, 0:KV_RANK] = q_lat[hd].astype(BF16)
        qcat_ref[hd, :, KV_RANK:QK_PAD] = _compact_rope(q_rope[hd]).astype(BF16)
    lat_ref[...] = lat
    kc = _compact_rope(k_rope)
    kcat_ref[:, 0:KV_RANK] = lat.astype(BF16)
    kcat_ref[:, KV_RANK:QK_PAD] = _with_ones_lane(kc).astype(BF16)
    kr_ref[...] = kc[:, :QK_ROPE_DIM]

    def ext(jj):
        if jj < POOL_BUF:
            return state_ref[jj]
        return u[(jj - POOL_BUF) * n_b:(jj - POOL_BUF + 1) * n_b, :]

    for t in range(n_t):
        ut = ext(POOL_BUF + t)
        for g, w in enumerate(POOL_WINDOWS):
            cols = slice(POOL_GROUP_DIM * g, POOL_GROUP_DIM * (g + 1))
            ssum = ut[:, cols]
            for k in range(1, w):
                ssum = ssum + ext(POOL_BUF + t - k)[:, cols]
            count = float(min(w, past_len + t + 1))
            y = _pool_group_out(ssum / count - ut[:, cols], g, w_pool_ref, pscale_ref)
            pooly_ref[t * n_b:(t + 1) * n_b, cols] = y.astype(BF16)
    for jj in range(POOL_BUF):
        pstate_ref[jj] = ext(n_t + jj)


def _proj_weight_specs():
    return [
        _full((D_MODEL, D_MODEL)),
        _full((1, Q_RANK)),
        _full((Q_RANK, MLA_HEADS * QK_PAD)),
        _full((1, KV_RANK)),
        _full((MLA_HEADS, QK_NOPE_DIM, KV_RANK)),
        _full((len(POOL_WINDOWS), POOL_GROUP_DIM, POOL_GROUP_DIM)),
        _full((1, POOL_WIDTH)),
    ]


def _proj_prompt(x2d, weights, cos_t, sin_t, n_batch, seq):
    tm = TOKEN_BLOCK
    n = n_batch * seq
    nj = seq // tm
    tok = lambda b, j: (b * nj + j, 0)
    return pl.pallas_call(
        _proj_prompt_kernel,
        grid=(n_batch, nj),
        in_specs=[pl.BlockSpec((tm, D_MODEL), tok)] + _proj_weight_specs() + [
            pl.BlockSpec((tm, LANES), lambda b, j: (j, 0)),
            pl.BlockSpec((tm, LANES), lambda b, j: (j, 0)),
        ],
        out_specs=[
            pl.BlockSpec((MLA_HEADS, tm, QK_PAD), lambda b, j: (0, b * nj + j, 0)),
            pl.BlockSpec((tm, QK_PAD), tok),
            pl.BlockSpec((tm, KV_RANK), tok),
            pl.BlockSpec((tm, QK_ROPE_DIM), tok),
            pl.BlockSpec((tm, POOL_WIDTH), tok),
            pl.BlockSpec((None, POOL_HALO, POOL_WIDTH), lambda b, j: (b, 0, 0)),
        ],
        out_shape=[
            jax.ShapeDtypeStruct((MLA_HEADS, n, QK_PAD), BF16),
            jax.ShapeDtypeStruct((n, QK_PAD), BF16),
            jax.ShapeDtypeStruct((n, KV_RANK), F32),
            jax.ShapeDtypeStruct((n, QK_ROPE_DIM), F32),
            jax.ShapeDtypeStruct((n, POOL_WIDTH), BF16),
            jax.ShapeDtypeStruct((n_batch, POOL_HALO, POOL_WIDTH), F32),
        ],
        scratch_shapes=[pltpu.VMEM((POOL_HALO + tm, POOL_WIDTH), F32)],
        compiler_params=_params("arbitrary", "arbitrary"),
        name="proj_prompt",
    )(x2d, *weights, cos_t, sin_t)


def _proj_sample(x2d, state_t, weights, cos_t, sin_t, past_len):
    n = x2d.shape[0]
    n_b = state_t.shape[1]
    return pl.pallas_call(
        functools.partial(_proj_sample_kernel, past_len=past_len),
        grid=(1,),
        in_specs=[_full((n, D_MODEL)), _full((POOL_BUF, n_b, POOL_WIDTH))] + _proj_weight_specs() + [
            _full((n, LANES)), _full((n, LANES))],
        out_specs=[
            _full((MLA_HEADS, n, QK_PAD)), _full((n, QK_PAD)), _full((n, KV_RANK)), _full((n, QK_ROPE_DIM)),
            _full((n, POOL_WIDTH)), _full((POOL_BUF, n_b, POOL_WIDTH)),
        ],
        out_shape=[
            jax.ShapeDtypeStruct((MLA_HEADS, n, QK_PAD), BF16),
            jax.ShapeDtypeStruct((n, QK_PAD), BF16),
            jax.ShapeDtypeStruct((n, KV_RANK), F32),
            jax.ShapeDtypeStruct((n, QK_ROPE_DIM), F32),
            jax.ShapeDtypeStruct((n, POOL_WIDTH), BF16),
            jax.ShapeDtypeStruct((POOL_BUF, n_b, POOL_WIDTH), F32),
        ],
        compiler_params=_params("arbitrary"),
        name="proj_sample",
    )(x2d, state_t, *weights, cos_t, sin_t)


def _attn_block(q, k, mask, m_ref, acc_ref):
    s = _dot_nt(q, k)
    if mask is not None:
        s = jnp.where(mask, s, -jnp.inf)
    tiles = [s[:, c * LANES:(c + 1) * LANES] for c in range(s.shape[1] // LANES)]
    m_old = m_ref[...]
    m_new = jnp.maximum(m_old, jnp.max(functools.reduce(jnp.maximum, tiles), axis=1, keepdims=True))
    alpha = jnp.exp(m_old - m_new)
    p = jnp.concatenate([jnp.exp(t - m_new) for t in tiles], axis=1).astype(BF16)
    acc_ref[...] = jnp.concatenate([alpha, alpha], axis=1) * acc_ref[...] + _dot(p, k)
    m_ref[...] = m_new


def _attn_output(acc):
    return acc[:, :KV_RANK] / acc[:, ONES_LANE:ONES_LANE + 1]


def _attn_prompt_kernel(q_ref, k_ref, w_uv_ref, out_ref, m_ref, acc_ref):
    i = pl.program_id(1)
    tq = q_ref.shape[1]
    tk = ATTN_K_BLOCK
    hpg = MLA_HEADS // ATTN_GROUPS
    rows = hpg * tq
    m_ref[...] = jnp.full(m_ref.shape, -jnp.inf, F32)
    acc_ref[...] = jnp.zeros(acc_ref.shape, F32)

    def block(jk, masked):
        k = k_ref[pl.ds(pl.multiple_of(jk * tk, tk), tk), :]
        mask = None
        if masked:
            q_pos = i * tq + lax.broadcasted_iota(I32, (rows, tk), 0) % tq
            k_pos = jk * tk + lax.broadcasted_iota(I32, (rows, tk), 1)
            mask = k_pos <= q_pos
        for g in range(ATTN_GROUPS):
            q = q_ref[hpg * g:hpg * (g + 1)].reshape(rows, QK_PAD)
            _attn_block(q, k, mask, m_ref.at[g], acc_ref.at[g])

    n_full = (i * tq + 1) // tk
    n_kv = (i * tq + tq - 1) // tk + 1

    def full_body(jk, c):
        block(jk, False)
        return c

    def diag_body(jk, c):
        block(jk, True)
        return c

    lax.fori_loop(0, n_full, full_body, 0)
    lax.fori_loop(n_full, n_kv, diag_body, 0)

    for g in range(ATTN_GROUPS):
        o = _attn_output(acc_ref[g])
        for hl in range(hpg):
            hd = hpg * g + hl
            oh = o[hl * tq:(hl + 1) * tq, :].astype(BF16)
            out_ref[:, V_HEAD_DIM * hd:V_HEAD_DIM * (hd + 1)] = _dot(oh, w_uv_ref[hd]).astype(BF16)


def _attn_prompt(qcat, kcat, w_uv_h, n_batch, seq):
    tq = ATTN_Q_BLOCK
    nq = seq // tq
    n = n_batch * seq
    rows = (MLA_HEADS // ATTN_GROUPS) * tq
    return pl.pallas_call(
        _attn_prompt_kernel,
        grid=(n_batch, nq),
        in_specs=[
            pl.BlockSpec((MLA_HEADS, tq, QK_PAD), lambda b, i: (0, b * nq + i, 0)),
            pl.BlockSpec((seq, QK_PAD), lambda b, i: (b, 0)),
            _full((MLA_HEADS, KV_RANK, V_HEAD_DIM)),
        ],
        out_specs=pl.BlockSpec((tq, MLA_HEADS * V_HEAD_DIM), lambda b, i: (b * nq + i, 0)),
        out_shape=jax.ShapeDtypeStruct((n, MLA_HEADS * V_HEAD_DIM), BF16),
        scratch_shapes=[pltpu.VMEM((ATTN_GROUPS, rows, LANES), F32), pltpu.VMEM((ATTN_GROUPS, rows, QK_PAD), F32)],
        compiler_params=_params("arbitrary", "arbitrary"),
        name="attn_prompt",
    )(qcat, kcat, w_uv_h)


def _attn_sample_kernel(pt_ref, q_ref, knew_ref, w_uv_ref, *rest, n_pages_step, n_new):
    lat_refs = rest[:n_pages_step]
    rope_refs = rest[n_pages_step:2 * n_pages_step]
    out_ref, m_ref, l_ref, acc_ref, res_ref = rest[2 * n_pages_step:]
    del pt_ref
    s_idx = pl.program_id(1)

    @pl.when(s_idx == 0)
    def _():
        m_ref[...] = jnp.full(m_ref.shape, -jnp.inf, F32)
        l_ref[...] = jnp.zeros(l_ref.shape, F32)
        acc_ref[...] = jnp.zeros(acc_ref.shape, F32)

    def update(s_tiles, values):
        m_old = m_ref[...]
        m_new = jnp.maximum(m_old, jnp.max(functools.reduce(jnp.maximum, s_tiles), axis=1, keepdims=True))
        alpha = jnp.exp(m_old - m_new)
        p_tiles = [jnp.exp(t - m_new) for t in s_tiles]
        row_sum = jnp.sum(functools.reduce(jnp.add, p_tiles), axis=1, keepdims=True)
        pv = functools.reduce(jnp.add, [_dot(p.astype(BF16), v) for p, v in zip(p_tiles, values)])
        l_ref[...] = alpha * l_ref[...] + row_sum
        acc_ref[...] = alpha * acc_ref[...] + pv
        m_ref[...] = m_new

    q = q_ref[...]
    q_lat = q[:, :KV_RANK]
    q_rope = q[:, KV_RANK:KV_RANK + QK_ROPE_DIM]
    lats = [lat_refs[p][...].astype(BF16) for p in range(n_pages_step)]
    update([_dot_nt(q_lat, lats[p]) + _dot(q_rope, rope_refs[p][...].astype(BF16)) for p in range(n_pages_step)],
           lats)

    @pl.when(s_idx == pl.num_programs(1) - 1)
    def _():
        kn = knew_ref[...]
        q_t = lax.broadcasted_iota(I32, (q.shape[0], kn.shape[0]), 0) % n_new
        k_t = lax.broadcasted_iota(I32, (q.shape[0], kn.shape[0]), 1)
        update([jnp.where(k_t <= q_t, _dot_nt(q, kn), -jnp.inf)], [kn[:, :KV_RANK]])
        o = (acc_ref[...] / l_ref[...]).astype(BF16)
        for hd in range(MLA_HEADS):
            res_ref[...] = _dot(o, w_uv_ref[hd])
            out_ref[:, V_HEAD_DIM * hd:V_HEAD_DIM * (hd + 1)] = res_ref[hd * n_new:(hd + 1) * n_new, :]


def _attn_sample(page_table, q_b, knew_b, w_uv_h, cache_lat, cache_rope, layer):
    n_b, n_pages = page_table.shape
    page = cache_lat.shape[2]
    pps = min(PAGES_PER_STEP, n_pages)
    assert n_pages % pps == 0
    n_steps = n_pages // pps
    rows = q_b.shape[1]
    n_new = rows // MLA_HEADS
    t_pad = knew_b.shape[1]

    def page_map(p):
        return lambda b, s, pt: (layer, pt[b * n_pages + s * pps + p], 0, 0)

    lat_specs = [pl.BlockSpec((None, None, page, KV_RANK), page_map(p)) for p in range(pps)]
    rope_specs = [pl.BlockSpec((None, None, QK_ROPE_DIM, page), page_map(p)) for p in range(pps)]
    grid_spec = pltpu.PrefetchScalarGridSpec(
        num_scalar_prefetch=1,
        grid=(n_b, n_steps),
        in_specs=[
            pl.BlockSpec((None, rows, QK_PAD), lambda b, s, pt: (b, 0, 0)),
            pl.BlockSpec((None, t_pad, QK_PAD), lambda b, s, pt: (b, 0, 0)),
            pl.BlockSpec((MLA_HEADS, KV_RANK, V_HEAD_DIM), lambda b, s, pt: (0, 0, 0)),
        ] + lat_specs + rope_specs,
        out_specs=pl.BlockSpec((None, n_new, MLA_HEADS * V_HEAD_DIM), lambda b, s, pt: (b, 0, 0)),
        scratch_shapes=[
            pltpu.VMEM((rows, LANES), F32), pltpu.VMEM((rows, LANES), F32), pltpu.VMEM((rows, KV_RANK), F32),
            pltpu.VMEM((rows, V_HEAD_DIM), F32),
        ],
    )
    return pl.pallas_call(
        functools.partial(_attn_sample_kernel, n_pages_step=pps, n_new=n_new),
        grid_spec=grid_spec,
        out_shape=jax.ShapeDtypeStruct((n_b, n_new, MLA_HEADS * V_HEAD_DIM), F32),
        compiler_params=_params("arbitrary", "arbitrary"),
        name="attn_sample",
    )(page_table.reshape(-1), q_b, knew_b, w_uv_h, *([cache_lat] * pps), *([cache_rope] * pps))


def _mem_kv_kernel(mem_ref, w_mk_ref, w_mv_ref, k_ref, v_ref, kb_ref, vb_ref):
    m = mem_ref[...].astype(BF16)
    k = _dot(m, w_mk_ref[...])
    v = _dot(m, w_mv_ref[...])
    k_ref[...] = k
    v_ref[...] = v
    kb_ref[...] = k.astype(BF16)
    vb_ref[...] = v.astype(BF16)


def _mem_kv(mem2d, w_mk, w_mv, n_batch, n_mem):
    blk = pl.BlockSpec((n_mem, MEM_WIDTH), lambda b: (b, 0))
    n = n_batch * n_mem
    return pl.pallas_call(
        _mem_kv_kernel,
        grid=(n_batch,),
        in_specs=[pl.BlockSpec((n_mem, D_MODEL), lambda b: (b, 0)), _full((D_MODEL, MEM_WIDTH)),
                  _full((D_MODEL, MEM_WIDTH))],
        out_specs=[blk, blk, blk, blk],
        out_shape=[jax.ShapeDtypeStruct((n, MEM_WIDTH), F32), jax.ShapeDtypeStruct((n, MEM_WIDTH), F32),
                   jax.ShapeDtypeStruct((n, MEM_WIDTH), BF16), jax.ShapeDtypeStruct((n, MEM_WIDTH), BF16)],
        compiler_params=_params("arbitrary"),
        name="mem_kv",
    )(mem2d, w_mk, w_mv)


def _mix_ln1(x, pool_y, mla_y, w_out_a_ref, w_out_b_ref, g_ref, b_ref):
    mix = _dot(pool_y, w_out_a_ref[...]) + _dot(mla_y, w_out_b_ref[...])
    return _layer_norm(DEEPNORM_ALPHA * x + mix, g_ref[...], b_ref[...])


def _route(x2, w_r_ref, b_r_ref, carry):
    m = x2.shape[0]
    logits = jnp.dot(x2, w_r_ref[...], preferred_element_type=F32, precision=lax.Precision.HIGHEST) + b_r_ref[...]
    e_idx = lax.broadcasted_iota(I32, (m, N_EXPERTS), 1).astype(F32)
    work = logits
    vals, picks = [], []
    for _ in range(TOP_K):
        mx = jnp.max(work, axis=1, keepdims=True)
        pick = jnp.min(jnp.where(work == mx, e_idx, float(N_EXPERTS)), axis=1, keepdims=True)
        vals.append(mx)
        picks.append(pick)
        work = jnp.where(e_idx == pick, -jnp.inf, work)
    hot = jnp.where(work == -jnp.inf, 1.0, 0.0)
    exps = [jnp.exp(v - vals[0]) for v in vals]
    denom = exps[0] + exps[1] + exps[2] + exps[3]
    r_i = lax.broadcasted_iota(I32, (m, m), 0)
    c_i = lax.broadcasted_iota(I32, (m, m), 1)
    strict_lower = jnp.where(c_i < r_i, 1.0, 0.0).astype(BF16)
    before = _dot(strict_lower, hot.astype(BF16)) + carry
    lane4 = lax.broadcasted_iota(I32, (m, TOP_K), 1)
    idx = jnp.zeros((m, TOP_K), I32)
    gates = jnp.zeros((m, TOP_K), F32)
    rank = jnp.zeros((m, TOP_K), I32)
    for k in range(TOP_K):
        rk = jnp.sum(jnp.where(e_idx == picks[k], before, 0.0), axis=1, keepdims=True).astype(I32)
        idx = jnp.where(lane4 == k, picks[k].astype(I32), idx)
        gates = jnp.where(lane4 == k, exps[k] / denom, gates)
        rank = jnp.where(lane4 == k, rk, rank)
    return idx, gates, rank, carry + jnp.sum(hot, axis=0, keepdims=True)


def _mo_ln2_route(x1, o_bf, w_mo_ref, g_ref, b_ref, w_r_ref, b_r_ref, carry):
    x2 = _layer_norm(DEEPNORM_ALPHA * x1 + _dot(o_bf, w_mo_ref[...]), g_ref[...], b_ref[...])
    return (x2,) + _route(x2, w_r_ref, b_r_ref, carry)


def _post_prompt_kernel(x_ref, pooly_ref, mlay_ref, mk_ref, mv_ref,
                        w_out_a_ref, w_out_b_ref, g1_ref, b1_ref, w_mq_ref, w_mo_ref, g2_ref, b2_ref,
                        w_r_ref, b_r_ref,
                        x2_ref, idx_ref, gate_ref, rank_ref, cnt_ref, carry_ref):
    first = jnp.logical_and(pl.program_id(0) == 0, pl.program_id(1) == 0)

    @pl.when(first)
    def _():
        carry_ref[...] = jnp.zeros(carry_ref.shape, F32)

    x1 = _mix_ln1(x_ref[...], pooly_ref[...], mlay_ref[...], w_out_a_ref, w_out_b_ref, g1_ref, b1_ref)
    qm = (_dot(x1.astype(BF16), w_mq_ref[...]) * MEM_SCALE).astype(BF16)
    outs = []
    for hd in range(MEM_HEADS):
        cols = slice(MEM_HEAD_DIM * hd, MEM_HEAD_DIM * (hd + 1))
        s = _dot_nt(qm[:, cols], mk_ref[:, cols])
        p = jnp.exp(s - jnp.max(s, axis=1, keepdims=True))
        o = _dot(p.astype(BF16), mv_ref[:, cols]) / jnp.sum(p, axis=1, keepdims=True)
        outs.append(o.astype(BF16))
    o_all = jnp.concatenate(outs, axis=1)
    x2, idx, gates, rank, carry = _mo_ln2_route(x1, o_all, w_mo_ref, g2_ref, b2_ref, w_r_ref, b_r_ref,
                                                carry_ref[...])
    _store_row_tiles(x2_ref, (), x2)
    idx_ref[...] = idx
    gate_ref[...] = gates
    rank_ref[...] = rank
    carry_ref[...] = carry
    cnt_ref[...] = carry


def _post_weight_specs():
    half = (POOL_WIDTH, D_MODEL)
    return [
        _full(half), _full(half), _full((1, D_MODEL)), _full((1, D_MODEL)),
        _full((D_MODEL, MEM_WIDTH)), _full((MEM_WIDTH, D_MODEL)), _full((1, D_MODEL)), _full((1, D_MODEL)),
        _full((D_MODEL, N_EXPERTS)), _full((1, N_EXPERTS)),
    ]


def _post_prompt(x2d, pool_y, mla_y, mk_b, mv_b, weights, n_batch, seq, n_mem, n_all):
    tm = TOKEN_BLOCK
    nj = seq // tm
    tok = lambda b, j: (b * nj + j, 0)
    return pl.pallas_call(
        _post_prompt_kernel,
        grid=(n_batch, nj),
        in_specs=[
            pl.BlockSpec((tm, D_MODEL), tok), pl.BlockSpec((tm, POOL_WIDTH), tok), pl.BlockSpec((tm, POOL_WIDTH), tok),
            pl.BlockSpec((n_mem, MEM_WIDTH), lambda b, j: (b, 0)), pl.BlockSpec((n_mem, MEM_WIDTH), lambda b, j: (b, 0)),
        ] + _post_weight_specs(),
        out_specs=[
            pl.BlockSpec((tm * ROW_TILE, LANES), tok), pl.BlockSpec((tm, TOP_K), tok), pl.BlockSpec((tm, TOP_K), tok),
            pl.BlockSpec((tm, TOP_K), tok), _full((1, N_EXPERTS)),
        ],
        out_shape=[
            jax.ShapeDtypeStruct((n_all * ROW_TILE, LANES), F32), jax.ShapeDtypeStruct((n_all, TOP_K), I32),
            jax.ShapeDtypeStruct((n_all, TOP_K), F32), jax.ShapeDtypeStruct((n_all, TOP_K), I32),
            jax.ShapeDtypeStruct((1, N_EXPERTS), F32),
        ],
        scratch_shapes=[pltpu.VMEM((1, N_EXPERTS), F32)],
        compiler_params=_params("arbitrary", "arbitrary"),
        name="post_prompt",
    )(x2d, pool_y, mla_y, mk_b, mv_b, *weights)


def _mix_sample_kernel(x_ref, pooly_ref, mlay_ref, w_out_a_ref, w_out_b_ref, g1_ref, b1_ref, w_mq_ref,
                       x1_ref, qm_ref):
    x1 = _mix_ln1(x_ref[...], pooly_ref[...], mlay_ref[...], w_out_a_ref, w_out_b_ref, g1_ref, b1_ref)
    x1_ref[...] = x1
    qm_ref[...] = (_dot(x1.astype(BF16), w_mq_ref[...]) * MEM_SCALE).astype(BF16)


def _mix_sample(x2d, pool_y, mla_y, weights):
    n = x2d.shape[0]
    w_out_a, w_out_b, g1, b1, w_mq = weights[:5]
    return pl.pallas_call(
        _mix_sample_kernel,
        grid=(1,),
        in_specs=[_full((n, D_MODEL)), _full((n, POOL_WIDTH)), _full((n, POOL_WIDTH))] + _post_weight_specs()[:5],
        out_specs=[_full((n, D_MODEL)), _full((n, MEM_WIDTH))],
        out_shape=[jax.ShapeDtypeStruct((n, D_MODEL), F32), jax.ShapeDtypeStruct((n, MEM_WIDTH), BF16)],
        compiler_params=_params("arbitrary"),
        name="mix_sample",
    )(x2d, pool_y, mla_y, w_out_a, w_out_b, g1, b1, w_mq)


def _mem_attn_sample_kernel(q_ref, k_ref, v_ref, out_ref, res_ref, *, n_new):
    for b in range(q_ref.shape[0]):
        s = _dot_nt(q_ref[b], k_ref[b].astype(BF16))
        p = jnp.exp(s - jnp.max(s, axis=1, keepdims=True))
        res_ref[...] = _dot(p.astype(BF16), v_ref[b].astype(BF16)) / jnp.sum(p, axis=1, keepdims=True)
        for hd in range(MEM_HEADS):
            cols = slice(MEM_HEAD_DIM * hd, MEM_HEAD_DIM * (hd + 1))
            out_ref[b, :, cols] = res_ref[hd * n_new:(hd + 1) * n_new, cols].astype(BF16)


def _mem_attn_sample(q_bd, mem_k, mem_v, n_new):
    n_b, rows, _ = q_bd.shape
    n_mem = mem_k.shape[1]
    gb = min(MEM_BATCH_BLOCK, n_b)
    assert n_b % gb == 0
    blk = lambda i: (i, 0, 0)
    return pl.pallas_call(
        functools.partial(_mem_attn_sample_kernel, n_new=n_new),
        grid=(n_b // gb,),
        in_specs=[pl.BlockSpec((gb, rows, MEM_WIDTH), blk), pl.BlockSpec((gb, n_mem, MEM_WIDTH), blk),
                  pl.BlockSpec((gb, n_mem, MEM_WIDTH), blk)],
        out_specs=pl.BlockSpec((gb, n_new, MEM_WIDTH), blk),
        out_shape=jax.ShapeDtypeStruct((n_b, n_new, MEM_WIDTH), BF16),
        scratch_shapes=[pltpu.VMEM((rows, MEM_WIDTH), F32)],
        compiler_params=_params("arbitrary"),
        name="mem_attn_sample",
    )(q_bd, mem_k, mem_v)


def _route_sample_kernel(x1_ref, o_ref, cnt_in_ref, w_mo_ref, g2_ref, b2_ref, w_r_ref, b_r_ref,
                         x2_in, idx_in, gate_in, rank_in,
                         x2_ref, idx_ref, gate_ref, rank_ref, cnt_ref):
    del x2_in, idx_in, gate_in, rank_in
    x2, idx, gates, rank, carry = _mo_ln2_route(x1_ref[...], o_ref[...], w_mo_ref, g2_ref, b2_ref, w_r_ref, b_r_ref,
                                                cnt_in_ref[...])
    _store_row_tiles(x2_ref, (), x2)
    idx_ref[...] = idx
    gate_ref[...] = gates
    rank_ref[...] = rank
    cnt_ref[...] = carry


def _route_sample(x1, o_bf, cnt_in, weights, x2_all, idx_all, gate_all, rank_all, n_prompt):
    n = x1.shape[0]
    assert n_prompt % n == 0
    tail = lambda i: (n_prompt // n, 0)
    anyspec = pl.BlockSpec(memory_space=pl.ANY)
    w_mo, g2, b2, w_r, b_r = weights[5:]
    return pl.pallas_call(
        _route_sample_kernel,
        grid=(1,),
        in_specs=[_full((n, D_MODEL)), _full((n, MEM_WIDTH)), _full((1, N_EXPERTS))] + _post_weight_specs()[5:] + [
            anyspec, anyspec, anyspec, anyspec],
        out_specs=[pl.BlockSpec((n * ROW_TILE, LANES), tail), pl.BlockSpec((n, TOP_K), tail), pl.BlockSpec((n, TOP_K), tail),
                   pl.BlockSpec((n, TOP_K), tail), _full((1, N_EXPERTS))],
        out_shape=[jax.ShapeDtypeStruct(x2_all.shape, F32), jax.ShapeDtypeStruct(idx_all.shape, I32),
                   jax.ShapeDtypeStruct(gate_all.shape, F32), jax.ShapeDtypeStruct(rank_all.shape, I32),
                   jax.ShapeDtypeStruct((1, N_EXPERTS), F32)],
        input_output_aliases={8: 0, 9: 1, 10: 2, 11: 3},
        compiler_params=_params("arbitrary"),
        name="route_sample",
    )(x1, o_bf, cnt_in, w_mo, g2, b2, w_r, b_r, x2_all, idx_all, gate_all, rank_all)


def _moe_kernel(blk_exp_ref, idx_hbm, x_hbm, wgu_ref, bgu_ref, wd_ref, bd_ref, y_hbm,
                idx_smem, xbuf, ybuf, idx_sem, x_sem, y_sem):
    del blk_exp_ref
    i = pl.program_id(0)
    last = pl.num_programs(0) - 1
    tr = xbuf.shape[1] // ROW_TILE
    n_chunks = D_EXPERT // EXPERT_CHUNK
    rows_per_chunk = tr // n_chunks

    def idx_copy(row, s):
        return pltpu.make_async_copy(idx_hbm.at[row], idx_smem.at[s], idx_sem.at[s])

    def gather(s_idx, xs, rows):
        for r in rows:
            src = pl.multiple_of(idx_smem[s_idx, tr + r], ROW_TILE)
            pltpu.make_async_copy(x_hbm.at[pl.ds(src, ROW_TILE)], xbuf.at[xs, pl.ds(ROW_TILE * r, ROW_TILE)],
                                  x_sem.at[xs]).start(priority=r % 2)

    def scatter(s_idx, ys, rows):
        for r in rows:
            dst = pl.multiple_of(idx_smem[s_idx, r], ROW_TILE)
            pltpu.make_async_copy(ybuf.at[ys, pl.ds(ROW_TILE * r, ROW_TILE)], y_hbm.at[pl.ds(dst, ROW_TILE)],
                                  y_sem.at[ys]).start(priority=r % 2)

    def wait_rows(buf, sem, s):
        pltpu.make_async_copy(buf.at[s], buf.at[s], sem.at[s]).wait()

    @pl.when(i == 0)
    def _():
        idx_copy(0, 1).start()
        idx_copy(0, 1).wait()
        gather(1, 0, range(tr))
        idx_copy(1, 0).start()
        ybuf[1] = jnp.zeros(ybuf.shape[1:], F32)

    def step(slot):
        nxt = 1 - slot
        idx_copy(i + 1, slot).wait()
        idx_copy(i + 2, nxt).start()
        wait_rows(xbuf, x_sem, slot)
        xb = _load_row_tiles(xbuf, (slot,), tr).astype(BF16)
        acc = jnp.zeros((tr, D_MODEL), F32) + bd_ref[...]

        groups = 3 * n_chunks
        per_group = -(-2 * tr // groups)

        def issue(g):
            for n in range(g * per_group, min((g + 1) * per_group, 2 * tr)):
                if n < tr:
                    gather(slot, nxt, (n,))
                else:
                    scatter(slot, nxt, (n - tr,))

        for c in range(n_chunks):
            cg = slice(c * EXPERT_CHUNK, (c + 1) * EXPERT_CHUNK)
            cu = slice(D_EXPERT + c * EXPERT_CHUNK, D_EXPERT + (c + 1) * EXPERT_CHUNK)
            gate = jnp.minimum(_dot(xb, wgu_ref[:, cg]) + bgu_ref[:, cg], SWIGLU_LIMIT)
            issue(3 * c)
            up = jnp.clip(_dot(xb, wgu_ref[:, cu]) + bgu_ref[:, cu], -SWIGLU_LIMIT, SWIGLU_LIMIT)
            issue(3 * c + 1)
            act = gate * (1.0 / (1.0 + jnp.exp(-SWIGLU_ALPHA * gate))) * (up + 1.0)
            acc = acc + _dot(act.astype(BF16), wd_ref[cg, :])
            issue(3 * c + 2)

        @pl.when(i >= 1)
        def _():
            wait_rows(ybuf, y_sem, slot)

        _store_row_tiles(ybuf, (slot,), acc)

        @pl.when(i == last)
        def _():
            idx_copy(i + 2, nxt).wait()
            wait_rows(xbuf, x_sem, nxt)
            wait_rows(ybuf, y_sem, nxt)

    for parity in range(2):
        pl.when(i % 2 == parity)(functools.partial(step, parity))


def _moe(blk_exp, idx_rows, x_all, w_gu, b_gu, w_d, b_d, n_assign):
    tr = EXPERT_ROWS
    n_blocks = blk_exp.shape[0]
    assert n_blocks >= 2 and idx_rows.shape == (n_blocks + 2, 2 * tr) and tr % (D_EXPERT // EXPERT_CHUNK) == 0
    anyspec = pl.BlockSpec(memory_space=pl.ANY)
    emap3 = lambda i, be: (be[i], 0, 0)
    grid_spec = pltpu.PrefetchScalarGridSpec(
        num_scalar_prefetch=1,
        grid=(n_blocks,),
        in_specs=[
            anyspec, anyspec,
            pl.BlockSpec((None, D_MODEL, 2 * D_EXPERT), emap3), pl.BlockSpec((None, 1, 2 * D_EXPERT), emap3),
            pl.BlockSpec((None, D_EXPERT, D_MODEL), emap3), pl.BlockSpec((None, 1, D_MODEL), emap3),
        ],
        out_specs=anyspec,
        scratch_shapes=[
            pltpu.SMEM((2, 2 * tr), I32),
            pltpu.VMEM((2, tr * ROW_TILE, LANES), F32), pltpu.VMEM((2, tr * ROW_TILE, LANES), F32),
            pltpu.SemaphoreType.DMA((2,)), pltpu.SemaphoreType.DMA((2,)), pltpu.SemaphoreType.DMA((2,)),
        ],
    )
    return pl.pallas_call(
        _moe_kernel,
        grid_spec=grid_spec,
        out_shape=jax.ShapeDtypeStruct(((n_assign + 2 * tr) * ROW_TILE, LANES), F32),
        compiler_params=_params("arbitrary"),
        name="moe_experts",
    )(blk_exp, idx_rows, x_all, w_gu, b_gu, w_d, b_d)


def _combine_kernel(x2_ref, gate_ref, g3_ref, b3_ref, *rest):
    y_refs, out_ref = rest[:TOP_K], rest[TOP_K]
    gates = gate_ref[...]
    tm = out_ref.shape[0]
    moe = jnp.zeros(out_ref.shape, F32)
    for k in range(TOP_K):
        moe = moe + gates[:, k:k + 1] * _load_row_tiles(y_refs[k], (), tm)
    out_ref[...] = _layer_norm(DEEPNORM_ALPHA * _load_row_tiles(x2_ref, (), tm) + moe, g3_ref[...], b3_ref[...])


def _combine(x2_all, y_tok, gates_all, g3, b3, row0, n_rows, tm):
    n_all = gates_all.shape[0]
    assert row0 % tm == 0 and n_rows % tm == 0 and n_all % tm == 0
    off = row0 // tm
    tok = lambda i: (off + i, 0)
    tiles = (tm * ROW_TILE, LANES)
    slot_specs = [pl.BlockSpec(tiles, functools.partial(lambda i, k: (k * (n_all // tm) + off + i, 0), k=k))
                  for k in range(TOP_K)]
    return pl.pallas_call(
        _combine_kernel,
        grid=(n_rows // tm,),
        in_specs=[pl.BlockSpec(tiles, tok), pl.BlockSpec((tm, TOP_K), tok), _full((1, D_MODEL)),
                  _full((1, D_MODEL))] + slot_specs,
        out_specs=pl.BlockSpec((tm, D_MODEL), lambda i: (i, 0)),
        out_shape=jax.ShapeDtypeStruct((n_rows, D_MODEL), F32),
        compiler_params=_params("arbitrary"),
        name="combine_ln3",
    )(x2_all, gates_all, g3, b3, *([y_tok] * TOP_K))


def _rope_tables(pos):
    inv_freq = ROPE_THETA ** (-jnp.arange(ROPE_HALF, dtype=F32) / ROPE_HALF)
    ang = pos.astype(F32)[:, None] * inv_freq[None, :]
    c, s, z = jnp.cos(ang), jnp.sin(ang), jnp.zeros_like(ang)
    return jnp.concatenate([c, z, c, z], axis=1), jnp.concatenate([-s, z, s, z], axis=1)


def _pad_rope_cols(w):
    z = jnp.zeros(w.shape[:-1] + (ROPE_HALF,), w.dtype)
    return jnp.concatenate([w[..., :ROPE_HALF], z, w[..., ROPE_HALF:], z], axis=-1)


def _layer_weights(l, w_in, w_pool, pool_scale, q_norm_g, w_uq, kv_norm_g, w_uk, w_uv, w_out, ln1_g, ln1_b,
                   w_mq, w_mo, ln2_g, ln2_b, w_router, b_router):
    o3 = POOL_WIDTH + Q_RANK + KV_RANK
    w_in_p = jnp.concatenate([w_in[l][:, :o3], _pad_rope_cols(w_in[l][:, o3:])], axis=1).astype(BF16)
    uq = w_uq[l].reshape(Q_RANK, MLA_HEADS, QK_NOPE_DIM + QK_ROPE_DIM)
    uq_p = jnp.concatenate([uq[..., :QK_NOPE_DIM], _pad_rope_cols(uq[..., QK_NOPE_DIM:])], axis=-1)
    uq_p = uq_p.reshape(Q_RANK, MLA_HEADS * QK_PAD).astype(BF16)
    w_ukt = jnp.transpose(w_uk[l], (1, 2, 0)).astype(BF16)
    w_uv_h = jnp.transpose(w_uv[l], (1, 0, 2)).astype(BF16)
    row = lambda v: v[l].reshape(1, -1)
    proj = [w_in_p, row(q_norm_g), uq_p, row(kv_norm_g), w_ukt, w_pool[l].astype(BF16), row(pool_scale)]
    post = [w_out[l][:POOL_WIDTH].astype(BF16), w_out[l][POOL_WIDTH:].astype(BF16), row(ln1_g), row(ln1_b),
            w_mq[l].astype(BF16), w_mo[l].astype(BF16), row(ln2_g), row(ln2_b), w_router[l], row(b_router)]
    return proj, w_uv_h, post


def _routing_tables(idx_all, rank_all, counts, n_blocks):
    tr = EXPERT_ROWS
    cnt = counts.reshape(-1).astype(I32)
    padded = ((cnt + tr - 1) // tr) * tr
    pend = jnp.cumsum(padded)
    pstart = pend - padded
    dest = (pstart[idx_all] + rank_all).reshape(-1)
    n_all = idx_all.shape[0]
    m = dest.shape[0]
    n_rows = n_blocks * tr
    assign = (jnp.arange(TOP_K, dtype=I32)[None, :] * n_all + jnp.arange(n_all, dtype=I32)[:, None]).reshape(-1)
    row_src = jnp.full((n_rows,), -1, I32).at[dest].set(assign, unique_indices=True)
    spare = m + jnp.arange(n_rows, dtype=I32) % (2 * tr)
    dst_rows = (jnp.where(row_src >= 0, row_src, spare) * ROW_TILE).reshape(n_blocks, tr)
    src_tok = ((jnp.maximum(row_src, 0) % n_all) * ROW_TILE).reshape(n_blocks, tr)
    spare_row = spare[:tr][None] * ROW_TILE
    idx_rows = jnp.concatenate([jnp.concatenate([spare_row, spare_row, dst_rows, spare_row], axis=0),
                                jnp.concatenate([src_tok, jnp.zeros((3, tr), I32)], axis=0)], axis=1)
    n_used = pend[-1] // tr
    blk = jnp.minimum(jnp.arange(n_blocks + 1, dtype=I32), n_used - 1)
    blk_exp = jnp.sum((pend[None, :] <= (blk * tr)[:, None]).astype(I32), axis=1)
    return jnp.clip(blk_exp, 0, N_EXPERTS - 1), idx_rows


def kernel(x_prompt, x_sample, mem_prompt, cache_kv_latent, cache_k_rope, cache_mem_k, cache_mem_v, state_pool,
           page_table, w_in, w_pool, pool_scale, q_norm_g, w_uq, kv_norm_g, w_uk, w_uv, w_out, ln1_g, ln1_b,
           w_mq, w_mk, w_mv, w_mo, ln2_g, ln2_b, w_router, b_router, w_gate_up, b_gate_up, w_down, b_down,
           ln3_g, ln3_b):
    n_bp, seq, _ = x_prompt.shape
    n_bs, n_new, _ = x_sample.shape
    n_mem = mem_prompt.shape[1]
    page = cache_kv_latent.shape[2]
    past_len = page_table.shape[1] * page
    n_p = n_bp * seq
    n_s = n_bs * n_new
    n_all = n_p + n_s
    n_assign = n_all * TOP_K
    assert w_in.shape[0] == DEPTH == 1
    assert seq % TOKEN_BLOCK == 0 and seq % ATTN_K_BLOCK == 0 and seq % ATTN_Q_BLOCK == 0 and seq >= POOL_HALO
    assert n_p % n_s == 0 and (n_s % TOKEN_BLOCK == 0 or TOKEN_BLOCK % n_s == 0)
    l = 0

    proj_w, w_uv_h, post_w = _layer_weights(l, w_in, w_pool, pool_scale, q_norm_g, w_uq, kv_norm_g, w_uk, w_uv,
                                            w_out, ln1_g, ln1_b, w_mq, w_mo, ln2_g, ln2_b, w_router, b_router)

    xp = x_prompt.reshape(n_p, D_MODEL)
    cos_p, sin_p = _rope_tables(jnp.arange(seq))
    qcat, kcat, lat_p, kr_p, pooly_p, pstate_p = _proj_prompt(xp, proj_w, cos_p, sin_p, n_bp, seq)
    mlay_p = _attn_prompt(qcat, kcat, w_uv_h, n_bp, seq)
    mk, mv, mk_b, mv_b = _mem_kv(mem_prompt.reshape(n_bp * n_mem, D_MODEL), w_mk[l].astype(BF16),
                                 w_mv[l].astype(BF16), n_bp, n_mem)
    x2_all, idx_all, gate_all, rank_all, cnt_p = _post_prompt(xp, pooly_p, mlay_p, mk_b, mv_b, post_w,
                                                              n_bp, seq, n_mem, n_all)

    xs = jnp.transpose(x_sample, (1, 0, 2)).reshape(n_s, D_MODEL)
    state_t = jnp.transpose(state_pool[l], (1, 0, 2))
    cos_s, sin_s = _rope_tables(jnp.repeat(past_len + jnp.arange(n_new), n_bs))
    qcat_s, kcat_s, lat_s, kr_s, pooly_s, pstate_s = _proj_sample(xs, state_t, proj_w, cos_s, sin_s, past_len)
    q_b = jnp.transpose(qcat_s.reshape(MLA_HEADS, n_new, n_bs, QK_PAD), (2, 0, 1, 3)).reshape(
        n_bs, MLA_HEADS * n_new, QK_PAD)
    knew_b = jnp.transpose(kcat_s.reshape(n_new, n_bs, QK_PAD), (1, 0, 2))
    knew_b = jnp.pad(knew_b, ((0, 0), (0, LANES - n_new), (0, 0)))
    mlay_b = _attn_sample(page_table, q_b, knew_b, w_uv_h, cache_kv_latent,
                          jnp.transpose(cache_k_rope, (0, 1, 3, 2)), l)
    mlay_s = jnp.transpose(mlay_b, (1, 0, 2)).reshape(n_s, POOL_WIDTH).astype(BF16)
    x1_s, qm_s = _mix_sample(xs, pooly_s, mlay_s, post_w)
    qm_b = jnp.transpose(qm_s.reshape(n_new, n_bs, MEM_WIDTH), (1, 0, 2))
    head_of_col = jnp.arange(MEM_WIDTH) // MEM_HEAD_DIM
    head_mask = (head_of_col[None, :] == jnp.arange(MEM_HEADS)[:, None]).astype(BF16)
    q_bd = (qm_b[:, None, :, :] * head_mask[None, :, None, :]).reshape(n_bs, MEM_HEADS * n_new, MEM_WIDTH)
    o_b = _mem_attn_sample(q_bd, cache_mem_k[l].reshape(n_bs, n_mem, MEM_WIDTH),
                           cache_mem_v[l].reshape(n_bs, n_mem, MEM_WIDTH), n_new)
    o_s = jnp.transpose(o_b, (1, 0, 2)).reshape(n_s, MEM_WIDTH)
    x2_all, idx_all, gate_all, rank_all, counts = _route_sample(x1_s, o_s, cnt_p, post_w, x2_all, idx_all, gate_all,
                                                               rank_all, n_p)

    n_blocks = -(-n_assign // EXPERT_ROWS) + N_EXPERTS + 1
    blk_exp, idx_rows = _routing_tables(idx_all, rank_all, counts, n_blocks)
    y_tok = _moe(blk_exp, idx_rows, x2_all, w_gate_up[l].astype(BF16), b_gate_up[l][:, None, :],
                 w_down[l].astype(BF16), b_down[l][:, None, :], n_assign)
    g3, b3 = ln3_g[l].reshape(1, -1), ln3_b[l].reshape(1, -1)
    y_p = _combine(x2_all, y_tok, gate_all, g3, b3, 0, n_p, TOKEN_BLOCK)
    y_s = _combine(x2_all, y_tok, gate_all, g3, b3, n_p, n_s, min(TOKEN_BLOCK, n_s))

    def from_steps(a, width):
        return jnp.transpose(a.reshape(n_new, n_bs, width), (1, 0, 2))[None]

    return (
        y_p.reshape(n_bp, seq, D_MODEL),
        from_steps(y_s, D_MODEL)[0],
        lat_p.reshape(1, n_bp, seq, KV_RANK),
        kr_p.reshape(1, n_bp, seq, QK_ROPE_DIM),
        pstate_p[None, :, POOL_HALO - POOL_BUF:, :],
        mk.reshape(1, n_bp, n_mem, MEM_HEADS, MEM_HEAD_DIM),
        mv.reshape(1, n_bp, n_mem, MEM_HEADS, MEM_HEAD_DIM),
        from_steps(lat_s, KV_RANK),
        from_steps(kr_s, QK_ROPE_DIM),
        jnp.transpose(pstate_s, (1, 0, 2))[None],
    )
```

```python
import functools

import jax
import jax.numpy as jnp
from jax import lax
from jax.experimental import pallas as pl
from jax.experimental.pallas import tpu as pltpu

F32 = jnp.float32
BF16 = jnp.bfloat16
I32 = jnp.int32

D_MODEL = 1024
POOL_WIDTH = 512
POOL_WINDOWS = (2, 4, 8, 16)
POOL_GROUP_DIM = 128
POOL_BUF = 15
POOL_HALO = 16
MLA_HEADS = 4
QK_NOPE_DIM = 128
QK_ROPE_DIM = 64
ROPE_HALF = 32
V_HEAD_DIM = 128
Q_RANK = 256
KV_RANK = 128
ROPE_THETA = 10000.0
QK_PAD = 256
ONES_LANE = QK_PAD - 1
MEM_HEADS = 4
MEM_HEAD_DIM = 128
MEM_WIDTH = 512
N_EXPERTS = 32
TOP_K = 4
D_EXPERT = 1024
SWIGLU_LIMIT = 7.0
SWIGLU_ALPHA = 1.702
LN_EPS = 1e-5
RMS_EPS = 1e-6
DEPTH = 1
DEEPNORM_ALPHA = (2.0 * DEPTH) ** 0.25
ATTN_SCALE = (QK_NOPE_DIM + QK_ROPE_DIM) ** -0.5
MEM_SCALE = MEM_HEAD_DIM ** -0.5

LANES = 128
VMEM_LIMIT = 48 * 1024 * 1024

TOKEN_BLOCK = 256
ATTN_Q_BLOCK = 256
ATTN_K_BLOCK = 512
ATTN_GROUPS = 2
PAGES_PER_STEP = 32
MEM_BATCH_BLOCK = 8
DISPATCH_BLOCK = 256
EXPERT_ROWS = 256
EXPERT_CHUNK = 256


def _dot(a, b):
    return jnp.dot(a, b, preferred_element_type=F32)


def _dot_nt(a, b):
    return lax.dot_general(a, b, (((1,), (1,)), ((), ())), preferred_element_type=F32)


def _rms(x, g):
    return x * lax.rsqrt(jnp.mean(x * x, axis=-1, keepdims=True) + RMS_EPS) * g


def _layer_norm(x, g, b):
    mu = jnp.mean(x, axis=-1, keepdims=True)
    xc = x - mu
    var = jnp.mean(xc * xc, axis=-1, keepdims=True)
    return xc * lax.rsqrt(var + LN_EPS) * g + b


ROW_TILE = D_MODEL // LANES


def _store_row_tiles(ref, lead, val):
    m = val.shape[0]
    for j in range(ROW_TILE):
        ref[(*lead, pl.ds(j, m, stride=ROW_TILE), slice(None))] = val[:, LANES * j:LANES * (j + 1)]


def _load_row_tiles(ref, lead, m):
    return jnp.concatenate([ref[(*lead, pl.ds(j, m, stride=ROW_TILE), slice(None))] for j in range(ROW_TILE)], axis=1)


def _params(*semantics):
    return pltpu.CompilerParams(dimension_semantics=semantics, vmem_limit_bytes=VMEM_LIMIT)


def _full(shape):
    n = len(shape)
    return pl.BlockSpec(shape, lambda *_: (0,) * n)


def _project(x_bf, w_in_ref, qg_ref, w_uq_ref, kvg_ref, w_ukt_ref, cos, sin):
    h = _dot(x_bf, w_in_ref[...])
    u = h[:, :POOL_WIDTH]
    qn = _rms(h[:, POOL_WIDTH:POOL_WIDTH + Q_RANK], qg_ref[...])
    q = _dot(qn.astype(BF16), w_uq_ref[...])
    q_lat, q_rope = [], []
    for hd in range(MLA_HEADS):
        nope = q[:, QK_PAD * hd:QK_PAD * hd + QK_NOPE_DIM]
        rp = q[:, QK_PAD * hd + QK_NOPE_DIM:QK_PAD * (hd + 1)]
        q_lat.append(_dot(nope.astype(BF16), w_ukt_ref[hd]) * ATTN_SCALE)
        q_rope.append((rp * cos + pltpu.roll(rp, 64, 1) * sin) * ATTN_SCALE)
    o2 = POOL_WIDTH + Q_RANK
    lat = _rms(h[:, o2:o2 + KV_RANK], kvg_ref[...])
    kp = h[:, o2 + KV_RANK:]
    k_rope = kp * cos + pltpu.roll(kp, 64, 1) * sin
    return u, q_lat, q_rope, lat, k_rope


def _compact_rope(r):
    lane = lax.broadcasted_iota(I32, r.shape, 1)
    moved = pltpu.roll(r, 96, 1)
    return jnp.where(lane < ROPE_HALF, r, jnp.where(lane < QK_ROPE_DIM, moved, 0.0))


def _with_ones_lane(r):
    lane = lax.broadcasted_iota(I32, r.shape, 1)
    return jnp.where(lane == ONES_LANE - KV_RANK, 1.0, r)


def _pool_group_out(diff, g, w_pool_ref, pscale_ref):
    cols = slice(POOL_GROUP_DIM * g, POOL_GROUP_DIM * (g + 1))
    return _dot(diff.astype(BF16), w_pool_ref[g]) * pscale_ref[:, cols]


def _proj_prompt_kernel(x_ref, w_in_ref, qg_ref, w_uq_ref, kvg_ref, w_ukt_ref, w_pool_ref, pscale_ref,
                        cos_ref, sin_ref,
                        qcat_ref, kcat_ref, lat_ref, kr_ref, pooly_ref, pstate_ref, ubuf):
    j = pl.program_id(1)
    tm = x_ref.shape[0]

    @pl.when(j == 0)
    def _():
        ubuf[0:POOL_HALO, :] = jnp.zeros((POOL_HALO, POOL_WIDTH), F32)

    u, q_lat, q_rope, lat, k_rope = _project(
        x_ref[...].astype(BF16), w_in_ref, qg_ref, w_uq_ref, kvg_ref, w_ukt_ref, cos_ref[...], sin_ref[...])
    for hd in range(MLA_HEADS):
        qcat_ref[hd, :, 0:KV_RANK] = q_lat[hd].astype(BF16)
        qcat_ref[hd, :, KV_RANK:QK_PAD] = q_rope[hd].astype(BF16)
    lat_ref[...] = lat
    kcat_ref[:, 0:KV_RANK] = lat.astype(BF16)
    kcat_ref[:, KV_RANK:QK_PAD] = _with_ones_lane(k_rope).astype(BF16)
    kr_ref[...] = _compact_rope(k_rope)[:, :QK_ROPE_DIM]

    ubuf[POOL_HALO:POOL_HALO + tm, :] = u
    pos = j * tm + lax.broadcasted_iota(I32, (tm, 1), 0)
    for g, w in enumerate(POOL_WINDOWS):
        cols = slice(POOL_GROUP_DIM * g, POOL_GROUP_DIM * (g + 1))
        ug = u[:, cols]
        ssum = ug
        for k in range(1, w):
            ssum = ssum + ubuf[POOL_HALO - k:POOL_HALO - k + tm, cols]
        count = jnp.minimum(w, pos + 1).astype(F32)
        pooly_ref[:, cols] = _pool_group_out(ssum / count - ug, g, w_pool_ref, pscale_ref).astype(BF16)
    tail = ubuf[tm:tm + POOL_HALO, :]
    ubuf[0:POOL_HALO, :] = tail

    @pl.when(j == pl.num_programs(1) - 1)
    def _():
        pstate_ref[...] = tail


def _proj_sample_kernel(x_ref, state_ref, w_in_ref, qg_ref, w_uq_ref, kvg_ref, w_ukt_ref, w_pool_ref, pscale_ref,
                        cos_ref, sin_ref,
                        qcat_ref, kcat_ref, lat_ref, kr_ref, pooly_ref, pstate_ref, *, past_len):
    n_b = state_ref.shape[1]
    n_t = x_ref.shape[0] // n_b
    u, q_lat, q_rope, lat, k_rope = _project(
        x_ref[...].astype(BF16), w_in_ref, qg_ref, w_uq_ref, kvg_ref, w_ukt_ref, cos_ref[...], sin_ref[...])
    for hd in range(MLA_HEADS):
        qcat_ref[hd, :, 0:KV_RANK] = q_lat[hd].astype(BF16)
        qcat_ref[hd, :, KV_RANK:QK_PAD] = _compact_rope(q_rope[hd]).astype(BF16)
    lat_ref[...] = lat
    kc = _compact_rope(k_rope)
    kcat_ref[:, 0:KV_RANK] = lat.astype(BF16)
    kcat_ref[:, KV_RANK:QK_PAD] = _with_ones_lane(kc).astype(BF16)
    kr_ref[...] = kc[:, :QK_ROPE_DIM]

    def ext(jj):
        if jj < POOL_BUF:
            return state_ref[jj]
        return u[(jj - POOL_BUF) * n_b:(jj - POOL_BUF + 1) * n_b, :]

    for t in range(n_t):
        ut = ext(POOL_BUF + t)
        for g, w in enumerate(POOL_WINDOWS):
            cols = slice(POOL_GROUP_DIM * g, POOL_GROUP_DIM * (g + 1))
            ssum = ut[:, cols]
            for k in range(1, w):
                ssum = ssum + ext(POOL_BUF + t - k)[:, cols]
            count = float(min(w, past_len + t + 1))
            y = _pool_group_out(ssum / count - ut[:, cols], g, w_pool_ref, pscale_ref)
            pooly_ref[t * n_b:(t + 1) * n_b, cols] = y.astype(BF16)
    for jj in range(POOL_BUF):
        pstate_ref[jj] = ext(n_t + jj)


def _proj_weight_specs():
    return [
        _full((D_MODEL, D_MODEL)),
        _full((1, Q_RANK)),
        _full((Q_RANK, MLA_HEADS * QK_PAD)),
        _full((1, KV_RANK)),
        _full((MLA_HEADS, QK_NOPE_DIM, KV_RANK)),
        _full((len(POOL_WINDOWS), POOL_GROUP_DIM, POOL_GROUP_DIM)),
        _full((1, POOL_WIDTH)),
    ]


def _proj_prompt(x2d, weights, cos_t, sin_t, n_batch, seq):
    tm = TOKEN_BLOCK
    n = n_batch * seq
    nj = seq // tm
    tok = lambda b, j: (b * nj + j, 0)
    return pl.pallas_call(
        _proj_prompt_kernel,
        grid=(n_batch, nj),
        in_specs=[pl.BlockSpec((tm, D_MODEL), tok)] + _proj_weight_specs() + [
            pl.BlockSpec((tm, LANES), lambda b, j: (j, 0)),
            pl.BlockSpec((tm, LANES), lambda b, j: (j, 0)),
        ],
        out_specs=[
            pl.BlockSpec((MLA_HEADS, tm, QK_PAD), lambda b, j: (0, b * nj + j, 0)),
            pl.BlockSpec((tm, QK_PAD), tok),
            pl.BlockSpec((tm, KV_RANK), tok),
            pl.BlockSpec((tm, QK_ROPE_DIM), tok),
            pl.BlockSpec((tm, POOL_WIDTH), tok),
            pl.BlockSpec((None, POOL_HALO, POOL_WIDTH), lambda b, j: (b, 0, 0)),
        ],
        out_shape=[
            jax.ShapeDtypeStruct((MLA_HEADS, n, QK_PAD), BF16),
            jax.ShapeDtypeStruct((n, QK_PAD), BF16),
            jax.ShapeDtypeStruct((n, KV_RANK), F32),
            jax.ShapeDtypeStruct((n, QK_ROPE_DIM), F32),
            jax.ShapeDtypeStruct((n, POOL_WIDTH), BF16),
            jax.ShapeDtypeStruct((n_batch, POOL_HALO, POOL_WIDTH), F32),
        ],
        scratch_shapes=[pltpu.VMEM((POOL_HALO + tm, POOL_WIDTH), F32)],
        compiler_params=_params("arbitrary", "arbitrary"),
        name="proj_prompt",
    )(x2d, *weights, cos_t, sin_t)


def _proj_sample(x2d, state_t, weights, cos_t, sin_t, past_len):
    n = x2d.shape[0]
    n_b = state_t.shape[1]
    return pl.pallas_call(
        functools.partial(_proj_sample_kernel, past_len=past_len),
        grid=(1,),
        in_specs=[_full((n, D_MODEL)), _full((POOL_BUF, n_b, POOL_WIDTH))] + _proj_weight_specs() + [
            _full((n, LANES)), _full((n, LANES))],
        out_specs=[
            _full((MLA_HEADS, n, QK_PAD)), _full((n, QK_PAD)), _full((n, KV_RANK)), _full((n, QK_ROPE_DIM)),
            _full((n, POOL_WIDTH)), _full((POOL_BUF, n_b, POOL_WIDTH)),
        ],
        out_shape=[
            jax.ShapeDtypeStruct((MLA_HEADS, n, QK_PAD), BF16),
            jax.ShapeDtypeStruct((n, QK_PAD), BF16),
            jax.ShapeDtypeStruct((n, KV_RANK), F32),
            jax.ShapeDtypeStruct((n, QK_ROPE_DIM), F32),
            jax.ShapeDtypeStruct((n, POOL_WIDTH), BF16),
            jax.ShapeDtypeStruct((POOL_BUF, n_b, POOL_WIDTH), F32),
        ],
        compiler_params=_params("arbitrary"),
        name="proj_sample",
    )(x2d, state_t, *weights, cos_t, sin_t)


def _attn_block(q, k, mask, m_ref, acc_ref):
    s = _dot_nt(q, k)
    if mask is not None:
        s = jnp.where(mask, s, -jnp.inf)
    tiles = [s[:, c * LANES:(c + 1) * LANES] for c in range(s.shape[1] // LANES)]
    m_old = m_ref[...]
    m_new = jnp.maximum(m_old, jnp.max(functools.reduce(jnp.maximum, tiles), axis=1, keepdims=True))
    alpha = jnp.exp(m_old - m_new)
    p = jnp.concatenate([jnp.exp(t - m_new) for t in tiles], axis=1).astype(BF16)
    acc_ref[...] = jnp.concatenate([alpha, alpha], axis=1) * acc_ref[...] + _dot(p, k)
    m_ref[...] = m_new


def _attn_output(acc):
    return acc[:, :KV_RANK] / acc[:, ONES_LANE:ONES_LANE + 1]


def _attn_prompt_kernel(q_ref, k_ref, w_uv_ref, out_ref, m_ref, acc_ref):
    i = pl.program_id(1)
    tq = q_ref.shape[1]
    tk = ATTN_K_BLOCK
    hpg = MLA_HEADS // ATTN_GROUPS
    rows = hpg * tq
    m_ref[...] = jnp.full(m_ref.shape, -jnp.inf, F32)
    acc_ref[...] = jnp.zeros(acc_ref.shape, F32)

    def block(jk, masked):
        k = k_ref[pl.ds(pl.multiple_of(jk * tk, tk), tk), :]
        mask = None
        if masked:
            q_pos = i * tq + lax.broadcasted_iota(I32, (rows, tk), 0) % tq
            k_pos = jk * tk + lax.broadcasted_iota(I32, (rows, tk), 1)
            mask = k_pos <= q_pos
        for g in range(ATTN_GROUPS):
            q = q_ref[hpg * g:hpg * (g + 1)].reshape(rows, QK_PAD)
            _attn_block(q, k, mask, m_ref.at[g], acc_ref.at[g])

    n_full = (i * tq + 1) // tk
    n_kv = (i * tq + tq - 1) // tk + 1

    def full_body(jk, c):
        block(jk, False)
        return c

    def diag_body(jk, c):
        block(jk, True)
        return c

    lax.fori_loop(0, n_full, full_body, 0)
    lax.fori_loop(n_full, n_kv, diag_body, 0)

    for g in range(ATTN_GROUPS):
        o = _attn_output(acc_ref[g])
        for hl in range(hpg):
            hd = hpg * g + hl
            oh = o[hl * tq:(hl + 1) * tq, :].astype(BF16)
            out_ref[:, V_HEAD_DIM * hd:V_HEAD_DIM * (hd + 1)] = _dot(oh, w_uv_ref[hd]).astype(BF16)


def _attn_prompt(qcat, kcat, w_uv_h, n_batch, seq):
    tq = ATTN_Q_BLOCK
    nq = seq // tq
    n = n_batch * seq
    rows = (MLA_HEADS // ATTN_GROUPS) * tq
    return pl.pallas_call(
        _attn_prompt_kernel,
        grid=(n_batch, nq),
        in_specs=[
            pl.BlockSpec((MLA_HEADS, tq, QK_PAD), lambda b, i: (0, b * nq + i, 0)),
            pl.BlockSpec((seq, QK_PAD), lambda b, i: (b, 0)),
            _full((MLA_HEADS, KV_RANK, V_HEAD_DIM)),
        ],
        out_specs=pl.BlockSpec((tq, MLA_HEADS * V_HEAD_DIM), lambda b, i: (b * nq + i, 0)),
        out_shape=jax.ShapeDtypeStruct((n, MLA_HEADS * V_HEAD_DIM), BF16),
        scratch_shapes=[pltpu.VMEM((ATTN_GROUPS, rows, LANES), F32), pltpu.VMEM((ATTN_GROUPS, rows, QK_PAD), F32)],
        compiler_params=_params("arbitrary", "arbitrary"),
        name="attn_prompt",
    )(qcat, kcat, w_uv_h)


def _attn_sample_kernel(pt_ref, q_ref, knew_ref, w_uv_ref, *rest, n_pages_step, n_new):
    lat_refs = rest[:n_pages_step]
    rope_refs = rest[n_pages_step:2 * n_pages_step]
    out_ref, m_ref, l_ref, acc_ref, res_ref = rest[2 * n_pages_step:]
    del pt_ref
    s_idx = pl.program_id(1)

    @pl.when(s_idx == 0)
    def _():
        m_ref[...] = jnp.full(m_ref.shape, -jnp.inf, F32)
        l_ref[...] = jnp.zeros(l_ref.shape, F32)
        acc_ref[...] = jnp.zeros(acc_ref.shape, F32)

    def update(s_tiles, values):
        m_old = m_ref[...]
        m_new = jnp.maximum(m_old, jnp.max(functools.reduce(jnp.maximum, s_tiles), axis=1, keepdims=True))
        alpha = jnp.exp(m_old - m_new)
        p_tiles = [jnp.exp(t - m_new) for t in s_tiles]
        row_sum = jnp.sum(functools.reduce(jnp.add, p_tiles), axis=1, keepdims=True)
        pv = functools.reduce(jnp.add, [_dot(p.astype(BF16), v) for p, v in zip(p_tiles, values)])
        l_ref[...] = alpha * l_ref[...] + row_sum
        acc_ref[...] = alpha * acc_ref[...] + pv
        m_ref[...] = m_new

    q = q_ref[...]
    q_lat = q[:, :KV_RANK]
    q_rope = q[:, KV_RANK:KV_RANK + QK_ROPE_DIM]
    lats = [lat_refs[p][...].astype(BF16) for p in range(n_pages_step)]
    update([_dot_nt(q_lat, lats[p]) + _dot(q_rope, rope_refs[p][...].astype(BF16)) for p in range(n_pages_step)],
           lats)

    @pl.when(s_idx == pl.num_programs(1) - 1)
    def _():
        kn = knew_ref[...]
        q_t = lax.broadcasted_iota(I32, (q.shape[0], kn.shape[0]), 0) % n_new
        k_t = lax.broadcasted_iota(I32, (q.shape[0], kn.shape[0]), 1)
        update([jnp.where(k_t <= q_t, _dot_nt(q, kn), -jnp.inf)], [kn[:, :KV_RANK]])
        o = (acc_ref[...] / l_ref[...]).astype(BF16)
        for hd in range(MLA_HEADS):
            res_ref[...] = _dot(o, w_uv_ref[hd])
            out_ref[:, V_HEAD_DIM * hd:V_HEAD_DIM * (hd + 1)] = res_ref[hd * n_new:(hd + 1) * n_new, :]


def _attn_sample(page_table, q_b, knew_b, w_uv_h, cache_lat, cache_rope, layer):
    n_b, n_pages = page_table.shape
    page = cache_lat.shape[2]
    pps = min(PAGES_PER_STEP, n_pages)
    assert n_pages % pps == 0
    n_steps = n_pages // pps
    rows = q_b.shape[1]
    n_new = rows // MLA_HEADS
    t_pad = knew_b.shape[1]

    def page_map(p):
        return lambda b, s, pt: (layer, pt[b * n_pages + s * pps + p], 0, 0)

    lat_specs = [pl.BlockSpec((None, None, page, KV_RANK), page_map(p)) for p in range(pps)]
    rope_specs = [pl.BlockSpec((None, None, QK_ROPE_DIM, page), page_map(p)) for p in range(pps)]
    grid_spec = pltpu.PrefetchScalarGridSpec(
        num_scalar_prefetch=1,
        grid=(n_b, n_steps),
        in_specs=[
            pl.BlockSpec((None, rows, QK_PAD), lambda b, s, pt: (b, 0, 0)),
            pl.BlockSpec((None, t_pad, QK_PAD), lambda b, s, pt: (b, 0, 0)),
            pl.BlockSpec((MLA_HEADS, KV_RANK, V_HEAD_DIM), lambda b, s, pt: (0, 0, 0)),
        ] + lat_specs + rope_specs,
        out_specs=pl.BlockSpec((None, n_new, MLA_HEADS * V_HEAD_DIM), lambda b, s, pt: (b, 0, 0)),
        scratch_shapes=[
            pltpu.VMEM((rows, LANES), F32), pltpu.VMEM((rows, LANES), F32), pltpu.VMEM((rows, KV_RANK), F32),
            pltpu.VMEM((rows, V_HEAD_DIM), F32),
        ],
    )
    return pl.pallas_call(
        functools.partial(_attn_sample_kernel, n_pages_step=pps, n_new=n_new),
        grid_spec=grid_spec,
        out_shape=jax.ShapeDtypeStruct((n_b, n_new, MLA_HEADS * V_HEAD_DIM), F32),
        compiler_params=_params("arbitrary", "arbitrary"),
        name="attn_sample",
    )(page_table.reshape(-1), q_b, knew_b, w_uv_h, *([cache_lat] * pps), *([cache_rope] * pps))


def _mem_kv_kernel(mem_ref, w_mk_ref, w_mv_ref, k_ref, v_ref, kb_ref, vb_ref):
    m = mem_ref[...].astype(BF16)
    k = _dot(m, w_mk_ref[...])
    v = _dot(m, w_mv_ref[...])
    k_ref[...] = k
    v_ref[...] = v
    kb_ref[...] = k.astype(BF16)
    vb_ref[...] = v.astype(BF16)


def _mem_kv(mem2d, w_mk, w_mv, n_batch, n_mem):
    blk = pl.BlockSpec((n_mem, MEM_WIDTH), lambda b: (b, 0))
    n = n_batch * n_mem
    return pl.pallas_call(
        _mem_kv_kernel,
        grid=(n_batch,),
        in_specs=[pl.BlockSpec((n_mem, D_MODEL), lambda b: (b, 0)), _full((D_MODEL, MEM_WIDTH)),
                  _full((D_MODEL, MEM_WIDTH))],
        out_specs=[blk, blk, blk, blk],
        out_shape=[jax.ShapeDtypeStruct((n, MEM_WIDTH), F32), jax.ShapeDtypeStruct((n, MEM_WIDTH), F32),
                   jax.ShapeDtypeStruct((n, MEM_WIDTH), BF16), jax.ShapeDtypeStruct((n, MEM_WIDTH), BF16)],
        compiler_params=_params("arbitrary"),
        name="mem_kv",
    )(mem2d, w_mk, w_mv)


def _mix_ln1(x, pool_y, mla_y, w_out_a_ref, w_out_b_ref, g_ref, b_ref):
    mix = _dot(pool_y, w_out_a_ref[...]) + _dot(mla_y, w_out_b_ref[...])
    return _layer_norm(DEEPNORM_ALPHA * x + mix, g_ref[...], b_ref[...])


def _route(x2, w_r_ref, b_r_ref, carry):
    m = x2.shape[0]
    logits = jnp.dot(x2, w_r_ref[...], preferred_element_type=F32, precision=lax.Precision.HIGHEST) + b_r_ref[...]
    e_idx = lax.broadcasted_iota(I32, (m, N_EXPERTS), 1).astype(F32)
    work = logits
    vals, picks = [], []
    for _ in range(TOP_K):
        mx = jnp.max(work, axis=1, keepdims=True)
        pick = jnp.min(jnp.where(work == mx, e_idx, float(N_EXPERTS)), axis=1, keepdims=True)
        vals.append(mx)
        picks.append(pick)
        work = jnp.where(e_idx == pick, -jnp.inf, work)
    hot = jnp.where(work == -jnp.inf, 1.0, 0.0)
    exps = [jnp.exp(v - vals[0]) for v in vals]
    denom = exps[0] + exps[1] + exps[2] + exps[3]
    r_i = lax.broadcasted_iota(I32, (m, m), 0)
    c_i = lax.broadcasted_iota(I32, (m, m), 1)
    strict_lower = jnp.where(c_i < r_i, 1.0, 0.0).astype(BF16)
    before = _dot(strict_lower, hot.astype(BF16)) + carry
    lane4 = lax.broadcasted_iota(I32, (m, TOP_K), 1)
    idx = jnp.zeros((m, TOP_K), I32)
    gates = jnp.zeros((m, TOP_K), F32)
    rank = jnp.zeros((m, TOP_K), I32)
    for k in range(TOP_K):
        rk = jnp.sum(jnp.where(e_idx == picks[k], before, 0.0), axis=1, keepdims=True).astype(I32)
        idx = jnp.where(lane4 == k, picks[k].astype(I32), idx)
        gates = jnp.where(lane4 == k, exps[k] / denom, gates)
        rank = jnp.where(lane4 == k, rk, rank)
    return idx, gates, rank, carry + jnp.sum(hot, axis=0, keepdims=True)


def _mo_ln2_route(x1, o_bf, w_mo_ref, g_ref, b_ref, w_r_ref, b_r_ref, carry):
    x2 = _layer_norm(DEEPNORM_ALPHA * x1 + _dot(o_bf, w_mo_ref[...]), g_ref[...], b_ref[...])
    return (x2,) + _route(x2, w_r_ref, b_r_ref, carry)


def _post_prompt_kernel(x_ref, pooly_ref, mlay_ref, mk_ref, mv_ref,
                        w_out_a_ref, w_out_b_ref, g1_ref, b1_ref, w_mq_ref, w_mo_ref, g2_ref, b2_ref,
                        w_r_ref, b_r_ref,
                        x2_ref, idx_ref, gate_ref, rank_ref, cnt_ref, carry_ref):
    first = jnp.logical_and(pl.program_id(0) == 0, pl.program_id(1) == 0)

    @pl.when(first)
    def _():
        carry_ref[...] = jnp.zeros(carry_ref.shape, F32)

    x1 = _mix_ln1(x_ref[...], pooly_ref[...], mlay_ref[...], w_out_a_ref, w_out_b_ref, g1_ref, b1_ref)
    qm = (_dot(x1.astype(BF16), w_mq_ref[...]) * MEM_SCALE).astype(BF16)
    outs = []
    for hd in range(MEM_HEADS):
        cols = slice(MEM_HEAD_DIM * hd, MEM_HEAD_DIM * (hd + 1))
        s = _dot_nt(qm[:, cols], mk_ref[:, cols])
        p = jnp.exp(s - jnp.max(s, axis=1, keepdims=True))
        o = _dot(p.astype(BF16), mv_ref[:, cols]) / jnp.sum(p, axis=1, keepdims=True)
        outs.append(o.astype(BF16))
    o_all = jnp.concatenate(outs, axis=1)
    x2, idx, gates, rank, carry = _mo_ln2_route(x1, o_all, w_mo_ref, g2_ref, b2_ref, w_r_ref, b_r_ref,
                                                carry_ref[...])
    _store_row_tiles(x2_ref, (), x2)
    idx_ref[...] = idx
    gate_ref[...] = gates
    rank_ref[...] = rank
    carry_ref[...] = carry
    cnt_ref[...] = carry


def _post_weight_specs():
    half = (POOL_WIDTH, D_MODEL)
    return [
        _full(half), _full(half), _full((1, D_MODEL)), _full((1, D_MODEL)),
        _full((D_MODEL, MEM_WIDTH)), _full((MEM_WIDTH, D_MODEL)), _full((1, D_MODEL)), _full((1, D_MODEL)),
        _full((D_MODEL, N_EXPERTS)), _full((1, N_EXPERTS)),
    ]


def _post_prompt(x2d, pool_y, mla_y, mk_b, mv_b, weights, n_batch, seq, n_mem, n_all):
    tm = TOKEN_BLOCK
    nj = seq // tm
    tok = lambda b, j: (b * nj + j, 0)
    return pl.pallas_call(
        _post_prompt_kernel,
        grid=(n_batch, nj),
        in_specs=[
            pl.BlockSpec((tm, D_MODEL), tok), pl.BlockSpec((tm, POOL_WIDTH), tok), pl.BlockSpec((tm, POOL_WIDTH), tok),
            pl.BlockSpec((n_mem, MEM_WIDTH), lambda b, j: (b, 0)), pl.BlockSpec((n_mem, MEM_WIDTH), lambda b, j: (b, 0)),
        ] + _post_weight_specs(),
        out_specs=[
            pl.BlockSpec((tm * ROW_TILE, LANES), tok), pl.BlockSpec((tm, TOP_K), tok), pl.BlockSpec((tm, TOP_K), tok),
            pl.BlockSpec((tm, TOP_K), tok), _full((1, N_EXPERTS)),
        ],
        out_shape=[
            jax.ShapeDtypeStruct((n_all * ROW_TILE, LANES), F32), jax.ShapeDtypeStruct((n_all, TOP_K), I32),
            jax.ShapeDtypeStruct((n_all, TOP_K), F32), jax.ShapeDtypeStruct((n_all, TOP_K), I32),
            jax.ShapeDtypeStruct((1, N_EXPERTS), F32),
        ],
        scratch_shapes=[pltpu.VMEM((1, N_EXPERTS), F32)],
        compiler_params=_params("arbitrary", "arbitrary"),
        name="post_prompt",
    )(x2d, pool_y, mla_y, mk_b, mv_b, *weights)


def _mix_sample_kernel(x_ref, pooly_ref, mlay_ref, w_out_a_ref, w_out_b_ref, g1_ref, b1_ref, w_mq_ref,
                       x1_ref, qm_ref):
    x1 = _mix_ln1(x_ref[...], pooly_ref[...], mlay_ref[...], w_out_a_ref, w_out_b_ref, g1_ref, b1_ref)
    x1_ref[...] = x1
    qm_ref[...] = (_dot(x1.astype(BF16), w_mq_ref[...]) * MEM_SCALE).astype(BF16)


def _mix_sample(x2d, pool_y, mla_y, weights):
    n = x2d.shape[0]
    w_out_a, w_out_b, g1, b1, w_mq = weights[:5]
    return pl.pallas_call(
        _mix_sample_kernel,
        grid=(1,),
        in_specs=[_full((n, D_MODEL)), _full((n, POOL_WIDTH)), _full((n, POOL_WIDTH))] + _post_weight_specs()[:5],
        out_specs=[_full((n, D_MODEL)), _full((n, MEM_WIDTH))],
        out_shape=[jax.ShapeDtypeStruct((n, D_MODEL), F32), jax.ShapeDtypeStruct((n, MEM_WIDTH), BF16)],
        compiler_params=_params("arbitrary"),
        name="mix_sample",
    )(x2d, pool_y, mla_y, w_out_a, w_out_b, g1, b1, w_mq)


def _mem_attn_sample_kernel(q_ref, k_ref, v_ref, out_ref, res_ref, *, n_new):
    for b in range(q_ref.shape[0]):
        s = _dot_nt(q_ref[b], k_ref[b].astype(BF16))
        p = jnp.exp(s - jnp.max(s, axis=1, keepdims=True))
        res_ref[...] = _dot(p.astype(BF16), v_ref[b].astype(BF16)) / jnp.sum(p, axis=1, keepdims=True)
        for hd in range(MEM_HEADS):
            cols = slice(MEM_HEAD_DIM * hd, MEM_HEAD_DIM * (hd + 1))
            out_ref[b, :, cols] = res_ref[hd * n_new:(hd + 1) * n_new, cols].astype(BF16)


def _mem_attn_sample(q_bd, mem_k, mem_v, n_new):
    n_b, rows, _ = q_bd.shape
    n_mem = mem_k.shape[1]
    gb = min(MEM_BATCH_BLOCK, n_b)
    assert n_b % gb == 0
    blk = lambda i: (i, 0, 0)
    return pl.pallas_call(
        functools.partial(_mem_attn_sample_kernel, n_new=n_new),
        grid=(n_b // gb,),
        in_specs=[pl.BlockSpec((gb, rows, MEM_WIDTH), blk), pl.BlockSpec((gb, n_mem, MEM_WIDTH), blk),
                  pl.BlockSpec((gb, n_mem, MEM_WIDTH), blk)],
        out_specs=pl.BlockSpec((gb, n_new, MEM_WIDTH), blk),
        out_shape=jax.ShapeDtypeStruct((n_b, n_new, MEM_WIDTH), BF16),
        scratch_shapes=[pltpu.VMEM((rows, MEM_WIDTH), F32)],
        compiler_params=_params("arbitrary"),
        name="mem_attn_sample",
    )(q_bd, mem_k, mem_v)


def _route_sample_kernel(x1_ref, o_ref, cnt_in_ref, w_mo_ref, g2_ref, b2_ref, w_r_ref, b_r_ref,
                         x2_in, idx_in, gate_in, rank_in,
                         x2_ref, idx_ref, gate_ref, rank_ref, cnt_ref):
    del x2_in, idx_in, gate_in, rank_in
    x2, idx, gates, rank, carry = _mo_ln2_route(x1_ref[...], o_ref[...], w_mo_ref, g2_ref, b2_ref, w_r_ref, b_r_ref,
                                                cnt_in_ref[...])
    _store_row_tiles(x2_ref, (), x2)
    idx_ref[...] = idx
    gate_ref[...] = gates
    rank_ref[...] = rank
    cnt_ref[...] = carry


def _route_sample(x1, o_bf, cnt_in, weights, x2_all, idx_all, gate_all, rank_all, n_prompt):
    n = x1.shape[0]
    assert n_prompt % n == 0
    tail = lambda i: (n_prompt // n, 0)
    anyspec = pl.BlockSpec(memory_space=pl.ANY)
    w_mo, g2, b2, w_r, b_r = weights[5:]
    return pl.pallas_call(
        _route_sample_kernel,
        grid=(1,),
        in_specs=[_full((n, D_MODEL)), _full((n, MEM_WIDTH)), _full((1, N_EXPERTS))] + _post_weight_specs()[5:] + [
            anyspec, anyspec, anyspec, anyspec],
        out_specs=[pl.BlockSpec((n * ROW_TILE, LANES), tail), pl.BlockSpec((n, TOP_K), tail), pl.BlockSpec((n, TOP_K), tail),
                   pl.BlockSpec((n, TOP_K), tail), _full((1, N_EXPERTS))],
        out_shape=[jax.ShapeDtypeStruct(x2_all.shape, F32), jax.ShapeDtypeStruct(idx_all.shape, I32),
                   jax.ShapeDtypeStruct(gate_all.shape, F32), jax.ShapeDtypeStruct(rank_all.shape, I32),
                   jax.ShapeDtypeStruct((1, N_EXPERTS), F32)],
        input_output_aliases={8: 0, 9: 1, 10: 2, 11: 3},
        compiler_params=_params("arbitrary"),
        name="route_sample",
    )(x1, o_bf, cnt_in, w_mo, g2, b2, w_r, b_r, x2_all, idx_all, gate_all, rank_all)


def _dispatch_kernel(pad_start_ref, n_pad_ref, n_used_ref, idx_hbm, x_ref, xrows_hbm,
                     idx_smem, zbuf, idx_sem, row_sem, pad_sem):
    i = pl.program_id(0)
    last = pl.num_programs(0) - 1
    tm = x_ref.shape[0] // ROW_TILE
    blk = zbuf.shape[0]
    n_blk = xrows_hbm.shape[0] // blk

    def idx_copy(row, s):
        return pltpu.make_async_copy(idx_hbm.at[row], idx_smem.at[s], idx_sem.at[s])

    def pad_copy(e, r):
        dst = pl.multiple_of((pad_start_ref[e] + r) * ROW_TILE, ROW_TILE)
        return pltpu.make_async_copy(zbuf.at[pl.ds(0, ROW_TILE)], xrows_hbm.at[pl.ds(dst, ROW_TILE)], pad_sem)

    def tail_copy(j):
        return pltpu.make_async_copy(zbuf, xrows_hbm.at[pl.ds(pl.multiple_of(j * blk, blk), blk)], pad_sem)

    @pl.when(i == 0)
    def _():
        idx_copy(0, 0).start()
        zbuf[...] = jnp.zeros(zbuf.shape, F32)
        for e in range(N_EXPERTS):
            def start(r, c, e=e):
                pad_copy(e, r).start()
                return c

            lax.fori_loop(0, n_pad_ref[e], start, 0)

        def start_tail(j, c):
            tail_copy(j).start()
            return c

        lax.fori_loop(n_used_ref[0], n_blk, start_tail, 0)
        for e in range(N_EXPERTS):
            def wait(r, c, e=e):
                pad_copy(e, r).wait()
                return c

            lax.fori_loop(0, n_pad_ref[e], wait, 0)

        def wait_tail(j, c):
            tail_copy(j).wait()
            return c

        lax.fori_loop(n_used_ref[0], n_blk, wait_tail, 0)

    def step(slot):
        idx_copy(i, slot).wait()

        @pl.when(i < last)
        def _():
            idx_copy(i + 1, 1 - slot).start()

        for t in range(tm):
            for k in range(TOP_K):
                dst = pl.multiple_of(idx_smem[slot, TOP_K * t + k], ROW_TILE)
                pltpu.make_async_copy(x_ref.at[pl.ds(ROW_TILE * t, ROW_TILE)], xrows_hbm.at[pl.ds(dst, ROW_TILE)],
                                      row_sem).start(priority=k % 2)
        for k in range(TOP_K):
            pltpu.make_async_copy(x_ref, x_ref, row_sem).wait()

    for parity in range(2):
        pl.when(i % 2 == parity)(functools.partial(step, parity))


def _dispatch(pad_start, n_pad, n_used, idx_rows, x_all, n_rows):
    tm = DISPATCH_BLOCK
    n_steps = idx_rows.shape[0]
    assert x_all.shape[0] == n_steps * tm * ROW_TILE and idx_rows.shape[1] == TOP_K * tm
    assert n_rows % EXPERT_ROWS == 0
    grid_spec = pltpu.PrefetchScalarGridSpec(
        num_scalar_prefetch=3,
        grid=(n_steps,),
        in_specs=[pl.BlockSpec(memory_space=pl.ANY),
                  pl.BlockSpec((tm * ROW_TILE, LANES), lambda i, ps, npd, nu: (i, 0))],
        out_specs=pl.BlockSpec(memory_space=pl.ANY),
        scratch_shapes=[pltpu.SMEM((2, TOP_K * tm), I32), pltpu.VMEM((EXPERT_ROWS * ROW_TILE, LANES), F32),
                        pltpu.SemaphoreType.DMA((2,)), pltpu.SemaphoreType.DMA, pltpu.SemaphoreType.DMA],
    )
    return pl.pallas_call(
        _dispatch_kernel,
        grid_spec=grid_spec,
        out_shape=jax.ShapeDtypeStruct((n_rows * ROW_TILE, LANES), F32),
        compiler_params=_params("arbitrary"),
        name="moe_dispatch",
    )(pad_start, n_pad, n_used, idx_rows, x_all)


def _moe_kernel(blk_exp_ref, blk_row_ref, idx_hbm, x_ref, wgu_ref, bgu_ref, wd_ref, bd_ref, y_hbm,
                idx_smem, ybuf, idx_sem, y_sem):
    del blk_exp_ref, blk_row_ref
    i = pl.program_id(0)
    last = pl.num_programs(0) - 1
    tr = x_ref.shape[0] // ROW_TILE
    n_chunks = D_EXPERT // EXPERT_CHUNK

    def idx_copy(row, s):
        return pltpu.make_async_copy(idx_hbm.at[row], idx_smem.at[s], idx_sem.at[s])

    def scatter(s_idx, ys, rows):
        for r in rows:
            dst = pl.multiple_of(idx_smem[s_idx, r], ROW_TILE)
            pltpu.make_async_copy(ybuf.at[ys, pl.ds(ROW_TILE * r, ROW_TILE)], y_hbm.at[pl.ds(dst, ROW_TILE)],
                                  y_sem.at[ys]).start(priority=r % 2)

    def wait_rows(s):
        pltpu.make_async_copy(ybuf.at[s], ybuf.at[s], y_sem.at[s]).wait()

    @pl.when(i == 0)
    def _():
        idx_copy(0, 0).start()
        ybuf[1] = jnp.zeros(ybuf.shape[1:], F32)

    def step(slot):
        nxt = 1 - slot
        idx_copy(i, slot).wait()
        idx_copy(i + 1, nxt).start()
        xb = _load_row_tiles(x_ref, (), tr).astype(BF16)
        acc = jnp.zeros((tr, D_MODEL), F32) + bd_ref[...]

        groups = 3 * n_chunks
        per_group = -(-tr // groups)

        def issue(g):
            scatter(slot, nxt, range(g * per_group, min((g + 1) * per_group, tr)))

        for c in range(n_chunks):
            cg = slice(c * EXPERT_CHUNK, (c + 1) * EXPERT_CHUNK)
            cu = slice(D_EXPERT + c * EXPERT_CHUNK, D_EXPERT + (c + 1) * EXPERT_CHUNK)
            gate = jnp.minimum(_dot(xb, wgu_ref[:, cg]) + bgu_ref[:, cg], SWIGLU_LIMIT)
            issue(3 * c)
            up = jnp.clip(_dot(xb, wgu_ref[:, cu]) + bgu_ref[:, cu], -SWIGLU_LIMIT, SWIGLU_LIMIT)
            issue(3 * c + 1)
            act = gate * (1.0 / (1.0 + jnp.exp(-SWIGLU_ALPHA * gate))) * (up + 1.0)
            acc = acc + _dot(act.astype(BF16), wd_ref[cg, :])
            issue(3 * c + 2)

        @pl.when(i >= 1)
        def _():
            wait_rows(slot)

        _store_row_tiles(ybuf, (slot,), acc)

        @pl.when(i == last)
        def _():
            idx_copy(i + 1, nxt).wait()
            wait_rows(nxt)

    for parity in range(2):
        pl.when(i % 2 == parity)(functools.partial(step, parity))


def _moe(blk_exp, blk_row, idx_rows, x_rows, w_gu, b_gu, w_d, b_d, n_assign):
    tr = EXPERT_ROWS
    n_steps = blk_exp.shape[0]
    assert n_steps >= 2 and idx_rows.shape == (n_steps + 1, tr)
    anyspec = pl.BlockSpec(memory_space=pl.ANY)
    emap3 = lambda i, be, br: (be[i], 0, 0)
    grid_spec = pltpu.PrefetchScalarGridSpec(
        num_scalar_prefetch=2,
        grid=(n_steps,),
        in_specs=[
            anyspec, pl.BlockSpec((tr * ROW_TILE, LANES), lambda i, be, br: (br[i], 0)),
            pl.BlockSpec((None, D_MODEL, 2 * D_EXPERT), emap3), pl.BlockSpec((None, 1, 2 * D_EXPERT), emap3),
            pl.BlockSpec((None, D_EXPERT, D_MODEL), emap3), pl.BlockSpec((None, 1, D_MODEL), emap3),
        ],
        out_specs=anyspec,
        scratch_shapes=[
            pltpu.SMEM((2, tr), I32), pltpu.VMEM((2, tr * ROW_TILE, LANES), F32),
            pltpu.SemaphoreType.DMA((2,)), pltpu.SemaphoreType.DMA((2,)),
        ],
    )
    return pl.pallas_call(
        _moe_kernel,
        grid_spec=grid_spec,
        out_shape=jax.ShapeDtypeStruct(((n_assign + 2 * tr) * ROW_TILE, LANES), F32),
        compiler_params=_params("arbitrary"),
        name="moe_experts",
    )(blk_exp, blk_row, idx_rows, x_rows, w_gu, b_gu, w_d, b_d)


def _combine_kernel(x2_ref, gate_ref, g3_ref, b3_ref, *rest):
    y_refs, out_ref = rest[:TOP_K], rest[TOP_K]
    gates = gate_ref[...]
    tm = out_ref.shape[0]
    moe = jnp.zeros(out_ref.shape, F32)
    for k in range(TOP_K):
        moe = moe + gates[:, k:k + 1] * _load_row_tiles(y_refs[k], (), tm)
    out_ref[...] = _layer_norm(DEEPNORM_ALPHA * _load_row_tiles(x2_ref, (), tm) + moe, g3_ref[...], b3_ref[...])


def _combine(x2_all, y_tok, gates_all, g3, b3, row0, n_rows, tm):
    n_all = gates_all.shape[0]
    assert row0 % tm == 0 and n_rows % tm == 0 and n_all % tm == 0
    off = row0 // tm
    tok = lambda i: (off + i, 0)
    tiles = (tm * ROW_TILE, LANES)
    slot_specs = [pl.BlockSpec(tiles, functools.partial(lambda i, k: (k * (n_all // tm) + off + i, 0), k=k))
                  for k in range(TOP_K)]
    return pl.pallas_call(
        _combine_kernel,
        grid=(n_rows // tm,),
        in_specs=[pl.BlockSpec(tiles, tok), pl.BlockSpec((tm, TOP_K), tok), _full((1, D_MODEL)),
                  _full((1, D_MODEL))] + slot_specs,
        out_specs=pl.BlockSpec((tm, D_MODEL), lambda i: (i, 0)),
        out_shape=jax.ShapeDtypeStruct((n_rows, D_MODEL), F32),
        compiler_params=_params("arbitrary"),
        name="combine_ln3",
    )(x2_all, gates_all, g3, b3, *([y_tok] * TOP_K))


def _rope_tables(pos):
    inv_freq = ROPE_THETA ** (-jnp.arange(ROPE_HALF, dtype=F32) / ROPE_HALF)
    ang = pos.astype(F32)[:, None] * inv_freq[None, :]
    c, s, z = jnp.cos(ang), jnp.sin(ang), jnp.zeros_like(ang)
    return jnp.concatenate([c, z, c, z], axis=1), jnp.concatenate([-s, z, s, z], axis=1)


def _pad_rope_cols(w):
    z = jnp.zeros(w.shape[:-1] + (ROPE_HALF,), w.dtype)
    return jnp.concatenate([w[..., :ROPE_HALF], z, w[..., ROPE_HALF:], z], axis=-1)


def _layer_weights(l, w_in, w_pool, pool_scale, q_norm_g, w_uq, kv_norm_g, w_uk, w_uv, w_out, ln1_g, ln1_b,
                   w_mq, w_mo, ln2_g, ln2_b, w_router, b_router):
    o3 = POOL_WIDTH + Q_RANK + KV_RANK
    w_in_p = jnp.concatenate([w_in[l][:, :o3], _pad_rope_cols(w_in[l][:, o3:])], axis=1).astype(BF16)
    uq = w_uq[l].reshape(Q_RANK, MLA_HEADS, QK_NOPE_DIM + QK_ROPE_DIM)
    uq_p = jnp.concatenate([uq[..., :QK_NOPE_DIM], _pad_rope_cols(uq[..., QK_NOPE_DIM:])], axis=-1)
    uq_p = uq_p.reshape(Q_RANK, MLA_HEADS * QK_PAD).astype(BF16)
    w_ukt = jnp.transpose(w_uk[l], (1, 2, 0)).astype(BF16)
    w_uv_h = jnp.transpose(w_uv[l], (1, 0, 2)).astype(BF16)
    row = lambda v: v[l].reshape(1, -1)
    proj = [w_in_p, row(q_norm_g), uq_p, row(kv_norm_g), w_ukt, w_pool[l].astype(BF16), row(pool_scale)]
    post = [w_out[l][:POOL_WIDTH].astype(BF16), w_out[l][POOL_WIDTH:].astype(BF16), row(ln1_g), row(ln1_b),
            w_mq[l].astype(BF16), w_mo[l].astype(BF16), row(ln2_g), row(ln2_b), w_router[l], row(b_router)]
    return proj, w_uv_h, post


def _routing_tables(idx_all, rank_all, counts, n_blocks):
    tr = EXPERT_ROWS
    cnt = counts.reshape(-1).astype(I32)
    padded = ((cnt + tr - 1) // tr) * tr
    pend = jnp.cumsum(padded)
    pstart = pend - padded
    dest2d = pstart[idx_all] + rank_all
    dest = dest2d.reshape(-1)
    n_all = idx_all.shape[0]
    m = dest.shape[0]
    n_rows = n_blocks * tr
    disp_rows = (dest2d * ROW_TILE).reshape(n_all // DISPATCH_BLOCK, DISPATCH_BLOCK * TOP_K)
    assign = (jnp.arange(TOP_K, dtype=I32)[None, :] * n_all + jnp.arange(n_all, dtype=I32)[:, None]).reshape(-1)
    row_src = jnp.full((n_rows,), -1, I32).at[dest].set(assign, unique_indices=True)
    spare = m + jnp.arange(n_rows, dtype=I32) % (2 * tr)
    dst_rows = (jnp.where(row_src >= 0, row_src, spare) * ROW_TILE).reshape(n_blocks, tr)
    spare_row = spare[:tr][None] * ROW_TILE
    idx_rows = jnp.concatenate([spare_row, dst_rows, spare_row], axis=0)
    n_used = pend[-1] // tr
    blk_row = jnp.minimum(jnp.arange(n_blocks + 1, dtype=I32), n_used - 1)
    blk_exp = jnp.sum((pend[None, :] <= (blk_row * tr)[:, None]).astype(I32), axis=1)
    return (jnp.clip(blk_exp, 0, N_EXPERTS - 1), blk_row, idx_rows, disp_rows, pstart + cnt, padded - cnt,
            n_used.astype(I32).reshape(1))


def kernel(x_prompt, x_sample, mem_prompt, cache_kv_latent, cache_k_rope, cache_mem_k, cache_mem_v, state_pool,
           page_table, w_in, w_pool, pool_scale, q_norm_g, w_uq, kv_norm_g, w_uk, w_uv, w_out, ln1_g, ln1_b,
           w_mq, w_mk, w_mv, w_mo, ln2_g, ln2_b, w_router, b_router, w_gate_up, b_gate_up, w_down, b_down,
           ln3_g, ln3_b):
    n_bp, seq, _ = x_prompt.shape
    n_bs, n_new, _ = x_sample.shape
    n_mem = mem_prompt.shape[1]
    page = cache_kv_latent.shape[2]
    past_len = page_table.shape[1] * page
    n_p = n_bp * seq
    n_s = n_bs * n_new
    n_all = n_p + n_s
    n_assign = n_all * TOP_K
    assert w_in.shape[0] == DEPTH == 1
    assert seq % TOKEN_BLOCK == 0 and seq % ATTN_K_BLOCK == 0 and seq % ATTN_Q_BLOCK == 0 and seq >= POOL_HALO
    assert n_p % n_s == 0 and (n_s % TOKEN_BLOCK == 0 or TOKEN_BLOCK % n_s == 0)
    l = 0

    proj_w, w_uv_h, post_w = _layer_weights(l, w_in, w_pool, pool_scale, q_norm_g, w_uq, kv_norm_g, w_uk, w_uv,
                                            w_out, ln1_g, ln1_b, w_mq, w_mo, ln2_g, ln2_b, w_router, b_router)

    xp = x_prompt.reshape(n_p, D_MODEL)
    cos_p, sin_p = _rope_tables(jnp.arange(seq))
    qcat, kcat, lat_p, kr_p, pooly_p, pstate_p = _proj_prompt(xp, proj_w, cos_p, sin_p, n_bp, seq)
    mlay_p = _attn_prompt(qcat, kcat, w_uv_h, n_bp, seq)
    mk, mv, mk_b, mv_b = _mem_kv(mem_prompt.reshape(n_bp * n_mem, D_MODEL), w_mk[l].astype(BF16),
                                 w_mv[l].astype(BF16), n_bp, n_mem)
    x2_all, idx_all, gate_all, rank_all, cnt_p = _post_prompt(xp, pooly_p, mlay_p, mk_b, mv_b, post_w,
                                                              n_bp, seq, n_mem, n_all)

    xs = jnp.transpose(x_sample, (1, 0, 2)).reshape(n_s, D_MODEL)
    state_t = jnp.transpose(state_pool[l], (1, 0, 2))
    cos_s, sin_s = _rope_tables(jnp.repeat(past_len + jnp.arange(n_new), n_bs))
    qcat_s, kcat_s, lat_s, kr_s, pooly_s, pstate_s = _proj_sample(xs, state_t, proj_w, cos_s, sin_s, past_len)
    q_b = jnp.transpose(qcat_s.reshape(MLA_HEADS, n_new, n_bs, QK_PAD), (2, 0, 1, 3)).reshape(
        n_bs, MLA_HEADS * n_new, QK_PAD)
    knew_b = jnp.transpose(kcat_s.reshape(n_new, n_bs, QK_PAD), (1, 0, 2))
    knew_b = jnp.pad(knew_b, ((0, 0), (0, LANES - n_new), (0, 0)))
    mlay_b = _attn_sample(page_table, q_b, knew_b, w_uv_h, cache_kv_latent,
                          jnp.transpose(cache_k_rope, (0, 1, 3, 2)), l)
    mlay_s = jnp.transpose(mlay_b, (1, 0, 2)).reshape(n_s, POOL_WIDTH).astype(BF16)
    x1_s, qm_s = _mix_sample(xs, pooly_s, mlay_s, post_w)
    qm_b = jnp.transpose(qm_s.reshape(n_new, n_bs, MEM_WIDTH), (1, 0, 2))
    head_of_col = jnp.arange(MEM_WIDTH) // MEM_HEAD_DIM
    head_mask = (head_of_col[None, :] == jnp.arange(MEM_HEADS)[:, None]).astype(BF16)
    q_bd = (qm_b[:, None, :, :] * head_mask[None, :, None, :]).reshape(n_bs, MEM_HEADS * n_new, MEM_WIDTH)
    o_b = _mem_attn_sample(q_bd, cache_mem_k[l].reshape(n_bs, n_mem, MEM_WIDTH),
                           cache_mem_v[l].reshape(n_bs, n_mem, MEM_WIDTH), n_new)
    o_s = jnp.transpose(o_b, (1, 0, 2)).reshape(n_s, MEM_WIDTH)
    x2_all, idx_all, gate_all, rank_all, counts = _route_sample(x1_s, o_s, cnt_p, post_w, x2_all, idx_all, gate_all,
                                                               rank_all, n_p)

    n_blocks = -(-n_assign // EXPERT_ROWS) + N_EXPERTS + 1
    blk_exp, blk_row, idx_rows, disp_rows, pad_start, n_pad, n_used = _routing_tables(idx_all, rank_all, counts,
                                                                                      n_blocks)
    x_rows = _dispatch(pad_start, n_pad, n_used, disp_rows, x2_all, n_blocks * EXPERT_ROWS)
    y_tok = _moe(blk_exp, blk_row, idx_rows, x_rows, w_gate_up[l].astype(BF16), b_gate_up[l][:, None, :],
                 w_down[l].astype(BF16), b_down[l][:, None, :], n_assign)
    g3, b3 = ln3_g[l].reshape(1, -1), ln3_b[l].reshape(1, -1)
    y_p = _combine(x2_all, y_tok, gate_all, g3, b3, 0, n_p, TOKEN_BLOCK)
    y_s = _combine(x2_all, y_tok, gate_all, g3, b3, n_p, n_s, min(TOKEN_BLOCK, n_s))

    def from_steps(a, width):
        return jnp.transpose(a.reshape(n_new, n_bs, width), (1, 0, 2))[None]

    return (
        y_p.reshape(n_bp, seq, D_MODEL),
        from_steps(y_s, D_MODEL)[0],
        lat_p.reshape(1, n_bp, seq, KV_RANK),
        kr_p.reshape(1, n_bp, seq, QK_ROPE_DIM),
        pstate_p[None, :, POOL_HALO - POOL_BUF:, :],
        mk.reshape(1, n_bp, n_mem, MEM_HEADS, MEM_HEAD_DIM),
        mv.reshape(1, n_bp, n_mem, MEM_HEADS, MEM_HEAD_DIM),
        from_steps(lat_s, KV_RANK),
        from_steps(kr_s, QK_ROPE_DIM),
        jnp.transpose(pstate_s, (1, 0, 2))[None],
    )
```

```python
import functools

import jax
import jax.numpy as jnp
from jax import lax
from jax.experimental import pallas as pl
from jax.experimental.pallas import tpu as pltpu

F32 = jnp.float32
BF16 = jnp.bfloat16
I32 = jnp.int32

D_MODEL = 1024
POOL_WIDTH = 512
POOL_WINDOWS = (2, 4, 8, 16)
POOL_GROUP_DIM = 128
POOL_BUF = 15
POOL_HALO = 16
MLA_HEADS = 4
QK_NOPE_DIM = 128
QK_ROPE_DIM = 64
ROPE_HALF = 32
V_HEAD_DIM = 128
Q_RANK = 256
KV_RANK = 128
ROPE_THETA = 10000.0
QK_PAD = 256
ONES_LANE = QK_PAD - 1
MEM_HEADS = 4
MEM_HEAD_DIM = 128
MEM_WIDTH = 512
N_EXPERTS = 32
TOP_K = 4
D_EXPERT = 1024
SWIGLU_LIMIT = 7.0
SWIGLU_ALPHA = 1.702
LN_EPS = 1e-5
RMS_EPS = 1e-6
DEPTH = 1
DEEPNORM_ALPHA = (2.0 * DEPTH) ** 0.25
ATTN_SCALE = (QK_NOPE_DIM + QK_ROPE_DIM) ** -0.5
MEM_SCALE = MEM_HEAD_DIM ** -0.5

LANES = 128
VMEM_LIMIT = 48 * 1024 * 1024

TOKEN_BLOCK = 256
ATTN_Q_BLOCK = 256
ATTN_K_BLOCK = 512
ATTN_GROUPS = 2
PAGES_PER_STEP = 32
MEM_BATCH_BLOCK = 8
DISPATCH_BLOCK = 256
EXPERT_ROWS = 256
EXPERT_CHUNK = 256


def _dot(a, b):
    return jnp.dot(a, b, preferred_element_type=F32)


def _dot_nt(a, b):
    return lax.dot_general(a, b, (((1,), (1,)), ((), ())), preferred_element_type=F32)


def _rms(x, g):
    return x * lax.rsqrt(jnp.mean(x * x, axis=-1, keepdims=True) + RMS_EPS) * g


def _layer_norm(x, g, b):
    mu = jnp.mean(x, axis=-1, keepdims=True)
    xc = x - mu
    var = jnp.mean(xc * xc, axis=-1, keepdims=True)
    return xc * lax.rsqrt(var + LN_EPS) * g + b


ROW_TILE = D_MODEL // LANES


def _store_row_tiles(ref, lead, val):
    m = val.shape[0]
    for j in range(ROW_TILE):
        ref[(*lead, pl.ds(j, m, stride=ROW_TILE), slice(None))] = val[:, LANES * j:LANES * (j + 1)]


def _load_row_tiles(ref, lead, m):
    return jnp.concatenate([ref[(*lead, pl.ds(j, m, stride=ROW_TILE), slice(None))] for j in range(ROW_TILE)], axis=1)


def _params(*semantics):
    return pltpu.CompilerParams(dimension_semantics=semantics, vmem_limit_bytes=VMEM_LIMIT)


def _full(shape):
    n = len(shape)
    return pl.BlockSpec(shape, lambda *_: (0,) * n)


def _project(x_bf, w_in_ref, qg_ref, w_uq_ref, kvg_ref, w_ukt_ref, cos, sin):
    h = _dot(x_bf, w_in_ref[...])
    u = h[:, :POOL_WIDTH]
    qn = _rms(h[:, POOL_WIDTH:POOL_WIDTH + Q_RANK], qg_ref[...])
    q = _dot(qn.astype(BF16), w_uq_ref[...])
    q_lat, q_rope = [], []
    for hd in range(MLA_HEADS):
        nope = q[:, QK_PAD * hd:QK_PAD * hd + QK_NOPE_DIM]
        rp = q[:, QK_PAD * hd + QK_NOPE_DIM:QK_PAD * (hd + 1)]
        q_lat.append(_dot(nope.astype(BF16), w_ukt_ref[hd]) * ATTN_SCALE)
        q_rope.append((rp * cos + pltpu.roll(rp, 64, 1) * sin) * ATTN_SCALE)
    o2 = POOL_WIDTH + Q_RANK
    lat = _rms(h[:, o2:o2 + KV_RANK], kvg_ref[...])
    kp = h[:, o2 + KV_RANK:]
    k_rope = kp * cos + pltpu.roll(kp, 64, 1) * sin
    return u, q_lat, q_rope, lat, k_rope


def _compact_rope(r):
    lane = lax.broadcasted_iota(I32, r.shape, 1)
    moved = pltpu.roll(r, 96, 1)
    return jnp.where(lane < ROPE_HALF, r, jnp.where(lane < QK_ROPE_DIM, moved, 0.0))


def _with_ones_lane(r):
    lane = lax.broadcasted_iota(I32, r.shape, 1)
    return jnp.where(lane == ONES_LANE - KV_RANK, 1.0, r)


def _pool_group_out(diff, g, w_pool_ref, pscale_ref):
    cols = slice(POOL_GROUP_DIM * g, POOL_GROUP_DIM * (g + 1))
    return _dot(diff.astype(BF16), w_pool_ref[g]) * pscale_ref[:, cols]


def _proj_prompt_kernel(x_ref, w_in_ref, qg_ref, w_uq_ref, kvg_ref, w_ukt_ref, w_pool_ref, pscale_ref,
                        cos_ref, sin_ref,
                        qcat_ref, kcat_ref, lat_ref, kr_ref, pooly_ref, pstate_ref, ubuf):
    j = pl.program_id(1)
    tm = x_ref.shape[0]

    @pl.when(j == 0)
    def _():
        ubuf[0:POOL_HALO, :] = jnp.zeros((POOL_HALO, POOL_WIDTH), F32)

    u, q_lat, q_rope, lat, k_rope = _project(
        x_ref[...].astype(BF16), w_in_ref, qg_ref, w_uq_ref, kvg_ref, w_ukt_ref, cos_ref[...], sin_ref[...])
    for hd in range(MLA_HEADS):
        qcat_ref[hd, :, 0:KV_RANK] = q_lat[hd].astype(BF16)
        qcat_ref[hd, :, KV_RANK:QK_PAD] = q_rope[hd].astype(BF16)
    lat_ref[...] = lat
    kcat_ref[:, 0:KV_RANK] = lat.astype(BF16)
    kcat_ref[:, KV_RANK:QK_PAD] = _with_ones_lane(k_rope).astype(BF16)
    kr_ref[...] = _compact_rope(k_rope)[:, :QK_ROPE_DIM]

    ubuf[POOL_HALO:POOL_HALO + tm, :] = u
    pos = j * tm + lax.broadcasted_iota(I32, (tm, 1), 0)
    for g, w in enumerate(POOL_WINDOWS):
        cols = slice(POOL_GROUP_DIM * g, POOL_GROUP_DIM * (g + 1))
        ug = u[:, cols]
        ssum = ug
        for k in range(1, w):
            ssum = ssum + ubuf[POOL_HALO - k:POOL_HALO - k + tm, cols]
        count = jnp.minimum(w, pos + 1).astype(F32)
        pooly_ref[:, cols] = _pool_group_out(ssum / count - ug, g, w_pool_ref, pscale_ref).astype(BF16)
    tail = ubuf[tm:tm + POOL_HALO, :]
    ubuf[0:POOL_HALO, :] = tail

    @pl.when(j == pl.num_programs(1) - 1)
    def _():
        pstate_ref[...] = tail


def _proj_sample_kernel(x_ref, state_ref, w_in_ref, qg_ref, w_uq_ref, kvg_ref, w_ukt_ref, w_pool_ref, pscale_ref,
                        cos_ref, sin_ref,
                        qcat_ref, kcat_ref, lat_ref, kr_ref, pooly_ref, pstate_ref, *, past_len):
    n_b = state_ref.shape[1]
    n_t = x_ref.shape[0] // n_b
    u, q_lat, q_rope, lat, k_rope = _project(
        x_ref[...].astype(BF16), w_in_ref, qg_ref, w_uq_ref, kvg_ref, w_ukt_ref, cos_ref[...], sin_ref[...])
    for hd in range(MLA_HEADS):
        qcat_ref[hd, :, 0:KV_RANK] = q_lat[hd].astype(BF16)
        qcat_ref[hd, :, KV_RANK:QK_PAD] = _compact_rope(q_rope[hd]).astype(BF16)
    lat_ref[...] = lat
    kc = _compact_rope(k_rope)
    kcat_ref[:, 0:KV_RANK] = lat.astype(BF16)
    kcat_ref[:, KV_RANK:QK_PAD] = _with_ones_lane(kc).astype(BF16)
    kr_ref[...] = kc[:, :QK_ROPE_DIM]

    def ext(jj):
        if jj < POOL_BUF:
            return state_ref[jj]
        return u[(jj - POOL_BUF) * n_b:(jj - POOL_BUF + 1) * n_b, :]

    for t in range(n_t):
        ut = ext(POOL_BUF + t)
        for g, w in enumerate(POOL_WINDOWS):
            cols = slice(POOL_GROUP_DIM * g, POOL_GROUP_DIM * (g + 1))
            ssum = ut[:, cols]
            for k in range(1, w):
                ssum = ssum + ext(POOL_BUF + t - k)[:, cols]
            count = float(min(w, past_len + t + 1))
            y = _pool_group_out(ssum / count - ut[:, cols], g, w_pool_ref, pscale_ref)
            pooly_ref[t * n_b:(t + 1) * n_b, cols] = y.astype(BF16)
    for jj in range(POOL_BUF):
        pstate_ref[jj] = ext(n_t + jj)


def _proj_weight_specs():
    return [
        _full((D_MODEL, D_MODEL)),
        _full((1, Q_RANK)),
        _full((Q_RANK, MLA_HEADS * QK_PAD)),
        _full((1, KV_RANK)),
        _full((MLA_HEADS, QK_NOPE_DIM, KV_RANK)),
        _full((len(POOL_WINDOWS), POOL_GROUP_DIM, POOL_GROUP_DIM)),
        _full((1, POOL_WIDTH)),
    ]


def _proj_prompt(x2d, weights, cos_t, sin_t, n_batch, seq):
    tm = TOKEN_BLOCK
    n = n_batch * seq
    nj = seq // tm
    tok = lambda b, j: (b * nj + j, 0)
    return pl.pallas_call(
        _proj_prompt_kernel,
        grid=(n_batch, nj),
        in_specs=[pl.BlockSpec((tm, D_MODEL), tok)] + _proj_weight_specs() + [
            pl.BlockSpec((tm, LANES), lambda b, j: (j, 0)),
            pl.BlockSpec((tm, LANES), lambda b, j: (j, 0)),
        ],
        out_specs=[
            pl.BlockSpec((MLA_HEADS, tm, QK_PAD), lambda b, j: (0, b * nj + j, 0)),
            pl.BlockSpec((tm, QK_PAD), tok),
            pl.BlockSpec((tm, KV_RANK), tok),
            pl.BlockSpec((tm, QK_ROPE_DIM), tok),
            pl.BlockSpec((tm, POOL_WIDTH), tok),
            pl.BlockSpec((None, POOL_HALO, POOL_WIDTH), lambda b, j: (b, 0, 0)),
        ],
        out_shape=[
            jax.ShapeDtypeStruct((MLA_HEADS, n, QK_PAD), BF16),
            jax.ShapeDtypeStruct((n, QK_PAD), BF16),
            jax.ShapeDtypeStruct((n, KV_RANK), F32),
            jax.ShapeDtypeStruct((n, QK_ROPE_DIM), F32),
            jax.ShapeDtypeStruct((n, POOL_WIDTH), BF16),
            jax.ShapeDtypeStruct((n_batch, POOL_HALO, POOL_WIDTH), F32),
        ],
        scratch_shapes=[pltpu.VMEM((POOL_HALO + tm, POOL_WIDTH), F32)],
        compiler_params=_params("arbitrary", "arbitrary"),
        name="proj_prompt",
    )(x2d, *weights, cos_t, sin_t)


def _proj_sample(x2d, state_t, weights, cos_t, sin_t, past_len):
    n = x2d.shape[0]
    n_b = state_t.shape[1]
    return pl.pallas_call(
        functools.partial(_proj_sample_kernel, past_len=past_len),
        grid=(1,),
        in_specs=[_full((n, D_MODEL)), _full((POOL_BUF, n_b, POOL_WIDTH))] + _proj_weight_specs() + [
            _full((n, LANES)), _full((n, LANES))],
        out_specs=[
            _full((MLA_HEADS, n, QK_PAD)), _full((n, QK_PAD)), _full((n, KV_RANK)), _full((n, QK_ROPE_DIM)),
            _full((n, POOL_WIDTH)), _full((POOL_BUF, n_b, POOL_WIDTH)),
        ],
        out_shape=[
            jax.ShapeDtypeStruct((MLA_HEADS, n, QK_PAD), BF16),
            jax.ShapeDtypeStruct((n, QK_PAD), BF16),
            jax.ShapeDtypeStruct((n, KV_RANK), F32),
            jax.ShapeDtypeStruct((n, QK_ROPE_DIM), F32),
            jax.ShapeDtypeStruct((n, POOL_WIDTH), BF16),
            jax.ShapeDtypeStruct((POOL_BUF, n_b, POOL_WIDTH), F32),
        ],
        compiler_params=_params("arbitrary"),
        name="proj_sample",
    )(x2d, state_t, *weights, cos_t, sin_t)


def _attn_block(q, k, mask, m_ref, acc_ref):
    s = _dot_nt(q, k)
    if mask is not None:
        s = jnp.where(mask, s, -jnp.inf)
    tiles = [s[:, c * LANES:(c + 1) * LANES] for c in range(s.shape[1] // LANES)]
    m_old = m_ref[...]
    m_new = jnp.maximum(m_old, jnp.max(functools.reduce(jnp.maximum, tiles), axis=1, keepdims=True))
    alpha = jnp.exp(m_old - m_new)
    p = jnp.concatenate([jnp.exp(t - m_new) for t in tiles], axis=1).astype(BF16)
    acc_ref[...] = jnp.concatenate([alpha, alpha], axis=1) * acc_ref[...] + _dot(p, k)
    m_ref[...] = m_new


def _attn_output(acc):
    return acc[:, :KV_RANK] / acc[:, ONES_LANE:ONES_LANE + 1]


def _attn_prompt_kernel(q_ref, k_ref, w_uv_ref, out_ref, m_ref, acc_ref):
    i = pl.program_id(1)
    tq = q_ref.shape[1]
    tk = ATTN_K_BLOCK
    hpg = MLA_HEADS // ATTN_GROUPS
    rows = hpg * tq
    m_ref[...] = jnp.full(m_ref.shape, -jnp.inf, F32)
    acc_ref[...] = jnp.zeros(acc_ref.shape, F32)

    def block(jk, masked):
        k = k_ref[pl.ds(pl.multiple_of(jk * tk, tk), tk), :]
        mask = None
        if masked:
            q_pos = i * tq + lax.broadcasted_iota(I32, (rows, tk), 0) % tq
            k_pos = jk * tk + lax.broadcasted_iota(I32, (rows, tk), 1)
            mask = k_pos <= q_pos
        for g in range(ATTN_GROUPS):
            q = q_ref[hpg * g:hpg * (g + 1)].reshape(rows, QK_PAD)
            _attn_block(q, k, mask, m_ref.at[g], acc_ref.at[g])

    n_full = (i * tq + 1) // tk
    n_kv = (i * tq + tq - 1) // tk + 1

    def full_body(jk, c):
        block(jk, False)
        return c

    def diag_body(jk, c):
        block(jk, True)
        return c

    lax.fori_loop(0, n_full, full_body, 0)
    lax.fori_loop(n_full, n_kv, diag_body, 0)

    for g in range(ATTN_GROUPS):
        o = _attn_output(acc_ref[g])
        for hl in range(hpg):
            hd = hpg * g + hl
            oh = o[hl * tq:(hl + 1) * tq, :].astype(BF16)
            out_ref[:, V_HEAD_DIM * hd:V_HEAD_DIM * (hd + 1)] = _dot(oh, w_uv_ref[hd]).astype(BF16)


def _attn_prompt(qcat, kcat, w_uv_h, n_batch, seq):
    tq = ATTN_Q_BLOCK
    nq = seq // tq
    n = n_batch * seq
    rows = (MLA_HEADS // ATTN_GROUPS) * tq
    return pl.pallas_call(
        _attn_prompt_kernel,
        grid=(n_batch, nq),
        in_specs=[
            pl.BlockSpec((MLA_HEADS, tq, QK_PAD), lambda b, i: (0, b * nq + i, 0)),
            pl.BlockSpec((seq, QK_PAD), lambda b, i: (b, 0)),
            _full((MLA_HEADS, KV_RANK, V_HEAD_DIM)),
        ],
        out_specs=pl.BlockSpec((tq, MLA_HEADS * V_HEAD_DIM), lambda b, i: (b * nq + i, 0)),
        out_shape=jax.ShapeDtypeStruct((n, MLA_HEADS * V_HEAD_DIM), BF16),
        scratch_shapes=[pltpu.VMEM((ATTN_GROUPS, rows, LANES), F32), pltpu.VMEM((ATTN_GROUPS, rows, QK_PAD), F32)],
        compiler_params=_params("arbitrary", "arbitrary"),
        name="attn_prompt",
    )(qcat, kcat, w_uv_h)


def _attn_sample_kernel(pt_ref, q_ref, knew_ref, w_uv_ref, *rest, n_pages_step, n_new):
    lat_refs = rest[:n_pages_step]
    rope_refs = rest[n_pages_step:2 * n_pages_step]
    out_ref, m_ref, l_ref, acc_ref, res_ref = rest[2 * n_pages_step:]
    del pt_ref
    s_idx = pl.program_id(1)

    @pl.when(s_idx == 0)
    def _():
        m_ref[...] = jnp.full(m_ref.shape, -jnp.inf, F32)
        l_ref[...] = jnp.zeros(l_ref.shape, F32)
        acc_ref[...] = jnp.zeros(acc_ref.shape, F32)

    def update(s_tiles, values):
        m_old = m_ref[...]
        m_new = jnp.maximum(m_old, jnp.max(functools.reduce(jnp.maximum, s_tiles), axis=1, keepdims=True))
        alpha = jnp.exp(m_old - m_new)
        p_tiles = [jnp.exp(t - m_new) for t in s_tiles]
        row_sum = jnp.sum(functools.reduce(jnp.add, p_tiles), axis=1, keepdims=True)
        pv = functools.reduce(jnp.add, [_dot(p.astype(BF16), v) for p, v in zip(p_tiles, values)])
        l_ref[...] = alpha * l_ref[...] + row_sum
        acc_ref[...] = alpha * acc_ref[...] + pv
        m_ref[...] = m_new

    q = q_ref[...]
    q_lat = q[:, :KV_RANK]
    q_rope = q[:, KV_RANK:KV_RANK + QK_ROPE_DIM]
    lats = [lat_refs[p][...].astype(BF16) for p in range(n_pages_step)]
    update([_dot_nt(q_lat, lats[p]) + _dot(q_rope, rope_refs[p][...].astype(BF16)) for p in range(n_pages_step)],
           lats)

    @pl.when(s_idx == pl.num_programs(1) - 1)
    def _():
        kn = knew_ref[...]
        q_t = lax.broadcasted_iota(I32, (q.shape[0], kn.shape[0]), 0) % n_new
        k_t = lax.broadcasted_iota(I32, (q.shape[0], kn.shape[0]), 1)
        update([jnp.where(k_t <= q_t, _dot_nt(q, kn), -jnp.inf)], [kn[:, :KV_RANK]])
        o = (acc_ref[...] / l_ref[...]).astype(BF16)
        for hd in range(MLA_HEADS):
            res_ref[...] = _dot(o, w_uv_ref[hd])
            out_ref[:, V_HEAD_DIM * hd:V_HEAD_DIM * (hd + 1)] = res_ref[hd * n_new:(hd + 1) * n_new, :]


def _attn_sample(page_table, q_b, knew_b, w_uv_h, cache_lat, cache_rope, layer):
    n_b, n_pages = page_table.shape
    page = cache_lat.shape[2]
    pps = min(PAGES_PER_STEP, n_pages)
    assert n_pages % pps == 0
    n_steps = n_pages // pps
    rows = q_b.shape[1]
    n_new = rows // MLA_HEADS
    t_pad = knew_b.shape[1]

    def page_map(p):
        return lambda b, s, pt: (layer, pt[b * n_pages + s * pps + p], 0, 0)

    lat_specs = [pl.BlockSpec((None, None, page, KV_RANK), page_map(p)) for p in range(pps)]
    rope_specs = [pl.BlockSpec((None, None, QK_ROPE_DIM, page), page_map(p)) for p in range(pps)]
    grid_spec = pltpu.PrefetchScalarGridSpec(
        num_scalar_prefetch=1,
        grid=(n_b, n_steps),
        in_specs=[
            pl.BlockSpec((None, rows, QK_PAD), lambda b, s, pt: (b, 0, 0)),
            pl.BlockSpec((None, t_pad, QK_PAD), lambda b, s, pt: (b, 0, 0)),
            pl.BlockSpec((MLA_HEADS, KV_RANK, V_HEAD_DIM), lambda b, s, pt: (0, 0, 0)),
        ] + lat_specs + rope_specs,
        out_specs=pl.BlockSpec((None, n_new, MLA_HEADS * V_HEAD_DIM), lambda b, s, pt: (b, 0, 0)),
        scratch_shapes=[
            pltpu.VMEM((rows, LANES), F32), pltpu.VMEM((rows, LANES), F32), pltpu.VMEM((rows, KV_RANK), F32),
            pltpu.VMEM((rows, V_HEAD_DIM), F32),
        ],
    )
    return pl.pallas_call(
        functools.partial(_attn_sample_kernel, n_pages_step=pps, n_new=n_new),
        grid_spec=grid_spec,
        out_shape=jax.ShapeDtypeStruct((n_b, n_new, MLA_HEADS * V_HEAD_DIM), F32),
        compiler_params=_params("arbitrary", "arbitrary"),
        name="attn_sample",
    )(page_table.reshape(-1), q_b, knew_b, w_uv_h, *([cache_lat] * pps), *([cache_rope] * pps))


def _mem_kv_kernel(mem_ref, w_mk_ref, w_mv_ref, k_ref, v_ref, kb_ref, vb_ref):
    m = mem_ref[...].astype(BF16)
    k = _dot(m, w_mk_ref[...])
    v = _dot(m, w_mv_ref[...])
    k_ref[...] = k
    v_ref[...] = v
    kb_ref[...] = k.astype(BF16)
    vb_ref[...] = v.astype(BF16)


def _mem_kv(mem2d, w_mk, w_mv, n_batch, n_mem):
    blk = pl.BlockSpec((n_mem, MEM_WIDTH), lambda b: (b, 0))
    n = n_batch * n_mem
    return pl.pallas_call(
        _mem_kv_kernel,
        grid=(n_batch,),
        in_specs=[pl.BlockSpec((n_mem, D_MODEL), lambda b: (b, 0)), _full((D_MODEL, MEM_WIDTH)),
                  _full((D_MODEL, MEM_WIDTH))],
        out_specs=[blk, blk, blk, blk],
        out_shape=[jax.ShapeDtypeStruct((n, MEM_WIDTH), F32), jax.ShapeDtypeStruct((n, MEM_WIDTH), F32),
                   jax.ShapeDtypeStruct((n, MEM_WIDTH), BF16), jax.ShapeDtypeStruct((n, MEM_WIDTH), BF16)],
        compiler_params=_params("arbitrary"),
        name="mem_kv",
    )(mem2d, w_mk, w_mv)


def _mix_ln1(x, pool_y, mla_y, w_out_a_ref, w_out_b_ref, g_ref, b_ref):
    mix = _dot(pool_y, w_out_a_ref[...]) + _dot(mla_y, w_out_b_ref[...])
    return _layer_norm(DEEPNORM_ALPHA * x + mix, g_ref[...], b_ref[...])


def _route(x2, w_r_ref, b_r_ref, carry):
    m = x2.shape[0]
    x_hi = x2.astype(BF16)
    x_lo = (x2 - x_hi.astype(F32)).astype(BF16)
    logits_t = _dot_nt(w_r_ref[0], x_hi) + (_dot_nt(w_r_ref[0], x_lo) + _dot_nt(w_r_ref[1], x_hi))
    logits = logits_t.T + b_r_ref[...]
    e_idx = lax.broadcasted_iota(I32, (m, N_EXPERTS), 1).astype(F32)
    work = logits
    vals, picks = [], []
    for _ in range(TOP_K):
        mx = jnp.max(work, axis=1, keepdims=True)
        pick = jnp.min(jnp.where(work == mx, e_idx, float(N_EXPERTS)), axis=1, keepdims=True)
        vals.append(mx)
        picks.append(pick)
        work = jnp.where(e_idx == pick, -jnp.inf, work)
    hot = jnp.where(work == -jnp.inf, 1.0, 0.0)
    exps = [jnp.exp(v - vals[0]) for v in vals]
    denom = exps[0] + exps[1] + exps[2] + exps[3]
    r_i = lax.broadcasted_iota(I32, (m, m), 0)
    c_i = lax.broadcasted_iota(I32, (m, m), 1)
    strict_lower = jnp.where(c_i < r_i, 1.0, 0.0).astype(BF16)
    before = _dot(strict_lower, hot.astype(BF16)) + carry
    lane4 = lax.broadcasted_iota(I32, (m, TOP_K), 1)
    idx = jnp.zeros((m, TOP_K), I32)
    gates = jnp.zeros((m, TOP_K), F32)
    rank = jnp.zeros((m, TOP_K), I32)
    for k in range(TOP_K):
        rk = jnp.sum(jnp.where(e_idx == picks[k], before, 0.0), axis=1, keepdims=True).astype(I32)
        idx = jnp.where(lane4 == k, picks[k].astype(I32), idx)
        gates = jnp.where(lane4 == k, exps[k] / denom, gates)
        rank = jnp.where(lane4 == k, rk, rank)
    return idx, gates, rank, carry + jnp.sum(hot, axis=0, keepdims=True)


def _mo_ln2_route(x1, o_bf, w_mo_ref, g_ref, b_ref, w_r_ref, b_r_ref, carry):
    x2 = _layer_norm(DEEPNORM_ALPHA * x1 + _dot(o_bf, w_mo_ref[...]), g_ref[...], b_ref[...])
    return (x2,) + _route(x2, w_r_ref, b_r_ref, carry)


def _post_prompt_kernel(x_ref, pooly_ref, mlay_ref, mk_ref, mv_ref,
                        w_out_a_ref, w_out_b_ref, g1_ref, b1_ref, w_mq_ref, w_mo_ref, g2_ref, b2_ref,
                        w_r_ref, b_r_ref,
                        x2_ref, idx_ref, gate_ref, rank_ref, cnt_ref, carry_ref):
    first = jnp.logical_and(pl.program_id(0) == 0, pl.program_id(1) == 0)

    @pl.when(first)
    def _():
        carry_ref[...] = jnp.zeros(carry_ref.shape, F32)

    x1 = _mix_ln1(x_ref[...], pooly_ref[...], mlay_ref[...], w_out_a_ref, w_out_b_ref, g1_ref, b1_ref)
    qm = (_dot(x1.astype(BF16), w_mq_ref[...]) * MEM_SCALE).astype(BF16)
    outs = []
    for hd in range(MEM_HEADS):
        cols = slice(MEM_HEAD_DIM * hd, MEM_HEAD_DIM * (hd + 1))
        s = _dot_nt(qm[:, cols], mk_ref[:, cols])
        p = jnp.exp(s - jnp.max(s, axis=1, keepdims=True))
        o = _dot(p.astype(BF16), mv_ref[:, cols]) / jnp.sum(p, axis=1, keepdims=True)
        outs.append(o.astype(BF16))
    o_all = jnp.concatenate(outs, axis=1)
    x2, idx, gates, rank, carry = _mo_ln2_route(x1, o_all, w_mo_ref, g2_ref, b2_ref, w_r_ref, b_r_ref,
                                                carry_ref[...])
    _store_row_tiles(x2_ref, (), x2)
    idx_ref[...] = idx
    gate_ref[...] = gates
    rank_ref[...] = rank
    carry_ref[...] = carry
    cnt_ref[...] = carry


def _post_weight_specs():
    half = (POOL_WIDTH, D_MODEL)
    return [
        _full(half), _full(half), _full((1, D_MODEL)), _full((1, D_MODEL)),
        _full((D_MODEL, MEM_WIDTH)), _full((MEM_WIDTH, D_MODEL)), _full((1, D_MODEL)), _full((1, D_MODEL)),
        _full((2, N_EXPERTS, D_MODEL)), _full((1, N_EXPERTS)),
    ]


def _post_prompt(x2d, pool_y, mla_y, mk_b, mv_b, weights, n_batch, seq, n_mem, n_all):
    tm = TOKEN_BLOCK
    nj = seq // tm
    tok = lambda b, j: (b * nj + j, 0)
    return pl.pallas_call(
        _post_prompt_kernel,
        grid=(n_batch, nj),
        in_specs=[
            pl.BlockSpec((tm, D_MODEL), tok), pl.BlockSpec((tm, POOL_WIDTH), tok), pl.BlockSpec((tm, POOL_WIDTH), tok),
            pl.BlockSpec((n_mem, MEM_WIDTH), lambda b, j: (b, 0)), pl.BlockSpec((n_mem, MEM_WIDTH), lambda b, j: (b, 0)),
        ] + _post_weight_specs(),
        out_specs=[
            pl.BlockSpec((tm * ROW_TILE, LANES), tok), pl.BlockSpec((tm, TOP_K), tok), pl.BlockSpec((tm, TOP_K), tok),
            pl.BlockSpec((tm, TOP_K), tok), _full((1, N_EXPERTS)),
        ],
        out_shape=[
            jax.ShapeDtypeStruct((n_all * ROW_TILE, LANES), F32), jax.ShapeDtypeStruct((n_all, TOP_K), I32),
            jax.ShapeDtypeStruct((n_all, TOP_K), F32), jax.ShapeDtypeStruct((n_all, TOP_K), I32),
            jax.ShapeDtypeStruct((1, N_EXPERTS), F32),
        ],
        scratch_shapes=[pltpu.VMEM((1, N_EXPERTS), F32)],
        compiler_params=_params("arbitrary", "arbitrary"),
        name="post_prompt",
    )(x2d, pool_y, mla_y, mk_b, mv_b, *weights)


def _mix_sample_kernel(x_ref, pooly_ref, mlay_ref, w_out_a_ref, w_out_b_ref, g1_ref, b1_ref, w_mq_ref,
                       x1_ref, qm_ref):
    x1 = _mix_ln1(x_ref[...], pooly_ref[...], mlay_ref[...], w_out_a_ref, w_out_b_ref, g1_ref, b1_ref)
    x1_ref[...] = x1
    qm_ref[...] = (_dot(x1.astype(BF16), w_mq_ref[...]) * MEM_SCALE).astype(BF16)


def _mix_sample(x2d, pool_y, mla_y, weights):
    n = x2d.shape[0]
    w_out_a, w_out_b, g1, b1, w_mq = weights[:5]
    return pl.pallas_call(
        _mix_sample_kernel,
        grid=(1,),
        in_specs=[_full((n, D_MODEL)), _full((n, POOL_WIDTH)), _full((n, POOL_WIDTH))] + _post_weight_specs()[:5],
        out_specs=[_full((n, D_MODEL)), _full((n, MEM_WIDTH))],
        out_shape=[jax.ShapeDtypeStruct((n, D_MODEL), F32), jax.ShapeDtypeStruct((n, MEM_WIDTH), BF16)],
        compiler_params=_params("arbitrary"),
        name="mix_sample",
    )(x2d, pool_y, mla_y, w_out_a, w_out_b, g1, b1, w_mq)


def _mem_attn_sample_kernel(q_ref, k_ref, v_ref, out_ref, res_ref, *, n_new):
    for b in range(q_ref.shape[0]):
        s = _dot_nt(q_ref[b], k_ref[b].astype(BF16))
        p = jnp.exp(s - jnp.max(s, axis=1, keepdims=True))
        res_ref[...] = _dot(p.astype(BF16), v_ref[b].astype(BF16)) / jnp.sum(p, axis=1, keepdims=True)
        for hd in range(MEM_HEADS):
            cols = slice(MEM_HEAD_DIM * hd, MEM_HEAD_DIM * (hd + 1))
            out_ref[b, :, cols] = res_ref[hd * n_new:(hd + 1) * n_new, cols].astype(BF16)


def _mem_attn_sample(q_bd, mem_k, mem_v, n_new):
    n_b, rows, _ = q_bd.shape
    n_mem = mem_k.shape[1]
    gb = min(MEM_BATCH_BLOCK, n_b)
    assert n_b % gb == 0
    blk = lambda i: (i, 0, 0)
    return pl.pallas_call(
        functools.partial(_mem_attn_sample_kernel, n_new=n_new),
        grid=(n_b // gb,),
        in_specs=[pl.BlockSpec((gb, rows, MEM_WIDTH), blk), pl.BlockSpec((gb, n_mem, MEM_WIDTH), blk),
                  pl.BlockSpec((gb, n_mem, MEM_WIDTH), blk)],
        out_specs=pl.BlockSpec((gb, n_new, MEM_WIDTH), blk),
        out_shape=jax.ShapeDtypeStruct((n_b, n_new, MEM_WIDTH), BF16),
        scratch_shapes=[pltpu.VMEM((rows, MEM_WIDTH), F32)],
        compiler_params=_params("arbitrary"),
        name="mem_attn_sample",
    )(q_bd, mem_k, mem_v)


def _route_sample_kernel(x1_ref, o_ref, cnt_in_ref, w_mo_ref, g2_ref, b2_ref, w_r_ref, b_r_ref,
                         x2_in, idx_in, gate_in, rank_in,
                         x2_ref, idx_ref, gate_ref, rank_ref, cnt_ref):
    del x2_in, idx_in, gate_in, rank_in
    x2, idx, gates, rank, carry = _mo_ln2_route(x1_ref[...], o_ref[...], w_mo_ref, g2_ref, b2_ref, w_r_ref, b_r_ref,
                                                cnt_in_ref[...])
    _store_row_tiles(x2_ref, (), x2)
    idx_ref[...] = idx
    gate_ref[...] = gates
    rank_ref[...] = rank
    cnt_ref[...] = carry


def _route_sample(x1, o_bf, cnt_in, weights, x2_all, idx_all, gate_all, rank_all, n_prompt):
    n = x1.shape[0]
    assert n_prompt % n == 0
    tail = lambda i: (n_prompt // n, 0)
    anyspec = pl.BlockSpec(memory_space=pl.ANY)
    w_mo, g2, b2, w_r, b_r = weights[5:]
    return pl.pallas_call(
        _route_sample_kernel,
        grid=(1,),
        in_specs=[_full((n, D_MODEL)), _full((n, MEM_WIDTH)), _full((1, N_EXPERTS))] + _post_weight_specs()[5:] + [
            anyspec, anyspec, anyspec, anyspec],
        out_specs=[pl.BlockSpec((n * ROW_TILE, LANES), tail), pl.BlockSpec((n, TOP_K), tail), pl.BlockSpec((n, TOP_K), tail),
                   pl.BlockSpec((n, TOP_K), tail), _full((1, N_EXPERTS))],
        out_shape=[jax.ShapeDtypeStruct(x2_all.shape, F32), jax.ShapeDtypeStruct(idx_all.shape, I32),
                   jax.ShapeDtypeStruct(gate_all.shape, F32), jax.ShapeDtypeStruct(rank_all.shape, I32),
                   jax.ShapeDtypeStruct((1, N_EXPERTS), F32)],
        input_output_aliases={8: 0, 9: 1, 10: 2, 11: 3},
        compiler_params=_params("arbitrary"),
        name="route_sample",
    )(x1, o_bf, cnt_in, w_mo, g2, b2, w_r, b_r, x2_all, idx_all, gate_all, rank_all)


def _dispatch_kernel(pad_start_ref, n_pad_ref, n_used_ref, idx_hbm, x_ref, xrows_hbm,
                     idx_smem, zbuf, idx_sem, row_sem, pad_sem):
    i = pl.program_id(0)
    last = pl.num_programs(0) - 1
    tm = x_ref.shape[0] // ROW_TILE
    blk = zbuf.shape[0]
    n_blk = xrows_hbm.shape[0] // blk

    def idx_copy(row, s):
        return pltpu.make_async_copy(idx_hbm.at[row], idx_smem.at[s], idx_sem.at[s])

    def pad_copy(e, r):
        dst = pl.multiple_of((pad_start_ref[e] + r) * ROW_TILE, ROW_TILE)
        return pltpu.make_async_copy(zbuf.at[pl.ds(0, ROW_TILE)], xrows_hbm.at[pl.ds(dst, ROW_TILE)], pad_sem)

    def tail_copy(j):
        return pltpu.make_async_copy(zbuf, xrows_hbm.at[pl.ds(pl.multiple_of(j * blk, blk), blk)], pad_sem)

    @pl.when(i == 0)
    def _():
        idx_copy(0, 0).start()
        zbuf[...] = jnp.zeros(zbuf.shape, F32)
        for e in range(N_EXPERTS):
            def start(r, c, e=e):
                pad_copy(e, r).start()
                return c

            lax.fori_loop(0, n_pad_ref[e], start, 0)

        def start_tail(j, c):
            tail_copy(j).start()
            return c

        lax.fori_loop(n_used_ref[0], n_blk, start_tail, 0)
        for e in range(N_EXPERTS):
            def wait(r, c, e=e):
                pad_copy(e, r).wait()
                return c

            lax.fori_loop(0, n_pad_ref[e], wait, 0)

        def wait_tail(j, c):
            tail_copy(j).wait()
            return c

        lax.fori_loop(n_used_ref[0], n_blk, wait_tail, 0)

    def step(slot):
        idx_copy(i, slot).wait()

        @pl.when(i < last)
        def _():
            idx_copy(i + 1, 1 - slot).start()

        for t in range(tm):
            for k in range(TOP_K):
                dst = pl.multiple_of(idx_smem[slot, TOP_K * t + k], ROW_TILE)
                pltpu.make_async_copy(x_ref.at[pl.ds(ROW_TILE * t, ROW_TILE)], xrows_hbm.at[pl.ds(dst, ROW_TILE)],
                                      row_sem).start(priority=k % 2)
        for k in range(TOP_K):
            pltpu.make_async_copy(x_ref, x_ref, row_sem).wait()

    for parity in range(2):
        pl.when(i % 2 == parity)(functools.partial(step, parity))


def _dispatch(pad_start, n_pad, n_used, idx_rows, x_all, n_rows):
    tm = DISPATCH_BLOCK
    n_steps = idx_rows.shape[0]
    assert x_all.shape[0] == n_steps * tm * ROW_TILE and idx_rows.shape[1] == TOP_K * tm
    assert n_rows % EXPERT_ROWS == 0
    grid_spec = pltpu.PrefetchScalarGridSpec(
        num_scalar_prefetch=3,
        grid=(n_steps,),
        in_specs=[pl.BlockSpec(memory_space=pl.ANY),
                  pl.BlockSpec((tm * ROW_TILE, LANES), lambda i, ps, npd, nu: (i, 0))],
        out_specs=pl.BlockSpec(memory_space=pl.ANY),
        scratch_shapes=[pltpu.SMEM((2, TOP_K * tm), I32), pltpu.VMEM((EXPERT_ROWS * ROW_TILE, LANES), F32),
                        pltpu.SemaphoreType.DMA((2,)), pltpu.SemaphoreType.DMA, pltpu.SemaphoreType.DMA],
    )
    return pl.pallas_call(
        _dispatch_kernel,
        grid_spec=grid_spec,
        out_shape=jax.ShapeDtypeStruct((n_rows * ROW_TILE, LANES), F32),
        compiler_params=_params("arbitrary"),
        name="moe_dispatch",
    )(pad_start, n_pad, n_used, idx_rows, x_all)


def _moe_kernel(blk_exp_ref, blk_row_ref, idx_hbm, x_ref, wgu_ref, bgu_ref, wd_ref, bd_ref, y_hbm,
                idx_smem, ybuf, idx_sem, y_sem):
    del blk_exp_ref, blk_row_ref
    i = pl.program_id(0)
    last = pl.num_programs(0) - 1
    tr = x_ref.shape[0] // ROW_TILE
    n_chunks = D_EXPERT // EXPERT_CHUNK

    def idx_copy(row, s):
        return pltpu.make_async_copy(idx_hbm.at[row], idx_smem.at[s], idx_sem.at[s])

    def scatter(s_idx, ys, rows):
        for r in rows:
            dst = pl.multiple_of(idx_smem[s_idx, r], ROW_TILE)
            pltpu.make_async_copy(ybuf.at[ys, pl.ds(ROW_TILE * r, ROW_TILE)], y_hbm.at[pl.ds(dst, ROW_TILE)],
                                  y_sem.at[ys]).start(priority=r % 2)

    def wait_rows(s):
        pltpu.make_async_copy(ybuf.at[s], ybuf.at[s], y_sem.at[s]).wait()

    @pl.when(i == 0)
    def _():
        idx_copy(0, 0).start()
        ybuf[1] = jnp.zeros(ybuf.shape[1:], F32)

    def step(slot):
        nxt = 1 - slot
        idx_copy(i, slot).wait()
        idx_copy(i + 1, nxt).start()
        xb = _load_row_tiles(x_ref, (), tr).astype(BF16)
        acc = jnp.zeros((tr, D_MODEL), F32) + bd_ref[...]

        groups = 3 * n_chunks
        per_group = -(-tr // groups)

        def issue(g):
            scatter(slot, nxt, range(g * per_group, min((g + 1) * per_group, tr)))

        for c in range(n_chunks):
            cg = slice(c * EXPERT_CHUNK, (c + 1) * EXPERT_CHUNK)
            cu = slice(D_EXPERT + c * EXPERT_CHUNK, D_EXPERT + (c + 1) * EXPERT_CHUNK)
            gate = jnp.minimum(_dot(xb, wgu_ref[:, cg]) + bgu_ref[:, cg], SWIGLU_LIMIT)
            issue(3 * c)
            up = jnp.clip(_dot(xb, wgu_ref[:, cu]) + bgu_ref[:, cu], -SWIGLU_LIMIT, SWIGLU_LIMIT)
            issue(3 * c + 1)
            act = gate * (1.0 / (1.0 + jnp.exp(-SWIGLU_ALPHA * gate))) * (up + 1.0)
            acc = acc + _dot(act.astype(BF16), wd_ref[cg, :])
            issue(3 * c + 2)

        @pl.when(i >= 1)
        def _():
            wait_rows(slot)

        _store_row_tiles(ybuf, (slot,), acc)

        @pl.when(i == last)
        def _():
            idx_copy(i + 1, nxt).wait()
            wait_rows(nxt)

    for parity in range(2):
        pl.when(i % 2 == parity)(functools.partial(step, parity))


def _moe(blk_exp, blk_row, idx_rows, x_rows, w_gu, b_gu, w_d, b_d, n_assign):
    tr = EXPERT_ROWS
    n_steps = blk_exp.shape[0]
    assert n_steps >= 2 and idx_rows.shape == (n_steps + 1, tr)
    anyspec = pl.BlockSpec(memory_space=pl.ANY)
    emap3 = lambda i, be, br: (be[i], 0, 0)
    grid_spec = pltpu.PrefetchScalarGridSpec(
        num_scalar_prefetch=2,
        grid=(n_steps,),
        in_specs=[
            anyspec, pl.BlockSpec((tr * ROW_TILE, LANES), lambda i, be, br: (br[i], 0)),
            pl.BlockSpec((None, D_MODEL, 2 * D_EXPERT), emap3), pl.BlockSpec((None, 1, 2 * D_EXPERT), emap3),
            pl.BlockSpec((None, D_EXPERT, D_MODEL), emap3), pl.BlockSpec((None, 1, D_MODEL), emap3),
        ],
        out_specs=anyspec,
        scratch_shapes=[
            pltpu.SMEM((2, tr), I32), pltpu.VMEM((2, tr * ROW_TILE, LANES), F32),
            pltpu.SemaphoreType.DMA((2,)), pltpu.SemaphoreType.DMA((2,)),
        ],
    )
    return pl.pallas_call(
        _moe_kernel,
        grid_spec=grid_spec,
        out_shape=jax.ShapeDtypeStruct(((n_assign + 2 * tr) * ROW_TILE, LANES), F32),
        compiler_params=_params("arbitrary"),
        name="moe_experts",
    )(blk_exp, blk_row, idx_rows, x_rows, w_gu, b_gu, w_d, b_d)


def _combine_kernel(x2_ref, gate_ref, g3_ref, b3_ref, *rest):
    y_refs, out_ref = rest[:TOP_K], rest[TOP_K]
    gates = gate_ref[...]
    tm = out_ref.shape[0]
    moe = jnp.zeros(out_ref.shape, F32)
    for k in range(TOP_K):
        moe = moe + gates[:, k:k + 1] * _load_row_tiles(y_refs[k], (), tm)
    out_ref[...] = _layer_norm(DEEPNORM_ALPHA * _load_row_tiles(x2_ref, (), tm) + moe, g3_ref[...], b3_ref[...])


def _combine(x2_all, y_tok, gates_all, g3, b3, row0, n_rows, tm):
    n_all = gates_all.shape[0]
    assert row0 % tm == 0 and n_rows % tm == 0 and n_all % tm == 0
    off = row0 // tm
    tok = lambda i: (off + i, 0)
    tiles = (tm * ROW_TILE, LANES)
    slot_specs = [pl.BlockSpec(tiles, functools.partial(lambda i, k: (k * (n_all // tm) + off + i, 0), k=k))
                  for k in range(TOP_K)]
    return pl.pallas_call(
        _combine_kernel,
        grid=(n_rows // tm,),
        in_specs=[pl.BlockSpec(tiles, tok), pl.BlockSpec((tm, TOP_K), tok), _full((1, D_MODEL)),
                  _full((1, D_MODEL))] + slot_specs,
        out_specs=pl.BlockSpec((tm, D_MODEL), lambda i: (i, 0)),
        out_shape=jax.ShapeDtypeStruct((n_rows, D_MODEL), F32),
        compiler_params=_params("arbitrary"),
        name="combine_ln3",
    )(x2_all, gates_all, g3, b3, *([y_tok] * TOP_K))


def _rope_tables(pos):
    inv_freq = ROPE_THETA ** (-jnp.arange(ROPE_HALF, dtype=F32) / ROPE_HALF)
    ang = pos.astype(F32)[:, None] * inv_freq[None, :]
    c, s, z = jnp.cos(ang), jnp.sin(ang), jnp.zeros_like(ang)
    return jnp.concatenate([c, z, c, z], axis=1), jnp.concatenate([-s, z, s, z], axis=1)


def _pad_rope_cols(w):
    z = jnp.zeros(w.shape[:-1] + (ROPE_HALF,), w.dtype)
    return jnp.concatenate([w[..., :ROPE_HALF], z, w[..., ROPE_HALF:], z], axis=-1)


def _layer_weights(l, w_in, w_pool, pool_scale, q_norm_g, w_uq, kv_norm_g, w_uk, w_uv, w_out, ln1_g, ln1_b,
                   w_mq, w_mo, ln2_g, ln2_b, w_router, b_router):
    o3 = POOL_WIDTH + Q_RANK + KV_RANK
    w_in_p = jnp.concatenate([w_in[l][:, :o3], _pad_rope_cols(w_in[l][:, o3:])], axis=1).astype(BF16)
    uq = w_uq[l].reshape(Q_RANK, MLA_HEADS, QK_NOPE_DIM + QK_ROPE_DIM)
    uq_p = jnp.concatenate([uq[..., :QK_NOPE_DIM], _pad_rope_cols(uq[..., QK_NOPE_DIM:])], axis=-1)
    uq_p = uq_p.reshape(Q_RANK, MLA_HEADS * QK_PAD).astype(BF16)
    w_ukt = jnp.transpose(w_uk[l], (1, 2, 0)).astype(BF16)
    w_uv_h = jnp.transpose(w_uv[l], (1, 0, 2)).astype(BF16)
    row = lambda v: v[l].reshape(1, -1)
    w_rt = w_router[l].T
    w_rt_hi = w_rt.astype(BF16)
    w_r_split = jnp.stack([w_rt_hi, (w_rt - w_rt_hi.astype(F32)).astype(BF16)])
    proj =[w_in_p, row(q_norm_g), uq_p, row(kv_norm_g), w_ukt, w_pool[l].astype(BF16), row(pool_scale)]
    post = [w_out[l][:POOL_WIDTH].astype(BF16), w_out[l][POOL_WIDTH:].astype(BF16), row(ln1_g), row(ln1_b),
            w_mq[l].astype(BF16), w_mo[l].astype(BF16), row(ln2_g), row(ln2_b), w_r_split, row(b_router)]
    return proj, w_uv_h, post


def _routing_tables(idx_all, rank_all, counts, n_blocks):
    tr = EXPERT_ROWS
    cnt = counts.reshape(-1).astype(I32)
    padded = ((cnt + tr - 1) // tr) * tr
    pend = jnp.cumsum(padded)
    pstart = pend - padded
    dest2d = pstart[idx_all] + rank_all
    dest = dest2d.reshape(-1)
    n_all = idx_all.shape[0]
    m = dest.shape[0]
    n_rows = n_blocks * tr
    disp_rows = (dest2d * ROW_TILE).reshape(n_all // DISPATCH_BLOCK, DISPATCH_BLOCK * TOP_K)
    assign = (jnp.arange(TOP_K, dtype=I32)[None, :] * n_all + jnp.arange(n_all, dtype=I32)[:, None]).reshape(-1)
    _, packed = lax.sort_key_val(dest, assign)
    start = jnp.cumsum(cnt) - cnt
    shift = pstart - start
    none = jnp.full((n_rows,), -1, I32)
    doubled = jnp.concatenate([none, packed, none[:n_rows - m]])
    rows = jnp.arange(n_rows, dtype=I32)
    row_src = none
    for e in range(N_EXPERTS):
        moved = lax.dynamic_slice(doubled, (n_rows - shift[e],), (n_rows,))
        row_src = jnp.where((rows >= pstart[e]) & (rows < pstart[e] + cnt[e]), moved, row_src)
    spare = m + jnp.arange(n_rows, dtype=I32) % (2 * tr)
    dst_rows = (jnp.where(row_src >= 0, row_src, spare) * ROW_TILE).reshape(n_blocks, tr)
    spare_row = spare[:tr][None] * ROW_TILE
    idx_rows = jnp.concatenate([spare_row, dst_rows, spare_row], axis=0)
    n_used = pend[-1] // tr
    blk_row = jnp.minimum(jnp.arange(n_blocks + 1, dtype=I32), n_used - 1)
    blk_exp = jnp.sum((pend[None, :] <= (blk_row * tr)[:, None]).astype(I32), axis=1)
    return (jnp.clip(blk_exp, 0, N_EXPERTS - 1), blk_row, idx_rows, disp_rows, pstart + cnt, padded - cnt,
            n_used.astype(I32).reshape(1))


def kernel(x_prompt, x_sample, mem_prompt, cache_kv_latent, cache_k_rope, cache_mem_k, cache_mem_v, state_pool,
           page_table, w_in, w_pool, pool_scale, q_norm_g, w_uq, kv_norm_g, w_uk, w_uv, w_out, ln1_g, ln1_b,
           w_mq, w_mk, w_mv, w_mo, ln2_g, ln2_b, w_router, b_router, w_gate_up, b_gate_up, w_down, b_down,
           ln3_g, ln3_b):
    n_bp, seq, _ = x_prompt.shape
    n_bs, n_new, _ = x_sample.shape
    n_mem = mem_prompt.shape[1]
    page = cache_kv_latent.shape[2]
    past_len = page_table.shape[1] * page
    n_p = n_bp * seq
    n_s = n_bs * n_new
    n_all = n_p + n_s
    n_assign = n_all * TOP_K
    assert w_in.shape[0] == DEPTH == 1
    assert seq % TOKEN_BLOCK == 0 and seq % ATTN_K_BLOCK == 0 and seq % ATTN_Q_BLOCK == 0 and seq >= POOL_HALO
    assert n_p % n_s == 0 and (n_s % TOKEN_BLOCK == 0 or TOKEN_BLOCK % n_s == 0)
    l = 0

    proj_w, w_uv_h, post_w = _layer_weights(l, w_in, w_pool, pool_scale, q_norm_g, w_uq, kv_norm_g, w_uk, w_uv,
                                            w_out, ln1_g, ln1_b, w_mq, w_mo, ln2_g, ln2_b, w_router, b_router)

    xp = x_prompt.reshape(n_p, D_MODEL)
    cos_p, sin_p = _rope_tables(jnp.arange(seq))
    qcat, kcat, lat_p, kr_p, pooly_p, pstate_p = _proj_prompt(xp, proj_w, cos_p, sin_p, n_bp, seq)
    mlay_p = _attn_prompt(qcat, kcat, w_uv_h, n_bp, seq)
    mk, mv, mk_b, mv_b = _mem_kv(mem_prompt.reshape(n_bp * n_mem, D_MODEL), w_mk[l].astype(BF16),
                                 w_mv[l].astype(BF16), n_bp, n_mem)
    x2_all, idx_all, gate_all, rank_all, cnt_p = _post_prompt(xp, pooly_p, mlay_p, mk_b, mv_b, post_w,
                                                              n_bp, seq, n_mem, n_all)

    xs = jnp.transpose(x_sample, (1, 0, 2)).reshape(n_s, D_MODEL)
    state_t = jnp.transpose(state_pool[l], (1, 0, 2))
    cos_s, sin_s = _rope_tables(jnp.repeat(past_len + jnp.arange(n_new), n_bs))
    qcat_s, kcat_s, lat_s, kr_s, pooly_s, pstate_s = _proj_sample(xs, state_t, proj_w, cos_s, sin_s, past_len)
    q_b = jnp.transpose(qcat_s.reshape(MLA_HEADS, n_new, n_bs, QK_PAD), (2, 0, 1, 3)).reshape(
        n_bs, MLA_HEADS * n_new, QK_PAD)
    knew_b = jnp.transpose(kcat_s.reshape(n_new, n_bs, QK_PAD), (1, 0, 2))
    knew_b = jnp.pad(knew_b, ((0, 0), (0, LANES - n_new), (0, 0)))
    mlay_b = _attn_sample(page_table, q_b, knew_b, w_uv_h, cache_kv_latent,
                          jnp.transpose(cache_k_rope, (0, 1, 3, 2)), l)
    mlay_s = jnp.transpose(mlay_b, (1, 0, 2)).reshape(n_s, POOL_WIDTH).astype(BF16)
    x1_s, qm_s = _mix_sample(xs, pooly_s, mlay_s, post_w)
    qm_b = jnp.transpose(qm_s.reshape(n_new, n_bs, MEM_WIDTH), (1, 0, 2))
    head_of_col = jnp.arange(MEM_WIDTH) // MEM_HEAD_DIM
    head_mask = (head_of_col[None, :] == jnp.arange(MEM_HEADS)[:, None]).astype(BF16)
    q_bd = (qm_b[:, None, :, :] * head_mask[None, :, None, :]).reshape(n_bs, MEM_HEADS * n_new, MEM_WIDTH)
    o_b = _mem_attn_sample(q_bd, cache_mem_k[l].reshape(n_bs, n_mem, MEM_WIDTH),
                           cache_mem_v[l].reshape(n_bs, n_mem, MEM_WIDTH), n_new)
    o_s = jnp.transpose(o_b, (1, 0, 2)).reshape(n_s, MEM_WIDTH)
    x2_all, idx_all, gate_all, rank_all, counts = _route_sample(x1_s, o_s, cnt_p, post_w, x2_all, idx_all, gate_all,
                                                               rank_all, n_p)

    n_blocks = -(-n_assign // EXPERT_ROWS) + N_EXPERTS + 1
    blk_exp, blk_row, idx_rows, disp_rows, pad_start, n_pad, n_used = _routing_tables(idx_all, rank_all, counts,
                                                                                      n_blocks)
    x_rows = _dispatch(pad_start, n_pad, n_used, disp_rows, x2_all, n_blocks * EXPERT_ROWS)
    y_tok = _moe(blk_exp, blk_row, idx_rows, x_rows, w_gate_up[l].astype(BF16), b_gate_up[l][:, None, :],
                 w_down[l].astype(BF16), b_down[l][:, None, :], n_assign)
    g3, b3 = ln3_g[l].reshape(1, -1), ln3_b[l].reshape(1, -1)
    y_p = _combine(x2_all, y_tok, gate_all, g3, b3, 0, n_p, TOKEN_BLOCK)
    y_s = _combine(x2_all, y_tok, gate_all, g3, b3, n_p, n_s, min(TOKEN_BLOCK, n_s))

    def from_steps(a, width):
        return jnp.transpose(a.reshape(n_new, n_bs, width), (1, 0, 2))[None]

    return (
        y_p.reshape(n_bp, seq, D_MODEL),
        from_steps(y_s, D_MODEL)[0],
        lat_p.reshape(1, n_bp, seq, KV_RANK),
        kr_p.reshape(1, n_bp, seq, QK_ROPE_DIM),
        pstate_p[None, :, POOL_HALO - POOL_BUF:, :],
        mk.reshape(1, n_bp, n_mem, MEM_HEADS, MEM_HEAD_DIM),
        mv.reshape(1, n_bp, n_mem, MEM_HEADS, MEM_HEAD_DIM),
        from_steps(lat_s, KV_RANK),
        from_steps(kr_s, QK_ROPE_DIM),
        jnp.transpose(pstate_s, (1, 0, 2))[None],
    )
```

```python
import functools

import jax
import jax.numpy as jnp
from jax import lax
from jax.experimental import pallas as pl
from jax.experimental.pallas import tpu as pltpu

F32 = jnp.float32
BF16 = jnp.bfloat16
I32 = jnp.int32

D_MODEL = 1024
POOL_WIDTH = 512
POOL_WINDOWS = (2, 4, 8, 16)
POOL_GROUP_DIM = 128
POOL_BUF = 15
POOL_HALO = 16
MLA_HEADS = 4
QK_NOPE_DIM = 128
QK_ROPE_DIM = 64
ROPE_HALF = 32
V_HEAD_DIM = 128
Q_RANK = 256
KV_RANK = 128
ROPE_THETA = 10000.0
QK_PAD = 256
ONES_LANE = QK_PAD - 1
MEM_HEADS = 4
MEM_HEAD_DIM = 128
MEM_WIDTH = 512
N_EXPERTS = 32
TOP_K = 4
D_EXPERT = 1024
SWIGLU_LIMIT = 7.0
SWIGLU_ALPHA = 1.702
LN_EPS = 1e-5
RMS_EPS = 1e-6
DEPTH = 1
DEEPNORM_ALPHA = (2.0 * DEPTH) ** 0.25
ATTN_SCALE = (QK_NOPE_DIM + QK_ROPE_DIM) ** -0.5
LOG2_E = 1.4426950408889634
MEM_SCALE = MEM_HEAD_DIM ** -0.5

LANES = 128
VMEM_LIMIT = 48 * 1024 * 1024

TOKEN_BLOCK = 256
ATTN_Q_BLOCK = 256
ATTN_K_BLOCK = 512
ATTN_GROUPS = 2
PAGES_PER_STEP = 32
MEM_BATCH_BLOCK = 8
DISPATCH_BLOCK = 256
EXPERT_ROWS = 256
EXPERT_CHUNK = 256


def _dot(a, b):
    return jnp.dot(a, b, preferred_element_type=F32)


def _dot_nt(a, b):
    return lax.dot_general(a, b, (((1,), (1,)), ((), ())), preferred_element_type=F32)


def _rms(x, g):
    return x * lax.rsqrt(jnp.mean(x * x, axis=-1, keepdims=True) + RMS_EPS) * g


def _layer_norm(x, g, b):
    mu = jnp.mean(x, axis=-1, keepdims=True)
    xc = x - mu
    var = jnp.mean(xc * xc, axis=-1, keepdims=True)
    return xc * lax.rsqrt(var + LN_EPS) * g + b


ROW_TILE = D_MODEL // LANES


def _store_row_tiles(ref, lead, val):
    m = val.shape[0]
    for j in range(ROW_TILE):
        ref[(*lead, pl.ds(j, m, stride=ROW_TILE), slice(None))] = val[:, LANES * j:LANES * (j + 1)]


def _load_row_tiles(ref, lead, m):
    return jnp.concatenate([ref[(*lead, pl.ds(j, m, stride=ROW_TILE), slice(None))] for j in range(ROW_TILE)], axis=1)


def _params(*semantics):
    return pltpu.CompilerParams(dimension_semantics=semantics, vmem_limit_bytes=VMEM_LIMIT)


def _full(shape):
    n = len(shape)
    return pl.BlockSpec(shape, lambda *_: (0,) * n)


def _project(x_bf, w_in_ref, qg_ref, w_uq_ref, kvg_ref, w_ukt_ref, cos, sin, q_scale):
    h = _dot(x_bf, w_in_ref[...])
    u = h[:, :POOL_WIDTH]
    qn = _rms(h[:, POOL_WIDTH:POOL_WIDTH + Q_RANK], qg_ref[...])
    q = _dot(qn.astype(BF16), w_uq_ref[...])
    q_lat, q_rope = [], []
    for hd in range(MLA_HEADS):
        nope = q[:, QK_PAD * hd:QK_PAD * hd + QK_NOPE_DIM]
        rp = q[:, QK_PAD * hd + QK_NOPE_DIM:QK_PAD * (hd + 1)]
        q_lat.append(_dot(nope.astype(BF16), w_ukt_ref[hd]) * q_scale)
        q_rope.append((rp * cos + pltpu.roll(rp, 64, 1) * sin) * q_scale)
    o2 = POOL_WIDTH + Q_RANK
    lat = _rms(h[:, o2:o2 + KV_RANK], kvg_ref[...])
    kp = h[:, o2 + KV_RANK:]
    k_rope = kp * cos + pltpu.roll(kp, 64, 1) * sin
    return u, q_lat, q_rope, lat, k_rope


def _compact_rope(r):
    lane = lax.broadcasted_iota(I32, r.shape, 1)
    moved = pltpu.roll(r, 96, 1)
    return jnp.where(lane < ROPE_HALF, r, jnp.where(lane < QK_ROPE_DIM, moved, 0.0))


def _with_ones_lane(r):
    lane = lax.broadcasted_iota(I32, r.shape, 1)
    return jnp.where(lane == ONES_LANE - KV_RANK, 1.0, r)


def _pool_group_out(diff, g, w_pool_ref, pscale_ref):
    cols = slice(POOL_GROUP_DIM * g, POOL_GROUP_DIM * (g + 1))
    return _dot(diff.astype(BF16), w_pool_ref[g]) * pscale_ref[:, cols]


def _proj_prompt_kernel(x_ref, w_in_ref, qg_ref, w_uq_ref, kvg_ref, w_ukt_ref, w_pool_ref, pscale_ref,
                        cos_ref, sin_ref,
                        qcat_ref, kcat_ref, lat_ref, kr_ref, pooly_ref, pstate_ref, ubuf):
    j = pl.program_id(1)
    tm = x_ref.shape[0]

    @pl.when(j == 0)
    def _():
        ubuf[0:POOL_HALO, :] = jnp.zeros((POOL_HALO, POOL_WIDTH), F32)

    u, q_lat, q_rope, lat, k_rope = _project(
        x_ref[...].astype(BF16), w_in_ref, qg_ref, w_uq_ref, kvg_ref, w_ukt_ref, cos_ref[...], sin_ref[...],
        ATTN_SCALE * LOG2_E)
    for hd in range(MLA_HEADS):
        qcat_ref[hd, :, 0:KV_RANK] = q_lat[hd].astype(BF16)
        qcat_ref[hd, :, KV_RANK:QK_PAD] = q_rope[hd].astype(BF16)
    lat_ref[...] = lat
    kcat_ref[:, 0:KV_RANK] = lat.astype(BF16)
    kcat_ref[:, KV_RANK:QK_PAD] = _with_ones_lane(k_rope).astype(BF16)
    kr_ref[...] = _compact_rope(k_rope)[:, :QK_ROPE_DIM]

    ubuf[POOL_HALO:POOL_HALO + tm, :] = u
    pos = j * tm + lax.broadcasted_iota(I32, (tm, 1), 0)
    for g, w in enumerate(POOL_WINDOWS):
        cols = slice(POOL_GROUP_DIM * g, POOL_GROUP_DIM * (g + 1))
        ug = u[:, cols]
        ssum = ug
        for k in range(1, w):
            ssum = ssum + ubuf[POOL_HALO - k:POOL_HALO - k + tm, cols]
        count = jnp.minimum(w, pos + 1).astype(F32)
        pooly_ref[:, cols] = _pool_group_out(ssum / count - ug, g, w_pool_ref, pscale_ref).astype(BF16)
    tail = ubuf[tm:tm + POOL_HALO, :]
    ubuf[0:POOL_HALO, :] = tail

    @pl.when(j == pl.num_programs(1) - 1)
    def _():
        pstate_ref[...] = tail


def _proj_sample_kernel(x_ref, state_ref, w_in_ref, qg_ref, w_uq_ref, kvg_ref, w_ukt_ref, w_pool_ref, pscale_ref,
                        cos_ref, sin_ref,
                        qcat_ref, kcat_ref, lat_ref, kr_ref, pooly_ref, pstate_ref, *, past_len):
    n_b = state_ref.shape[1]
    n_t = x_ref.shape[0] // n_b
    u, q_lat, q_rope, lat, k_rope = _project(
        x_ref[...].astype(BF16), w_in_ref, qg_ref, w_uq_ref, kvg_ref, w_ukt_ref, cos_ref[...], sin_ref[...],
        ATTN_SCALE)
    for hd in range(MLA_HEADS):
        qcat_ref[hd, :, 0:KV_RANK] = q_lat[hd].astype(BF16)
        qcat_ref[hd, :, KV_RANK:QK_PAD] = _compact_rope(q_rope[hd]).astype(BF16)
    lat_ref[...] = lat
    kc = _compact_rope(k_rope)
    kcat_ref[:, 0:KV_RANK] = lat.astype(BF16)
    kcat_ref[:, KV_RANK:QK_PAD] = _with_ones_lane(kc).astype(BF16)
    kr_ref[...] = kc[:, :QK_ROPE_DIM]

    def ext(jj):
        if jj < POOL_BUF:
            return state_ref[jj]
        return u[(jj - POOL_BUF) * n_b:(jj - POOL_BUF + 1) * n_b, :]

    for t in range(n_t):
        ut = ext(POOL_BUF + t)
        for g, w in enumerate(POOL_WINDOWS):
            cols = slice(POOL_GROUP_DIM * g, POOL_GROUP_DIM * (g + 1))
            ssum = ut[:, cols]
            for k in range(1, w):
                ssum = ssum + ext(POOL_BUF + t - k)[:, cols]
            count = float(min(w, past_len + t + 1))
            y = _pool_group_out(ssum / count - ut[:, cols], g, w_pool_ref, pscale_ref)
            pooly_ref[t * n_b:(t + 1) * n_b, cols] = y.astype(BF16)
    for jj in range(POOL_BUF):
        pstate_ref[jj] = ext(n_t + jj)


def _proj_weight_specs():
    return [
        _full((D_MODEL, D_MODEL)),
        _full((1, Q_RANK)),
        _full((Q_RANK, MLA_HEADS * QK_PAD)),
        _full((1, KV_RANK)),
        _full((MLA_HEADS, QK_NOPE_DIM, KV_RANK)),
        _full((len(POOL_WINDOWS), POOL_GROUP_DIM, POOL_GROUP_DIM)),
        _full((1, POOL_WIDTH)),
    ]


def _proj_prompt(x2d, weights, cos_t, sin_t, n_batch, seq):
    tm = TOKEN_BLOCK
    n = n_batch * seq
    nj = seq // tm
    tok = lambda b, j: (b * nj + j, 0)
    return pl.pallas_call(
        _proj_prompt_kernel,
        grid=(n_batch, nj),
        in_specs=[pl.BlockSpec((tm, D_MODEL), tok)] + _proj_weight_specs() + [
            pl.BlockSpec((tm, LANES), lambda b, j: (j, 0)),
            pl.BlockSpec((tm, LANES), lambda b, j: (j, 0)),
        ],
        out_specs=[
            pl.BlockSpec((MLA_HEADS, tm, QK_PAD), lambda b, j: (0, b * nj + j, 0)),
            pl.BlockSpec((tm, QK_PAD), tok),
            pl.BlockSpec((tm, KV_RANK), tok),
            pl.BlockSpec((tm, QK_ROPE_DIM), tok),
            pl.BlockSpec((tm, POOL_WIDTH), tok),
            pl.BlockSpec((None, POOL_HALO, POOL_WIDTH), lambda b, j: (b, 0, 0)),
        ],
        out_shape=[
            jax.ShapeDtypeStruct((MLA_HEADS, n, QK_PAD), BF16),
            jax.ShapeDtypeStruct((n, QK_PAD), BF16),
            jax.ShapeDtypeStruct((n, KV_RANK), F32),
            jax.ShapeDtypeStruct((n, QK_ROPE_DIM), F32),
            jax.ShapeDtypeStruct((n, POOL_WIDTH), BF16),
            jax.ShapeDtypeStruct((n_batch, POOL_HALO, POOL_WIDTH), F32),
        ],
        scratch_shapes=[pltpu.VMEM((POOL_HALO + tm, POOL_WIDTH), F32)],
        compiler_params=_params("arbitrary", "arbitrary"),
        name="proj_prompt",
    )(x2d, *weights, cos_t, sin_t)


def _proj_sample(x2d, state_t, weights, cos_t, sin_t, past_len):
    n = x2d.shape[0]
    n_b = state_t.shape[1]
    return pl.pallas_call(
        functools.partial(_proj_sample_kernel, past_len=past_len),
        grid=(1,),
        in_specs=[_full((n, D_MODEL)), _full((POOL_BUF, n_b, POOL_WIDTH))] + _proj_weight_specs() + [
            _full((n, LANES)), _full((n, LANES))],
        out_specs=[
            _full((MLA_HEADS, n, QK_PAD)), _full((n, QK_PAD)), _full((n, KV_RANK)), _full((n, QK_ROPE_DIM)),
            _full((n, POOL_WIDTH)), _full((POOL_BUF, n_b, POOL_WIDTH)),
        ],
        out_shape=[
            jax.ShapeDtypeStruct((MLA_HEADS, n, QK_PAD), BF16),
            jax.ShapeDtypeStruct((n, QK_PAD), BF16),
            jax.ShapeDtypeStruct((n, KV_RANK), F32),
            jax.ShapeDtypeStruct((n, QK_ROPE_DIM), F32),
            jax.ShapeDtypeStruct((n, POOL_WIDTH), BF16),
            jax.ShapeDtypeStruct((POOL_BUF, n_b, POOL_WIDTH), F32),
        ],
        compiler_params=_params("arbitrary"),
        name="proj_sample",
    )(x2d, state_t, *weights, cos_t, sin_t)


def _attn_block(q, k, mask, m_ref, acc_ref):
    s = _dot_nt(q, k)
    if mask is not None:
        s = jnp.where(mask, s, -jnp.inf)
    tiles = [s[:, c * LANES:(c + 1) * LANES] for c in range(s.shape[1] // LANES)]
    m_old = m_ref[...]
    m_new = jnp.maximum(m_old, jnp.max(functools.reduce(jnp.maximum, tiles), axis=1, keepdims=True))
    alpha = jnp.exp2(m_old - m_new)
    p = jnp.concatenate([jnp.exp2(t - m_new) for t in tiles], axis=1).astype(BF16)
    acc_ref[...] = jnp.concatenate([alpha, alpha], axis=1) * acc_ref[...] + _dot(p, k)
    m_ref[...] = m_new


def _attn_output(acc):
    return acc[:, :KV_RANK] / acc[:, ONES_LANE:ONES_LANE + 1]


def _attn_prompt_kernel(q_ref, k_ref, w_uv_ref, out_ref, m_ref, acc_ref):
    i = pl.program_id(1)
    tq = q_ref.shape[1]
    tk = ATTN_K_BLOCK
    hpg = MLA_HEADS // ATTN_GROUPS
    rows = hpg * tq
    m_ref[...] = jnp.full(m_ref.shape, -jnp.inf, F32)
    acc_ref[...] = jnp.zeros(acc_ref.shape, F32)

    def block(jk, masked):
        k = k_ref[pl.ds(pl.multiple_of(jk * tk, tk), tk), :]
        mask = None
        if masked:
            q_pos = i * tq + lax.broadcasted_iota(I32, (rows, tk), 0) % tq
            k_pos = jk * tk + lax.broadcasted_iota(I32, (rows, tk), 1)
            mask = k_pos <= q_pos
        for g in range(ATTN_GROUPS):
            q = q_ref[hpg * g:hpg * (g + 1)].reshape(rows, QK_PAD)
            _attn_block(q, k, mask, m_ref.at[g], acc_ref.at[g])

    n_full = (i * tq + 1) // tk
    n_kv = (i * tq + tq - 1) // tk + 1

    def full_pair(jp, c):
        block(2 * jp, False)
        block(2 * jp + 1, False)
        return c

    def diag_body(jk, c):
        block(jk, True)
        return c

    lax.fori_loop(0, n_full // 2, full_pair, 0)

    @pl.when(n_full % 2 == 1)
    def _():
        block(n_full - 1, False)

    lax.fori_loop(n_full, n_kv, diag_body, 0)

    for g in range(ATTN_GROUPS):
        o = _attn_output(acc_ref[g])
        for hl in range(hpg):
            hd = hpg * g + hl
            oh = o[hl * tq:(hl + 1) * tq, :].astype(BF16)
            out_ref[:, V_HEAD_DIM * hd:V_HEAD_DIM * (hd + 1)] = _dot(oh, w_uv_ref[hd]).astype(BF16)


def _attn_prompt(qcat, kcat, w_uv_h, n_batch, seq):
    tq = ATTN_Q_BLOCK
    nq = seq // tq
    n = n_batch * seq
    rows = (MLA_HEADS // ATTN_GROUPS) * tq
    return pl.pallas_call(
        _attn_prompt_kernel,
        grid=(n_batch, nq),
        in_specs=[
            pl.BlockSpec((MLA_HEADS, tq, QK_PAD), lambda b, i: (0, b * nq + i, 0)),
            pl.BlockSpec((seq, QK_PAD), lambda b, i: (b, 0)),
            _full((MLA_HEADS, KV_RANK, V_HEAD_DIM)),
        ],
        out_specs=pl.BlockSpec((tq, MLA_HEADS * V_HEAD_DIM), lambda b, i: (b * nq + i, 0)),
        out_shape=jax.ShapeDtypeStruct((n, MLA_HEADS * V_HEAD_DIM), BF16),
        scratch_shapes=[pltpu.VMEM((ATTN_GROUPS, rows, LANES), F32), pltpu.VMEM((ATTN_GROUPS, rows, QK_PAD), F32)],
        compiler_params=_params("arbitrary", "arbitrary"),
        name="attn_prompt",
    )(qcat, kcat, w_uv_h)


def _attn_sample_kernel(pt_ref, q_ref, knew_ref, w_uv_ref, *rest, n_pages_step, n_new):
    lat_refs = rest[:n_pages_step]
    rope_refs = rest[n_pages_step:2 * n_pages_step]
    out_ref, m_ref, l_ref, acc_ref, res_ref = rest[2 * n_pages_step:]
    del pt_ref
    s_idx = pl.program_id(1)

    @pl.when(s_idx == 0)
    def _():
        m_ref[...] = jnp.full(m_ref.shape, -jnp.inf, F32)
        l_ref[...] = jnp.zeros(l_ref.shape, F32)
        acc_ref[...] = jnp.zeros(acc_ref.shape, F32)

    def update(s_tiles, values):
        m_old = m_ref[...]
        m_new = jnp.maximum(m_old, jnp.max(functools.reduce(jnp.maximum, s_tiles), axis=1, keepdims=True))
        alpha = jnp.exp(m_old - m_new)
        p_tiles = [jnp.exp(t - m_new) for t in s_tiles]
        row_sum = jnp.sum(functools.reduce(jnp.add, p_tiles), axis=1, keepdims=True)
        pv = functools.reduce(jnp.add, [_dot(p.astype(BF16), v) for p, v in zip(p_tiles, values)])
        l_ref[...] = alpha * l_ref[...] + row_sum
        acc_ref[...] = alpha * acc_ref[...] + pv
        m_ref[...] = m_new

    q = q_ref[...]
    q_lat = q[:, :KV_RANK]
    q_rope = q[:, KV_RANK:KV_RANK + QK_ROPE_DIM]
    lats = [lat_refs[p][...].astype(BF16) for p in range(n_pages_step)]
    update([_dot_nt(q_lat, lats[p]) + _dot(q_rope, rope_refs[p][...].astype(BF16)) for p in range(n_pages_step)],
           lats)

    @pl.when(s_idx == pl.num_programs(1) - 1)
    def _():
        kn = knew_ref[...]
        q_t = lax.broadcasted_iota(I32, (q.shape[0], kn.shape[0]), 0) % n_new
        k_t = lax.broadcasted_iota(I32, (q.shape[0], kn.shape[0]), 1)
        update([jnp.where(k_t <= q_t, _dot_nt(q, kn), -jnp.inf)], [kn[:, :KV_RANK]])
        o = (acc_ref[...] / l_ref[...]).astype(BF16)
        for hd in range(MLA_HEADS):
            res_ref[...] = _dot(o, w_uv_ref[hd])
            out_ref[:, V_HEAD_DIM * hd:V_HEAD_DIM * (hd + 1)] = res_ref[hd * n_new:(hd + 1) * n_new, :]


def _attn_sample(page_table, q_b, knew_b, w_uv_h, cache_lat, cache_rope, layer):
    n_b, n_pages = page_table.shape
    page = cache_lat.shape[2]
    pps = min(PAGES_PER_STEP, n_pages)
    assert n_pages % pps == 0
    n_steps = n_pages // pps
    rows = q_b.shape[1]
    n_new = rows // MLA_HEADS
    t_pad = knew_b.shape[1]

    def page_map(p):
        return lambda b, s, pt: (layer, pt[b * n_pages + s * pps + p], 0, 0)

    lat_specs = [pl.BlockSpec((None, None, page, KV_RANK), page_map(p)) for p in range(pps)]
    rope_specs = [pl.BlockSpec((None, None, QK_ROPE_DIM, page), page_map(p)) for p in range(pps)]
    grid_spec = pltpu.PrefetchScalarGridSpec(
        num_scalar_prefetch=1,
        grid=(n_b, n_steps),
        in_specs=[
            pl.BlockSpec((None, rows, QK_PAD), lambda b, s, pt: (b, 0, 0)),
            pl.BlockSpec((None, t_pad, QK_PAD), lambda b, s, pt: (b, 0, 0)),
            pl.BlockSpec((MLA_HEADS, KV_RANK, V_HEAD_DIM), lambda b, s, pt: (0, 0, 0)),
        ] + lat_specs + rope_specs,
        out_specs=pl.BlockSpec((None, n_new, MLA_HEADS * V_HEAD_DIM), lambda b, s, pt: (b, 0, 0)),
        scratch_shapes=[
            pltpu.VMEM((rows, LANES), F32), pltpu.VMEM((rows, LANES), F32), pltpu.VMEM((rows, KV_RANK), F32),
            pltpu.VMEM((rows, V_HEAD_DIM), F32),
        ],
    )
    return pl.pallas_call(
        functools.partial(_attn_sample_kernel, n_pages_step=pps, n_new=n_new),
        grid_spec=grid_spec,
        out_shape=jax.ShapeDtypeStruct((n_b, n_new, MLA_HEADS * V_HEAD_DIM), F32),
        compiler_params=_params("arbitrary", "arbitrary"),
        name="attn_sample",
    )(page_table.reshape(-1), q_b, knew_b, w_uv_h, *([cache_lat] * pps), *([cache_rope] * pps))


def _mem_kv_kernel(mem_ref, w_mk_ref, w_mv_ref, k_ref, v_ref, kb_ref, vb_ref):
    m = mem_ref[...].astype(BF16)
    k = _dot(m, w_mk_ref[...])
    v = _dot(m, w_mv_ref[...])
    k_ref[...] = k
    v_ref[...] = v
    kb_ref[...] = k.astype(BF16)
    vb_ref[...] = v.astype(BF16)


def _mem_kv(mem2d, w_mk, w_mv, n_batch, n_mem):
    blk = pl.BlockSpec((n_mem, MEM_WIDTH), lambda b: (b, 0))
    n = n_batch * n_mem
    return pl.pallas_call(
        _mem_kv_kernel,
        grid=(n_batch,),
        in_specs=[pl.BlockSpec((n_mem, D_MODEL), lambda b: (b, 0)), _full((D_MODEL, MEM_WIDTH)),
                  _full((D_MODEL, MEM_WIDTH))],
        out_specs=[blk, blk, blk, blk],
        out_shape=[jax.ShapeDtypeStruct((n, MEM_WIDTH), F32), jax.ShapeDtypeStruct((n, MEM_WIDTH), F32),
                   jax.ShapeDtypeStruct((n, MEM_WIDTH), BF16), jax.ShapeDtypeStruct((n, MEM_WIDTH), BF16)],
        compiler_params=_params("arbitrary"),
        name="mem_kv",
    )(mem2d, w_mk, w_mv)


def _mix_ln1(x, pool_y, mla_y, w_out_a_ref, w_out_b_ref, g_ref, b_ref):
    mix = _dot(pool_y, w_out_a_ref[...]) + _dot(mla_y, w_out_b_ref[...])
    return _layer_norm(DEEPNORM_ALPHA * x + mix, g_ref[...], b_ref[...])


def _route(x2, w_r_ref, b_r_ref, carry):
    m = x2.shape[0]
    x_hi = x2.astype(BF16)
    x_lo = (x2 - x_hi.astype(F32)).astype(BF16)
    logits_t = _dot_nt(w_r_ref[0], x_hi) + (_dot_nt(w_r_ref[0], x_lo) + _dot_nt(w_r_ref[1], x_hi))
    logits = logits_t.T + b_r_ref[...]
    e_idx = lax.broadcasted_iota(I32, (m, N_EXPERTS), 1).astype(F32)
    work = logits
    vals, picks = [], []
    for _ in range(TOP_K):
        mx = jnp.max(work, axis=1, keepdims=True)
        pick = jnp.min(jnp.where(work == mx, e_idx, float(N_EXPERTS)), axis=1, keepdims=True)
        vals.append(mx)
        picks.append(pick)
        work = jnp.where(e_idx == pick, -jnp.inf, work)
    hot = jnp.where(work == -jnp.inf, 1.0, 0.0)
    exps = [jnp.exp(v - vals[0]) for v in vals]
    denom = exps[0] + exps[1] + exps[2] + exps[3]
    r_i = lax.broadcasted_iota(I32, (m, m), 0)
    c_i = lax.broadcasted_iota(I32, (m, m), 1)
    strict_lower = jnp.where(c_i < r_i, 1.0, 0.0).astype(BF16)
    before = _dot(strict_lower, hot.astype(BF16)) + carry
    lane4 = lax.broadcasted_iota(I32, (m, TOP_K), 1)
    idx = jnp.zeros((m, TOP_K), I32)
    gates = jnp.zeros((m, TOP_K), F32)
    rank = jnp.zeros((m, TOP_K), I32)
    for k in range(TOP_K):
        rk = jnp.sum(jnp.where(e_idx == picks[k], before, 0.0), axis=1, keepdims=True).astype(I32)
        idx = jnp.where(lane4 == k, picks[k].astype(I32), idx)
        gates = jnp.where(lane4 == k, exps[k] / denom, gates)
        rank = jnp.where(lane4 == k, rk, rank)
    return idx, gates, rank, carry + jnp.sum(hot, axis=0, keepdims=True)


def _mo_ln2_route(x1, o_bf, w_mo_ref, g_ref, b_ref, w_r_ref, b_r_ref, carry):
    x2 = _layer_norm(DEEPNORM_ALPHA * x1 + _dot(o_bf, w_mo_ref[...]), g_ref[...], b_ref[...])
    return (x2,) + _route(x2, w_r_ref, b_r_ref, carry)


def _post_prompt_kernel(x_ref, pooly_ref, mlay_ref, mk_ref, mv_ref,
                        w_out_a_ref, w_out_b_ref, g1_ref, b1_ref, w_mq_ref, w_mo_ref, g2_ref, b2_ref,
                        w_r_ref, b_r_ref,
                        x2_ref, idx_ref, gate_ref, rank_ref, cnt_ref, carry_ref):
    first = jnp.logical_and(pl.program_id(0) == 0, pl.program_id(1) == 0)

    @pl.when(first)
    def _():
        carry_ref[...] = jnp.zeros(carry_ref.shape, F32)

    x1 = _mix_ln1(x_ref[...], pooly_ref[...], mlay_ref[...], w_out_a_ref, w_out_b_ref, g1_ref, b1_ref)
    qm = (_dot(x1.astype(BF16), w_mq_ref[...]) * MEM_SCALE).astype(BF16)
    outs = []
    for hd in range(MEM_HEADS):
        cols = slice(MEM_HEAD_DIM * hd, MEM_HEAD_DIM * (hd + 1))
        s = _dot_nt(qm[:, cols], mk_ref[:, cols])
        p = jnp.exp(s - jnp.max(s, axis=1, keepdims=True))
        o = _dot(p.astype(BF16), mv_ref[:, cols]) / jnp.sum(p, axis=1, keepdims=True)
        outs.append(o.astype(BF16))
    o_all = jnp.concatenate(outs, axis=1)
    x2, idx, gates, rank, carry = _mo_ln2_route(x1, o_all, w_mo_ref, g2_ref, b2_ref, w_r_ref, b_r_ref,
                                                carry_ref[...])
    _store_row_tiles(x2_ref, (), x2)
    idx_ref[...] = idx
    gate_ref[...] = gates
    rank_ref[...] = rank
    carry_ref[...] = carry
    cnt_ref[...] = carry


def _post_weight_specs():
    half = (POOL_WIDTH, D_MODEL)
    return [
        _full(half), _full(half), _full((1, D_MODEL)), _full((1, D_MODEL)),
        _full((D_MODEL, MEM_WIDTH)), _full((MEM_WIDTH, D_MODEL)), _full((1, D_MODEL)), _full((1, D_MODEL)),
        _full((2, N_EXPERTS, D_MODEL)), _full((1, N_EXPERTS)),
    ]


def _post_prompt(x2d, pool_y, mla_y, mk_b, mv_b, weights, n_batch, seq, n_mem, n_all):
    tm = TOKEN_BLOCK
    nj = seq // tm
    tok = lambda b, j: (b * nj + j, 0)
    return pl.pallas_call(
        _post_prompt_kernel,
        grid=(n_batch, nj),
        in_specs=[
            pl.BlockSpec((tm, D_MODEL), tok), pl.BlockSpec((tm, POOL_WIDTH), tok), pl.BlockSpec((tm, POOL_WIDTH), tok),
            pl.BlockSpec((n_mem, MEM_WIDTH), lambda b, j: (b, 0)), pl.BlockSpec((n_mem, MEM_WIDTH), lambda b, j: (b, 0)),
        ] + _post_weight_specs(),
        out_specs=[
            pl.BlockSpec((tm * ROW_TILE, LANES), tok), pl.BlockSpec((tm, TOP_K), tok), pl.BlockSpec((tm, TOP_K), tok),
            pl.BlockSpec((tm, TOP_K), tok), _full((1, N_EXPERTS)),
        ],
        out_shape=[
            jax.ShapeDtypeStruct((n_all * ROW_TILE, LANES), F32), jax.ShapeDtypeStruct((n_all, TOP_K), I32),
            jax.ShapeDtypeStruct((n_all, TOP_K), F32), jax.ShapeDtypeStruct((n_all, TOP_K), I32),
            jax.ShapeDtypeStruct((1, N_EXPERTS), F32),
        ],
        scratch_shapes=[pltpu.VMEM((1, N_EXPERTS), F32)],
        compiler_params=_params("arbitrary", "arbitrary"),
        name="post_prompt",
    )(x2d, pool_y, mla_y, mk_b, mv_b, *weights)


def _mix_sample_kernel(x_ref, pooly_ref, mlay_ref, w_out_a_ref, w_out_b_ref, g1_ref, b1_ref, w_mq_ref,
                       x1_ref, qm_ref):
    x1 = _mix_ln1(x_ref[...], pooly_ref[...], mlay_ref[...], w_out_a_ref, w_out_b_ref, g1_ref, b1_ref)
    x1_ref[...] = x1
    qm_ref[...] = (_dot(x1.astype(BF16), w_mq_ref[...]) * MEM_SCALE).astype(BF16)


def _mix_sample(x2d, pool_y, mla_y, weights):
    n = x2d.shape[0]
    w_out_a, w_out_b, g1, b1, w_mq = weights[:5]
    return pl.pallas_call(
        _mix_sample_kernel,
        grid=(1,),
        in_specs=[_full((n, D_MODEL)), _full((n, POOL_WIDTH)), _full((n, POOL_WIDTH))] + _post_weight_specs()[:5],
        out_specs=[_full((n, D_MODEL)), _full((n, MEM_WIDTH))],
        out_shape=[jax.ShapeDtypeStruct((n, D_MODEL), F32), jax.ShapeDtypeStruct((n, MEM_WIDTH), BF16)],
        compiler_params=_params("arbitrary"),
        name="mix_sample",
    )(x2d, pool_y, mla_y, w_out_a, w_out_b, g1, b1, w_mq)


def _mem_attn_sample_kernel(q_ref, k_ref, v_ref, out_ref, res_ref, *, n_new):
    for b in range(q_ref.shape[0]):
        s = _dot_nt(q_ref[b], k_ref[b].astype(BF16))
        p = jnp.exp(s - jnp.max(s, axis=1, keepdims=True))
        res_ref[...] = _dot(p.astype(BF16), v_ref[b].astype(BF16)) / jnp.sum(p, axis=1, keepdims=True)
        for hd in range(MEM_HEADS):
            cols = slice(MEM_HEAD_DIM * hd, MEM_HEAD_DIM * (hd + 1))
            out_ref[b, :, cols] = res_ref[hd * n_new:(hd + 1) * n_new, cols].astype(BF16)


def _mem_attn_sample(q_bd, mem_k, mem_v, n_new):
    n_b, rows, _ = q_bd.shape
    n_mem = mem_k.shape[1]
    gb = min(MEM_BATCH_BLOCK, n_b)
    assert n_b % gb == 0
    blk = lambda i: (i, 0, 0)
    return pl.pallas_call(
        functools.partial(_mem_attn_sample_kernel, n_new=n_new),
        grid=(n_b // gb,),
        in_specs=[pl.BlockSpec((gb, rows, MEM_WIDTH), blk), pl.BlockSpec((gb, n_mem, MEM_WIDTH), blk),
                  pl.BlockSpec((gb, n_mem, MEM_WIDTH), blk)],
        out_specs=pl.BlockSpec((gb, n_new, MEM_WIDTH), blk),
        out_shape=jax.ShapeDtypeStruct((n_b, n_new, MEM_WIDTH), BF16),
        scratch_shapes=[pltpu.VMEM((rows, MEM_WIDTH), F32)],
        compiler_params=_params("arbitrary"),
        name="mem_attn_sample",
    )(q_bd, mem_k, mem_v)


def _route_sample_kernel(x1_ref, o_ref, cnt_in_ref, w_mo_ref, g2_ref, b2_ref, w_r_ref, b_r_ref,
                         x2_in, idx_in, gate_in, rank_in,
                         x2_ref, idx_ref, gate_ref, rank_ref, cnt_ref):
    del x2_in, idx_in, gate_in, rank_in
    x2, idx, gates, rank, carry = _mo_ln2_route(x1_ref[...], o_ref[...], w_mo_ref, g2_ref, b2_ref, w_r_ref, b_r_ref,
                                                cnt_in_ref[...])
    _store_row_tiles(x2_ref, (), x2)
    idx_ref[...] = idx
    gate_ref[...] = gates
    rank_ref[...] = rank
    cnt_ref[...] = carry


def _route_sample(x1, o_bf, cnt_in, weights, x2_all, idx_all, gate_all, rank_all, n_prompt):
    n = x1.shape[0]
    assert n_prompt % n == 0
    tail = lambda i: (n_prompt // n, 0)
    anyspec = pl.BlockSpec(memory_space=pl.ANY)
    w_mo, g2, b2, w_r, b_r = weights[5:]
    return pl.pallas_call(
        _route_sample_kernel,
        grid=(1,),
        in_specs=[_full((n, D_MODEL)), _full((n, MEM_WIDTH)), _full((1, N_EXPERTS))] + _post_weight_specs()[5:] + [
            anyspec, anyspec, anyspec, anyspec],
        out_specs=[pl.BlockSpec((n * ROW_TILE, LANES), tail), pl.BlockSpec((n, TOP_K), tail), pl.BlockSpec((n, TOP_K), tail),
                   pl.BlockSpec((n, TOP_K), tail), _full((1, N_EXPERTS))],
        out_shape=[jax.ShapeDtypeStruct(x2_all.shape, F32), jax.ShapeDtypeStruct(idx_all.shape, I32),
                   jax.ShapeDtypeStruct(gate_all.shape, F32), jax.ShapeDtypeStruct(rank_all.shape, I32),
                   jax.ShapeDtypeStruct((1, N_EXPERTS), F32)],
        input_output_aliases={8: 0, 9: 1, 10: 2, 11: 3},
        compiler_params=_params("arbitrary"),
        name="route_sample",
    )(x1, o_bf, cnt_in, w_mo, g2, b2, w_r, b_r, x2_all, idx_all, gate_all, rank_all)


def _dispatch_kernel(pad_start_ref, n_pad_ref, n_used_ref, idx_hbm, x_ref, xrows_hbm,
                     idx_smem, zbuf, idx_sem, row_sem, pad_sem):
    i = pl.program_id(0)
    last = pl.num_programs(0) - 1
    tm = x_ref.shape[0] // ROW_TILE
    blk = zbuf.shape[0]
    n_blk = xrows_hbm.shape[0] // blk

    def idx_copy(row, s):
        return pltpu.make_async_copy(idx_hbm.at[row], idx_smem.at[s], idx_sem.at[s])

    def pad_copy(e, r):
        dst = pl.multiple_of((pad_start_ref[e] + r) * ROW_TILE, ROW_TILE)
        return pltpu.make_async_copy(zbuf.at[pl.ds(0, ROW_TILE)], xrows_hbm.at[pl.ds(dst, ROW_TILE)], pad_sem)

    def tail_copy(j):
        return pltpu.make_async_copy(zbuf, xrows_hbm.at[pl.ds(pl.multiple_of(j * blk, blk), blk)], pad_sem)

    @pl.when(i == 0)
    def _():
        idx_copy(0, 0).start()
        zbuf[...] = jnp.zeros(zbuf.shape, F32)
        for e in range(N_EXPERTS):
            def start(r, c, e=e):
                pad_copy(e, r).start()
                return c

            lax.fori_loop(0, n_pad_ref[e], start, 0)

        def start_tail(j, c):
            tail_copy(j).start()
            return c

        lax.fori_loop(n_used_ref[0], n_blk, start_tail, 0)
        for e in range(N_EXPERTS):
            def wait(r, c, e=e):
                pad_copy(e, r).wait()
                return c

            lax.fori_loop(0, n_pad_ref[e], wait, 0)

        def wait_tail(j, c):
            tail_copy(j).wait()
            return c

        lax.fori_loop(n_used_ref[0], n_blk, wait_tail, 0)

    def step(slot):
        idx_copy(i, slot).wait()

        @pl.when(i < last)
        def _():
            idx_copy(i + 1, 1 - slot).start()

        for t in range(tm):
            for k in range(TOP_K):
                dst = pl.multiple_of(idx_smem[slot, TOP_K * t + k], ROW_TILE)
                pltpu.make_async_copy(x_ref.at[pl.ds(ROW_TILE * t, ROW_TILE)], xrows_hbm.at[pl.ds(dst, ROW_TILE)],
                                      row_sem).start(priority=k % 2)
        for k in range(TOP_K):
            pltpu.make_async_copy(x_ref, x_ref, row_sem).wait()

    for parity in range(2):
        pl.when(i % 2 == parity)(functools.partial(step, parity))


def _dispatch(pad_start, n_pad, n_used, idx_rows, x_all, n_rows):
    tm = DISPATCH_BLOCK
    n_steps = idx_rows.shape[0]
    assert x_all.shape[0] == n_steps * tm * ROW_TILE and idx_rows.shape[1] == TOP_K * tm
    assert n_rows % EXPERT_ROWS == 0
    grid_spec = pltpu.PrefetchScalarGridSpec(
        num_scalar_prefetch=3,
        grid=(n_steps,),
        in_specs=[pl.BlockSpec(memory_space=pl.ANY),
                  pl.BlockSpec((tm * ROW_TILE, LANES), lambda i, ps, npd, nu: (i, 0))],
        out_specs=pl.BlockSpec(memory_space=pl.ANY),
        scratch_shapes=[pltpu.SMEM((2, TOP_K * tm), I32), pltpu.VMEM((EXPERT_ROWS * ROW_TILE, LANES), F32),
                        pltpu.SemaphoreType.DMA((2,)), pltpu.SemaphoreType.DMA, pltpu.SemaphoreType.DMA],
    )
    return pl.pallas_call(
        _dispatch_kernel,
        grid_spec=grid_spec,
        out_shape=jax.ShapeDtypeStruct((n_rows * ROW_TILE, LANES), F32),
        compiler_params=_params("arbitrary"),
        name="moe_dispatch",
    )(pad_start, n_pad, n_used, idx_rows, x_all)


def _moe_kernel(blk_exp_ref, blk_row_ref, idx_hbm, x_ref, wgu_f32_ref, bgu_ref, wd_f32_ref, bd_ref, y_hbm,
                idx_smem, ybuf, wgu_ref, wd_ref, idx_sem, y_sem):
    del blk_row_ref
    i = pl.program_id(0)
    last = pl.num_programs(0) - 1
    tr = x_ref.shape[0] // ROW_TILE

    @pl.when(jnp.logical_or(i == 0, blk_exp_ref[i] != blk_exp_ref[jnp.maximum(i - 1, 0)]))
    def _():
        wgu_ref[...] = wgu_f32_ref[...].astype(BF16)
        wd_ref[...] = wd_f32_ref[...].astype(BF16)

    n_chunks = D_EXPERT // EXPERT_CHUNK

    def idx_copy(row, s):
        return pltpu.make_async_copy(idx_hbm.at[row], idx_smem.at[s], idx_sem.at[s])

    def scatter(s_idx, ys, rows):
        for r in rows:
            dst = pl.multiple_of(idx_smem[s_idx, r], ROW_TILE)
            pltpu.make_async_copy(ybuf.at[ys, pl.ds(ROW_TILE * r, ROW_TILE)], y_hbm.at[pl.ds(dst, ROW_TILE)],
                                  y_sem.at[ys]).start(priority=r % 2)

    def wait_rows(s):
        pltpu.make_async_copy(ybuf.at[s], ybuf.at[s], y_sem.at[s]).wait()

    @pl.when(i == 0)
    def _():
        idx_copy(0, 0).start()
        ybuf[1] = jnp.zeros(ybuf.shape[1:], F32)

    def step(slot):
        nxt = 1 - slot
        idx_copy(i, slot).wait()
        idx_copy(i + 1, nxt).start()
        xb = _load_row_tiles(x_ref, (), tr).astype(BF16)
        acc = jnp.zeros((tr, D_MODEL), F32) + bd_ref[...]

        groups = 3 * n_chunks
        per_group = -(-tr // groups)

        def issue(g):
            scatter(slot, nxt, range(g * per_group, min((g + 1) * per_group, tr)))

        for c in range(n_chunks):
            cg = slice(c * EXPERT_CHUNK, (c + 1) * EXPERT_CHUNK)
            cu = slice(D_EXPERT + c * EXPERT_CHUNK, D_EXPERT + (c + 1) * EXPERT_CHUNK)
            gate = jnp.minimum(_dot(xb, wgu_ref[:, cg]) + bgu_ref[:, cg], SWIGLU_LIMIT)
            issue(3 * c)
            up = jnp.clip(_dot(xb, wgu_ref[:, cu]) + bgu_ref[:, cu], -SWIGLU_LIMIT, SWIGLU_LIMIT)
            issue(3 * c + 1)
            act = gate * (1.0 / (1.0 + jnp.exp(-SWIGLU_ALPHA * gate))) * (up + 1.0)
            acc = acc + _dot(act.astype(BF16), wd_ref[cg, :])
            issue(3 * c + 2)

        @pl.when(i >= 1)
        def _():
            wait_rows(slot)

        _store_row_tiles(ybuf, (slot,), acc)

        @pl.when(i == last)
        def _():
            idx_copy(i + 1, nxt).wait()
            wait_rows(nxt)

    for parity in range(2):
        pl.when(i % 2 == parity)(functools.partial(step, parity))


def _moe(blk_exp, blk_row, idx_rows, x_rows, w_gu, b_gu, w_d, b_d, n_assign):
    tr = EXPERT_ROWS
    n_steps = blk_exp.shape[0]
    assert n_steps >= 2 and idx_rows.shape == (n_steps + 1, tr)
    anyspec = pl.BlockSpec(memory_space=pl.ANY)
    emap3 = lambda i, be, br: (be[i], 0, 0)
    grid_spec = pltpu.PrefetchScalarGridSpec(
        num_scalar_prefetch=2,
        grid=(n_steps,),
        in_specs=[
            anyspec, pl.BlockSpec((tr * ROW_TILE, LANES), lambda i, be, br: (br[i], 0)),
            pl.BlockSpec((None, D_MODEL, 2 * D_EXPERT), emap3), pl.BlockSpec((None, 1, 2 * D_EXPERT), emap3),
            pl.BlockSpec((None, D_EXPERT, D_MODEL), emap3), pl.BlockSpec((None, 1, D_MODEL), emap3),
        ],
        out_specs=anyspec,
        scratch_shapes=[
            pltpu.SMEM((2, tr), I32), pltpu.VMEM((2, tr * ROW_TILE, LANES), F32),
            pltpu.VMEM((D_MODEL, 2 * D_EXPERT), BF16), pltpu.VMEM((D_EXPERT, D_MODEL), BF16),
            pltpu.SemaphoreType.DMA((2,)), pltpu.SemaphoreType.DMA((2,)),
        ],
    )
    return pl.pallas_call(
        _moe_kernel,
        grid_spec=grid_spec,
        out_shape=jax.ShapeDtypeStruct(((n_assign + 2 * tr) * ROW_TILE, LANES), F32),
        compiler_params=_params("arbitrary"),
        name="moe_experts",
    )(blk_exp, blk_row, idx_rows, x_rows, w_gu, b_gu, w_d, b_d)


def _combine_kernel(x2_ref, gate_ref, g3_ref, b3_ref, *rest):
    y_refs, out_ref = rest[:TOP_K], rest[TOP_K]
    gates = gate_ref[...]
    tm = out_ref.shape[0]
    moe = jnp.zeros(out_ref.shape, F32)
    for k in range(TOP_K):
        moe = moe + gates[:, k:k + 1] * _load_row_tiles(y_refs[k], (), tm)
    out_ref[...] = _layer_norm(DEEPNORM_ALPHA * _load_row_tiles(x2_ref, (), tm) + moe, g3_ref[...], b3_ref[...])


def _combine(x2_all, y_tok, gates_all, g3, b3, row0, n_rows, tm):
    n_all = gates_all.shape[0]
    assert row0 % tm == 0 and n_rows % tm == 0 and n_all % tm == 0
    off = row0 // tm
    tok = lambda i: (off + i, 0)
    tiles = (tm * ROW_TILE, LANES)
    slot_specs = [pl.BlockSpec(tiles, functools.partial(lambda i, k: (k * (n_all // tm) + off + i, 0), k=k))
                  for k in range(TOP_K)]
    return pl.pallas_call(
        _combine_kernel,
        grid=(n_rows // tm,),
        in_specs=[pl.BlockSpec(tiles, tok), pl.BlockSpec((tm, TOP_K), tok), _full((1, D_MODEL)),
                  _full((1, D_MODEL))] + slot_specs,
        out_specs=pl.BlockSpec((tm, D_MODEL), lambda i: (i, 0)),
        out_shape=jax.ShapeDtypeStruct((n_rows, D_MODEL), F32),
        compiler_params=_params("arbitrary"),
        name="combine_ln3",
    )(x2_all, gates_all, g3, b3, *([y_tok] * TOP_K))


def _rope_tables(pos):
    inv_freq = ROPE_THETA ** (-jnp.arange(ROPE_HALF, dtype=F32) / ROPE_HALF)
    ang = pos.astype(F32)[:, None] * inv_freq[None, :]
    c, s, z = jnp.cos(ang), jnp.sin(ang), jnp.zeros_like(ang)
    return jnp.concatenate([c, z, c, z], axis=1), jnp.concatenate([-s, z, s, z], axis=1)


def _pad_rope_cols(w):
    z = jnp.zeros(w.shape[:-1] + (ROPE_HALF,), w.dtype)
    return jnp.concatenate([w[..., :ROPE_HALF], z, w[..., ROPE_HALF:], z], axis=-1)


def _layer_weights(l, w_in, w_pool, pool_scale, q_norm_g, w_uq, kv_norm_g, w_uk, w_uv, w_out, ln1_g, ln1_b,
                   w_mq, w_mo, ln2_g, ln2_b, w_router, b_router):
    o3 = POOL_WIDTH + Q_RANK + KV_RANK
    w_in_p = jnp.concatenate([w_in[l][:, :o3], _pad_rope_cols(w_in[l][:, o3:])], axis=1).astype(BF16)
    uq = w_uq[l].reshape(Q_RANK, MLA_HEADS, QK_NOPE_DIM + QK_ROPE_DIM)
    uq_p = jnp.concatenate([uq[..., :QK_NOPE_DIM], _pad_rope_cols(uq[..., QK_NOPE_DIM:])], axis=-1)
    uq_p = uq_p.reshape(Q_RANK, MLA_HEADS * QK_PAD).astype(BF16)
    w_ukt = jnp.transpose(w_uk[l], (1, 2, 0)).astype(BF16)
    w_uv_h = jnp.transpose(w_uv[l], (1, 0, 2)).astype(BF16)
    row = lambda v: v[l].reshape(1, -1)
    w_rt = w_router[l].T
    w_rt_hi = w_rt.astype(BF16)
    w_r_split = jnp.stack([w_rt_hi, (w_rt - w_rt_hi.astype(F32)).astype(BF16)])
    proj =[w_in_p, row(q_norm_g), uq_p, row(kv_norm_g), w_ukt, w_pool[l].astype(BF16), row(pool_scale)]
    post = [w_out[l][:POOL_WIDTH].astype(BF16), w_out[l][POOL_WIDTH:].astype(BF16), row(ln1_g), row(ln1_b),
            w_mq[l].astype(BF16), w_mo[l].astype(BF16), row(ln2_g), row(ln2_b), w_r_split, row(b_router)]
    return proj, w_uv_h, post


def _routing_tables(idx_all, rank_all, counts, n_blocks):
    tr = EXPERT_ROWS
    cnt = counts.reshape(-1).astype(I32)
    padded = ((cnt + tr - 1) // tr) * tr
    pend = jnp.cumsum(padded)
    pstart = pend - padded
    dest2d = pstart[idx_all] + rank_all
    dest = dest2d.reshape(-1)
    n_all = idx_all.shape[0]
    m = dest.shape[0]
    n_rows = n_blocks * tr
    disp_rows = (dest2d * ROW_TILE).reshape(n_all // DISPATCH_BLOCK, DISPATCH_BLOCK * TOP_K)
    assign = (jnp.arange(TOP_K, dtype=I32)[None, :] * n_all + jnp.arange(n_all, dtype=I32)[:, None]).reshape(-1)
    _, packed = lax.sort_key_val(dest, assign)
    start = jnp.cumsum(cnt) - cnt
    shift = pstart - start
    none = jnp.full((n_rows,), -1, I32)
    doubled = jnp.concatenate([none, packed, none[:n_rows - m]])
    rows = jnp.arange(n_rows, dtype=I32)
    row_src = none
    for e in range(N_EXPERTS):
        moved = lax.dynamic_slice(doubled, (n_rows - shift[e],), (n_rows,))
        row_src = jnp.where((rows >= pstart[e]) & (rows < pstart[e] + cnt[e]), moved, row_src)
    spare = m + jnp.arange(n_rows, dtype=I32) % (2 * tr)
    dst_rows = (jnp.where(row_src >= 0, row_src, spare) * ROW_TILE).reshape(n_blocks, tr)
    spare_row = spare[:tr][None] * ROW_TILE
    idx_rows = jnp.concatenate([spare_row, dst_rows, spare_row], axis=0)
    n_used = pend[-1] // tr
    blk_row = jnp.minimum(jnp.arange(n_blocks + 1, dtype=I32), n_used - 1)
    blk_exp = jnp.sum((pend[None, :] <= (blk_row * tr)[:, None]).astype(I32), axis=1)
    return (jnp.clip(blk_exp, 0, N_EXPERTS - 1), blk_row, idx_rows, disp_rows, pstart + cnt, padded - cnt,
            n_used.astype(I32).reshape(1))


def kernel(x_prompt, x_sample, mem_prompt, cache_kv_latent, cache_k_rope, cache_mem_k, cache_mem_v, state_pool,
           page_table, w_in, w_pool, pool_scale, q_norm_g, w_uq, kv_norm_g, w_uk, w_uv, w_out, ln1_g, ln1_b,
           w_mq, w_mk, w_mv, w_mo, ln2_g, ln2_b, w_router, b_router, w_gate_up, b_gate_up, w_down, b_down,
           ln3_g, ln3_b):
    n_bp, seq, _ = x_prompt.shape
    n_bs, n_new, _ = x_sample.shape
    n_mem = mem_prompt.shape[1]
    page = cache_kv_latent.shape[2]
    past_len = page_table.shape[1] * page
    n_p = n_bp * seq
    n_s = n_bs * n_new
    n_all = n_p + n_s
    n_assign = n_all * TOP_K
    assert w_in.shape[0] == DEPTH == 1
    assert seq % TOKEN_BLOCK == 0 and seq % ATTN_K_BLOCK == 0 and seq % ATTN_Q_BLOCK == 0 and seq >= POOL_HALO
    assert n_p % n_s == 0 and (n_s % TOKEN_BLOCK == 0 or TOKEN_BLOCK % n_s == 0)
    l = 0

    proj_w, w_uv_h, post_w = _layer_weights(l, w_in, w_pool, pool_scale, q_norm_g, w_uq, kv_norm_g, w_uk, w_uv,
                                            w_out, ln1_g, ln1_b, w_mq, w_mo, ln2_g, ln2_b, w_router, b_router)

    xp = x_prompt.reshape(n_p, D_MODEL)
    cos_p, sin_p = _rope_tables(jnp.arange(seq))
    qcat, kcat, lat_p, kr_p, pooly_p, pstate_p = _proj_prompt(xp, proj_w, cos_p, sin_p, n_bp, seq)
    mlay_p = _attn_prompt(qcat, kcat, w_uv_h, n_bp, seq)
    mk, mv, mk_b, mv_b = _mem_kv(mem_prompt.reshape(n_bp * n_mem, D_MODEL), w_mk[l].astype(BF16),
                                 w_mv[l].astype(BF16), n_bp, n_mem)
    x2_all, idx_all, gate_all, rank_all, cnt_p = _post_prompt(xp, pooly_p, mlay_p, mk_b, mv_b, post_w,
                                                              n_bp, seq, n_mem, n_all)

    xs = jnp.transpose(x_sample, (1, 0, 2)).reshape(n_s, D_MODEL)
    state_t = jnp.transpose(state_pool[l], (1, 0, 2))
    cos_s, sin_s = _rope_tables(jnp.repeat(past_len + jnp.arange(n_new), n_bs))
    qcat_s, kcat_s, lat_s, kr_s, pooly_s, pstate_s = _proj_sample(xs, state_t, proj_w, cos_s, sin_s, past_len)
    q_b = jnp.transpose(qcat_s.reshape(MLA_HEADS, n_new, n_bs, QK_PAD), (2, 0, 1, 3)).reshape(
        n_bs, MLA_HEADS * n_new, QK_PAD)
    knew_b = jnp.transpose(kcat_s.reshape(n_new, n_bs, QK_PAD), (1, 0, 2))
    knew_b = jnp.pad(knew_b, ((0, 0), (0, LANES - n_new), (0, 0)))
    mlay_b = _attn_sample(page_table, q_b, knew_b, w_uv_h, cache_kv_latent,
                          jnp.transpose(cache_k_rope, (0, 1, 3, 2)), l)
    mlay_s = jnp.transpose(mlay_b, (1, 0, 2)).reshape(n_s, POOL_WIDTH).astype(BF16)
    x1_s, qm_s = _mix_sample(xs, pooly_s, mlay_s, post_w)
    qm_b = jnp.transpose(qm_s.reshape(n_new, n_bs, MEM_WIDTH), (1, 0, 2))
    head_of_col = jnp.arange(MEM_WIDTH) // MEM_HEAD_DIM
    head_mask = (head_of_col[None, :] == jnp.arange(MEM_HEADS)[:, None]).astype(BF16)
    q_bd = (qm_b[:, None, :, :] * head_mask[None, :, None, :]).reshape(n_bs, MEM_HEADS * n_new, MEM_WIDTH)
    o_b = _mem_attn_sample(q_bd, cache_mem_k[l].reshape(n_bs, n_mem, MEM_WIDTH),
                           cache_mem_v[l].reshape(n_bs, n_mem, MEM_WIDTH), n_new)
    o_s = jnp.transpose(o_b, (1, 0, 2)).reshape(n_s, MEM_WIDTH)
    x2_all, idx_all, gate_all, rank_all, counts = _route_sample(x1_s, o_s, cnt_p, post_w, x2_all, idx_all, gate_all,
                                                               rank_all, n_p)

    n_blocks = -(-n_assign // EXPERT_ROWS) + N_EXPERTS + 1
    blk_exp, blk_row, idx_rows, disp_rows, pad_start, n_pad, n_used = _routing_tables(idx_all, rank_all, counts,
                                                                                      n_blocks)
    x_rows = _dispatch(pad_start, n_pad, n_used, disp_rows, x2_all, n_blocks * EXPERT_ROWS)
    y_tok = _moe(blk_exp, blk_row, idx_rows, x_rows, w_gate_up[l], b_gate_up[l][:, None, :],
                 w_down[l], b_down[l][:, None, :], n_assign)
    g3, b3 = ln3_g[l].reshape(1, -1), ln3_b[l].reshape(1, -1)
    y_p = _combine(x2_all, y_tok, gate_all, g3, b3, 0, n_p, TOKEN_BLOCK)
    y_s = _combine(x2_all, y_tok, gate_all, g3, b3, n_p, n_s, min(TOKEN_BLOCK, n_s))

    def from_steps(a, width):
        return jnp.transpose(a.reshape(n_new, n_bs, width), (1, 0, 2))[None]

    return (
        y_p.reshape(n_bp, seq, D_MODEL),
        from_steps(y_s, D_MODEL)[0],
        lat_p.reshape(1, n_bp, seq, KV_RANK),
        kr_p.reshape(1, n_bp, seq, QK_ROPE_DIM),
        pstate_p[None, :, POOL_HALO - POOL_BUF:, :],
        mk.reshape(1, n_bp, n_mem, MEM_HEADS, MEM_HEAD_DIM),
        mv.reshape(1, n_bp, n_mem, MEM_HEADS, MEM_HEAD_DIM),
        from_steps(lat_s, KV_RANK),
        from_steps(kr_s, QK_ROPE_DIM),
        jnp.transpose(pstate_s, (1, 0, 2))[None],
    )
```

```python
import functools

import jax
import jax.numpy as jnp
from jax import lax
from jax.experimental import pallas as pl
from jax.experimental.pallas import tpu as pltpu

F32 = jnp.float32
BF16 = jnp.bfloat16
I32 = jnp.int32

D_MODEL = 1024
POOL_WIDTH = 512
POOL_WINDOWS = (2, 4, 8, 16)
POOL_GROUP_DIM = 128
POOL_BUF = 15
POOL_HALO = 16
MLA_HEADS = 4
QK_NOPE_DIM = 128
QK_ROPE_DIM = 64
ROPE_HALF = 32
V_HEAD_DIM = 128
Q_RANK = 256
KV_RANK = 128
ROPE_THETA = 10000.0
QK_PAD = 256
ONES_LANE = QK_PAD - 1
MEM_HEADS = 4
MEM_HEAD_DIM = 128
MEM_WIDTH = 512
N_EXPERTS = 32
TOP_K = 4
D_EXPERT = 1024
SWIGLU_LIMIT = 7.0
SWIGLU_ALPHA = 1.702
LN_EPS = 1e-5
RMS_EPS = 1e-6
DEPTH = 1
DEEPNORM_ALPHA = (2.0 * DEPTH) ** 0.25
ATTN_SCALE = (QK_NOPE_DIM + QK_ROPE_DIM) ** -0.5
LOG2_E = 1.4426950408889634
MEM_SCALE = MEM_HEAD_DIM ** -0.5

LANES = 128
VMEM_LIMIT = 48 * 1024 * 1024

TOKEN_BLOCK = 512
ATTN_Q_BLOCK = 256
ATTN_K_BLOCK = 512
ATTN_GROUPS = 2
PAGES_PER_STEP = 32
MEM_BATCH_BLOCK = 8
DISPATCH_BLOCK = 256
EXPERT_ROWS = 256
EXPERT_CHUNK = 512


def _dot(a, b):
    return jnp.dot(a, b, preferred_element_type=F32)


def _dot_nt(a, b):
    return lax.dot_general(a, b, (((1,), (1,)), ((), ())), preferred_element_type=F32)


def _rms(x, g):
    return x * lax.rsqrt(jnp.mean(x * x, axis=-1, keepdims=True) + RMS_EPS) * g


def _layer_norm(x, g, b):
    mu = jnp.mean(x, axis=-1, keepdims=True)
    xc = x - mu
    var = jnp.mean(xc * xc, axis=-1, keepdims=True)
    return xc * lax.rsqrt(var + LN_EPS) * g + b


ROW_TILE = D_MODEL // LANES


def _store_row_tiles(ref, lead, val):
    m = val.shape[0]
    for j in range(ROW_TILE):
        ref[(*lead, pl.ds(j, m, stride=ROW_TILE), slice(None))] = val[:, LANES * j:LANES * (j + 1)]


def _load_row_tiles(ref, lead, m):
    return jnp.concatenate([ref[(*lead, pl.ds(j, m, stride=ROW_TILE), slice(None))] for j in range(ROW_TILE)], axis=1)


def _params(*semantics):
    return pltpu.CompilerParams(dimension_semantics=semantics, vmem_limit_bytes=VMEM_LIMIT)


def _full(shape):
    n = len(shape)
    return pl.BlockSpec(shape, lambda *_: (0,) * n)


def _project(x_bf, w_in_ref, qg_ref, w_uq_ref, kvg_ref, w_ukt_ref, cos, sin, q_scale):
    h = _dot(x_bf, w_in_ref[...])
    u = h[:, :POOL_WIDTH]
    qn = _rms(h[:, POOL_WIDTH:POOL_WIDTH + Q_RANK], qg_ref[...])
    q = _dot(qn.astype(BF16), w_uq_ref[...])
    q_lat, q_rope = [], []
    for hd in range(MLA_HEADS):
        nope = q[:, QK_PAD * hd:QK_PAD * hd + QK_NOPE_DIM]
        rp = q[:, QK_PAD * hd + QK_NOPE_DIM:QK_PAD * (hd + 1)]
        q_lat.append(_dot(nope.astype(BF16), w_ukt_ref[hd]) * q_scale)
        q_rope.append((rp * cos + pltpu.roll(rp, 64, 1) * sin) * q_scale)
    o2 = POOL_WIDTH + Q_RANK
    lat = _rms(h[:, o2:o2 + KV_RANK], kvg_ref[...])
    kp = h[:, o2 + KV_RANK:]
    k_rope = kp * cos + pltpu.roll(kp, 64, 1) * sin
    return u, q_lat, q_rope, lat, k_rope


def _compact_rope(r):
    lane = lax.broadcasted_iota(I32, r.shape, 1)
    moved = pltpu.roll(r, 96, 1)
    return jnp.where(lane < ROPE_HALF, r, jnp.where(lane < QK_ROPE_DIM, moved, 0.0))


def _with_ones_lane(r):
    lane = lax.broadcasted_iota(I32, r.shape, 1)
    return jnp.where(lane == ONES_LANE - KV_RANK, 1.0, r)


def _pool_group_out(diff, g, w_pool_ref, pscale_ref):
    cols = slice(POOL_GROUP_DIM * g, POOL_GROUP_DIM * (g + 1))
    return _dot(diff.astype(BF16), w_pool_ref[g]) * pscale_ref[:, cols]


def _proj_prompt_kernel(x_ref, w_in_ref, qg_ref, w_uq_ref, kvg_ref, w_ukt_ref, w_pool_ref, pscale_ref,
                        cos_ref, sin_ref,
                        qcat_ref, kcat_ref, lat_ref, kr_ref, pooly_ref, pstate_ref, ubuf):
    j = pl.program_id(1)
    tm = x_ref.shape[0]

    @pl.when(j == 0)
    def _():
        ubuf[0:POOL_HALO, :] = jnp.zeros((POOL_HALO, POOL_WIDTH), F32)

    u, q_lat, q_rope, lat, k_rope = _project(
        x_ref[...].astype(BF16), w_in_ref, qg_ref, w_uq_ref, kvg_ref, w_ukt_ref, cos_ref[...], sin_ref[...],
        ATTN_SCALE * LOG2_E)
    for hd in range(MLA_HEADS):
        qcat_ref[hd, :, 0:KV_RANK] = q_lat[hd].astype(BF16)
        qcat_ref[hd, :, KV_RANK:QK_PAD] = q_rope[hd].astype(BF16)
    lat_ref[...] = lat
    kcat_ref[:, 0:KV_RANK] = lat.astype(BF16)
    kcat_ref[:, KV_RANK:QK_PAD] = _with_ones_lane(k_rope).astype(BF16)
    kr_ref[...] = _compact_rope(k_rope)[:, :QK_ROPE_DIM]

    ubuf[POOL_HALO:POOL_HALO + tm, :] = u
    pos = j * tm + lax.broadcasted_iota(I32, (tm, 1), 0)
    for g, w in enumerate(POOL_WINDOWS):
        cols = slice(POOL_GROUP_DIM * g, POOL_GROUP_DIM * (g + 1))
        ug = u[:, cols]
        ssum = ug
        for k in range(1, w):
            ssum = ssum + ubuf[POOL_HALO - k:POOL_HALO - k + tm, cols]
        count = jnp.minimum(w, pos + 1).astype(F32)
        pooly_ref[:, cols] = _pool_group_out(ssum / count - ug, g, w_pool_ref, pscale_ref).astype(BF16)
    tail = ubuf[tm:tm + POOL_HALO, :]
    ubuf[0:POOL_HALO, :] = tail

    @pl.when(j == pl.num_programs(1) - 1)
    def _():
        pstate_ref[...] = tail


def _proj_sample_kernel(x_ref, state_ref, w_in_ref, qg_ref, w_uq_ref, kvg_ref, w_ukt_ref, w_pool_ref, pscale_ref,
                        cos_ref, sin_ref,
                        qcat_ref, kcat_ref, lat_ref, kr_ref, pooly_ref, pstate_ref, *, past_len):
    n_b = state_ref.shape[1]
    n_t = x_ref.shape[0] // n_b
    u, q_lat, q_rope, lat, k_rope = _project(
        x_ref[...].astype(BF16), w_in_ref, qg_ref, w_uq_ref, kvg_ref, w_ukt_ref, cos_ref[...], sin_ref[...],
        ATTN_SCALE)
    for hd in range(MLA_HEADS):
        qcat_ref[hd, :, 0:KV_RANK] = q_lat[hd].astype(BF16)
        qcat_ref[hd, :, KV_RANK:QK_PAD] = _compact_rope(q_rope[hd]).astype(BF16)
    lat_ref[...] = lat
    kc = _compact_rope(k_rope)
    kcat_ref[:, 0:KV_RANK] = lat.astype(BF16)
    kcat_ref[:, KV_RANK:QK_PAD] = _with_ones_lane(kc).astype(BF16)
    kr_ref[...] = kc[:, :QK_ROPE_DIM]

    def ext(jj):
        if jj < POOL_BUF:
            return state_ref[jj]
        return u[(jj - POOL_BUF) * n_b:(jj - POOL_BUF + 1) * n_b, :]

    for t in range(n_t):
        ut = ext(POOL_BUF + t)
        for g, w in enumerate(POOL_WINDOWS):
            cols = slice(POOL_GROUP_DIM * g, POOL_GROUP_DIM * (g + 1))
            ssum = ut[:, cols]
            for k in range(1, w):
                ssum = ssum + ext(POOL_BUF + t - k)[:, cols]
            count = float(min(w, past_len + t + 1))
            y = _pool_group_out(ssum / count - ut[:, cols], g, w_pool_ref, pscale_ref)
            pooly_ref[t * n_b:(t + 1) * n_b, cols] = y.astype(BF16)
    for jj in range(POOL_BUF):
        pstate_ref[jj] = ext(n_t + jj)


def _proj_weight_specs():
    return [
        _full((D_MODEL, D_MODEL)),
        _full((1, Q_RANK)),
        _full((Q_RANK, MLA_HEADS * QK_PAD)),
        _full((1, KV_RANK)),
        _full((MLA_HEADS, QK_NOPE_DIM, KV_RANK)),
        _full((len(POOL_WINDOWS), POOL_GROUP_DIM, POOL_GROUP_DIM)),
        _full((1, POOL_WIDTH)),
    ]


def _proj_prompt(x2d, weights, cos_t, sin_t, n_batch, seq):
    tm = TOKEN_BLOCK
    n = n_batch * seq
    nj = seq // tm
    tok = lambda b, j: (b * nj + j, 0)
    return pl.pallas_call(
        _proj_prompt_kernel,
        grid=(n_batch, nj),
        in_specs=[pl.BlockSpec((tm, D_MODEL), tok)] + _proj_weight_specs() + [
            pl.BlockSpec((tm, LANES), lambda b, j: (j, 0)),
            pl.BlockSpec((tm, LANES), lambda b, j: (j, 0)),
        ],
        out_specs=[
            pl.BlockSpec((MLA_HEADS, tm, QK_PAD), lambda b, j: (0, b * nj + j, 0)),
            pl.BlockSpec((tm, QK_PAD), tok),
            pl.BlockSpec((tm, KV_RANK), tok),
            pl.BlockSpec((tm, QK_ROPE_DIM), tok),
            pl.BlockSpec((tm, POOL_WIDTH), tok),
            pl.BlockSpec((None, POOL_HALO, POOL_WIDTH), lambda b, j: (b, 0, 0)),
        ],
        out_shape=[
            jax.ShapeDtypeStruct((MLA_HEADS, n, QK_PAD), BF16),
            jax.ShapeDtypeStruct((n, QK_PAD), BF16),
            jax.ShapeDtypeStruct((n, KV_RANK), F32),
            jax.ShapeDtypeStruct((n, QK_ROPE_DIM), F32),
            jax.ShapeDtypeStruct((n, POOL_WIDTH), BF16),
            jax.ShapeDtypeStruct((n_batch, POOL_HALO, POOL_WIDTH), F32),
        ],
        scratch_shapes=[pltpu.VMEM((POOL_HALO + tm, POOL_WIDTH), F32)],
        compiler_params=_params("arbitrary", "arbitrary"),
        name="proj_prompt",
    )(x2d, *weights, cos_t, sin_t)


def _proj_sample(x2d, state_t, weights, cos_t, sin_t, past_len):
    n = x2d.shape[0]
    n_b = state_t.shape[1]
    return pl.pallas_call(
        functools.partial(_proj_sample_kernel, past_len=past_len),
        grid=(1,),
        in_specs=[_full((n, D_MODEL)), _full((POOL_BUF, n_b, POOL_WIDTH))] + _proj_weight_specs() + [
            _full((n, LANES)), _full((n, LANES))],
        out_specs=[
            _full((MLA_HEADS, n, QK_PAD)), _full((n, QK_PAD)), _full((n, KV_RANK)), _full((n, QK_ROPE_DIM)),
            _full((n, POOL_WIDTH)), _full((POOL_BUF, n_b, POOL_WIDTH)),
        ],
        out_shape=[
            jax.ShapeDtypeStruct((MLA_HEADS, n, QK_PAD), BF16),
            jax.ShapeDtypeStruct((n, QK_PAD), BF16),
            jax.ShapeDtypeStruct((n, KV_RANK), F32),
            jax.ShapeDtypeStruct((n, QK_ROPE_DIM), F32),
            jax.ShapeDtypeStruct((n, POOL_WIDTH), BF16),
            jax.ShapeDtypeStruct((POOL_BUF, n_b, POOL_WIDTH), F32),
        ],
        compiler_params=_params("arbitrary"),
        name="proj_sample",
    )(x2d, state_t, *weights, cos_t, sin_t)


def _attn_block(q, k, mask, m_ref, acc_ref):
    s = _dot_nt(q, k)
    if mask is not None:
        s = jnp.where(mask, s, -jnp.inf)
    tiles = [s[:, c * LANES:(c + 1) * LANES] for c in range(s.shape[1] // LANES)]
    m_old = m_ref[...]
    m_new = jnp.maximum(m_old, jnp.max(functools.reduce(jnp.maximum, tiles), axis=1, keepdims=True))
    alpha = jnp.exp2(m_old - m_new)
    p = jnp.concatenate([jnp.exp2(t - m_new) for t in tiles], axis=1).astype(BF16)
    acc_ref[...] = jnp.concatenate([alpha, alpha], axis=1) * acc_ref[...] + _dot(p, k)
    m_ref[...] = m_new


def _attn_output(acc):
    return acc[:, :KV_RANK] / acc[:, ONES_LANE:ONES_LANE + 1]


def _attn_prompt_kernel(q_ref, k_ref, w_uv_ref, out_ref, m_ref, acc_ref):
    i = pl.program_id(1)
    tq = q_ref.shape[1]
    tk = ATTN_K_BLOCK
    hpg = MLA_HEADS // ATTN_GROUPS
    rows = hpg * tq
    m_ref[...] = jnp.full(m_ref.shape, -jnp.inf, F32)
    acc_ref[...] = jnp.zeros(acc_ref.shape, F32)

    def block(jk, masked):
        k = k_ref[pl.ds(pl.multiple_of(jk * tk, tk), tk), :]
        mask = None
        if masked:
            q_pos = i * tq + lax.broadcasted_iota(I32, (rows, tk), 0) % tq
            k_pos = jk * tk + lax.broadcasted_iota(I32, (rows, tk), 1)
            mask = k_pos <= q_pos
        for g in range(ATTN_GROUPS):
            q = q_ref[hpg * g:hpg * (g + 1)].reshape(rows, QK_PAD)
            _attn_block(q, k, mask, m_ref.at[g], acc_ref.at[g])

    n_full = (i * tq + 1) // tk
    n_kv = (i * tq + tq - 1) // tk + 1

    def full_pair(jp, c):
        block(2 * jp, False)
        block(2 * jp + 1, False)
        return c

    def diag_body(jk, c):
        block(jk, True)
        return c

    lax.fori_loop(0, n_full // 2, full_pair, 0)

    @pl.when(n_full % 2 == 1)
    def _():
        block(n_full - 1, False)

    lax.fori_loop(n_full, n_kv, diag_body, 0)

    for g in range(ATTN_GROUPS):
        o = _attn_output(acc_ref[g])
        for hl in range(hpg):
            hd = hpg * g + hl
            oh = o[hl * tq:(hl + 1) * tq, :].astype(BF16)
            out_ref[:, V_HEAD_DIM * hd:V_HEAD_DIM * (hd + 1)] = _dot(oh, w_uv_ref[hd]).astype(BF16)


def _attn_prompt(qcat, kcat, w_uv_h, n_batch, seq):
    tq = ATTN_Q_BLOCK
    nq = seq // tq
    n = n_batch * seq
    rows = (MLA_HEADS // ATTN_GROUPS) * tq
    return pl.pallas_call(
        _attn_prompt_kernel,
        grid=(n_batch, nq),
        in_specs=[
            pl.BlockSpec((MLA_HEADS, tq, QK_PAD), lambda b, i: (0, b * nq + i, 0)),
            pl.BlockSpec((seq, QK_PAD), lambda b, i: (b, 0)),
            _full((MLA_HEADS, KV_RANK, V_HEAD_DIM)),
        ],
        out_specs=pl.BlockSpec((tq, MLA_HEADS * V_HEAD_DIM), lambda b, i: (b * nq + i, 0)),
        out_shape=jax.ShapeDtypeStruct((n, MLA_HEADS * V_HEAD_DIM), BF16),
        scratch_shapes=[pltpu.VMEM((ATTN_GROUPS, rows, LANES), F32), pltpu.VMEM((ATTN_GROUPS, rows, QK_PAD), F32)],
        compiler_params=_params("arbitrary", "arbitrary"),
        name="attn_prompt",
    )(qcat, kcat, w_uv_h)


def _attn_sample_kernel(pt_ref, q_ref, knew_ref, w_uv_ref, *rest, n_pages_step, n_new):
    lat_refs = rest[:n_pages_step]
    rope_refs = rest[n_pages_step:2 * n_pages_step]
    out_ref, m_ref, l_ref, acc_ref, res_ref = rest[2 * n_pages_step:]
    del pt_ref
    s_idx = pl.program_id(1)

    @pl.when(s_idx == 0)
    def _():
        m_ref[...] = jnp.full(m_ref.shape, -jnp.inf, F32)
        l_ref[...] = jnp.zeros(l_ref.shape, F32)
        acc_ref[...] = jnp.zeros(acc_ref.shape, F32)

    def update(s_tiles, values):
        m_old = m_ref[...]
        m_new = jnp.maximum(m_old, jnp.max(functools.reduce(jnp.maximum, s_tiles), axis=1, keepdims=True))
        alpha = jnp.exp(m_old - m_new)
        p_tiles = [jnp.exp(t - m_new) for t in s_tiles]
        row_sum = jnp.sum(functools.reduce(jnp.add, p_tiles), axis=1, keepdims=True)
        pv = functools.reduce(jnp.add, [_dot(p.astype(BF16), v) for p, v in zip(p_tiles, values)])
        l_ref[...] = alpha * l_ref[...] + row_sum
        acc_ref[...] = alpha * acc_ref[...] + pv
        m_ref[...] = m_new

    q = q_ref[...]
    q_lat = q[:, :KV_RANK]
    q_rope = q[:, KV_RANK:KV_RANK + QK_ROPE_DIM]
    lats = [lat_refs[p][...].astype(BF16) for p in range(n_pages_step)]
    update([_dot_nt(q_lat, lats[p]) + _dot(q_rope, rope_refs[p][...].astype(BF16)) for p in range(n_pages_step)],
           lats)

    @pl.when(s_idx == pl.num_programs(1) - 1)
    def _():
        kn = knew_ref[...]
        q_t = lax.broadcasted_iota(I32, (q.shape[0], kn.shape[0]), 0) % n_new
        k_t = lax.broadcasted_iota(I32, (q.shape[0], kn.shape[0]), 1)
        update([jnp.where(k_t <= q_t, _dot_nt(q, kn), -jnp.inf)], [kn[:, :KV_RANK]])
        o = (acc_ref[...] / l_ref[...]).astype(BF16)
        for hd in range(MLA_HEADS):
            res_ref[...] = _dot(o, w_uv_ref[hd])
            out_ref[:, V_HEAD_DIM * hd:V_HEAD_DIM * (hd + 1)] = res_ref[hd * n_new:(hd + 1) * n_new, :]


def _attn_sample(page_table, q_b, knew_b, w_uv_h, cache_lat, cache_rope, layer):
    n_b, n_pages = page_table.shape
    page = cache_lat.shape[2]
    pps = min(PAGES_PER_STEP, n_pages)
    assert n_pages % pps == 0
    n_steps = n_pages // pps
    rows = q_b.shape[1]
    n_new = rows // MLA_HEADS
    t_pad = knew_b.shape[1]

    def page_map(p):
        return lambda b, s, pt: (layer, pt[b * n_pages + s * pps + p], 0, 0)

    lat_specs = [pl.BlockSpec((None, None, page, KV_RANK), page_map(p)) for p in range(pps)]
    rope_specs = [pl.BlockSpec((None, None, QK_ROPE_DIM, page), page_map(p)) for p in range(pps)]
    grid_spec = pltpu.PrefetchScalarGridSpec(
        num_scalar_prefetch=1,
        grid=(n_b, n_steps),
        in_specs=[
            pl.BlockSpec((None, rows, QK_PAD), lambda b, s, pt: (b, 0, 0)),
            pl.BlockSpec((None, t_pad, QK_PAD), lambda b, s, pt: (b, 0, 0)),
            pl.BlockSpec((MLA_HEADS, KV_RANK, V_HEAD_DIM), lambda b, s, pt: (0, 0, 0)),
        ] + lat_specs + rope_specs,
        out_specs=pl.BlockSpec((None, n_new, MLA_HEADS * V_HEAD_DIM), lambda b, s, pt: (b, 0, 0)),
        scratch_shapes=[
            pltpu.VMEM((rows, LANES), F32), pltpu.VMEM((rows, LANES), F32), pltpu.VMEM((rows, KV_RANK), F32),
            pltpu.VMEM((rows, V_HEAD_DIM), F32),
        ],
    )
    return pl.pallas_call(
        functools.partial(_attn_sample_kernel, n_pages_step=pps, n_new=n_new),
        grid_spec=grid_spec,
        out_shape=jax.ShapeDtypeStruct((n_b, n_new, MLA_HEADS * V_HEAD_DIM), F32),
        compiler_params=_params("arbitrary", "arbitrary"),
        name="attn_sample",
    )(page_table.reshape(-1), q_b, knew_b, w_uv_h, *([cache_lat] * pps), *([cache_rope] * pps))


def _mem_kv_kernel(mem_ref, w_mk_ref, w_mv_ref, k_ref, v_ref, kb_ref, vb_ref):
    m = mem_ref[...].astype(BF16)
    k = _dot(m, w_mk_ref[...])
    v = _dot(m, w_mv_ref[...])
    k_ref[...] = k
    v_ref[...] = v
    kb_ref[...] = k.astype(BF16)
    vb_ref[...] = v.astype(BF16)


def _mem_kv(mem2d, w_mk, w_mv, n_batch, n_mem):
    blk = pl.BlockSpec((n_mem, MEM_WIDTH), lambda b: (b, 0))
    n = n_batch * n_mem
    return pl.pallas_call(
        _mem_kv_kernel,
        grid=(n_batch,),
        in_specs=[pl.BlockSpec((n_mem, D_MODEL), lambda b: (b, 0)), _full((D_MODEL, MEM_WIDTH)),
                  _full((D_MODEL, MEM_WIDTH))],
        out_specs=[blk, blk, blk, blk],
        out_shape=[jax.ShapeDtypeStruct((n, MEM_WIDTH), F32), jax.ShapeDtypeStruct((n, MEM_WIDTH), F32),
                   jax.ShapeDtypeStruct((n, MEM_WIDTH), BF16), jax.ShapeDtypeStruct((n, MEM_WIDTH), BF16)],
        compiler_params=_params("arbitrary"),
        name="mem_kv",
    )(mem2d, w_mk, w_mv)


def _mix_ln1(x, pool_y, mla_y, w_out_a_ref, w_out_b_ref, g_ref, b_ref):
    mix = _dot(pool_y, w_out_a_ref[...]) + _dot(mla_y, w_out_b_ref[...])
    return _layer_norm(DEEPNORM_ALPHA * x + mix, g_ref[...], b_ref[...])


def _route(x2, w_r_ref, b_r_ref, carry):
    m = x2.shape[0]
    x_hi = x2.astype(BF16)
    x_lo = (x2 - x_hi.astype(F32)).astype(BF16)
    logits_t = _dot_nt(w_r_ref[0], x_hi) + (_dot_nt(w_r_ref[0], x_lo) + _dot_nt(w_r_ref[1], x_hi))
    logits = logits_t.T + b_r_ref[...]
    e_idx = lax.broadcasted_iota(I32, (m, N_EXPERTS), 1).astype(F32)
    work = logits
    vals, picks = [], []
    for _ in range(TOP_K):
        mx = jnp.max(work, axis=1, keepdims=True)
        pick = jnp.min(jnp.where(work == mx, e_idx, float(N_EXPERTS)), axis=1, keepdims=True)
        vals.append(mx)
        picks.append(pick)
        work = jnp.where(e_idx == pick, -jnp.inf, work)
    hot = jnp.where(work == -jnp.inf, 1.0, 0.0)
    exps = [jnp.exp(v - vals[0]) for v in vals]
    denom = exps[0] + exps[1] + exps[2] + exps[3]
    r_i = lax.broadcasted_iota(I32, (m, m), 0)
    c_i = lax.broadcasted_iota(I32, (m, m), 1)
    strict_lower = jnp.where(c_i < r_i, 1.0, 0.0).astype(BF16)
    before = _dot(strict_lower, hot.astype(BF16)) + carry
    lane4 = lax.broadcasted_iota(I32, (m, TOP_K), 1)
    idx = jnp.zeros((m, TOP_K), I32)
    gates = jnp.zeros((m, TOP_K), F32)
    rank = jnp.zeros((m, TOP_K), I32)
    for k in range(TOP_K):
        rk = jnp.sum(jnp.where(e_idx == picks[k], before, 0.0), axis=1, keepdims=True).astype(I32)
        idx = jnp.where(lane4 == k, picks[k].astype(I32), idx)
        gates = jnp.where(lane4 == k, exps[k] / denom, gates)
        rank = jnp.where(lane4 == k, rk, rank)
    return idx, gates, rank, carry + jnp.sum(hot, axis=0, keepdims=True)


def _mo_ln2_route(x1, o_bf, w_mo_ref, g_ref, b_ref, w_r_ref, b_r_ref, carry):
    x2 = _layer_norm(DEEPNORM_ALPHA * x1 + _dot(o_bf, w_mo_ref[...]), g_ref[...], b_ref[...])
    return (x2,) + _route(x2, w_r_ref, b_r_ref, carry)


def _post_prompt_kernel(x_ref, pooly_ref, mlay_ref, mk_ref, mv_ref,
                        w_out_a_ref, w_out_b_ref, g1_ref, b1_ref, w_mq_ref, w_mo_ref, g2_ref, b2_ref,
                        w_r_ref, b_r_ref,
                        x2_ref, idx_ref, gate_ref, rank_ref, cnt_ref, carry_ref):
    first = jnp.logical_and(pl.program_id(0) == 0, pl.program_id(1) == 0)

    @pl.when(first)
    def _():
        carry_ref[...] = jnp.zeros(carry_ref.shape, F32)

    x1 = _mix_ln1(x_ref[...], pooly_ref[...], mlay_ref[...], w_out_a_ref, w_out_b_ref, g1_ref, b1_ref)
    qm = (_dot(x1.astype(BF16), w_mq_ref[...]) * MEM_SCALE).astype(BF16)
    outs = []
    for hd in range(MEM_HEADS):
        cols = slice(MEM_HEAD_DIM * hd, MEM_HEAD_DIM * (hd + 1))
        s = _dot_nt(qm[:, cols], mk_ref[:, cols])
        p = jnp.exp(s - jnp.max(s, axis=1, keepdims=True))
        o = _dot(p.astype(BF16), mv_ref[:, cols]) / jnp.sum(p, axis=1, keepdims=True)
        outs.append(o.astype(BF16))
    o_all = jnp.concatenate(outs, axis=1)
    x2, idx, gates, rank, carry = _mo_ln2_route(x1, o_all, w_mo_ref, g2_ref, b2_ref, w_r_ref, b_r_ref,
                                                carry_ref[...])
    _store_row_tiles(x2_ref, (), x2)
    idx_ref[...] = idx
    gate_ref[...] = gates
    rank_ref[...] = rank
    carry_ref[...] = carry
    cnt_ref[...] = carry


def _post_weight_specs():
    half = (POOL_WIDTH, D_MODEL)
    return [
        _full(half), _full(half), _full((1, D_MODEL)), _full((1, D_MODEL)),
        _full((D_MODEL, MEM_WIDTH)), _full((MEM_WIDTH, D_MODEL)), _full((1, D_MODEL)), _full((1, D_MODEL)),
        _full((2, N_EXPERTS, D_MODEL)), _full((1, N_EXPERTS)),
    ]


def _post_prompt(x2d, pool_y, mla_y, mk_b, mv_b, weights, n_batch, seq, n_mem, n_all):
    tm = TOKEN_BLOCK
    nj = seq // tm
    tok = lambda b, j: (b * nj + j, 0)
    return pl.pallas_call(
        _post_prompt_kernel,
        grid=(n_batch, nj),
        in_specs=[
            pl.BlockSpec((tm, D_MODEL), tok), pl.BlockSpec((tm, POOL_WIDTH), tok), pl.BlockSpec((tm, POOL_WIDTH), tok),
            pl.BlockSpec((n_mem, MEM_WIDTH), lambda b, j: (b, 0)), pl.BlockSpec((n_mem, MEM_WIDTH), lambda b, j: (b, 0)),
        ] + _post_weight_specs(),
        out_specs=[
            pl.BlockSpec((tm * ROW_TILE, LANES), tok), pl.BlockSpec((tm, TOP_K), tok), pl.BlockSpec((tm, TOP_K), tok),
            pl.BlockSpec((tm, TOP_K), tok), _full((1, N_EXPERTS)),
        ],
        out_shape=[
            jax.ShapeDtypeStruct((n_all * ROW_TILE, LANES), F32), jax.ShapeDtypeStruct((n_all, TOP_K), I32),
            jax.ShapeDtypeStruct((n_all, TOP_K), F32), jax.ShapeDtypeStruct((n_all, TOP_K), I32),
            jax.ShapeDtypeStruct((1, N_EXPERTS), F32),
        ],
        scratch_shapes=[pltpu.VMEM((1, N_EXPERTS), F32)],
        compiler_params=_params("arbitrary", "arbitrary"),
        name="post_prompt",
    )(x2d, pool_y, mla_y, mk_b, mv_b, *weights)


def _mix_sample_kernel(x_ref, pooly_ref, mlay_ref, w_out_a_ref, w_out_b_ref, g1_ref, b1_ref, w_mq_ref,
                       x1_ref, qm_ref):
    x1 = _mix_ln1(x_ref[...], pooly_ref[...], mlay_ref[...], w_out_a_ref, w_out_b_ref, g1_ref, b1_ref)
    x1_ref[...] = x1
    qm_ref[...] = (_dot(x1.astype(BF16), w_mq_ref[...]) * MEM_SCALE).astype(BF16)


def _mix_sample(x2d, pool_y, mla_y, weights):
    n = x2d.shape[0]
    w_out_a, w_out_b, g1, b1, w_mq = weights[:5]
    return pl.pallas_call(
        _mix_sample_kernel,
        grid=(1,),
        in_specs=[_full((n, D_MODEL)), _full((n, POOL_WIDTH)), _full((n, POOL_WIDTH))] + _post_weight_specs()[:5],
        out_specs=[_full((n, D_MODEL)), _full((n, MEM_WIDTH))],
        out_shape=[jax.ShapeDtypeStruct((n, D_MODEL), F32), jax.ShapeDtypeStruct((n, MEM_WIDTH), BF16)],
        compiler_params=_params("arbitrary"),
        name="mix_sample",
    )(x2d, pool_y, mla_y, w_out_a, w_out_b, g1, b1, w_mq)


def _mem_attn_sample_kernel(q_ref, k_ref, v_ref, out_ref, res_ref, *, n_new):
    n_mem = k_ref.shape[1] // MEM_HEADS
    rows = q_ref.shape[1]
    row_head = lax.broadcasted_iota(I32, (rows, 1), 0) // n_new
    for b in range(q_ref.shape[0]):
        q = q_ref[b]
        s = jnp.zeros((rows, n_mem), F32)
        for hd in range(MEM_HEADS):
            k_h = k_ref[b, pl.ds(hd, n_mem, stride=MEM_HEADS), :].astype(BF16)
            s = jnp.where(row_head == hd, _dot_nt(q, k_h), s)
        p = jnp.exp(s - jnp.max(s, axis=1, keepdims=True))
        p_bf = p.astype(BF16)
        o = jnp.zeros((rows, MEM_HEAD_DIM), F32)
        for hd in range(MEM_HEADS):
            v_h = v_ref[b, pl.ds(hd, n_mem, stride=MEM_HEADS), :].astype(BF16)
            o = jnp.where(row_head == hd, _dot(p_bf, v_h), o)
        res_ref[...] = o / jnp.sum(p, axis=1, keepdims=True)
        for hd in range(MEM_HEADS):
            cols = slice(MEM_HEAD_DIM * hd, MEM_HEAD_DIM * (hd + 1))
            out_ref[b, :, cols] = res_ref[hd * n_new:(hd + 1) * n_new, :].astype(BF16)


def _mem_attn_sample(q_b, mem_k, mem_v, n_new):
    n_b, rows, _ = q_b.shape
    mh = mem_k.shape[1]
    gb = min(MEM_BATCH_BLOCK, n_b)
    assert n_b % gb == 0
    blk = lambda i: (i, 0, 0)
    return pl.pallas_call(
        functools.partial(_mem_attn_sample_kernel, n_new=n_new),
        grid=(n_b // gb,),
        in_specs=[pl.BlockSpec((gb, rows, MEM_HEAD_DIM), blk), pl.BlockSpec((gb, mh, MEM_HEAD_DIM), blk),
                  pl.BlockSpec((gb, mh, MEM_HEAD_DIM), blk)],
        out_specs=pl.BlockSpec((gb, n_new, MEM_WIDTH), blk),
        out_shape=jax.ShapeDtypeStruct((n_b, n_new, MEM_WIDTH), BF16),
        scratch_shapes=[pltpu.VMEM((rows, MEM_HEAD_DIM), F32)],
        compiler_params=_params("arbitrary"),
        name="mem_attn_sample",
    )(q_b, mem_k, mem_v)


def _route_sample_kernel(x1_ref, o_ref, cnt_in_ref, w_mo_ref, g2_ref, b2_ref, w_r_ref, b_r_ref,
                         x2_in, idx_in, gate_in, rank_in,
                         x2_ref, idx_ref, gate_ref, rank_ref, cnt_ref):
    del x2_in, idx_in, gate_in, rank_in
    x2, idx, gates, rank, carry = _mo_ln2_route(x1_ref[...], o_ref[...], w_mo_ref, g2_ref, b2_ref, w_r_ref, b_r_ref,
                                                cnt_in_ref[...])
    _store_row_tiles(x2_ref, (), x2)
    idx_ref[...] = idx
    gate_ref[...] = gates
    rank_ref[...] = rank
    cnt_ref[...] = carry


def _route_sample(x1, o_bf, cnt_in, weights, x2_all, idx_all, gate_all, rank_all, n_prompt):
    n = x1.shape[0]
    assert n_prompt % n == 0
    tail = lambda i: (n_prompt // n, 0)
    anyspec = pl.BlockSpec(memory_space=pl.ANY)
    w_mo, g2, b2, w_r, b_r = weights[5:]
    return pl.pallas_call(
        _route_sample_kernel,
        grid=(1,),
        in_specs=[_full((n, D_MODEL)), _full((n, MEM_WIDTH)), _full((1, N_EXPERTS))] + _post_weight_specs()[5:] + [
            anyspec, anyspec, anyspec, anyspec],
        out_specs=[pl.BlockSpec((n * ROW_TILE, LANES), tail), pl.BlockSpec((n, TOP_K), tail), pl.BlockSpec((n, TOP_K), tail),
                   pl.BlockSpec((n, TOP_K), tail), _full((1, N_EXPERTS))],
        out_shape=[jax.ShapeDtypeStruct(x2_all.shape, F32), jax.ShapeDtypeStruct(idx_all.shape, I32),
                   jax.ShapeDtypeStruct(gate_all.shape, F32), jax.ShapeDtypeStruct(rank_all.shape, I32),
                   jax.ShapeDtypeStruct((1, N_EXPERTS), F32)],
        input_output_aliases={8: 0, 9: 1, 10: 2, 11: 3},
        compiler_params=_params("arbitrary"),
        name="route_sample",
    )(x1, o_bf, cnt_in, w_mo, g2, b2, w_r, b_r, x2_all, idx_all, gate_all, rank_all)


def _dispatch_kernel(pad_start_ref, n_pad_ref, n_used_ref, idx_hbm, x_ref, xrows_hbm,
                     idx_smem, zbuf, idx_sem, row_sem, pad_sem):
    i = pl.program_id(0)
    last = pl.num_programs(0) - 1
    tm = x_ref.shape[0] // ROW_TILE
    blk = zbuf.shape[0]
    n_blk = xrows_hbm.shape[0] // blk

    def idx_copy(row, s):
        return pltpu.make_async_copy(idx_hbm.at[row], idx_smem.at[s], idx_sem.at[s])

    def pad_copy(e, r):
        dst = pl.multiple_of((pad_start_ref[e] + r) * ROW_TILE, ROW_TILE)
        return pltpu.make_async_copy(zbuf.at[pl.ds(0, ROW_TILE)], xrows_hbm.at[pl.ds(dst, ROW_TILE)], pad_sem)

    def tail_copy(j):
        return pltpu.make_async_copy(zbuf, xrows_hbm.at[pl.ds(pl.multiple_of(j * blk, blk), blk)], pad_sem)

    @pl.when(i == 0)
    def _():
        idx_copy(0, 0).start()
        zbuf[...] = jnp.zeros(zbuf.shape, F32)
        for e in range(N_EXPERTS):
            def start(r, c, e=e):
                pad_copy(e, r).start()
                return c

            lax.fori_loop(0, n_pad_ref[e], start, 0)

        def start_tail(j, c):
            tail_copy(j).start()
            return c

        lax.fori_loop(n_used_ref[0], n_blk, start_tail, 0)
        for e in range(N_EXPERTS):
            def wait(r, c, e=e):
                pad_copy(e, r).wait()
                return c

            lax.fori_loop(0, n_pad_ref[e], wait, 0)

        def wait_tail(j, c):
            tail_copy(j).wait()
            return c

        lax.fori_loop(n_used_ref[0], n_blk, wait_tail, 0)

    def step(slot):
        idx_copy(i, slot).wait()

        @pl.when(i < last)
        def _():
            idx_copy(i + 1, 1 - slot).start()

        for t in range(tm):
            for k in range(TOP_K):
                dst = pl.multiple_of(idx_smem[slot, TOP_K * t + k], ROW_TILE)
                pltpu.make_async_copy(x_ref.at[pl.ds(ROW_TILE * t, ROW_TILE)], xrows_hbm.at[pl.ds(dst, ROW_TILE)],
                                      row_sem).start(priority=k % 2)
        for k in range(TOP_K):
            pltpu.make_async_copy(x_ref, x_ref, row_sem).wait()

    for parity in range(2):
        pl.when(i % 2 == parity)(functools.partial(step, parity))


def _dispatch(pad_start, n_pad, n_used, idx_rows, x_all, n_rows):
    tm = DISPATCH_BLOCK
    n_steps = idx_rows.shape[0]
    assert x_all.shape[0] == n_steps * tm * ROW_TILE and idx_rows.shape[1] == TOP_K * tm
    assert n_rows % EXPERT_ROWS == 0
    grid_spec = pltpu.PrefetchScalarGridSpec(
        num_scalar_prefetch=3,
        grid=(n_steps,),
        in_specs=[pl.BlockSpec(memory_space=pl.ANY),
                  pl.BlockSpec((tm * ROW_TILE, LANES), lambda i, ps, npd, nu: (i, 0))],
        out_specs=pl.BlockSpec(memory_space=pl.ANY),
        scratch_shapes=[pltpu.SMEM((2, TOP_K * tm), I32), pltpu.VMEM((EXPERT_ROWS * ROW_TILE, LANES), F32),
                        pltpu.SemaphoreType.DMA((2,)), pltpu.SemaphoreType.DMA, pltpu.SemaphoreType.DMA],
    )
    return pl.pallas_call(
        _dispatch_kernel,
        grid_spec=grid_spec,
        out_shape=jax.ShapeDtypeStruct((n_rows * ROW_TILE, LANES), F32),
        compiler_params=_params("arbitrary"),
        name="moe_dispatch",
    )(pad_start, n_pad, n_used, idx_rows, x_all)


def _moe_kernel(blk_exp_ref, blk_row_ref, idx_hbm, x_ref, wgu_f32_ref, bgu_ref, wd_f32_ref, bd_ref, y_hbm,
                idx_smem, ybuf, wgu_ref, wd_ref, idx_sem, y_sem):
    del blk_row_ref
    i = pl.program_id(0)
    last = pl.num_programs(0) - 1
    tr = x_ref.shape[0] // ROW_TILE

    @pl.when(jnp.logical_or(i == 0, blk_exp_ref[i] != blk_exp_ref[jnp.maximum(i - 1, 0)]))
    def _():
        wgu_ref[...] = wgu_f32_ref[...].astype(BF16)
        wd_ref[...] = wd_f32_ref[...].astype(BF16)

    n_chunks = D_EXPERT // EXPERT_CHUNK

    def idx_copy(row, s):
        return pltpu.make_async_copy(idx_hbm.at[row], idx_smem.at[s], idx_sem.at[s])

    def scatter(s_idx, ys, rows):
        for r in rows:
            dst = pl.multiple_of(idx_smem[s_idx, r], ROW_TILE)
            pltpu.make_async_copy(ybuf.at[ys, pl.ds(ROW_TILE * r, ROW_TILE)], y_hbm.at[pl.ds(dst, ROW_TILE)],
                                  y_sem.at[ys]).start(priority=r % 2)

    def wait_rows(s):
        pltpu.make_async_copy(ybuf.at[s], ybuf.at[s], y_sem.at[s]).wait()

    @pl.when(i == 0)
    def _():
        idx_copy(0, 0).start()
        ybuf[1] = jnp.zeros(ybuf.shape[1:], F32)

    def step(slot):
        nxt = 1 - slot
        idx_copy(i, slot).wait()
        idx_copy(i + 1, nxt).start()
        xb = _load_row_tiles(x_ref, (), tr).astype(BF16)
        acc = jnp.zeros((tr, D_MODEL), F32) + bd_ref[...]

        groups = 3 * n_chunks
        per_group = -(-tr // groups)

        def issue(g):
            scatter(slot, nxt, range(g * per_group, min((g + 1) * per_group, tr)))

        for c in range(n_chunks):
            cg = slice(c * EXPERT_CHUNK, (c + 1) * EXPERT_CHUNK)
            cu = slice(D_EXPERT + c * EXPERT_CHUNK, D_EXPERT + (c + 1) * EXPERT_CHUNK)
            gate = jnp.minimum(_dot(xb, wgu_ref[:, cg]) + bgu_ref[:, cg], SWIGLU_LIMIT)
            issue(3 * c)
            up = jnp.clip(_dot(xb, wgu_ref[:, cu]) + bgu_ref[:, cu], -SWIGLU_LIMIT, SWIGLU_LIMIT)
            issue(3 * c + 1)
            act = gate * (1.0 / (1.0 + jnp.exp(-SWIGLU_ALPHA * gate))) * (up + 1.0)
            acc = acc + _dot(act.astype(BF16), wd_ref[cg, :])
            issue(3 * c + 2)

        @pl.when(i >= 1)
        def _():
            wait_rows(slot)

        _store_row_tiles(ybuf, (slot,), acc)

        @pl.when(i == last)
        def _():
            idx_copy(i + 1, nxt).wait()
            wait_rows(nxt)

    for parity in range(2):
        pl.when(i % 2 == parity)(functools.partial(step, parity))


def _moe(blk_exp, blk_row, idx_rows, x_rows, w_gu, b_gu, w_d, b_d, n_assign):
    tr = EXPERT_ROWS
    n_steps = blk_exp.shape[0]
    assert n_steps >= 2 and idx_rows.shape == (n_steps + 1, tr)
    anyspec = pl.BlockSpec(memory_space=pl.ANY)
    emap3 = lambda i, be, br: (be[i], 0, 0)
    grid_spec = pltpu.PrefetchScalarGridSpec(
        num_scalar_prefetch=2,
        grid=(n_steps,),
        in_specs=[
            anyspec, pl.BlockSpec((tr * ROW_TILE, LANES), lambda i, be, br: (br[i], 0)),
            pl.BlockSpec((None, D_MODEL, 2 * D_EXPERT), emap3), pl.BlockSpec((None, 1, 2 * D_EXPERT), emap3),
            pl.BlockSpec((None, D_EXPERT, D_MODEL), emap3), pl.BlockSpec((None, 1, D_MODEL), emap3),
        ],
        out_specs=anyspec,
        scratch_shapes=[
            pltpu.SMEM((2, tr), I32), pltpu.VMEM((2, tr * ROW_TILE, LANES), F32),
            pltpu.VMEM((D_MODEL, 2 * D_EXPERT), BF16), pltpu.VMEM((D_EXPERT, D_MODEL), BF16),
            pltpu.SemaphoreType.DMA((2,)), pltpu.SemaphoreType.DMA((2,)),
        ],
    )
    return pl.pallas_call(
        _moe_kernel,
        grid_spec=grid_spec,
        out_shape=jax.ShapeDtypeStruct(((n_assign + 2 * tr) * ROW_TILE, LANES), F32),
        compiler_params=_params("arbitrary"),
        name="moe_experts",
    )(blk_exp, blk_row, idx_rows, x_rows, w_gu, b_gu, w_d, b_d)


def _combine_kernel(x2_ref, gate_ref, g3_ref, b3_ref, *rest):
    y_refs, out_ref = rest[:TOP_K], rest[TOP_K]
    gates = gate_ref[...]
    tm = out_ref.shape[0]
    moe = jnp.zeros(out_ref.shape, F32)
    for k in range(TOP_K):
        moe = moe + gates[:, k:k + 1] * _load_row_tiles(y_refs[k], (), tm)
    out_ref[...] = _layer_norm(DEEPNORM_ALPHA * _load_row_tiles(x2_ref, (), tm) + moe, g3_ref[...], b3_ref[...])


def _combine(x2_all, y_tok, gates_all, g3, b3, row0, n_rows, tm):
    n_all = gates_all.shape[0]
    assert row0 % tm == 0 and n_rows % tm == 0 and n_all % tm == 0
    off = row0 // tm
    tok = lambda i: (off + i, 0)
    tiles = (tm * ROW_TILE, LANES)
    slot_specs = [pl.BlockSpec(tiles, functools.partial(lambda i, k: (k * (n_all // tm) + off + i, 0), k=k))
                  for k in range(TOP_K)]
    return pl.pallas_call(
        _combine_kernel,
        grid=(n_rows // tm,),
        in_specs=[pl.BlockSpec(tiles, tok), pl.BlockSpec((tm, TOP_K), tok), _full((1, D_MODEL)),
                  _full((1, D_MODEL))] + slot_specs,
        out_specs=pl.BlockSpec((tm, D_MODEL), lambda i: (i, 0)),
        out_shape=jax.ShapeDtypeStruct((n_rows, D_MODEL), F32),
        compiler_params=_params("arbitrary"),
        name="combine_ln3",
    )(x2_all, gates_all, g3, b3, *([y_tok] * TOP_K))


def _rope_tables(pos):
    inv_freq = ROPE_THETA ** (-jnp.arange(ROPE_HALF, dtype=F32) / ROPE_HALF)
    ang = pos.astype(F32)[:, None] * inv_freq[None, :]
    c, s, z = jnp.cos(ang), jnp.sin(ang), jnp.zeros_like(ang)
    return jnp.concatenate([c, z, c, z], axis=1), jnp.concatenate([-s, z, s, z], axis=1)


def _pad_rope_cols(w):
    z = jnp.zeros(w.shape[:-1] + (ROPE_HALF,), w.dtype)
    return jnp.concatenate([w[..., :ROPE_HALF], z, w[..., ROPE_HALF:], z], axis=-1)


def _layer_weights(l, w_in, w_pool, pool_scale, q_norm_g, w_uq, kv_norm_g, w_uk, w_uv, w_out, ln1_g, ln1_b,
                   w_mq, w_mo, ln2_g, ln2_b, w_router, b_router):
    o3 = POOL_WIDTH + Q_RANK + KV_RANK
    w_in_p = jnp.concatenate([w_in[l][:, :o3], _pad_rope_cols(w_in[l][:, o3:])], axis=1).astype(BF16)
    uq = w_uq[l].reshape(Q_RANK, MLA_HEADS, QK_NOPE_DIM + QK_ROPE_DIM)
    uq_p = jnp.concatenate([uq[..., :QK_NOPE_DIM], _pad_rope_cols(uq[..., QK_NOPE_DIM:])], axis=-1)
    uq_p = uq_p.reshape(Q_RANK, MLA_HEADS * QK_PAD).astype(BF16)
    w_ukt = jnp.transpose(w_uk[l], (1, 2, 0)).astype(BF16)
    w_uv_h = jnp.transpose(w_uv[l], (1, 0, 2)).astype(BF16)
    row = lambda v: v[l].reshape(1, -1)
    w_rt = w_router[l].T
    w_rt_hi = w_rt.astype(BF16)
    w_r_split = jnp.stack([w_rt_hi, (w_rt - w_rt_hi.astype(F32)).astype(BF16)])
    proj =[w_in_p, row(q_norm_g), uq_p, row(kv_norm_g), w_ukt, w_pool[l].astype(BF16), row(pool_scale)]
    post = [w_out[l][:POOL_WIDTH].astype(BF16), w_out[l][POOL_WIDTH:].astype(BF16), row(ln1_g), row(ln1_b),
            w_mq[l].astype(BF16), w_mo[l].astype(BF16), row(ln2_g), row(ln2_b), w_r_split, row(b_router)]
    return proj, w_uv_h, post


def _routing_tables(idx_all, rank_all, counts, n_blocks):
    tr = EXPERT_ROWS
    cnt = counts.reshape(-1).astype(I32)
    padded = ((cnt + tr - 1) // tr) * tr
    pend = jnp.cumsum(padded)
    pstart = pend - padded
    dest2d = pstart[idx_all] + rank_all
    dest = dest2d.reshape(-1)
    n_all = idx_all.shape[0]
    m = dest.shape[0]
    n_rows = n_blocks * tr
    disp_rows = (dest2d * ROW_TILE).reshape(n_all // DISPATCH_BLOCK, DISPATCH_BLOCK * TOP_K)
    assign = (jnp.arange(TOP_K, dtype=I32)[None, :] * n_all + jnp.arange(n_all, dtype=I32)[:, None]).reshape(-1)
    _, packed = lax.sort_key_val(dest, assign)
    start = jnp.cumsum(cnt) - cnt
    shift = pstart - start
    none = jnp.full((n_rows,), -1, I32)
    doubled = jnp.concatenate([none, packed, none[:n_rows - m]])
    rows = jnp.arange(n_rows, dtype=I32)
    row_src = none
    for e in range(N_EXPERTS):
        moved = lax.dynamic_slice(doubled, (n_rows - shift[e],), (n_rows,))
        row_src = jnp.where((rows >= pstart[e]) & (rows < pstart[e] + cnt[e]), moved, row_src)
    spare = m + jnp.arange(n_rows, dtype=I32) % (2 * tr)
    dst_rows = (jnp.where(row_src >= 0, row_src, spare) * ROW_TILE).reshape(n_blocks, tr)
    spare_row = spare[:tr][None] * ROW_TILE
    idx_rows = jnp.concatenate([spare_row, dst_rows, spare_row], axis=0)
    n_used = pend[-1] // tr
    blk_row = jnp.minimum(jnp.arange(n_blocks + 1, dtype=I32), n_used - 1)
    blk_exp = jnp.sum((pend[None, :] <= (blk_row * tr)[:, None]).astype(I32), axis=1)
    return (jnp.clip(blk_exp, 0, N_EXPERTS - 1), blk_row, idx_rows, disp_rows, pstart + cnt, padded - cnt,
            n_used.astype(I32).reshape(1))


def kernel(x_prompt, x_sample, mem_prompt, cache_kv_latent, cache_k_rope, cache_mem_k, cache_mem_v, state_pool,
           page_table, w_in, w_pool, pool_scale, q_norm_g, w_uq, kv_norm_g, w_uk, w_uv, w_out, ln1_g, ln1_b,
           w_mq, w_mk, w_mv, w_mo, ln2_g, ln2_b, w_router, b_router, w_gate_up, b_gate_up, w_down, b_down,
           ln3_g, ln3_b):
    n_bp, seq, _ = x_prompt.shape
    n_bs, n_new, _ = x_sample.shape
    n_mem = mem_prompt.shape[1]
    page = cache_kv_latent.shape[2]
    past_len = page_table.shape[1] * page
    n_p = n_bp * seq
    n_s = n_bs * n_new
    n_all = n_p + n_s
    n_assign = n_all * TOP_K
    assert w_in.shape[0] == DEPTH == 1
    assert seq % TOKEN_BLOCK == 0 and seq % ATTN_K_BLOCK == 0 and seq % ATTN_Q_BLOCK == 0 and seq >= POOL_HALO
    assert n_p % n_s == 0 and (n_s % TOKEN_BLOCK == 0 or TOKEN_BLOCK % n_s == 0)
    l = 0

    proj_w, w_uv_h, post_w = _layer_weights(l, w_in, w_pool, pool_scale, q_norm_g, w_uq, kv_norm_g, w_uk, w_uv,
                                            w_out, ln1_g, ln1_b, w_mq, w_mo, ln2_g, ln2_b, w_router, b_router)

    xp = x_prompt.reshape(n_p, D_MODEL)
    cos_p, sin_p = _rope_tables(jnp.arange(seq))
    qcat, kcat, lat_p, kr_p, pooly_p, pstate_p = _proj_prompt(xp, proj_w, cos_p, sin_p, n_bp, seq)
    mlay_p = _attn_prompt(qcat, kcat, w_uv_h, n_bp, seq)
    mk, mv, mk_b, mv_b = _mem_kv(mem_prompt.reshape(n_bp * n_mem, D_MODEL), w_mk[l].astype(BF16),
                                 w_mv[l].astype(BF16), n_bp, n_mem)
    x2_all, idx_all, gate_all, rank_all, cnt_p = _post_prompt(xp, pooly_p, mlay_p, mk_b, mv_b, post_w,
                                                              n_bp, seq, n_mem, n_all)

    xs = jnp.transpose(x_sample, (1, 0, 2)).reshape(n_s, D_MODEL)
    state_t = jnp.transpose(state_pool[l], (1, 0, 2))
    cos_s, sin_s = _rope_tables(jnp.repeat(past_len + jnp.arange(n_new), n_bs))
    qcat_s, kcat_s, lat_s, kr_s, pooly_s, pstate_s = _proj_sample(xs, state_t, proj_w, cos_s, sin_s, past_len)
    q_b = jnp.transpose(qcat_s.reshape(MLA_HEADS, n_new, n_bs, QK_PAD), (2, 0, 1, 3)).reshape(
        n_bs, MLA_HEADS * n_new, QK_PAD)
    knew_b = jnp.transpose(kcat_s.reshape(n_new, n_bs, QK_PAD), (1, 0, 2))
    knew_b = jnp.pad(knew_b, ((0, 0), (0, LANES - n_new), (0, 0)))
    mlay_b = _attn_sample(page_table, q_b, knew_b, w_uv_h, cache_kv_latent,
                          jnp.transpose(cache_k_rope, (0, 1, 3, 2)), l)
    mlay_s = jnp.transpose(mlay_b, (1, 0, 2)).reshape(n_s, POOL_WIDTH).astype(BF16)
    x1_s, qm_s = _mix_sample(xs, pooly_s, mlay_s, post_w)
    qm_b = jnp.transpose(qm_s.reshape(n_new, n_bs, MEM_HEADS, MEM_HEAD_DIM), (1, 2, 0, 3)).reshape(
        n_bs, MEM_HEADS * n_new, MEM_HEAD_DIM)
    o_b = _mem_attn_sample(qm_b, cache_mem_k[l].reshape(n_bs, n_mem * MEM_HEADS, MEM_HEAD_DIM),
                           cache_mem_v[l].reshape(n_bs, n_mem * MEM_HEADS, MEM_HEAD_DIM), n_new)
    o_s = jnp.transpose(o_b, (1, 0, 2)).reshape(n_s, MEM_WIDTH)
    x2_all, idx_all, gate_all, rank_all, counts = _route_sample(x1_s, o_s, cnt_p, post_w, x2_all, idx_all, gate_all,
                                                               rank_all, n_p)

    n_blocks = -(-n_assign // EXPERT_ROWS) + N_EXPERTS + 1
    blk_exp, blk_row, idx_rows, disp_rows, pad_start, n_pad, n_used = _routing_tables(idx_all, rank_all, counts,
                                                                                      n_blocks)
    x_rows = _dispatch(pad_start, n_pad, n_used, disp_rows, x2_all, n_blocks * EXPERT_ROWS)
    y_tok = _moe(blk_exp, blk_row, idx_rows, x_rows, w_gate_up[l], b_gate_up[l][:, None, :],
                 w_down[l], b_down[l][:, None, :], n_assign)
    g3, b3 = ln3_g[l].reshape(1, -1), ln3_b[l].reshape(1, -1)
    y_p = _combine(x2_all, y_tok, gate_all, g3, b3, 0, n_p, TOKEN_BLOCK)
    y_s = _combine(x2_all, y_tok, gate_all, g3, b3, n_p, n_s, min(TOKEN_BLOCK, n_s))

    def from_steps(a, width):
        return jnp.transpose(a.reshape(n_new, n_bs, width), (1, 0, 2))[None]

    return (
        y_p.reshape(n_bp, seq, D_MODEL),
        from_steps(y_s, D_MODEL)[0],
        lat_p.reshape(1, n_bp, seq, KV_RANK),
        kr_p.reshape(1, n_bp, seq, QK_ROPE_DIM),
        pstate_p[None, :, POOL_HALO - POOL_BUF:, :],
        mk.reshape(1, n_bp, n_mem, MEM_HEADS, MEM_HEAD_DIM),
        mv.reshape(1, n_bp, n_mem, MEM_HEADS, MEM_HEAD_DIM),
        from_steps(lat_s, KV_RANK),
        from_steps(kr_s, QK_ROPE_DIM),
        jnp.transpose(pstate_s, (1, 0, 2))[None],
    )
```

```python
import functools

import jax
import jax.numpy as jnp
from jax import lax
from jax.experimental import pallas as pl
from jax.experimental.pallas import tpu as pltpu

F32 = jnp.float32
BF16 = jnp.bfloat16
I32 = jnp.int32

D_MODEL = 1024
POOL_WIDTH = 512
POOL_WINDOWS = (2, 4, 8, 16)
POOL_GROUP_DIM = 128
POOL_BUF = 15
POOL_HALO = 16
MLA_HEADS = 4
QK_NOPE_DIM = 128
QK_ROPE_DIM = 64
ROPE_HALF = 32
V_HEAD_DIM = 128
Q_RANK = 256
KV_RANK = 128
ROPE_THETA = 10000.0
QK_PAD = 256
ONES_LANE = QK_PAD - 1
MEM_HEADS = 4
MEM_HEAD_DIM = 128
MEM_WIDTH = 512
N_EXPERTS = 32
TOP_K = 4
D_EXPERT = 1024
SWIGLU_LIMIT = 7.0
SWIGLU_ALPHA = 1.702
LN_EPS = 1e-5
RMS_EPS = 1e-6
DEPTH = 1
DEEPNORM_ALPHA = (2.0 * DEPTH) ** 0.25
ATTN_SCALE = (QK_NOPE_DIM + QK_ROPE_DIM) ** -0.5
LOG2_E = 1.4426950408889634
MEM_SCALE = MEM_HEAD_DIM ** -0.5

LANES = 128
VMEM_LIMIT = 48 * 1024 * 1024

TOKEN_BLOCK = 512
ATTN_Q_BLOCK = 256
ATTN_K_BLOCK = 512
ATTN_GROUPS = 2
PAGES_PER_STEP = 32
MEM_BATCH_BLOCK = 8
DISPATCH_BLOCK = 256
EXPERT_ROWS = 256
EXPERT_CHUNK = 512


def _dot(a, b):
    return jnp.dot(a, b, preferred_element_type=F32)


def _dot_nt(a, b):
    return lax.dot_general(a, b, (((1,), (1,)), ((), ())), preferred_element_type=F32)


def _rms(x, g):
    return x * lax.rsqrt(jnp.mean(x * x, axis=-1, keepdims=True) + RMS_EPS) * g


def _layer_norm(x, g, b):
    mu = jnp.mean(x, axis=-1, keepdims=True)
    xc = x - mu
    var = jnp.mean(xc * xc, axis=-1, keepdims=True)
    return xc * lax.rsqrt(var + LN_EPS) * g + b


ROW_TILE = D_MODEL // LANES


def _store_row_tiles(ref, lead, val):
    m = val.shape[0]
    for j in range(ROW_TILE):
        ref[(*lead, pl.ds(j, m, stride=ROW_TILE), slice(None))] = val[:, LANES * j:LANES * (j + 1)]


def _load_row_tiles(ref, lead, m):
    return jnp.concatenate([ref[(*lead, pl.ds(j, m, stride=ROW_TILE), slice(None))] for j in range(ROW_TILE)], axis=1)


def _params(*semantics):
    return pltpu.CompilerParams(dimension_semantics=semantics, vmem_limit_bytes=VMEM_LIMIT)


def _full(shape):
    n = len(shape)
    return pl.BlockSpec(shape, lambda *_: (0,) * n)


def _project(x_bf, w_in_ref, qg_ref, w_uq_ref, kvg_ref, w_ukt_ref, cos, sin, q_scale):
    h = _dot(x_bf, w_in_ref[...])
    u = h[:, :POOL_WIDTH]
    qn = _rms(h[:, POOL_WIDTH:POOL_WIDTH + Q_RANK], qg_ref[...])
    q = _dot(qn.astype(BF16), w_uq_ref[...])
    q_lat, q_rope = [], []
    for hd in range(MLA_HEADS):
        nope = q[:, QK_PAD * hd:QK_PAD * hd + QK_NOPE_DIM]
        rp = q[:, QK_PAD * hd + QK_NOPE_DIM:QK_PAD * (hd + 1)]
        q_lat.append(_dot(nope.astype(BF16), w_ukt_ref[hd]) * q_scale)
        q_rope.append((rp * cos + pltpu.roll(rp, 64, 1) * sin) * q_scale)
    o2 = POOL_WIDTH + Q_RANK
    lat = _rms(h[:, o2:o2 + KV_RANK], kvg_ref[...])
    kp = h[:, o2 + KV_RANK:]
    k_rope = kp * cos + pltpu.roll(kp, 64, 1) * sin
    return u, q_lat, q_rope, lat, k_rope


def _compact_rope(r):
    lane = lax.broadcasted_iota(I32, r.shape, 1)
    moved = pltpu.roll(r, 96, 1)
    return jnp.where(lane < ROPE_HALF, r, jnp.where(lane < QK_ROPE_DIM, moved, 0.0))


def _with_ones_lane(r):
    lane = lax.broadcasted_iota(I32, r.shape, 1)
    return jnp.where(lane == ONES_LANE - KV_RANK, 1.0, r)


def _pool_group_out(diff, g, w_pool_ref, pscale_ref):
    cols = slice(POOL_GROUP_DIM * g, POOL_GROUP_DIM * (g + 1))
    return _dot(diff.astype(BF16), w_pool_ref[g]) * pscale_ref[:, cols]


def _proj_prompt_kernel(x_ref, w_in_ref, qg_ref, w_uq_ref, kvg_ref, w_ukt_ref, w_pool_ref, pscale_ref,
                        cos_ref, sin_ref,
                        qcat_ref, kcat_ref, lat_ref, kr_ref, pooly_ref, pstate_ref, ubuf):
    j = pl.program_id(1)
    tm = x_ref.shape[0]

    @pl.when(j == 0)
    def _():
        ubuf[0:POOL_HALO, :] = jnp.zeros((POOL_HALO, POOL_WIDTH), F32)

    u, q_lat, q_rope, lat, k_rope = _project(
        x_ref[...].astype(BF16), w_in_ref, qg_ref, w_uq_ref, kvg_ref, w_ukt_ref, cos_ref[...], sin_ref[...],
        ATTN_SCALE * LOG2_E)
    for hd in range(MLA_HEADS):
        qcat_ref[hd, :, 0:KV_RANK] = q_lat[hd].astype(BF16)
        qcat_ref[hd, :, KV_RANK:QK_PAD] = q_rope[hd].astype(BF16)
    lat_ref[...] = lat
    kcat_ref[:, 0:KV_RANK] = lat.astype(BF16)
    kcat_ref[:, KV_RANK:QK_PAD] = _with_ones_lane(k_rope).astype(BF16)
    kr_ref[...] = _compact_rope(k_rope)[:, :QK_ROPE_DIM]

    ubuf[POOL_HALO:POOL_HALO + tm, :] = u
    pos = j * tm + lax.broadcasted_iota(I32, (tm, 1), 0)
    for g, w in enumerate(POOL_WINDOWS):
        cols = slice(POOL_GROUP_DIM * g, POOL_GROUP_DIM * (g + 1))
        ug = u[:, cols]
        ssum = ug
        for k in range(1, w):
            ssum = ssum + ubuf[POOL_HALO - k:POOL_HALO - k + tm, cols]
        count = jnp.minimum(w, pos + 1).astype(F32)
        pooly_ref[:, cols] = _pool_group_out(ssum / count - ug, g, w_pool_ref, pscale_ref).astype(BF16)
    tail = ubuf[tm:tm + POOL_HALO, :]
    ubuf[0:POOL_HALO, :] = tail

    @pl.when(j == pl.num_programs(1) - 1)
    def _():
        pstate_ref[...] = tail


def _proj_sample_kernel(x_ref, state_ref, w_in_ref, qg_ref, w_uq_ref, kvg_ref, w_ukt_ref, w_pool_ref, pscale_ref,
                        cos_ref, sin_ref,
                        qcat_ref, kcat_ref, lat_ref, kr_ref, pooly_ref, pstate_ref, *, past_len):
    n_b = state_ref.shape[1]
    n_t = x_ref.shape[0] // n_b
    u, q_lat, q_rope, lat, k_rope = _project(
        x_ref[...].astype(BF16), w_in_ref, qg_ref, w_uq_ref, kvg_ref, w_ukt_ref, cos_ref[...], sin_ref[...],
        ATTN_SCALE)
    for hd in range(MLA_HEADS):
        qcat_ref[hd, :, 0:KV_RANK] = q_lat[hd].astype(BF16)
        qcat_ref[hd, :, KV_RANK:QK_PAD] = _compact_rope(q_rope[hd]).astype(BF16)
    lat_ref[...] = lat
    kc = _compact_rope(k_rope)
    kcat_ref[:, 0:KV_RANK] = lat.astype(BF16)
    kcat_ref[:, KV_RANK:QK_PAD] = _with_ones_lane(kc).astype(BF16)
    kr_ref[...] = kc[:, :QK_ROPE_DIM]

    def ext(jj):
        if jj < POOL_BUF:
            return state_ref[jj]
        return u[(jj - POOL_BUF) * n_b:(jj - POOL_BUF + 1) * n_b, :]

    for t in range(n_t):
        ut = ext(POOL_BUF + t)
        for g, w in enumerate(POOL_WINDOWS):
            cols = slice(POOL_GROUP_DIM * g, POOL_GROUP_DIM * (g + 1))
            ssum = ut[:, cols]
            for k in range(1, w):
                ssum = ssum + ext(POOL_BUF + t - k)[:, cols]
            count = float(min(w, past_len + t + 1))
            y = _pool_group_out(ssum / count - ut[:, cols], g, w_pool_ref, pscale_ref)
            pooly_ref[t * n_b:(t + 1) * n_b, cols] = y.astype(BF16)
    for jj in range(POOL_BUF):
        pstate_ref[jj] = ext(n_t + jj)


def _proj_weight_specs():
    return [
        _full((D_MODEL, D_MODEL)),
        _full((1, Q_RANK)),
        _full((Q_RANK, MLA_HEADS * QK_PAD)),
        _full((1, KV_RANK)),
        _full((MLA_HEADS, QK_NOPE_DIM, KV_RANK)),
        _full((len(POOL_WINDOWS), POOL_GROUP_DIM, POOL_GROUP_DIM)),
        _full((1, POOL_WIDTH)),
    ]


def _proj_prompt(x2d, weights, cos_t, sin_t, n_batch, seq):
    tm = TOKEN_BLOCK
    n = n_batch * seq
    nj = seq // tm
    tok = lambda b, j: (b * nj + j, 0)
    return pl.pallas_call(
        _proj_prompt_kernel,
        grid=(n_batch, nj),
        in_specs=[pl.BlockSpec((tm, D_MODEL), tok)] + _proj_weight_specs() + [
            pl.BlockSpec((tm, LANES), lambda b, j: (j, 0)),
            pl.BlockSpec((tm, LANES), lambda b, j: (j, 0)),
        ],
        out_specs=[
            pl.BlockSpec((MLA_HEADS, tm, QK_PAD), lambda b, j: (0, b * nj + j, 0)),
            pl.BlockSpec((tm, QK_PAD), tok),
            pl.BlockSpec((tm, KV_RANK), tok),
            pl.BlockSpec((tm, QK_ROPE_DIM), tok),
            pl.BlockSpec((tm, POOL_WIDTH), tok),
            pl.BlockSpec((None, POOL_HALO, POOL_WIDTH), lambda b, j: (b, 0, 0)),
        ],
        out_shape=[
            jax.ShapeDtypeStruct((MLA_HEADS, n, QK_PAD), BF16),
            jax.ShapeDtypeStruct((n, QK_PAD), BF16),
            jax.ShapeDtypeStruct((n, KV_RANK), F32),
            jax.ShapeDtypeStruct((n, QK_ROPE_DIM), F32),
            jax.ShapeDtypeStruct((n, POOL_WIDTH), BF16),
            jax.ShapeDtypeStruct((n_batch, POOL_HALO, POOL_WIDTH), F32),
        ],
        scratch_shapes=[pltpu.VMEM((POOL_HALO + tm, POOL_WIDTH), F32)],
        compiler_params=_params("arbitrary", "arbitrary"),
        name="proj_prompt",
    )(x2d, *weights, cos_t, sin_t)


def _proj_sample(x2d, state_t, weights, cos_t, sin_t, past_len):
    n = x2d.shape[0]
    n_b = state_t.shape[1]
    return pl.pallas_call(
        functools.partial(_proj_sample_kernel, past_len=past_len),
        grid=(1,),
        in_specs=[_full((n, D_MODEL)), _full((POOL_BUF, n_b, POOL_WIDTH))] + _proj_weight_specs() + [
            _full((n, LANES)), _full((n, LANES))],
        out_specs=[
            _full((MLA_HEADS, n, QK_PAD)), _full((n, QK_PAD)), _full((n, KV_RANK)), _full((n, QK_ROPE_DIM)),
            _full((n, POOL_WIDTH)), _full((POOL_BUF, n_b, POOL_WIDTH)),
        ],
        out_shape=[
            jax.ShapeDtypeStruct((MLA_HEADS, n, QK_PAD), BF16),
            jax.ShapeDtypeStruct((n, QK_PAD), BF16),
            jax.ShapeDtypeStruct((n, KV_RANK), F32),
            jax.ShapeDtypeStruct((n, QK_ROPE_DIM), F32),
            jax.ShapeDtypeStruct((n, POOL_WIDTH), BF16),
            jax.ShapeDtypeStruct((POOL_BUF, n_b, POOL_WIDTH), F32),
        ],
        compiler_params=_params("arbitrary"),
        name="proj_sample",
    )(x2d, state_t, *weights, cos_t, sin_t)


def _attn_block(q, k, mask, m_ref, acc_ref):
    s = _dot_nt(q, k)
    if mask is not None:
        s = jnp.where(mask, s, -jnp.inf)
    tiles = [s[:, c * LANES:(c + 1) * LANES] for c in range(s.shape[1] // LANES)]
    m_old = m_ref[...]
    m_new = jnp.maximum(m_old, jnp.max(functools.reduce(jnp.maximum, tiles), axis=1, keepdims=True))
    alpha = jnp.exp2(m_old - m_new)
    p = jnp.concatenate([jnp.exp2(t - m_new) for t in tiles], axis=1).astype(BF16)
    acc_ref[...] = jnp.concatenate([alpha, alpha], axis=1) * acc_ref[...] + _dot(p, k)
    m_ref[...] = m_new


def _attn_output(acc):
    return acc[:, :KV_RANK] / acc[:, ONES_LANE:ONES_LANE + 1]


def _attn_prompt_kernel(q_ref, k_ref, w_uv_ref, out_ref, m_ref, acc_ref):
    i = pl.program_id(1)
    tq = q_ref.shape[1]
    tk = ATTN_K_BLOCK
    hpg = MLA_HEADS // ATTN_GROUPS
    rows = hpg * tq
    m_ref[...] = jnp.full(m_ref.shape, -jnp.inf, F32)
    acc_ref[...] = jnp.zeros(acc_ref.shape, F32)

    def block(jk, masked):
        k = k_ref[pl.ds(pl.multiple_of(jk * tk, tk), tk), :]
        mask = None
        if masked:
            q_pos = i * tq + lax.broadcasted_iota(I32, (rows, tk), 0) % tq
            k_pos = jk * tk + lax.broadcasted_iota(I32, (rows, tk), 1)
            mask = k_pos <= q_pos
        for g in range(ATTN_GROUPS):
            q = q_ref[hpg * g:hpg * (g + 1)].reshape(rows, QK_PAD)
            _attn_block(q, k, mask, m_ref.at[g], acc_ref.at[g])

    n_full = (i * tq + 1) // tk
    n_kv = (i * tq + tq - 1) // tk + 1

    def full_pair(jp, c):
        block(2 * jp, False)
        block(2 * jp + 1, False)
        return c

    def diag_body(jk, c):
        block(jk, True)
        return c

    lax.fori_loop(0, n_full // 2, full_pair, 0)

    @pl.when(n_full % 2 == 1)
    def _():
        block(n_full - 1, False)

    lax.fori_loop(n_full, n_kv, diag_body, 0)

    for g in range(ATTN_GROUPS):
        o = _attn_output(acc_ref[g])
        for hl in range(hpg):
            hd = hpg * g + hl
            oh = o[hl * tq:(hl + 1) * tq, :].astype(BF16)
            out_ref[:, V_HEAD_DIM * hd:V_HEAD_DIM * (hd + 1)] = _dot(oh, w_uv_ref[hd]).astype(BF16)


def _attn_prompt(qcat, kcat, w_uv_h, n_batch, seq):
    tq = ATTN_Q_BLOCK
    nq = seq // tq
    n = n_batch * seq
    rows = (MLA_HEADS // ATTN_GROUPS) * tq
    return pl.pallas_call(
        _attn_prompt_kernel,
        grid=(n_batch, nq),
        in_specs=[
            pl.BlockSpec((MLA_HEADS, tq, QK_PAD), lambda b, i: (0, b * nq + i, 0)),
            pl.BlockSpec((seq, QK_PAD), lambda b, i: (b, 0)),
            _full((MLA_HEADS, KV_RANK, V_HEAD_DIM)),
        ],
        out_specs=pl.BlockSpec((tq, MLA_HEADS * V_HEAD_DIM), lambda b, i: (b * nq + i, 0)),
        out_shape=jax.ShapeDtypeStruct((n, MLA_HEADS * V_HEAD_DIM), BF16),
        scratch_shapes=[pltpu.VMEM((ATTN_GROUPS, rows, LANES), F32), pltpu.VMEM((ATTN_GROUPS, rows, QK_PAD), F32)],
        compiler_params=_params("arbitrary", "arbitrary"),
        name="attn_prompt",
    )(qcat, kcat, w_uv_h)


def _attn_sample_kernel(pt_ref, q_ref, knew_ref, w_uv_ref, lat_hbm, rope_hbm, out_ref,
                        lat_buf, rope_buf, m_ref, l_ref, acc_ref, res_ref, page_sem, *, n_pages_step, n_new, layer):
    s_idx = pl.program_id(1)
    n_steps = pl.num_programs(1)
    g = pl.program_id(0) * n_steps + s_idx
    slot = g % 2

    def fetch(step, dst):
        for p in range(n_pages_step):
            page = pt_ref[step * n_pages_step + p]
            pltpu.make_async_copy(lat_hbm.at[layer, page], lat_buf.at[dst, p], page_sem.at[dst]).start()
            pltpu.make_async_copy(rope_hbm.at[layer, page], rope_buf.at[dst, p], page_sem.at[dst]).start()

    @pl.when(g == 0)
    def _():
        fetch(0, 0)

    @pl.when(g + 1 < pl.num_programs(0) * n_steps)
    def _():
        fetch(g + 1, 1 - slot)

    pltpu.make_async_copy(lat_buf.at[slot], lat_buf.at[slot], page_sem.at[slot]).wait()
    pltpu.make_async_copy(rope_buf.at[slot], rope_buf.at[slot], page_sem.at[slot]).wait()

    @pl.when(s_idx == 0)
    def _():
        m_ref[...] = jnp.full(m_ref.shape, -jnp.inf, F32)
        l_ref[...] = jnp.zeros(l_ref.shape, F32)
        acc_ref[...] = jnp.zeros(acc_ref.shape, F32)

    def update(s_tiles, values):
        m_old = m_ref[...]
        m_new = jnp.maximum(m_old, jnp.max(functools.reduce(jnp.maximum, s_tiles), axis=1, keepdims=True))
        alpha = jnp.exp(m_old - m_new)
        p_tiles = [jnp.exp(t - m_new) for t in s_tiles]
        row_sum = jnp.sum(functools.reduce(jnp.add, p_tiles), axis=1, keepdims=True)
        pv = functools.reduce(jnp.add, [_dot(p.astype(BF16), v) for p, v in zip(p_tiles, values)])
        l_ref[...] = alpha * l_ref[...] + row_sum
        acc_ref[...] = alpha * acc_ref[...] + pv
        m_ref[...] = m_new

    q = q_ref[...]
    q_lat = q[:, :KV_RANK]
    q_rope = q[:, KV_RANK:KV_RANK + QK_ROPE_DIM]
    lats = [lat_buf[slot, p].astype(BF16) for p in range(n_pages_step)]
    update([_dot_nt(q_lat, lats[p]) + _dot(q_rope, rope_buf[slot, p].astype(BF16)) for p in range(n_pages_step)],
           lats)

    @pl.when(s_idx == pl.num_programs(1) - 1)
    def _():
        kn = knew_ref[...]
        q_t = lax.broadcasted_iota(I32, (q.shape[0], kn.shape[0]), 0) % n_new
        k_t = lax.broadcasted_iota(I32, (q.shape[0], kn.shape[0]), 1)
        update([jnp.where(k_t <= q_t, _dot_nt(q, kn), -jnp.inf)], [kn[:, :KV_RANK]])
        o = (acc_ref[...] / l_ref[...]).astype(BF16)
        for hd in range(MLA_HEADS):
            res_ref[...] = _dot(o, w_uv_ref[hd])
            out_ref[:, V_HEAD_DIM * hd:V_HEAD_DIM * (hd + 1)] = res_ref[hd * n_new:(hd + 1) * n_new, :]


def _attn_sample(page_table, q_b, knew_b, w_uv_h, cache_lat, cache_rope, layer):
    n_b, n_pages = page_table.shape
    page = cache_lat.shape[2]
    pps = min(PAGES_PER_STEP, n_pages)
    assert n_pages % pps == 0
    n_steps = n_pages // pps
    rows = q_b.shape[1]
    n_new = rows // MLA_HEADS
    t_pad = knew_b.shape[1]

    anyspec = pl.BlockSpec(memory_space=pl.ANY)
    grid_spec = pltpu.PrefetchScalarGridSpec(
        num_scalar_prefetch=1,
        grid=(n_b, n_steps),
        in_specs=[
            pl.BlockSpec((None, rows, QK_PAD), lambda b, s, pt: (b, 0, 0)),
            pl.BlockSpec((None, t_pad, QK_PAD), lambda b, s, pt: (b, 0, 0)),
            pl.BlockSpec((MLA_HEADS, KV_RANK, V_HEAD_DIM), lambda b, s, pt: (0, 0, 0)),
            anyspec, anyspec,
        ],
        out_specs=pl.BlockSpec((None, n_new, MLA_HEADS * V_HEAD_DIM), lambda b, s, pt: (b, 0, 0)),
        scratch_shapes=[
            pltpu.VMEM((2, pps, page, KV_RANK), F32), pltpu.VMEM((2, pps, QK_ROPE_DIM, page), F32),
            pltpu.VMEM((rows, LANES), F32), pltpu.VMEM((rows, LANES), F32), pltpu.VMEM((rows, KV_RANK), F32),
            pltpu.VMEM((rows, V_HEAD_DIM), F32), pltpu.SemaphoreType.DMA((2,)),
        ],
    )
    return pl.pallas_call(
        functools.partial(_attn_sample_kernel, n_pages_step=pps, n_new=n_new, layer=layer),
        grid_spec=grid_spec,
        out_shape=jax.ShapeDtypeStruct((n_b, n_new, MLA_HEADS * V_HEAD_DIM), F32),
        compiler_params=_params("arbitrary", "arbitrary"),
        name="attn_sample",
    )(page_table.reshape(-1), q_b, knew_b, w_uv_h, cache_lat, cache_rope)


def _mem_kv_kernel(mem_ref, w_mk_ref, w_mv_ref, k_ref, v_ref, kb_ref, vb_ref):
    m = mem_ref[...].astype(BF16)
    k = _dot(m, w_mk_ref[...])
    v = _dot(m, w_mv_ref[...])
    k_ref[...] = k
    v_ref[...] = v
    kb_ref[...] = k.astype(BF16)
    vb_ref[...] = v.astype(BF16)


def _mem_kv(mem2d, w_mk, w_mv, n_batch, n_mem):
    blk = pl.BlockSpec((n_mem, MEM_WIDTH), lambda b: (b, 0))
    n = n_batch * n_mem
    return pl.pallas_call(
        _mem_kv_kernel,
        grid=(n_batch,),
        in_specs=[pl.BlockSpec((n_mem, D_MODEL), lambda b: (b, 0)), _full((D_MODEL, MEM_WIDTH)),
                  _full((D_MODEL, MEM_WIDTH))],
        out_specs=[blk, blk, blk, blk],
        out_shape=[jax.ShapeDtypeStruct((n, MEM_WIDTH), F32), jax.ShapeDtypeStruct((n, MEM_WIDTH), F32),
                   jax.ShapeDtypeStruct((n, MEM_WIDTH), BF16), jax.ShapeDtypeStruct((n, MEM_WIDTH), BF16)],
        compiler_params=_params("arbitrary"),
        name="mem_kv",
    )(mem2d, w_mk, w_mv)


def _mix_ln1(x, pool_y, mla_y, w_out_a_ref, w_out_b_ref, g_ref, b_ref):
    mix = _dot(pool_y, w_out_a_ref[...]) + _dot(mla_y, w_out_b_ref[...])
    return _layer_norm(DEEPNORM_ALPHA * x + mix, g_ref[...], b_ref[...])


def _route(x2, w_r_ref, b_r_ref, carry):
    m = x2.shape[0]
    x_hi = x2.astype(BF16)
    x_lo = (x2 - x_hi.astype(F32)).astype(BF16)
    logits_t = _dot_nt(w_r_ref[0], x_hi) + (_dot_nt(w_r_ref[0], x_lo) + _dot_nt(w_r_ref[1], x_hi))
    logits = logits_t.T + b_r_ref[...]
    e_idx = lax.broadcasted_iota(I32, (m, N_EXPERTS), 1).astype(F32)
    work = logits
    vals, picks = [], []
    for _ in range(TOP_K):
        mx = jnp.max(work, axis=1, keepdims=True)
        pick = jnp.min(jnp.where(work == mx, e_idx, float(N_EXPERTS)), axis=1, keepdims=True)
        vals.append(mx)
        picks.append(pick)
        work = jnp.where(e_idx == pick, -jnp.inf, work)
    hot = jnp.where(work == -jnp.inf, 1.0, 0.0)
    exps = [jnp.exp(v - vals[0]) for v in vals]
    denom = exps[0] + exps[1] + exps[2] + exps[3]
    r_i = lax.broadcasted_iota(I32, (m, m), 0)
    c_i = lax.broadcasted_iota(I32, (m, m), 1)
    strict_lower = jnp.where(c_i < r_i, 1.0, 0.0).astype(BF16)
    before = _dot(strict_lower, hot.astype(BF16)) + carry
    lane4 = lax.broadcasted_iota(I32, (m, TOP_K), 1)
    idx = jnp.zeros((m, TOP_K), I32)
    gates = jnp.zeros((m, TOP_K), F32)
    rank = jnp.zeros((m, TOP_K), I32)
    for k in range(TOP_K):
        rk = jnp.sum(jnp.where(e_idx == picks[k], before, 0.0), axis=1, keepdims=True).astype(I32)
        idx = jnp.where(lane4 == k, picks[k].astype(I32), idx)
        gates = jnp.where(lane4 == k, exps[k] / denom, gates)
        rank = jnp.where(lane4 == k, rk, rank)
    return idx, gates, rank, carry + jnp.sum(hot, axis=0, keepdims=True)


def _mo_ln2_route(x1, o_bf, w_mo_ref, g_ref, b_ref, w_r_ref, b_r_ref, carry):
    x2 = _layer_norm(DEEPNORM_ALPHA * x1 + _dot(o_bf, w_mo_ref[...]), g_ref[...], b_ref[...])
    return (x2,) + _route(x2, w_r_ref, b_r_ref, carry)


def _post_prompt_kernel(x_ref, pooly_ref, mlay_ref, mk_ref, mv_ref,
                        w_out_a_ref, w_out_b_ref, g1_ref, b1_ref, w_mq_ref, w_mo_ref, g2_ref, b2_ref,
                        w_r_ref, b_r_ref,
                        x2_ref, idx_ref, gate_ref, rank_ref, cnt_ref, carry_ref):
    first = jnp.logical_and(pl.program_id(0) == 0, pl.program_id(1) == 0)

    @pl.when(first)
    def _():
        carry_ref[...] = jnp.zeros(carry_ref.shape, F32)

    x1 = _mix_ln1(x_ref[...], pooly_ref[...], mlay_ref[...], w_out_a_ref, w_out_b_ref, g1_ref, b1_ref)
    qm = (_dot(x1.astype(BF16), w_mq_ref[...]) * MEM_SCALE).astype(BF16)
    outs = []
    for hd in range(MEM_HEADS):
        cols = slice(MEM_HEAD_DIM * hd, MEM_HEAD_DIM * (hd + 1))
        s = _dot_nt(qm[:, cols], mk_ref[:, cols])
        p = jnp.exp(s - jnp.max(s, axis=1, keepdims=True))
        o = _dot(p.astype(BF16), mv_ref[:, cols]) / jnp.sum(p, axis=1, keepdims=True)
        outs.append(o.astype(BF16))
    o_all = jnp.concatenate(outs, axis=1)
    x2, idx, gates, rank, carry = _mo_ln2_route(x1, o_all, w_mo_ref, g2_ref, b2_ref, w_r_ref, b_r_ref,
                                                carry_ref[...])
    _store_row_tiles(x2_ref, (), x2)
    idx_ref[...] = idx
    gate_ref[...] = gates
    rank_ref[...] = rank
    carry_ref[...] = carry
    cnt_ref[...] = carry


def _post_weight_specs():
    half = (POOL_WIDTH, D_MODEL)
    return [
        _full(half), _full(half), _full((1, D_MODEL)), _full((1, D_MODEL)),
        _full((D_MODEL, MEM_WIDTH)), _full((MEM_WIDTH, D_MODEL)), _full((1, D_MODEL)), _full((1, D_MODEL)),
        _full((2, N_EXPERTS, D_MODEL)), _full((1, N_EXPERTS)),
    ]


def _post_prompt(x2d, pool_y, mla_y, mk_b, mv_b, weights, n_batch, seq, n_mem, n_all):
    tm = TOKEN_BLOCK
    nj = seq // tm
    tok = lambda b, j: (b * nj + j, 0)
    return pl.pallas_call(
        _post_prompt_kernel,
        grid=(n_batch, nj),
        in_specs=[
            pl.BlockSpec((tm, D_MODEL), tok), pl.BlockSpec((tm, POOL_WIDTH), tok), pl.BlockSpec((tm, POOL_WIDTH), tok),
            pl.BlockSpec((n_mem, MEM_WIDTH), lambda b, j: (b, 0)), pl.BlockSpec((n_mem, MEM_WIDTH), lambda b, j: (b, 0)),
        ] + _post_weight_specs(),
        out_specs=[
            pl.BlockSpec((tm * ROW_TILE, LANES), tok), pl.BlockSpec((tm, TOP_K), tok), pl.BlockSpec((tm, TOP_K), tok),
            pl.BlockSpec((tm, TOP_K), tok), _full((1, N_EXPERTS)),
        ],
        out_shape=[
            jax.ShapeDtypeStruct((n_all * ROW_TILE, LANES), F32), jax.ShapeDtypeStruct((n_all, TOP_K), I32),
            jax.ShapeDtypeStruct((n_all, TOP_K), F32), jax.ShapeDtypeStruct((n_all, TOP_K), I32),
            jax.ShapeDtypeStruct((1, N_EXPERTS), F32),
        ],
        scratch_shapes=[pltpu.VMEM((1, N_EXPERTS), F32)],
        compiler_params=_params("arbitrary", "arbitrary"),
        name="post_prompt",
    )(x2d, pool_y, mla_y, mk_b, mv_b, *weights)


def _mix_sample_kernel(x_ref, pooly_ref, mlay_ref, w_out_a_ref, w_out_b_ref, g1_ref, b1_ref, w_mq_ref,
                       x1_ref, qm_ref):
    x1 = _mix_ln1(x_ref[...], pooly_ref[...], mlay_ref[...], w_out_a_ref, w_out_b_ref, g1_ref, b1_ref)
    x1_ref[...] = x1
    qm_ref[...] = (_dot(x1.astype(BF16), w_mq_ref[...]) * MEM_SCALE).astype(BF16)


def _mix_sample(x2d, pool_y, mla_y, weights):
    n = x2d.shape[0]
    w_out_a, w_out_b, g1, b1, w_mq = weights[:5]
    return pl.pallas_call(
        _mix_sample_kernel,
        grid=(1,),
        in_specs=[_full((n, D_MODEL)), _full((n, POOL_WIDTH)), _full((n, POOL_WIDTH))] + _post_weight_specs()[:5],
        out_specs=[_full((n, D_MODEL)), _full((n, MEM_WIDTH))],
        out_shape=[jax.ShapeDtypeStruct((n, D_MODEL), F32), jax.ShapeDtypeStruct((n, MEM_WIDTH), BF16)],
        compiler_params=_params("arbitrary"),
        name="mix_sample",
    )(x2d, pool_y, mla_y, w_out_a, w_out_b, g1, b1, w_mq)


def _mem_attn_sample_kernel(q_ref, k_ref, v_ref, out_ref, res_ref, *, n_new):
    n_mem = k_ref.shape[1] // MEM_HEADS
    rows = q_ref.shape[1]
    row_head = lax.broadcasted_iota(I32, (rows, 1), 0) // n_new
    for b in range(q_ref.shape[0]):
        q = q_ref[b]
        s = jnp.zeros((rows, n_mem), F32)
        for hd in range(MEM_HEADS):
            k_h = k_ref[b, pl.ds(hd, n_mem, stride=MEM_HEADS), :].astype(BF16)
            s = jnp.where(row_head == hd, _dot_nt(q, k_h), s)
        p = jnp.exp(s - jnp.max(s, axis=1, keepdims=True))
        p_bf = p.astype(BF16)
        o = jnp.zeros((rows, MEM_HEAD_DIM), F32)
        for hd in range(MEM_HEADS):
            v_h = v_ref[b, pl.ds(hd, n_mem, stride=MEM_HEADS), :].astype(BF16)
            o = jnp.where(row_head == hd, _dot(p_bf, v_h), o)
        res_ref[...] = o / jnp.sum(p, axis=1, keepdims=True)
        for hd in range(MEM_HEADS):
            cols = slice(MEM_HEAD_DIM * hd, MEM_HEAD_DIM * (hd + 1))
            out_ref[b, :, cols] = res_ref[hd * n_new:(hd + 1) * n_new, :].astype(BF16)


def _mem_attn_sample(q_b, mem_k, mem_v, n_new):
    n_b, rows, _ = q_b.shape
    mh = mem_k.shape[1]
    gb = min(MEM_BATCH_BLOCK, n_b)
    assert n_b % gb == 0
    blk = lambda i: (i, 0, 0)
    return pl.pallas_call(
        functools.partial(_mem_attn_sample_kernel, n_new=n_new),
        grid=(n_b // gb,),
        in_specs=[pl.BlockSpec((gb, rows, MEM_HEAD_DIM), blk), pl.BlockSpec((gb, mh, MEM_HEAD_DIM), blk),
                  pl.BlockSpec((gb, mh, MEM_HEAD_DIM), blk)],
        out_specs=pl.BlockSpec((gb, n_new, MEM_WIDTH), blk),
        out_shape=jax.ShapeDtypeStruct((n_b, n_new, MEM_WIDTH), BF16),
        scratch_shapes=[pltpu.VMEM((rows, MEM_HEAD_DIM), F32)],
        compiler_params=_params("arbitrary"),
        name="mem_attn_sample",
    )(q_b, mem_k, mem_v)


def _route_sample_kernel(x1_ref, o_ref, cnt_in_ref, w_mo_ref, g2_ref, b2_ref, w_r_ref, b_r_ref,
                         x2_in, idx_in, gate_in, rank_in,
                         x2_ref, idx_ref, gate_ref, rank_ref, cnt_ref):
    del x2_in, idx_in, gate_in, rank_in
    x2, idx, gates, rank, carry = _mo_ln2_route(x1_ref[...], o_ref[...], w_mo_ref, g2_ref, b2_ref, w_r_ref, b_r_ref,
                                                cnt_in_ref[...])
    _store_row_tiles(x2_ref, (), x2)
    idx_ref[...] = idx
    gate_ref[...] = gates
    rank_ref[...] = rank
    cnt_ref[...] = carry


def _route_sample(x1, o_bf, cnt_in, weights, x2_all, idx_all, gate_all, rank_all, n_prompt):
    n = x1.shape[0]
    assert n_prompt % n == 0
    tail = lambda i: (n_prompt // n, 0)
    anyspec = pl.BlockSpec(memory_space=pl.ANY)
    w_mo, g2, b2, w_r, b_r = weights[5:]
    return pl.pallas_call(
        _route_sample_kernel,
        grid=(1,),
        in_specs=[_full((n, D_MODEL)), _full((n, MEM_WIDTH)), _full((1, N_EXPERTS))] + _post_weight_specs()[5:] + [
            anyspec, anyspec, anyspec, anyspec],
        out_specs=[pl.BlockSpec((n * ROW_TILE, LANES), tail), pl.BlockSpec((n, TOP_K), tail), pl.BlockSpec((n, TOP_K), tail),
                   pl.BlockSpec((n, TOP_K), tail), _full((1, N_EXPERTS))],
        out_shape=[jax.ShapeDtypeStruct(x2_all.shape, F32), jax.ShapeDtypeStruct(idx_all.shape, I32),
                   jax.ShapeDtypeStruct(gate_all.shape, F32), jax.ShapeDtypeStruct(rank_all.shape, I32),
                   jax.ShapeDtypeStruct((1, N_EXPERTS), F32)],
        input_output_aliases={8: 0, 9: 1, 10: 2, 11: 3},
        compiler_params=_params("arbitrary"),
        name="route_sample",
    )(x1, o_bf, cnt_in, w_mo, g2, b2, w_r, b_r, x2_all, idx_all, gate_all, rank_all)


def _dispatch_kernel(pad_start_ref, n_pad_ref, n_used_ref, idx_hbm, x_ref, xrows_hbm,
                     idx_smem, zbuf, idx_sem, row_sem, pad_sem):
    i = pl.program_id(0)
    last = pl.num_programs(0) - 1
    tm = x_ref.shape[0] // ROW_TILE
    blk = zbuf.shape[0]
    n_blk = xrows_hbm.shape[0] // blk

    def idx_copy(row, s):
        return pltpu.make_async_copy(idx_hbm.at[row], idx_smem.at[s], idx_sem.at[s])

    def pad_copy(e, r):
        dst = pl.multiple_of((pad_start_ref[e] + r) * ROW_TILE, ROW_TILE)
        return pltpu.make_async_copy(zbuf.at[pl.ds(0, ROW_TILE)], xrows_hbm.at[pl.ds(dst, ROW_TILE)], pad_sem)

    def tail_copy(j):
        return pltpu.make_async_copy(zbuf, xrows_hbm.at[pl.ds(pl.multiple_of(j * blk, blk), blk)], pad_sem)

    @pl.when(i == 0)
    def _():
        idx_copy(0, 0).start()
        zbuf[...] = jnp.zeros(zbuf.shape, F32)
        for e in range(N_EXPERTS):
            def start(r, c, e=e):
                pad_copy(e, r).start()
                return c

            lax.fori_loop(0, n_pad_ref[e], start, 0)

        def start_tail(j, c):
            tail_copy(j).start()
            return c

        lax.fori_loop(n_used_ref[0], n_blk, start_tail, 0)
        for e in range(N_EXPERTS):
            def wait(r, c, e=e):
                pad_copy(e, r).wait()
                return c

            lax.fori_loop(0, n_pad_ref[e], wait, 0)

        def wait_tail(j, c):
            tail_copy(j).wait()
            return c

        lax.fori_loop(n_used_ref[0], n_blk, wait_tail, 0)

    def step(slot):
        idx_copy(i, slot).wait()

        @pl.when(i < last)
        def _():
            idx_copy(i + 1, 1 - slot).start()

        for t in range(tm):
            for k in range(TOP_K):
                dst = pl.multiple_of(idx_smem[slot, TOP_K * t + k], ROW_TILE)
                pltpu.make_async_copy(x_ref.at[pl.ds(ROW_TILE * t, ROW_TILE)], xrows_hbm.at[pl.ds(dst, ROW_TILE)],
                                      row_sem).start(priority=k % 2)
        for k in range(TOP_K):
            pltpu.make_async_copy(x_ref, x_ref, row_sem).wait()

    for parity in range(2):
        pl.when(i % 2 == parity)(functools.partial(step, parity))


def _dispatch(pad_start, n_pad, n_used, idx_rows, x_all, n_rows):
    tm = DISPATCH_BLOCK
    n_steps = idx_rows.shape[0]
    assert x_all.shape[0] == n_steps * tm * ROW_TILE and idx_rows.shape[1] == TOP_K * tm
    assert n_rows % EXPERT_ROWS == 0
    grid_spec = pltpu.PrefetchScalarGridSpec(
        num_scalar_prefetch=3,
        grid=(n_steps,),
        in_specs=[pl.BlockSpec(memory_space=pl.ANY),
                  pl.BlockSpec((tm * ROW_TILE, LANES), lambda i, ps, npd, nu: (i, 0))],
        out_specs=pl.BlockSpec(memory_space=pl.ANY),
        scratch_shapes=[pltpu.SMEM((2, TOP_K * tm), I32), pltpu.VMEM((EXPERT_ROWS * ROW_TILE, LANES), F32),
                        pltpu.SemaphoreType.DMA((2,)), pltpu.SemaphoreType.DMA, pltpu.SemaphoreType.DMA],
    )
    return pl.pallas_call(
        _dispatch_kernel,
        grid_spec=grid_spec,
        out_shape=jax.ShapeDtypeStruct((n_rows * ROW_TILE, LANES), F32),
        compiler_params=_params("arbitrary"),
        name="moe_dispatch",
    )(pad_start, n_pad, n_used, idx_rows, x_all)


def _moe_kernel(blk_exp_ref, blk_row_ref, idx_hbm, x_ref, wgu_f32_ref, bgu_ref, wd_f32_ref, bd_ref, y_hbm,
                idx_smem, ybuf, wgu_ref, wd_ref, idx_sem, y_sem):
    del blk_row_ref
    i = pl.program_id(0)
    last = pl.num_programs(0) - 1
    tr = x_ref.shape[0] // ROW_TILE

    @pl.when(jnp.logical_or(i == 0, blk_exp_ref[i] != blk_exp_ref[jnp.maximum(i - 1, 0)]))
    def _():
        wgu_ref[...] = wgu_f32_ref[...].astype(BF16)
        wd_ref[...] = wd_f32_ref[...].astype(BF16)

    n_chunks = D_EXPERT // EXPERT_CHUNK

    def idx_copy(row, s):
        return pltpu.make_async_copy(idx_hbm.at[row], idx_smem.at[s], idx_sem.at[s])

    def scatter(s_idx, ys, rows):
        for r in rows:
            dst = pl.multiple_of(idx_smem[s_idx, r], ROW_TILE)
            pltpu.make_async_copy(ybuf.at[ys, pl.ds(ROW_TILE * r, ROW_TILE)], y_hbm.at[pl.ds(dst, ROW_TILE)],
                                  y_sem.at[ys]).start(priority=r % 2)

    def wait_rows(s):
        pltpu.make_async_copy(ybuf.at[s], ybuf.at[s], y_sem.at[s]).wait()

    @pl.when(i == 0)
    def _():
        idx_copy(0, 0).start()
        ybuf[1] = jnp.zeros(ybuf.shape[1:], F32)

    def step(slot):
        nxt = 1 - slot
        idx_copy(i, slot).wait()
        idx_copy(i + 1, nxt).start()
        xb = _load_row_tiles(x_ref, (), tr).astype(BF16)
        acc = jnp.zeros((tr, D_MODEL), F32) + bd_ref[...]

        groups = 3 * n_chunks
        per_group = -(-tr // groups)

        def issue(g):
            scatter(slot, nxt, range(g * per_group, min((g + 1) * per_group, tr)))

        for c in range(n_chunks):
            cg = slice(c * EXPERT_CHUNK, (c + 1) * EXPERT_CHUNK)
            cu = slice(D_EXPERT + c * EXPERT_CHUNK, D_EXPERT + (c + 1) * EXPERT_CHUNK)
            gate = jnp.minimum(_dot(xb, wgu_ref[:, cg]) + bgu_ref[:, cg], SWIGLU_LIMIT)
            issue(3 * c)
            up = jnp.clip(_dot(xb, wgu_ref[:, cu]) + bgu_ref[:, cu], -SWIGLU_LIMIT, SWIGLU_LIMIT)
            issue(3 * c + 1)
            act = gate * (1.0 / (1.0 + jnp.exp(-SWIGLU_ALPHA * gate))) * (up + 1.0)
            acc = acc + _dot(act.astype(BF16), wd_ref[cg, :])
            issue(3 * c + 2)

        @pl.when(i >= 1)
        def _():
            wait_rows(slot)

        _store_row_tiles(ybuf, (slot,), acc)

        @pl.when(i == last)
        def _():
            idx_copy(i + 1, nxt).wait()
            wait_rows(nxt)

    for parity in range(2):
        pl.when(i % 2 == parity)(functools.partial(step, parity))


def _moe(blk_exp, blk_row, idx_rows, x_rows, w_gu, b_gu, w_d, b_d, n_assign):
    tr = EXPERT_ROWS
    n_steps = blk_exp.shape[0]
    assert n_steps >= 2 and idx_rows.shape == (n_steps + 1, tr)
    anyspec = pl.BlockSpec(memory_space=pl.ANY)
    emap3 = lambda i, be, br: (be[i], 0, 0)
    grid_spec = pltpu.PrefetchScalarGridSpec(
        num_scalar_prefetch=2,
        grid=(n_steps,),
        in_specs=[
            anyspec, pl.BlockSpec((tr * ROW_TILE, LANES), lambda i, be, br: (br[i], 0)),
            pl.BlockSpec((None, D_MODEL, 2 * D_EXPERT), emap3), pl.BlockSpec((None, 1, 2 * D_EXPERT), emap3),
            pl.BlockSpec((None, D_EXPERT, D_MODEL), emap3), pl.BlockSpec((None, 1, D_MODEL), emap3),
        ],
        out_specs=anyspec,
        scratch_shapes=[
            pltpu.SMEM((2, tr), I32), pltpu.VMEM((2, tr * ROW_TILE, LANES), F32),
            pltpu.VMEM((D_MODEL, 2 * D_EXPERT), BF16), pltpu.VMEM((D_EXPERT, D_MODEL), BF16),
            pltpu.SemaphoreType.DMA((2,)), pltpu.SemaphoreType.DMA((2,)),
        ],
    )
    return pl.pallas_call(
        _moe_kernel,
        grid_spec=grid_spec,
        out_shape=jax.ShapeDtypeStruct(((n_assign + 2 * tr) * ROW_TILE, LANES), F32),
        compiler_params=_params("arbitrary"),
        name="moe_experts",
    )(blk_exp, blk_row, idx_rows, x_rows, w_gu, b_gu, w_d, b_d)


def _combine_kernel(x2_ref, gate_ref, g3_ref, b3_ref, *rest):
    y_refs, out_ref = rest[:TOP_K], rest[TOP_K]
    gates = gate_ref[...]
    tm = out_ref.shape[0]
    moe = jnp.zeros(out_ref.shape, F32)
    for k in range(TOP_K):
        moe = moe + gates[:, k:k + 1] * _load_row_tiles(y_refs[k], (), tm)
    out_ref[...] = _layer_norm(DEEPNORM_ALPHA * _load_row_tiles(x2_ref, (), tm) + moe, g3_ref[...], b3_ref[...])


def _combine(x2_all, y_tok, gates_all, g3, b3, row0, n_rows, tm):
    n_all = gates_all.shape[0]
    assert row0 % tm == 0 and n_rows % tm == 0 and n_all % tm == 0
    off = row0 // tm
    tok = lambda i: (off + i, 0)
    tiles = (tm * ROW_TILE, LANES)
    slot_specs = [pl.BlockSpec(tiles, functools.partial(lambda i, k: (k * (n_all // tm) + off + i, 0), k=k))
                  for k in range(TOP_K)]
    return pl.pallas_call(
        _combine_kernel,
        grid=(n_rows // tm,),
        in_specs=[pl.BlockSpec(tiles, tok), pl.BlockSpec((tm, TOP_K), tok), _full((1, D_MODEL)),
                  _full((1, D_MODEL))] + slot_specs,
        out_specs=pl.BlockSpec((tm, D_MODEL), lambda i: (i, 0)),
        out_shape=jax.ShapeDtypeStruct((n_rows, D_MODEL), F32),
        compiler_params=_params("arbitrary"),
        name="combine_ln3",
    )(x2_all, gates_all, g3, b3, *([y_tok] * TOP_K))


def _rope_tables(pos):
    inv_freq = ROPE_THETA ** (-jnp.arange(ROPE_HALF, dtype=F32) / ROPE_HALF)
    ang = pos.astype(F32)[:, None] * inv_freq[None, :]
    c, s, z = jnp.cos(ang), jnp.sin(ang), jnp.zeros_like(ang)
    return jnp.concatenate([c, z, c, z], axis=1), jnp.concatenate([-s, z, s, z], axis=1)


def _pad_rope_cols(w):
    z = jnp.zeros(w.shape[:-1] + (ROPE_HALF,), w.dtype)
    return jnp.concatenate([w[..., :ROPE_HALF], z, w[..., ROPE_HALF:], z], axis=-1)


def _layer_weights(l, w_in, w_pool, pool_scale, q_norm_g, w_uq, kv_norm_g, w_uk, w_uv, w_out, ln1_g, ln1_b,
                   w_mq, w_mo, ln2_g, ln2_b, w_router, b_router):
    o3 = POOL_WIDTH + Q_RANK + KV_RANK
    w_in_p = jnp.concatenate([w_in[l][:, :o3], _pad_rope_cols(w_in[l][:, o3:])], axis=1).astype(BF16)
    uq = w_uq[l].reshape(Q_RANK, MLA_HEADS, QK_NOPE_DIM + QK_ROPE_DIM)
    uq_p = jnp.concatenate([uq[..., :QK_NOPE_DIM], _pad_rope_cols(uq[..., QK_NOPE_DIM:])], axis=-1)
    uq_p = uq_p.reshape(Q_RANK, MLA_HEADS * QK_PAD).astype(BF16)
    w_ukt = jnp.transpose(w_uk[l], (1, 2, 0)).astype(BF16)
    w_uv_h = jnp.transpose(w_uv[l], (1, 0, 2)).astype(BF16)
    row = lambda v: v[l].reshape(1, -1)
    w_rt = w_router[l].T
    w_rt_hi = w_rt.astype(BF16)
    w_r_split = jnp.stack([w_rt_hi, (w_rt - w_rt_hi.astype(F32)).astype(BF16)])
    proj =[w_in_p, row(q_norm_g), uq_p, row(kv_norm_g), w_ukt, w_pool[l].astype(BF16), row(pool_scale)]
    post = [w_out[l][:POOL_WIDTH].astype(BF16), w_out[l][POOL_WIDTH:].astype(BF16), row(ln1_g), row(ln1_b),
            w_mq[l].astype(BF16), w_mo[l].astype(BF16), row(ln2_g), row(ln2_b), w_r_split, row(b_router)]
    return proj, w_uv_h, post


def _routing_tables(idx_all, rank_all, counts, n_blocks):
    tr = EXPERT_ROWS
    cnt = counts.reshape(-1).astype(I32)
    padded = ((cnt + tr - 1) // tr) * tr
    pend = jnp.cumsum(padded)
    pstart = pend - padded
    dest2d = pstart[idx_all] + rank_all
    dest = dest2d.reshape(-1)
    n_all = idx_all.shape[0]
    m = dest.shape[0]
    n_rows = n_blocks * tr
    disp_rows = (dest2d * ROW_TILE).reshape(n_all // DISPATCH_BLOCK, DISPATCH_BLOCK * TOP_K)
    assign = (jnp.arange(TOP_K, dtype=I32)[None, :] * n_all + jnp.arange(n_all, dtype=I32)[:, None]).reshape(-1)
    _, packed = lax.sort_key_val(dest, assign)
    start = jnp.cumsum(cnt) - cnt
    shift = pstart - start
    none = jnp.full((n_rows,), -1, I32)
    doubled = jnp.concatenate([none, packed, none[:n_rows - m]])
    rows = jnp.arange(n_rows, dtype=I32)
    row_src = none
    for e in range(N_EXPERTS):
        moved = lax.dynamic_slice(doubled, (n_rows - shift[e],), (n_rows,))
        row_src = jnp.where((rows >= pstart[e]) & (rows < pstart[e] + cnt[e]), moved, row_src)
    spare = m + jnp.arange(n_rows, dtype=I32) % (2 * tr)
    dst_rows = (jnp.where(row_src >= 0, row_src, spare) * ROW_TILE).reshape(n_blocks, tr)
    spare_row = spare[:tr][None] * ROW_TILE
    idx_rows = jnp.concatenate([spare_row, dst_rows, spare_row], axis=0)
    n_used = pend[-1] // tr
    blk_row = jnp.minimum(jnp.arange(n_blocks + 1, dtype=I32), n_used - 1)
    blk_exp = jnp.sum((pend[None, :] <= (blk_row * tr)[:, None]).astype(I32), axis=1)
    return (jnp.clip(blk_exp, 0, N_EXPERTS - 1), blk_row, idx_rows, disp_rows, pstart + cnt, padded - cnt,
            n_used.astype(I32).reshape(1))


def kernel(x_prompt, x_sample, mem_prompt, cache_kv_latent, cache_k_rope, cache_mem_k, cache_mem_v, state_pool,
           page_table, w_in, w_pool, pool_scale, q_norm_g, w_uq, kv_norm_g, w_uk, w_uv, w_out, ln1_g, ln1_b,
           w_mq, w_mk, w_mv, w_mo, ln2_g, ln2_b, w_router, b_router, w_gate_up, b_gate_up, w_down, b_down,
           ln3_g, ln3_b):
    n_bp, seq, _ = x_prompt.shape
    n_bs, n_new, _ = x_sample.shape
    n_mem = mem_prompt.shape[1]
    page = cache_kv_latent.shape[2]
    past_len = page_table.shape[1] * page
    n_p = n_bp * seq
    n_s = n_bs * n_new
    n_all = n_p + n_s
    n_assign = n_all * TOP_K
    assert w_in.shape[0] == DEPTH == 1
    assert seq % TOKEN_BLOCK == 0 and seq % ATTN_K_BLOCK == 0 and seq % ATTN_Q_BLOCK == 0 and seq >= POOL_HALO
    assert n_p % n_s == 0 and (n_s % TOKEN_BLOCK == 0 or TOKEN_BLOCK % n_s == 0)
    l = 0

    proj_w, w_uv_h, post_w = _layer_weights(l, w_in, w_pool, pool_scale, q_norm_g, w_uq, kv_norm_g, w_uk, w_uv,
                                            w_out, ln1_g, ln1_b, w_mq, w_mo, ln2_g, ln2_b, w_router, b_router)

    xp = x_prompt.reshape(n_p, D_MODEL)
    cos_p, sin_p = _rope_tables(jnp.arange(seq))
    qcat, kcat, lat_p, kr_p, pooly_p, pstate_p = _proj_prompt(xp, proj_w, cos_p, sin_p, n_bp, seq)
    mlay_p = _attn_prompt(qcat, kcat, w_uv_h, n_bp, seq)
    mk, mv, mk_b, mv_b = _mem_kv(mem_prompt.reshape(n_bp * n_mem, D_MODEL), w_mk[l].astype(BF16),
                                 w_mv[l].astype(BF16), n_bp, n_mem)
    x2_all, idx_all, gate_all, rank_all, cnt_p = _post_prompt(xp, pooly_p, mlay_p, mk_b, mv_b, post_w,
                                                              n_bp, seq, n_mem, n_all)

    xs = jnp.transpose(x_sample, (1, 0, 2)).reshape(n_s, D_MODEL)
    state_t = jnp.transpose(state_pool[l], (1, 0, 2))
    cos_s, sin_s = _rope_tables(jnp.repeat(past_len + jnp.arange(n_new), n_bs))
    qcat_s, kcat_s, lat_s, kr_s, pooly_s, pstate_s = _proj_sample(xs, state_t, proj_w, cos_s, sin_s, past_len)
    q_b = jnp.transpose(qcat_s.reshape(MLA_HEADS, n_new, n_bs, QK_PAD), (2, 0, 1, 3)).reshape(
        n_bs, MLA_HEADS * n_new, QK_PAD)
    knew_b = jnp.transpose(kcat_s.reshape(n_new, n_bs, QK_PAD), (1, 0, 2))
    knew_b = jnp.pad(knew_b, ((0, 0), (0, LANES - n_new), (0, 0)))
    mlay_b = _attn_sample(page_table, q_b, knew_b, w_uv_h, cache_kv_latent,
                          jnp.transpose(cache_k_rope, (0, 1, 3, 2)), l)
    mlay_s = jnp.transpose(mlay_b, (1, 0, 2)).reshape(n_s, POOL_WIDTH).astype(BF16)
    x1_s, qm_s = _mix_sample(xs, pooly_s, mlay_s, post_w)
    qm_b = jnp.transpose(qm_s.reshape(n_new, n_bs, MEM_HEADS, MEM_HEAD_DIM), (1, 2, 0, 3)).reshape(
        n_bs, MEM_HEADS * n_new, MEM_HEAD_DIM)
    o_b = _mem_attn_sample(qm_b, cache_mem_k[l].reshape(n_bs, n_mem * MEM_HEADS, MEM_HEAD_DIM),
                           cache_mem_v[l].reshape(n_bs, n_mem * MEM_HEADS, MEM_HEAD_DIM), n_new)
    o_s = jnp.transpose(o_b, (1, 0, 2)).reshape(n_s, MEM_WIDTH)
    x2_all, idx_all, gate_all, rank_all, counts = _route_sample(x1_s, o_s, cnt_p, post_w, x2_all, idx_all, gate_all,
                                                               rank_all, n_p)

    n_blocks = -(-n_assign // EXPERT_ROWS) + N_EXPERTS + 1
    blk_exp, blk_row, idx_rows, disp_rows, pad_start, n_pad, n_used = _routing_tables(idx_all, rank_all, counts,
                                                                                      n_blocks)
    x_rows = _dispatch(pad_start, n_pad, n_used, disp_rows, x2_all, n_blocks * EXPERT_ROWS)
    y_tok = _moe(blk_exp, blk_row, idx_rows, x_rows, w_gate_up[l], b_gate_up[l][:, None, :],
                 w_down[l], b_down[l][:, None, :], n_assign)
    g3, b3 = ln3_g[l].reshape(1, -1), ln3_b[l].reshape(1, -1)
    y_p = _combine(x2_all, y_tok, gate_all, g3, b3, 0, n_p, TOKEN_BLOCK)
    y_s = _combine(x2_all, y_tok, gate_all, g3, b3, n_p, n_s, min(TOKEN_BLOCK, n_s))

    def from_steps(a, width):
        return jnp.transpose(a.reshape(n_new, n_bs, width), (1, 0, 2))[None]

    return (
        y_p.reshape(n_bp, seq, D_MODEL),
        from_steps(y_s, D_MODEL)[0],
        lat_p.reshape(1, n_bp, seq, KV_RANK),
        kr_p.reshape(1, n_bp, seq, QK_ROPE_DIM),
        pstate_p[None, :, POOL_HALO - POOL_BUF:, :],
        mk.reshape(1, n_bp, n_mem, MEM_HEADS, MEM_HEAD_DIM),
        mv.reshape(1, n_bp, n_mem, MEM_HEADS, MEM_HEAD_DIM),
        from_steps(lat_s, KV_RANK),
        from_steps(kr_s, QK_ROPE_DIM),
        jnp.transpose(pstate_s, (1, 0, 2))[None],
    )
```

```python
import functools

import jax
import jax.numpy as jnp
from jax import lax
from jax.experimental import pallas as pl
from jax.experimental.pallas import tpu as pltpu

F32 = jnp.float32
BF16 = jnp.bfloat16
I32 = jnp.int32

D_MODEL = 1024
POOL_WIDTH = 512
POOL_WINDOWS = (2, 4, 8, 16)
POOL_GROUP_DIM = 128
POOL_BUF = 15
POOL_HALO = 16
MLA_HEADS = 4
QK_NOPE_DIM = 128
QK_ROPE_DIM = 64
ROPE_HALF = 32
V_HEAD_DIM = 128
Q_RANK = 256
KV_RANK = 128
ROPE_THETA = 10000.0
QK_PAD = 256
ONES_LANE = QK_PAD - 1
MEM_HEADS = 4
MEM_HEAD_DIM = 128
MEM_WIDTH = 512
N_EXPERTS = 32
TOP_K = 4
D_EXPERT = 1024
SWIGLU_LIMIT = 7.0
SWIGLU_ALPHA = 1.702
LN_EPS = 1e-5
RMS_EPS = 1e-6
DEPTH = 1
DEEPNORM_ALPHA = (2.0 * DEPTH) ** 0.25
ATTN_SCALE = (QK_NOPE_DIM + QK_ROPE_DIM) ** -0.5
LOG2_E = 1.4426950408889634
MEM_SCALE = MEM_HEAD_DIM ** -0.5

LANES = 128
VMEM_LIMIT = 48 * 1024 * 1024

TOKEN_BLOCK = 512
ATTN_Q_BLOCK = 256
ATTN_K_BLOCK = 512
ATTN_GROUPS = 2
PAGES_PER_STEP = 64
MEM_BATCH_BLOCK = 8
DISPATCH_BLOCK = 256
EXPERT_ROWS = 256
EXPERT_CHUNK = 512


def _dot(a, b):
    return jnp.dot(a, b, preferred_element_type=F32)


def _dot_nt(a, b):
    return lax.dot_general(a, b, (((1,), (1,)), ((), ())), preferred_element_type=F32)


def _rms(x, g):
    return x * lax.rsqrt(jnp.mean(x * x, axis=-1, keepdims=True) + RMS_EPS) * g


def _layer_norm(x, g, b):
    mu = jnp.mean(x, axis=-1, keepdims=True)
    xc = x - mu
    var = jnp.mean(xc * xc, axis=-1, keepdims=True)
    return xc * lax.rsqrt(var + LN_EPS) * g + b


ROW_TILE = D_MODEL // LANES


def _store_row_tiles(ref, lead, val):
    m = val.shape[0]
    for j in range(ROW_TILE):
        ref[(*lead, pl.ds(j, m, stride=ROW_TILE), slice(None))] = val[:, LANES * j:LANES * (j + 1)]


def _load_row_tiles(ref, lead, m):
    return jnp.concatenate([ref[(*lead, pl.ds(j, m, stride=ROW_TILE), slice(None))] for j in range(ROW_TILE)], axis=1)


def _params(*semantics):
    return pltpu.CompilerParams(dimension_semantics=semantics, vmem_limit_bytes=VMEM_LIMIT)


def _full(shape):
    n = len(shape)
    return pl.BlockSpec(shape, lambda *_: (0,) * n)


def _project(x_bf, w_in_ref, qg_ref, w_uq_ref, kvg_ref, w_ukt_ref, cos, sin, q_scale):
    h = _dot(x_bf, w_in_ref[...])
    u = h[:, :POOL_WIDTH]
    qn = _rms(h[:, POOL_WIDTH:POOL_WIDTH + Q_RANK], qg_ref[...])
    q = _dot(qn.astype(BF16), w_uq_ref[...])
    q_lat, q_rope = [], []
    for hd in range(MLA_HEADS):
        nope = q[:, QK_PAD * hd:QK_PAD * hd + QK_NOPE_DIM]
        rp = q[:, QK_PAD * hd + QK_NOPE_DIM:QK_PAD * (hd + 1)]
        q_lat.append(_dot(nope.astype(BF16), w_ukt_ref[hd]) * q_scale)
        q_rope.append((rp * cos + pltpu.roll(rp, 64, 1) * sin) * q_scale)
    o2 = POOL_WIDTH + Q_RANK
    lat = _rms(h[:, o2:o2 + KV_RANK], kvg_ref[...])
    kp = h[:, o2 + KV_RANK:]
    k_rope = kp * cos + pltpu.roll(kp, 64, 1) * sin
    return u, q_lat, q_rope, lat, k_rope


def _compact_rope(r):
    lane = lax.broadcasted_iota(I32, r.shape, 1)
    moved = pltpu.roll(r, 96, 1)
    return jnp.where(lane < ROPE_HALF, r, jnp.where(lane < QK_ROPE_DIM, moved, 0.0))


def _with_ones_lane(r):
    lane = lax.broadcasted_iota(I32, r.shape, 1)
    return jnp.where(lane == ONES_LANE - KV_RANK, 1.0, r)


def _pool_group_out(diff, g, w_pool_ref, pscale_ref):
    cols = slice(POOL_GROUP_DIM * g, POOL_GROUP_DIM * (g + 1))
    return _dot(diff.astype(BF16), w_pool_ref[g]) * pscale_ref[:, cols]


def _proj_prompt_kernel(x_ref, w_in_ref, qg_ref, w_uq_ref, kvg_ref, w_ukt_ref, w_pool_ref, pscale_ref,
                        cos_ref, sin_ref,
                        qcat_ref, kcat_ref, lat_ref, kr_ref, pooly_ref, pstate_ref, ubuf):
    j = pl.program_id(1)
    tm = x_ref.shape[0]

    @pl.when(j == 0)
    def _():
        ubuf[0:POOL_HALO, :] = jnp.zeros((POOL_HALO, POOL_WIDTH), F32)

    u, q_lat, q_rope, lat, k_rope = _project(
        x_ref[...].astype(BF16), w_in_ref, qg_ref, w_uq_ref, kvg_ref, w_ukt_ref, cos_ref[...], sin_ref[...],
        ATTN_SCALE * LOG2_E)
    for hd in range(MLA_HEADS):
        qcat_ref[hd, :, 0:KV_RANK] = q_lat[hd].astype(BF16)
        qcat_ref[hd, :, KV_RANK:QK_PAD] = q_rope[hd].astype(BF16)
    lat_ref[...] = lat
    kcat_ref[:, 0:KV_RANK] = lat.astype(BF16)
    kcat_ref[:, KV_RANK:QK_PAD] = _with_ones_lane(k_rope).astype(BF16)
    kr_ref[...] = _compact_rope(k_rope)[:, :QK_ROPE_DIM]

    ubuf[POOL_HALO:POOL_HALO + tm, :] = u
    pos = j * tm + lax.broadcasted_iota(I32, (tm, 1), 0)
    for g, w in enumerate(POOL_WINDOWS):
        cols = slice(POOL_GROUP_DIM * g, POOL_GROUP_DIM * (g + 1))
        ug = u[:, cols]
        ssum = ug
        for k in range(1, w):
            ssum = ssum + ubuf[POOL_HALO - k:POOL_HALO - k + tm, cols]
        count = jnp.minimum(w, pos + 1).astype(F32)
        pooly_ref[:, cols] = _pool_group_out(ssum / count - ug, g, w_pool_ref, pscale_ref).astype(BF16)
    tail = ubuf[tm:tm + POOL_HALO, :]
    ubuf[0:POOL_HALO, :] = tail

    @pl.when(j == pl.num_programs(1) - 1)
    def _():
        pstate_ref[...] = tail


def _proj_sample_kernel(x_ref, state_ref, w_in_ref, qg_ref, w_uq_ref, kvg_ref, w_ukt_ref, w_pool_ref, pscale_ref,
                        cos_ref, sin_ref,
                        qcat_ref, kcat_ref, lat_ref, kr_ref, pooly_ref, pstate_ref, *, past_len):
    n_b = state_ref.shape[1]
    n_t = x_ref.shape[0] // n_b
    u, q_lat, q_rope, lat, k_rope = _project(
        x_ref[...].astype(BF16), w_in_ref, qg_ref, w_uq_ref, kvg_ref, w_ukt_ref, cos_ref[...], sin_ref[...],
        ATTN_SCALE)
    for hd in range(MLA_HEADS):
        qcat_ref[hd, :, 0:KV_RANK] = q_lat[hd].astype(BF16)
        qcat_ref[hd, :, KV_RANK:QK_PAD] = _compact_rope(q_rope[hd]).astype(BF16)
    lat_ref[...] = lat
    kc = _compact_rope(k_rope)
    kcat_ref[:, 0:KV_RANK] = lat.astype(BF16)
    kcat_ref[:, KV_RANK:QK_PAD] = _with_ones_lane(kc).astype(BF16)
    kr_ref[...] = kc[:, :QK_ROPE_DIM]

    def ext(jj):
        if jj < POOL_BUF:
            return state_ref[jj]
        return u[(jj - POOL_BUF) * n_b:(jj - POOL_BUF + 1) * n_b, :]

    for t in range(n_t):
        ut = ext(POOL_BUF + t)
        for g, w in enumerate(POOL_WINDOWS):
            cols = slice(POOL_GROUP_DIM * g, POOL_GROUP_DIM * (g + 1))
            ssum = ut[:, cols]
            for k in range(1, w):
                ssum = ssum + ext(POOL_BUF + t - k)[:, cols]
            count = float(min(w, past_len + t + 1))
            y = _pool_group_out(ssum / count - ut[:, cols], g, w_pool_ref, pscale_ref)
            pooly_ref[t * n_b:(t + 1) * n_b, cols] = y.astype(BF16)
    for jj in range(POOL_BUF):
        pstate_ref[jj] = ext(n_t + jj)


def _proj_weight_specs():
    return [
        _full((D_MODEL, D_MODEL)),
        _full((1, Q_RANK)),
        _full((Q_RANK, MLA_HEADS * QK_PAD)),
        _full((1, KV_RANK)),
        _full((MLA_HEADS, QK_NOPE_DIM, KV_RANK)),
        _full((len(POOL_WINDOWS), POOL_GROUP_DIM, POOL_GROUP_DIM)),
        _full((1, POOL_WIDTH)),
    ]


def _proj_prompt(x2d, weights, cos_t, sin_t, n_batch, seq):
    tm = TOKEN_BLOCK
    n = n_batch * seq
    nj = seq // tm
    tok = lambda b, j: (b * nj + j, 0)
    return pl.pallas_call(
        _proj_prompt_kernel,
        grid=(n_batch, nj),
        in_specs=[pl.BlockSpec((tm, D_MODEL), tok)] + _proj_weight_specs() + [
            pl.BlockSpec((tm, LANES), lambda b, j: (j, 0)),
            pl.BlockSpec((tm, LANES), lambda b, j: (j, 0)),
        ],
        out_specs=[
            pl.BlockSpec((MLA_HEADS, tm, QK_PAD), lambda b, j: (0, b * nj + j, 0)),
            pl.BlockSpec((tm, QK_PAD), tok),
            pl.BlockSpec((tm, KV_RANK), tok),
            pl.BlockSpec((tm, QK_ROPE_DIM), tok),
            pl.BlockSpec((tm, POOL_WIDTH), tok),
            pl.BlockSpec((None, POOL_HALO, POOL_WIDTH), lambda b, j: (b, 0, 0)),
        ],
        out_shape=[
            jax.ShapeDtypeStruct((MLA_HEADS, n, QK_PAD), BF16),
            jax.ShapeDtypeStruct((n, QK_PAD), BF16),
            jax.ShapeDtypeStruct((n, KV_RANK), F32),
            jax.ShapeDtypeStruct((n, QK_ROPE_DIM), F32),
            jax.ShapeDtypeStruct((n, POOL_WIDTH), BF16),
            jax.ShapeDtypeStruct((n_batch, POOL_HALO, POOL_WIDTH), F32),
        ],
        scratch_shapes=[pltpu.VMEM((POOL_HALO + tm, POOL_WIDTH), F32)],
        compiler_params=_params("arbitrary", "arbitrary"),
        name="proj_prompt",
    )(x2d, *weights, cos_t, sin_t)


def _proj_sample(x2d, state_t, weights, cos_t, sin_t, past_len):
    n = x2d.shape[0]
    n_b = state_t.shape[1]
    return pl.pallas_call(
        functools.partial(_proj_sample_kernel, past_len=past_len),
        grid=(1,),
        in_specs=[_full((n, D_MODEL)), _full((POOL_BUF, n_b, POOL_WIDTH))] + _proj_weight_specs() + [
            _full((n, LANES)), _full((n, LANES))],
        out_specs=[
            _full((MLA_HEADS, n, QK_PAD)), _full((n, QK_PAD)), _full((n, KV_RANK)), _full((n, QK_ROPE_DIM)),
            _full((n, POOL_WIDTH)), _full((POOL_BUF, n_b, POOL_WIDTH)),
        ],
        out_shape=[
            jax.ShapeDtypeStruct((MLA_HEADS, n, QK_PAD), BF16),
            jax.ShapeDtypeStruct((n, QK_PAD), BF16),
            jax.ShapeDtypeStruct((n, KV_RANK), F32),
            jax.ShapeDtypeStruct((n, QK_ROPE_DIM), F32),
            jax.ShapeDtypeStruct((n, POOL_WIDTH), BF16),
            jax.ShapeDtypeStruct((POOL_BUF, n_b, POOL_WIDTH), F32),
        ],
        compiler_params=_params("arbitrary"),
        name="proj_sample",
    )(x2d, state_t, *weights, cos_t, sin_t)


def _attn_block(q, k, mask, m_ref, acc_ref):
    s = _dot_nt(q, k)
    if mask is not None:
        s = jnp.where(mask, s, -jnp.inf)
    tiles = [s[:, c * LANES:(c + 1) * LANES] for c in range(s.shape[1] // LANES)]
    m_old = m_ref[...]
    m_new = jnp.maximum(m_old, jnp.max(functools.reduce(jnp.maximum, tiles), axis=1, keepdims=True))
    alpha = jnp.exp2(m_old - m_new)
    p = jnp.concatenate([jnp.exp2(t - m_new) for t in tiles], axis=1).astype(BF16)
    acc_ref[...] = jnp.concatenate([alpha, alpha], axis=1) * acc_ref[...] + _dot(p, k)
    m_ref[...] = m_new


def _attn_output(acc):
    return acc[:, :KV_RANK] / acc[:, ONES_LANE:ONES_LANE + 1]


def _attn_prompt_kernel(q_ref, k_ref, w_uv_ref, out_ref, m_ref, acc_ref):
    i = pl.program_id(1)
    tq = q_ref.shape[1]
    tk = ATTN_K_BLOCK
    hpg = MLA_HEADS // ATTN_GROUPS
    rows = hpg * tq
    m_ref[...] = jnp.full(m_ref.shape, -jnp.inf, F32)
    acc_ref[...] = jnp.zeros(acc_ref.shape, F32)

    def block(jk, masked):
        k = k_ref[pl.ds(pl.multiple_of(jk * tk, tk), tk), :]
        mask = None
        if masked:
            q_pos = i * tq + lax.broadcasted_iota(I32, (rows, tk), 0) % tq
            k_pos = jk * tk + lax.broadcasted_iota(I32, (rows, tk), 1)
            mask = k_pos <= q_pos
        for g in range(ATTN_GROUPS):
            q = q_ref[hpg * g:hpg * (g + 1)].reshape(rows, QK_PAD)
            _attn_block(q, k, mask, m_ref.at[g], acc_ref.at[g])

    n_full = (i * tq + 1) // tk
    n_kv = (i * tq + tq - 1) // tk + 1

    def full_pair(jp, c):
        block(2 * jp, False)
        block(2 * jp + 1, False)
        return c

    def diag_body(jk, c):
        block(jk, True)
        return c

    lax.fori_loop(0, n_full // 2, full_pair, 0)

    @pl.when(n_full % 2 == 1)
    def _():
        block(n_full - 1, False)

    lax.fori_loop(n_full, n_kv, diag_body, 0)

    for g in range(ATTN_GROUPS):
        o = _attn_output(acc_ref[g])
        for hl in range(hpg):
            hd = hpg * g + hl
            oh = o[hl * tq:(hl + 1) * tq, :].astype(BF16)
            out_ref[:, V_HEAD_DIM * hd:V_HEAD_DIM * (hd + 1)] = _dot(oh, w_uv_ref[hd]).astype(BF16)


def _attn_prompt(qcat, kcat, w_uv_h, n_batch, seq):
    tq = ATTN_Q_BLOCK
    nq = seq // tq
    n = n_batch * seq
    rows = (MLA_HEADS // ATTN_GROUPS) * tq
    return pl.pallas_call(
        _attn_prompt_kernel,
        grid=(n_batch, nq),
        in_specs=[
            pl.BlockSpec((MLA_HEADS, tq, QK_PAD), lambda b, i: (0, b * nq + i, 0)),
            pl.BlockSpec((seq, QK_PAD), lambda b, i: (b, 0)),
            _full((MLA_HEADS, KV_RANK, V_HEAD_DIM)),
        ],
        out_specs=pl.BlockSpec((tq, MLA_HEADS * V_HEAD_DIM), lambda b, i: (b * nq + i, 0)),
        out_shape=jax.ShapeDtypeStruct((n, MLA_HEADS * V_HEAD_DIM), BF16),
        scratch_shapes=[pltpu.VMEM((ATTN_GROUPS, rows, LANES), F32), pltpu.VMEM((ATTN_GROUPS, rows, QK_PAD), F32)],
        compiler_params=_params("arbitrary", "arbitrary"),
        name="attn_prompt",
    )(qcat, kcat, w_uv_h)


def _attn_sample_kernel(pt_ref, q_ref, knew_ref, w_uv_ref, lat_hbm, rope_hbm, out_ref,
                        lat_buf, rope_buf, m_ref, l_ref, acc_ref, res_ref, page_sem, *, n_pages_step, n_new, layer):
    s_idx = pl.program_id(1)
    n_steps = pl.num_programs(1)
    g = pl.program_id(0) * n_steps + s_idx
    slot = g % 2

    def fetch(step, dst):
        for p in range(n_pages_step):
            page = pt_ref[step * n_pages_step + p]
            pltpu.make_async_copy(lat_hbm.at[layer, page], lat_buf.at[dst, p], page_sem.at[dst]).start()
            pltpu.make_async_copy(rope_hbm.at[layer, page], rope_buf.at[dst, p], page_sem.at[dst]).start()

    @pl.when(g == 0)
    def _():
        fetch(0, 0)

    @pl.when(g + 1 < pl.num_programs(0) * n_steps)
    def _():
        fetch(g + 1, 1 - slot)

    pltpu.make_async_copy(lat_buf.at[slot], lat_buf.at[slot], page_sem.at[slot]).wait()
    pltpu.make_async_copy(rope_buf.at[slot], rope_buf.at[slot], page_sem.at[slot]).wait()

    @pl.when(s_idx == 0)
    def _():
        m_ref[...] = jnp.full(m_ref.shape, -jnp.inf, F32)
        l_ref[...] = jnp.zeros(l_ref.shape, F32)
        acc_ref[...] = jnp.zeros(acc_ref.shape, F32)

    def update(s_tiles, values):
        m_old = m_ref[...]
        m_new = jnp.maximum(m_old, jnp.max(functools.reduce(jnp.maximum, s_tiles), axis=1, keepdims=True))
        alpha = jnp.exp(m_old - m_new)
        p_tiles = [jnp.exp(t - m_new) for t in s_tiles]
        row_sum = jnp.sum(functools.reduce(jnp.add, p_tiles), axis=1, keepdims=True)
        pv = functools.reduce(jnp.add, [_dot(p.astype(BF16), v) for p, v in zip(p_tiles, values)])
        l_ref[...] = alpha * l_ref[...] + row_sum
        acc_ref[...] = alpha * acc_ref[...] + pv
        m_ref[...] = m_new

    q = q_ref[...]
    q_lat = q[:, :KV_RANK]
    q_rope = q[:, KV_RANK:KV_RANK + QK_ROPE_DIM]
    lats = [lat_buf[slot, p].astype(BF16) for p in range(n_pages_step)]
    update([_dot_nt(q_lat, lats[p]) + _dot(q_rope, rope_buf[slot, p].astype(BF16)) for p in range(n_pages_step)],
           lats)

    @pl.when(s_idx == pl.num_programs(1) - 1)
    def _():
        kn = knew_ref[...]
        q_t = lax.broadcasted_iota(I32, (q.shape[0], kn.shape[0]), 0) % n_new
        k_t = lax.broadcasted_iota(I32, (q.shape[0], kn.shape[0]), 1)
        update([jnp.where(k_t <= q_t, _dot_nt(q, kn), -jnp.inf)], [kn[:, :KV_RANK]])
        o = (acc_ref[...] / l_ref[...]).astype(BF16)
        for hd in range(MLA_HEADS):
            res_ref[...] = _dot(o, w_uv_ref[hd])
            out_ref[:, V_HEAD_DIM * hd:V_HEAD_DIM * (hd + 1)] = res_ref[hd * n_new:(hd + 1) * n_new, :]


def _attn_sample(page_table, q_b, knew_b, w_uv_h, cache_lat, cache_rope, layer):
    n_b, n_pages = page_table.shape
    page = cache_lat.shape[2]
    pps = min(PAGES_PER_STEP, n_pages)
    assert n_pages % pps == 0
    n_steps = n_pages // pps
    rows = q_b.shape[1]
    n_new = rows // MLA_HEADS
    t_pad = knew_b.shape[1]

    anyspec = pl.BlockSpec(memory_space=pl.ANY)
    grid_spec = pltpu.PrefetchScalarGridSpec(
        num_scalar_prefetch=1,
        grid=(n_b, n_steps),
        in_specs=[
            pl.BlockSpec((None, rows, QK_PAD), lambda b, s, pt: (b, 0, 0)),
            pl.BlockSpec((None, t_pad, QK_PAD), lambda b, s, pt: (b, 0, 0)),
            pl.BlockSpec((MLA_HEADS, KV_RANK, V_HEAD_DIM), lambda b, s, pt: (0, 0, 0)),
            anyspec, anyspec,
        ],
        out_specs=pl.BlockSpec((None, n_new, MLA_HEADS * V_HEAD_DIM), lambda b, s, pt: (b, 0, 0)),
        scratch_shapes=[
            pltpu.VMEM((2, pps, page, KV_RANK), F32), pltpu.VMEM((2, pps, QK_ROPE_DIM, page), F32),
            pltpu.VMEM((rows, LANES), F32), pltpu.VMEM((rows, LANES), F32), pltpu.VMEM((rows, KV_RANK), F32),
            pltpu.VMEM((rows, V_HEAD_DIM), F32), pltpu.SemaphoreType.DMA((2,)),
        ],
    )
    return pl.pallas_call(
        functools.partial(_attn_sample_kernel, n_pages_step=pps, n_new=n_new, layer=layer),
        grid_spec=grid_spec,
        out_shape=jax.ShapeDtypeStruct((n_b, n_new, MLA_HEADS * V_HEAD_DIM), F32),
        compiler_params=_params("arbitrary", "arbitrary"),
        name="attn_sample",
    )(page_table.reshape(-1), q_b, knew_b, w_uv_h, cache_lat, cache_rope)


def _mem_kv_kernel(mem_ref, w_mk_ref, w_mv_ref, k_ref, v_ref, kb_ref, vb_ref):
    m = mem_ref[...].astype(BF16)
    k = _dot(m, w_mk_ref[...])
    v = _dot(m, w_mv_ref[...])
    k_ref[...] = k
    v_ref[...] = v
    kb_ref[...] = k.astype(BF16)
    vb_ref[...] = v.astype(BF16)


def _mem_kv(mem2d, w_mk, w_mv, n_batch, n_mem):
    blk = pl.BlockSpec((n_mem, MEM_WIDTH), lambda b: (b, 0))
    n = n_batch * n_mem
    return pl.pallas_call(
        _mem_kv_kernel,
        grid=(n_batch,),
        in_specs=[pl.BlockSpec((n_mem, D_MODEL), lambda b: (b, 0)), _full((D_MODEL, MEM_WIDTH)),
                  _full((D_MODEL, MEM_WIDTH))],
        out_specs=[blk, blk, blk, blk],
        out_shape=[jax.ShapeDtypeStruct((n, MEM_WIDTH), F32), jax.ShapeDtypeStruct((n, MEM_WIDTH), F32),
                   jax.ShapeDtypeStruct((n, MEM_WIDTH), BF16), jax.ShapeDtypeStruct((n, MEM_WIDTH), BF16)],
        compiler_params=_params("arbitrary"),
        name="mem_kv",
    )(mem2d, w_mk, w_mv)


def _mix_ln1(x, pool_y, mla_y, w_out_a_ref, w_out_b_ref, g_ref, b_ref):
    mix = _dot(pool_y, w_out_a_ref[...]) + _dot(mla_y, w_out_b_ref[...])
    return _layer_norm(DEEPNORM_ALPHA * x + mix, g_ref[...], b_ref[...])


def _route(x2, w_r_ref, b_r_ref, carry):
    m = x2.shape[0]
    x_hi = x2.astype(BF16)
    x_lo = (x2 - x_hi.astype(F32)).astype(BF16)
    logits_t = _dot_nt(w_r_ref[0], x_hi) + (_dot_nt(w_r_ref[0], x_lo) + _dot_nt(w_r_ref[1], x_hi))
    logits = logits_t.T + b_r_ref[...]
    e_idx = lax.broadcasted_iota(I32, (m, N_EXPERTS), 1).astype(F32)
    work = logits
    vals, picks = [], []
    for _ in range(TOP_K):
        mx = jnp.max(work, axis=1, keepdims=True)
        pick = jnp.min(jnp.where(work == mx, e_idx, float(N_EXPERTS)), axis=1, keepdims=True)
        vals.append(mx)
        picks.append(pick)
        work = jnp.where(e_idx == pick, -jnp.inf, work)
    hot = jnp.where(work == -jnp.inf, 1.0, 0.0)
    exps = [jnp.exp(v - vals[0]) for v in vals]
    denom = exps[0] + exps[1] + exps[2] + exps[3]
    r_i = lax.broadcasted_iota(I32, (m, m), 0)
    c_i = lax.broadcasted_iota(I32, (m, m), 1)
    strict_lower = jnp.where(c_i < r_i, 1.0, 0.0).astype(BF16)
    before = _dot(strict_lower, hot.astype(BF16)) + carry
    lane4 = lax.broadcasted_iota(I32, (m, TOP_K), 1)
    idx = jnp.zeros((m, TOP_K), I32)
    gates = jnp.zeros((m, TOP_K), F32)
    rank = jnp.zeros((m, TOP_K), I32)
    for k in range(TOP_K):
        rk = jnp.sum(jnp.where(e_idx == picks[k], before, 0.0), axis=1, keepdims=True).astype(I32)
        idx = jnp.where(lane4 == k, picks[k].astype(I32), idx)
        gates = jnp.where(lane4 == k, exps[k] / denom, gates)
        rank = jnp.where(lane4 == k, rk, rank)
    return idx, gates, rank, carry + jnp.sum(hot, axis=0, keepdims=True)


def _mo_ln2_route(x1, o_bf, w_mo_ref, g_ref, b_ref, w_r_ref, b_r_ref, carry):
    x2 = _layer_norm(DEEPNORM_ALPHA * x1 + _dot(o_bf, w_mo_ref[...]), g_ref[...], b_ref[...])
    return (x2,) + _route(x2, w_r_ref, b_r_ref, carry)


def _post_prompt_kernel(x_ref, pooly_ref, mlay_ref, mk_ref, mv_ref,
                        w_out_a_ref, w_out_b_ref, g1_ref, b1_ref, w_mq_ref, w_mo_ref, g2_ref, b2_ref,
                        w_r_ref, b_r_ref,
                        x2_ref, idx_ref, gate_ref, rank_ref, cnt_ref, carry_ref):
    first = jnp.logical_and(pl.program_id(0) == 0, pl.program_id(1) == 0)

    @pl.when(first)
    def _():
        carry_ref[...] = jnp.zeros(carry_ref.shape, F32)

    x1 = _mix_ln1(x_ref[...], pooly_ref[...], mlay_ref[...], w_out_a_ref, w_out_b_ref, g1_ref, b1_ref)
    qm = (_dot(x1.astype(BF16), w_mq_ref[...]) * MEM_SCALE).astype(BF16)
    outs = []
    for hd in range(MEM_HEADS):
        cols = slice(MEM_HEAD_DIM * hd, MEM_HEAD_DIM * (hd + 1))
        s = _dot_nt(qm[:, cols], mk_ref[:, cols])
        p = jnp.exp(s - jnp.max(s, axis=1, keepdims=True))
        o = _dot(p.astype(BF16), mv_ref[:, cols]) / jnp.sum(p, axis=1, keepdims=True)
        outs.append(o.astype(BF16))
    o_all = jnp.concatenate(outs, axis=1)
    x2, idx, gates, rank, carry = _mo_ln2_route(x1, o_all, w_mo_ref, g2_ref, b2_ref, w_r_ref, b_r_ref,
                                                carry_ref[...])
    _store_row_tiles(x2_ref, (), x2)
    idx_ref[...] = idx
    gate_ref[...] = gates
    rank_ref[...] = rank
    carry_ref[...] = carry
    cnt_ref[...] = carry


def _post_weight_specs():
    half = (POOL_WIDTH, D_MODEL)
    return [
        _full(half), _full(half), _full((1, D_MODEL)), _full((1, D_MODEL)),
        _full((D_MODEL, MEM_WIDTH)), _full((MEM_WIDTH, D_MODEL)), _full((1, D_MODEL)), _full((1, D_MODEL)),
        _full((2, N_EXPERTS, D_MODEL)), _full((1, N_EXPERTS)),
    ]


def _post_prompt(x2d, pool_y, mla_y, mk_b, mv_b, weights, n_batch, seq, n_mem, n_all):
    tm = TOKEN_BLOCK
    nj = seq // tm
    tok = lambda b, j: (b * nj + j, 0)
    return pl.pallas_call(
        _post_prompt_kernel,
        grid=(n_batch, nj),
        in_specs=[
            pl.BlockSpec((tm, D_MODEL), tok), pl.BlockSpec((tm, POOL_WIDTH), tok), pl.BlockSpec((tm, POOL_WIDTH), tok),
            pl.BlockSpec((n_mem, MEM_WIDTH), lambda b, j: (b, 0)), pl.BlockSpec((n_mem, MEM_WIDTH), lambda b, j: (b, 0)),
        ] + _post_weight_specs(),
        out_specs=[
            pl.BlockSpec((tm * ROW_TILE, LANES), tok), pl.BlockSpec((tm, TOP_K), tok), pl.BlockSpec((tm, TOP_K), tok),
            pl.BlockSpec((tm, TOP_K), tok), _full((1, N_EXPERTS)),
        ],
        out_shape=[
            jax.ShapeDtypeStruct((n_all * ROW_TILE, LANES), F32), jax.ShapeDtypeStruct((n_all, TOP_K), I32),
            jax.ShapeDtypeStruct((n_all, TOP_K), F32), jax.ShapeDtypeStruct((n_all, TOP_K), I32),
            jax.ShapeDtypeStruct((1, N_EXPERTS), F32),
        ],
        scratch_shapes=[pltpu.VMEM((1, N_EXPERTS), F32)],
        compiler_params=_params("arbitrary", "arbitrary"),
        name="post_prompt",
    )(x2d, pool_y, mla_y, mk_b, mv_b, *weights)


def _mix_sample_kernel(x_ref, pooly_ref, mlay_ref, w_out_a_ref, w_out_b_ref, g1_ref, b1_ref, w_mq_ref,
                       x1_ref, qm_ref):
    x1 = _mix_ln1(x_ref[...], pooly_ref[...], mlay_ref[...], w_out_a_ref, w_out_b_ref, g1_ref, b1_ref)
    x1_ref[...] = x1
    qm_ref[...] = (_dot(x1.astype(BF16), w_mq_ref[...]) * MEM_SCALE).astype(BF16)


def _mix_sample(x2d, pool_y, mla_y, weights):
    n = x2d.shape[0]
    w_out_a, w_out_b, g1, b1, w_mq = weights[:5]
    return pl.pallas_call(
        _mix_sample_kernel,
        grid=(1,),
        in_specs=[_full((n, D_MODEL)), _full((n, POOL_WIDTH)), _full((n, POOL_WIDTH))] + _post_weight_specs()[:5],
        out_specs=[_full((n, D_MODEL)), _full((n, MEM_WIDTH))],
        out_shape=[jax.ShapeDtypeStruct((n, D_MODEL), F32), jax.ShapeDtypeStruct((n, MEM_WIDTH), BF16)],
        compiler_params=_params("arbitrary"),
        name="mix_sample",
    )(x2d, pool_y, mla_y, w_out_a, w_out_b, g1, b1, w_mq)


def _mem_attn_sample_kernel(q_ref, k_ref, v_ref, out_ref, res_ref, *, n_new):
    n_mem = k_ref.shape[1] // MEM_HEADS
    rows = q_ref.shape[1]
    row_head = lax.broadcasted_iota(I32, (rows, 1), 0) // n_new
    for b in range(q_ref.shape[0]):
        q = q_ref[b]
        s = jnp.zeros((rows, n_mem), F32)
        for hd in range(MEM_HEADS):
            k_h = k_ref[b, pl.ds(hd, n_mem, stride=MEM_HEADS), :].astype(BF16)
            s = jnp.where(row_head == hd, _dot_nt(q, k_h), s)
        p = jnp.exp(s - jnp.max(s, axis=1, keepdims=True))
        p_bf = p.astype(BF16)
        o = jnp.zeros((rows, MEM_HEAD_DIM), F32)
        for hd in range(MEM_HEADS):
            v_h = v_ref[b, pl.ds(hd, n_mem, stride=MEM_HEADS), :].astype(BF16)
            o = jnp.where(row_head == hd, _dot(p_bf, v_h), o)
        res_ref[...] = o / jnp.sum(p, axis=1, keepdims=True)
        for hd in range(MEM_HEADS):
            cols = slice(MEM_HEAD_DIM * hd, MEM_HEAD_DIM * (hd + 1))
            out_ref[b, :, cols] = res_ref[hd * n_new:(hd + 1) * n_new, :].astype(BF16)


def _mem_attn_sample(q_b, mem_k, mem_v, n_new):
    n_b, rows, _ = q_b.shape
    mh = mem_k.shape[1]
    gb = min(MEM_BATCH_BLOCK, n_b)
    assert n_b % gb == 0
    blk = lambda i: (i, 0, 0)
    return pl.pallas_call(
        functools.partial(_mem_attn_sample_kernel, n_new=n_new),
        grid=(n_b // gb,),
        in_specs=[pl.BlockSpec((gb, rows, MEM_HEAD_DIM), blk), pl.BlockSpec((gb, mh, MEM_HEAD_DIM), blk),
                  pl.BlockSpec((gb, mh, MEM_HEAD_DIM), blk)],
        out_specs=pl.BlockSpec((gb, n_new, MEM_WIDTH), blk),
        out_shape=jax.ShapeDtypeStruct((n_b, n_new, MEM_WIDTH), BF16),
        scratch_shapes=[pltpu.VMEM((rows, MEM_HEAD_DIM), F32)],
        compiler_params=_params("arbitrary"),
        name="mem_attn_sample",
    )(q_b, mem_k, mem_v)


def _route_sample_kernel(x1_ref, o_ref, cnt_in_ref, w_mo_ref, g2_ref, b2_ref, w_r_ref, b_r_ref,
                         x2_in, idx_in, gate_in, rank_in,
                         x2_ref, idx_ref, gate_ref, rank_ref, cnt_ref):
    del x2_in, idx_in, gate_in, rank_in
    x2, idx, gates, rank, carry = _mo_ln2_route(x1_ref[...], o_ref[...], w_mo_ref, g2_ref, b2_ref, w_r_ref, b_r_ref,
                                                cnt_in_ref[...])
    _store_row_tiles(x2_ref, (), x2)
    idx_ref[...] = idx
    gate_ref[...] = gates
    rank_ref[...] = rank
    cnt_ref[...] = carry


def _route_sample(x1, o_bf, cnt_in, weights, x2_all, idx_all, gate_all, rank_all, n_prompt):
    n = x1.shape[0]
    assert n_prompt % n == 0
    tail = lambda i: (n_prompt // n, 0)
    anyspec = pl.BlockSpec(memory_space=pl.ANY)
    w_mo, g2, b2, w_r, b_r = weights[5:]
    return pl.pallas_call(
        _route_sample_kernel,
        grid=(1,),
        in_specs=[_full((n, D_MODEL)), _full((n, MEM_WIDTH)), _full((1, N_EXPERTS))] + _post_weight_specs()[5:] + [
            anyspec, anyspec, anyspec, anyspec],
        out_specs=[pl.BlockSpec((n * ROW_TILE, LANES), tail), pl.BlockSpec((n, TOP_K), tail), pl.BlockSpec((n, TOP_K), tail),
                   pl.BlockSpec((n, TOP_K), tail), _full((1, N_EXPERTS))],
        out_shape=[jax.ShapeDtypeStruct(x2_all.shape, F32), jax.ShapeDtypeStruct(idx_all.shape, I32),
                   jax.ShapeDtypeStruct(gate_all.shape, F32), jax.ShapeDtypeStruct(rank_all.shape, I32),
                   jax.ShapeDtypeStruct((1, N_EXPERTS), F32)],
        input_output_aliases={8: 0, 9: 1, 10: 2, 11: 3},
        compiler_params=_params("arbitrary"),
        name="route_sample",
    )(x1, o_bf, cnt_in, w_mo, g2, b2, w_r, b_r, x2_all, idx_all, gate_all, rank_all)


def _dispatch_kernel(pad_start_ref, n_pad_ref, n_used_ref, idx_hbm, x_ref, xrows_hbm,
                     idx_smem, zbuf, idx_sem, row_sem, pad_sem):
    i = pl.program_id(0)
    last = pl.num_programs(0) - 1
    tm = x_ref.shape[0] // ROW_TILE
    blk = zbuf.shape[0]
    n_blk = xrows_hbm.shape[0] // blk

    def idx_copy(row, s):
        return pltpu.make_async_copy(idx_hbm.at[row], idx_smem.at[s], idx_sem.at[s])

    def pad_copy(e, r):
        dst = pl.multiple_of((pad_start_ref[e] + r) * ROW_TILE, ROW_TILE)
        return pltpu.make_async_copy(zbuf.at[pl.ds(0, ROW_TILE)], xrows_hbm.at[pl.ds(dst, ROW_TILE)], pad_sem)

    def tail_copy(j):
        return pltpu.make_async_copy(zbuf, xrows_hbm.at[pl.ds(pl.multiple_of(j * blk, blk), blk)], pad_sem)

    @pl.when(i == 0)
    def _():
        idx_copy(0, 0).start()
        zbuf[...] = jnp.zeros(zbuf.shape, F32)
        for e in range(N_EXPERTS):
            def start(r, c, e=e):
                pad_copy(e, r).start()
                return c

            lax.fori_loop(0, n_pad_ref[e], start, 0)

        def start_tail(j, c):
            tail_copy(j).start()
            return c

        lax.fori_loop(n_used_ref[0], n_blk, start_tail, 0)
        for e in range(N_EXPERTS):
            def wait(r, c, e=e):
                pad_copy(e, r).wait()
                return c

            lax.fori_loop(0, n_pad_ref[e], wait, 0)

        def wait_tail(j, c):
            tail_copy(j).wait()
            return c

        lax.fori_loop(n_used_ref[0], n_blk, wait_tail, 0)

    def step(slot):
        idx_copy(i, slot).wait()

        @pl.when(i < last)
        def _():
            idx_copy(i + 1, 1 - slot).start()

        for t in range(tm):
            for k in range(TOP_K):
                dst = pl.multiple_of(idx_smem[slot, TOP_K * t + k], ROW_TILE)
                pltpu.make_async_copy(x_ref.at[pl.ds(ROW_TILE * t, ROW_TILE)], xrows_hbm.at[pl.ds(dst, ROW_TILE)],
                                      row_sem).start(priority=k % 2)
        for k in range(TOP_K):
            pltpu.make_async_copy(x_ref, x_ref, row_sem).wait()

    for parity in range(2):
        pl.when(i % 2 == parity)(functools.partial(step, parity))


def _dispatch(pad_start, n_pad, n_used, idx_rows, x_all, n_rows):
    tm = DISPATCH_BLOCK
    n_steps = idx_rows.shape[0]
    assert x_all.shape[0] == n_steps * tm * ROW_TILE and idx_rows.shape[1] == TOP_K * tm
    assert n_rows % EXPERT_ROWS == 0
    grid_spec = pltpu.PrefetchScalarGridSpec(
        num_scalar_prefetch=3,
        grid=(n_steps,),
        in_specs=[pl.BlockSpec(memory_space=pl.ANY),
                  pl.BlockSpec((tm * ROW_TILE, LANES), lambda i, ps, npd, nu: (i, 0))],
        out_specs=pl.BlockSpec(memory_space=pl.ANY),
        scratch_shapes=[pltpu.SMEM((2, TOP_K * tm), I32), pltpu.VMEM((EXPERT_ROWS * ROW_TILE, LANES), F32),
                        pltpu.SemaphoreType.DMA((2,)), pltpu.SemaphoreType.DMA, pltpu.SemaphoreType.DMA],
    )
    return pl.pallas_call(
        _dispatch_kernel,
        grid_spec=grid_spec,
        out_shape=jax.ShapeDtypeStruct((n_rows * ROW_TILE, LANES), F32),
        compiler_params=_params("arbitrary"),
        name="moe_dispatch",
    )(pad_start, n_pad, n_used, idx_rows, x_all)


def _moe_kernel(blk_exp_ref, blk_row_ref, idx_hbm, x_ref, wgu_f32_ref, bgu_ref, wd_f32_ref, bd_ref, y_hbm,
                idx_smem, ybuf, wgu_ref, wd_ref, idx_sem, y_sem):
    del blk_row_ref
    i = pl.program_id(0)
    last = pl.num_programs(0) - 1
    tr = x_ref.shape[0] // ROW_TILE

    @pl.when(jnp.logical_or(i == 0, blk_exp_ref[i] != blk_exp_ref[jnp.maximum(i - 1, 0)]))
    def _():
        wgu_ref[...] = wgu_f32_ref[...].astype(BF16)
        wd_ref[...] = wd_f32_ref[...].astype(BF16)

    n_chunks = D_EXPERT // EXPERT_CHUNK

    def idx_copy(row, s):
        return pltpu.make_async_copy(idx_hbm.at[row], idx_smem.at[s], idx_sem.at[s])

    def scatter(s_idx, ys, rows):
        for r in rows:
            dst = pl.multiple_of(idx_smem[s_idx, r], ROW_TILE)
            pltpu.make_async_copy(ybuf.at[ys, pl.ds(ROW_TILE * r, ROW_TILE)], y_hbm.at[pl.ds(dst, ROW_TILE)],
                                  y_sem.at[ys]).start(priority=r % 2)

    def wait_rows(s):
        pltpu.make_async_copy(ybuf.at[s], ybuf.at[s], y_sem.at[s]).wait()

    @pl.when(i == 0)
    def _():
        idx_copy(0, 0).start()
        ybuf[1] = jnp.zeros(ybuf.shape[1:], F32)

    def step(slot):
        nxt = 1 - slot
        idx_copy(i, slot).wait()
        idx_copy(i + 1, nxt).start()
        xb = _load_row_tiles(x_ref, (), tr).astype(BF16)
        acc = jnp.zeros((tr, D_MODEL), F32) + bd_ref[...]

        groups = 3 * n_chunks
        per_group = -(-tr // groups)

        def issue(g):
            scatter(slot, nxt, range(g * per_group, min((g + 1) * per_group, tr)))

        for c in range(n_chunks):
            cg = slice(c * EXPERT_CHUNK, (c + 1) * EXPERT_CHUNK)
            cu = slice(D_EXPERT + c * EXPERT_CHUNK, D_EXPERT + (c + 1) * EXPERT_CHUNK)
            gate = jnp.minimum(_dot(xb, wgu_ref[:, cg]) + bgu_ref[:, cg], SWIGLU_LIMIT)
            issue(3 * c)
            up = jnp.clip(_dot(xb, wgu_ref[:, cu]) + bgu_ref[:, cu], -SWIGLU_LIMIT, SWIGLU_LIMIT)
            issue(3 * c + 1)
            act = gate * (1.0 / (1.0 + jnp.exp(-SWIGLU_ALPHA * gate))) * (up + 1.0)
            acc = acc + _dot(act.astype(BF16), wd_ref[cg, :])
            issue(3 * c + 2)

        @pl.when(i >= 1)
        def _():
            wait_rows(slot)

        _store_row_tiles(ybuf, (slot,), acc)

        @pl.when(i == last)
        def _():
            idx_copy(i + 1, nxt).wait()
            wait_rows(nxt)

    for parity in range(2):
        pl.when(i % 2 == parity)(functools.partial(step, parity))


def _moe(blk_exp, blk_row, idx_rows, x_rows, w_gu, b_gu, w_d, b_d, n_assign):
    tr = EXPERT_ROWS
    n_steps = blk_exp.shape[0]
    assert n_steps >= 2 and idx_rows.shape == (n_steps + 1, tr)
    anyspec = pl.BlockSpec(memory_space=pl.ANY)
    emap3 = lambda i, be, br: (be[i], 0, 0)
    grid_spec = pltpu.PrefetchScalarGridSpec(
        num_scalar_prefetch=2,
        grid=(n_steps,),
        in_specs=[
            anyspec, pl.BlockSpec((tr * ROW_TILE, LANES), lambda i, be, br: (br[i], 0)),
            pl.BlockSpec((None, D_MODEL, 2 * D_EXPERT), emap3), pl.BlockSpec((None, 1, 2 * D_EXPERT), emap3),
            pl.BlockSpec((None, D_EXPERT, D_MODEL), emap3), pl.BlockSpec((None, 1, D_MODEL), emap3),
        ],
        out_specs=anyspec,
        scratch_shapes=[
            pltpu.SMEM((2, tr), I32), pltpu.VMEM((2, tr * ROW_TILE, LANES), F32),
            pltpu.VMEM((D_MODEL, 2 * D_EXPERT), BF16), pltpu.VMEM((D_EXPERT, D_MODEL), BF16),
            pltpu.SemaphoreType.DMA((2,)), pltpu.SemaphoreType.DMA((2,)),
        ],
    )
    return pl.pallas_call(
        _moe_kernel,
        grid_spec=grid_spec,
        out_shape=jax.ShapeDtypeStruct(((n_assign + 2 * tr) * ROW_TILE, LANES), F32),
        compiler_params=_params("arbitrary"),
        name="moe_experts",
    )(blk_exp, blk_row, idx_rows, x_rows, w_gu, b_gu, w_d, b_d)


def _combine_kernel(x2_ref, gate_ref, g3_ref, b3_ref, *rest):
    y_refs, out_ref = rest[:TOP_K], rest[TOP_K]
    gates = gate_ref[...]
    tm = out_ref.shape[0]
    moe = jnp.zeros(out_ref.shape, F32)
    for k in range(TOP_K):
        moe = moe + gates[:, k:k + 1] * _load_row_tiles(y_refs[k], (), tm)
    out_ref[...] = _layer_norm(DEEPNORM_ALPHA * _load_row_tiles(x2_ref, (), tm) + moe, g3_ref[...], b3_ref[...])


def _combine(x2_all, y_tok, gates_all, g3, b3, row0, n_rows, tm):
    n_all = gates_all.shape[0]
    assert row0 % tm == 0 and n_rows % tm == 0 and n_all % tm == 0
    off = row0 // tm
    tok = lambda i: (off + i, 0)
    tiles = (tm * ROW_TILE, LANES)
    slot_specs = [pl.BlockSpec(tiles, functools.partial(lambda i, k: (k * (n_all // tm) + off + i, 0), k=k))
                  for k in range(TOP_K)]
    return pl.pallas_call(
        _combine_kernel,
        grid=(n_rows // tm,),
        in_specs=[pl.BlockSpec(tiles, tok), pl.BlockSpec((tm, TOP_K), tok), _full((1, D_MODEL)),
                  _full((1, D_MODEL))] + slot_specs,
        out_specs=pl.BlockSpec((tm, D_MODEL), lambda i: (i, 0)),
        out_shape=jax.ShapeDtypeStruct((n_rows, D_MODEL), F32),
        compiler_params=_params("arbitrary"),
        name="combine_ln3",
    )(x2_all, gates_all, g3, b3, *([y_tok] * TOP_K))


def _rope_tables(pos):
    inv_freq = ROPE_THETA ** (-jnp.arange(ROPE_HALF, dtype=F32) / ROPE_HALF)
    ang = pos.astype(F32)[:, None] * inv_freq[None, :]
    c, s, z = jnp.cos(ang), jnp.sin(ang), jnp.zeros_like(ang)
    return jnp.concatenate([c, z, c, z], axis=1), jnp.concatenate([-s, z, s, z], axis=1)


def _pad_rope_cols(w):
    z = jnp.zeros(w.shape[:-1] + (ROPE_HALF,), w.dtype)
    return jnp.concatenate([w[..., :ROPE_HALF], z, w[..., ROPE_HALF:], z], axis=-1)


def _layer_weights(l, w_in, w_pool, pool_scale, q_norm_g, w_uq, kv_norm_g, w_uk, w_uv, w_out, ln1_g, ln1_b,
                   w_mq, w_mo, ln2_g, ln2_b, w_router, b_router):
    o3 = POOL_WIDTH + Q_RANK + KV_RANK
    w_in_p = jnp.concatenate([w_in[l][:, :o3], _pad_rope_cols(w_in[l][:, o3:])], axis=1).astype(BF16)
    uq = w_uq[l].reshape(Q_RANK, MLA_HEADS, QK_NOPE_DIM + QK_ROPE_DIM)
    uq_p = jnp.concatenate([uq[..., :QK_NOPE_DIM], _pad_rope_cols(uq[..., QK_NOPE_DIM:])], axis=-1)
    uq_p = uq_p.reshape(Q_RANK, MLA_HEADS * QK_PAD).astype(BF16)
    w_ukt = jnp.transpose(w_uk[l], (1, 2, 0)).astype(BF16)
    w_uv_h = jnp.transpose(w_uv[l], (1, 0, 2)).astype(BF16)
    row = lambda v: v[l].reshape(1, -1)
    w_rt = w_router[l].T
    w_rt_hi = w_rt.astype(BF16)
    w_r_split = jnp.stack([w_rt_hi, (w_rt - w_rt_hi.astype(F32)).astype(BF16)])
    proj =[w_in_p, row(q_norm_g), uq_p, row(kv_norm_g), w_ukt, w_pool[l].astype(BF16), row(pool_scale)]
    post = [w_out[l][:POOL_WIDTH].astype(BF16), w_out[l][POOL_WIDTH:].astype(BF16), row(ln1_g), row(ln1_b),
            w_mq[l].astype(BF16), w_mo[l].astype(BF16), row(ln2_g), row(ln2_b), w_r_split, row(b_router)]
    return proj, w_uv_h, post


def _routing_tables(idx_all, rank_all, counts, n_blocks):
    tr = EXPERT_ROWS
    cnt = counts.reshape(-1).astype(I32)
    padded = ((cnt + tr - 1) // tr) * tr
    pend = jnp.cumsum(padded)
    pstart = pend - padded
    dest2d = pstart[idx_all] + rank_all
    dest = dest2d.reshape(-1)
    n_all = idx_all.shape[0]
    m = dest.shape[0]
    n_rows = n_blocks * tr
    disp_rows = (dest2d * ROW_TILE).reshape(n_all // DISPATCH_BLOCK, DISPATCH_BLOCK * TOP_K)
    assign = (jnp.arange(TOP_K, dtype=I32)[None, :] * n_all + jnp.arange(n_all, dtype=I32)[:, None]).reshape(-1)
    _, packed = lax.sort_key_val(dest, assign)
    start = jnp.cumsum(cnt) - cnt
    shift = pstart - start
    none = jnp.full((n_rows,), -1, I32)
    doubled = jnp.concatenate([none, packed, none[:n_rows - m]])
    rows = jnp.arange(n_rows, dtype=I32)
    row_src = none
    for e in range(N_EXPERTS):
        moved = lax.dynamic_slice(doubled, (n_rows - shift[e],), (n_rows,))
        row_src = jnp.where((rows >= pstart[e]) & (rows < pstart[e] + cnt[e]), moved, row_src)
    spare = m + jnp.arange(n_rows, dtype=I32) % (2 * tr)
    dst_rows = (jnp.where(row_src >= 0, row_src, spare) * ROW_TILE).reshape(n_blocks, tr)
    spare_row = spare[:tr][None] * ROW_TILE
    idx_rows = jnp.concatenate([spare_row, dst_rows, spare_row], axis=0)
    n_used = pend[-1] // tr
    blk_row = jnp.minimum(jnp.arange(n_blocks + 1, dtype=I32), n_used - 1)
    blk_exp = jnp.sum((pend[None, :] <= (blk_row * tr)[:, None]).astype(I32), axis=1)
    return (jnp.clip(blk_exp, 0, N_EXPERTS - 1), blk_row, idx_rows, disp_rows, pstart + cnt, padded - cnt,
            n_used.astype(I32).reshape(1))


def kernel(x_prompt, x_sample, mem_prompt, cache_kv_latent, cache_k_rope, cache_mem_k, cache_mem_v, state_pool,
           page_table, w_in, w_pool, pool_scale, q_norm_g, w_uq, kv_norm_g, w_uk, w_uv, w_out, ln1_g, ln1_b,
           w_mq, w_mk, w_mv, w_mo, ln2_g, ln2_b, w_router, b_router, w_gate_up, b_gate_up, w_down, b_down,
           ln3_g, ln3_b):
    n_bp, seq, _ = x_prompt.shape
    n_bs, n_new, _ = x_sample.shape
    n_mem = mem_prompt.shape[1]
    page = cache_kv_latent.shape[2]
    past_len = page_table.shape[1] * page
    n_p = n_bp * seq
    n_s = n_bs * n_new
    n_all = n_p + n_s
    n_assign = n_all * TOP_K
    assert w_in.shape[0] == DEPTH == 1
    assert seq % TOKEN_BLOCK == 0 and seq % ATTN_K_BLOCK == 0 and seq % ATTN_Q_BLOCK == 0 and seq >= POOL_HALO
    assert n_p % n_s == 0 and (n_s % TOKEN_BLOCK == 0 or TOKEN_BLOCK % n_s == 0)
    l = 0

    proj_w, w_uv_h, post_w = _layer_weights(l, w_in, w_pool, pool_scale, q_norm_g, w_uq, kv_norm_g, w_uk, w_uv,
                                            w_out, ln1_g, ln1_b, w_mq, w_mo, ln2_g, ln2_b, w_router, b_router)

    xp = x_prompt.reshape(n_p, D_MODEL)
    cos_p, sin_p = _rope_tables(jnp.arange(seq))
    qcat, kcat, lat_p, kr_p, pooly_p, pstate_p = _proj_prompt(xp, proj_w, cos_p, sin_p, n_bp, seq)
    mlay_p = _attn_prompt(qcat, kcat, w_uv_h, n_bp, seq)
    mk, mv, mk_b, mv_b = _mem_kv(mem_prompt.reshape(n_bp * n_mem, D_MODEL), w_mk[l].astype(BF16),
                                 w_mv[l].astype(BF16), n_bp, n_mem)
    x2_all, idx_all, gate_all, rank_all, cnt_p = _post_prompt(xp, pooly_p, mlay_p, mk_b, mv_b, post_w,
                                                              n_bp, seq, n_mem, n_all)

    xs = jnp.transpose(x_sample, (1, 0, 2)).reshape(n_s, D_MODEL)
    state_t = jnp.transpose(state_pool[l], (1, 0, 2))
    cos_s, sin_s = _rope_tables(jnp.repeat(past_len + jnp.arange(n_new), n_bs))
    qcat_s, kcat_s, lat_s, kr_s, pooly_s, pstate_s = _proj_sample(xs, state_t, proj_w, cos_s, sin_s, past_len)
    q_b = jnp.transpose(qcat_s.reshape(MLA_HEADS, n_new, n_bs, QK_PAD), (2, 0, 1, 3)).reshape(
        n_bs, MLA_HEADS * n_new, QK_PAD)
    knew_b = jnp.transpose(kcat_s.reshape(n_new, n_bs, QK_PAD), (1, 0, 2))
    knew_b = jnp.pad(knew_b, ((0, 0), (0, LANES - n_new), (0, 0)))
    mlay_b = _attn_sample(page_table, q_b, knew_b, w_uv_h, cache_kv_latent,
                          jnp.transpose(cache_k_rope, (0, 1, 3, 2)), l)
    mlay_s = jnp.transpose(mlay_b, (1, 0, 2)).reshape(n_s, POOL_WIDTH).astype(BF16)
    x1_s, qm_s = _mix_sample(xs, pooly_s, mlay_s, post_w)
    qm_b = jnp.transpose(qm_s.reshape(n_new, n_bs, MEM_HEADS, MEM_HEAD_DIM), (1, 2, 0, 3)).reshape(
        n_bs, MEM_HEADS * n_new, MEM_HEAD_DIM)
    o_b = _mem_attn_sample(qm_b, cache_mem_k[l].reshape(n_bs, n_mem * MEM_HEADS, MEM_HEAD_DIM),
                           cache_mem_v[l].reshape(n_bs, n_mem * MEM_HEADS, MEM_HEAD_DIM), n_new)
    o_s = jnp.transpose(o_b, (1, 0, 2)).reshape(n_s, MEM_WIDTH)
    x2_all, idx_all, gate_all, rank_all, counts = _route_sample(x1_s, o_s, cnt_p, post_w, x2_all, idx_all, gate_all,
                                                               rank_all, n_p)

    n_blocks = -(-n_assign // EXPERT_ROWS) + N_EXPERTS + 1
    blk_exp, blk_row, idx_rows, disp_rows, pad_start, n_pad, n_used = _routing_tables(idx_all, rank_all, counts,
                                                                                      n_blocks)
    x_rows = _dispatch(pad_start, n_pad, n_used, disp_rows, x2_all, n_blocks * EXPERT_ROWS)
    y_tok = _moe(blk_exp, blk_row, idx_rows, x_rows, w_gate_up[l], b_gate_up[l][:, None, :],
                 w_down[l], b_down[l][:, None, :], n_assign)
    g3, b3 = ln3_g[l].reshape(1, -1), ln3_b[l].reshape(1, -1)
    y_p = _combine(x2_all, y_tok, gate_all, g3, b3, 0, n_p, TOKEN_BLOCK)
    y_s = _combine(x2_all, y_tok, gate_all, g3, b3, n_p, n_s, min(TOKEN_BLOCK, n_s))

    def from_steps(a, width):
        return jnp.transpose(a.reshape(n_new, n_bs, width), (1, 0, 2))[None]

    return (
        y_p.reshape(n_bp, seq, D_MODEL),
        from_steps(y_s, D_MODEL)[0],
        lat_p.reshape(1, n_bp, seq, KV_RANK),
        kr_p.reshape(1, n_bp, seq, QK_ROPE_DIM),
        pstate_p[None, :, POOL_HALO - POOL_BUF:, :],
        mk.reshape(1, n_bp, n_mem, MEM_HEADS, MEM_HEAD_DIM),
        mv.reshape(1, n_bp, n_mem, MEM_HEADS, MEM_HEAD_DIM),
        from_steps(lat_s, KV_RANK),
        from_steps(kr_s, QK_ROPE_DIM),
        jnp.transpose(pstate_s, (1, 0, 2))[None],
    )
```

```python
import functools

import jax
import jax.numpy as jnp
from jax import lax
from jax.experimental import pallas as pl
from jax.experimental.pallas import tpu as pltpu

F32 = jnp.float32
BF16 = jnp.bfloat16
I32 = jnp.int32

D_MODEL = 1024
POOL_WIDTH = 512
POOL_WINDOWS = (2, 4, 8, 16)
POOL_GROUP_DIM = 128
POOL_BUF = 15
POOL_HALO = 16
MLA_HEADS = 4
QK_NOPE_DIM = 128
QK_ROPE_DIM = 64
ROPE_HALF = 32
V_HEAD_DIM = 128
Q_RANK = 256
KV_RANK = 128
ROPE_THETA = 10000.0
QK_PAD = 256
ONES_LANE = QK_PAD - 1
MEM_HEADS = 4
MEM_HEAD_DIM = 128
MEM_WIDTH = 512
N_EXPERTS = 32
TOP_K = 4
D_EXPERT = 1024
SWIGLU_LIMIT = 7.0
SWIGLU_ALPHA = 1.702
LN_EPS = 1e-5
RMS_EPS = 1e-6
DEPTH = 1
DEEPNORM_ALPHA = (2.0 * DEPTH) ** 0.25
ATTN_SCALE = (QK_NOPE_DIM + QK_ROPE_DIM) ** -0.5
LOG2_E = 1.4426950408889634
MEM_SCALE = MEM_HEAD_DIM ** -0.5

LANES = 128
VMEM_LIMIT = 48 * 1024 * 1024

TOKEN_BLOCK = 512
ATTN_Q_BLOCK = 256
ATTN_K_BLOCK = 512
ATTN_GROUPS = 2
PAGES_PER_STEP = 128
MEM_BATCH_BLOCK = 8
DISPATCH_BLOCK = 256
EXPERT_ROWS = 256
EXPERT_CHUNK = 512


def _dot(a, b):
    return jnp.dot(a, b, preferred_element_type=F32)


def _dot_nt(a, b):
    return lax.dot_general(a, b, (((1,), (1,)), ((), ())), preferred_element_type=F32)


def _rms(x, g):
    return x * lax.rsqrt(jnp.mean(x * x, axis=-1, keepdims=True) + RMS_EPS) * g


def _layer_norm(x, g, b):
    mu = jnp.mean(x, axis=-1, keepdims=True)
    xc = x - mu
    var = jnp.mean(xc * xc, axis=-1, keepdims=True)
    return xc * lax.rsqrt(var + LN_EPS) * g + b


ROW_TILE = D_MODEL // LANES


def _store_row_tiles(ref, lead, val):
    m = val.shape[0]
    for j in range(ROW_TILE):
        ref[(*lead, pl.ds(j, m, stride=ROW_TILE), slice(None))] = val[:, LANES * j:LANES * (j + 1)]


def _load_row_tiles(ref, lead, m):
    return jnp.concatenate([ref[(*lead, pl.ds(j, m, stride=ROW_TILE), slice(None))] for j in range(ROW_TILE)], axis=1)


def _params(*semantics):
    return pltpu.CompilerParams(dimension_semantics=semantics, vmem_limit_bytes=VMEM_LIMIT)


def _full(shape):
    n = len(shape)
    return pl.BlockSpec(shape, lambda *_: (0,) * n)


def _project(x_bf, w_in_ref, qg_ref, w_uq_ref, kvg_ref, w_ukt_ref, cos, sin, q_scale):
    h = _dot(x_bf, w_in_ref[...])
    u = h[:, :POOL_WIDTH]
    qn = _rms(h[:, POOL_WIDTH:POOL_WIDTH + Q_RANK], qg_ref[...])
    q = _dot(qn.astype(BF16), w_uq_ref[...])
    q_lat, q_rope = [], []
    for hd in range(MLA_HEADS):
        nope = q[:, QK_PAD * hd:QK_PAD * hd + QK_NOPE_DIM]
        rp = q[:, QK_PAD * hd + QK_NOPE_DIM:QK_PAD * (hd + 1)]
        q_lat.append(_dot(nope.astype(BF16), w_ukt_ref[hd]) * q_scale)
        q_rope.append((rp * cos + pltpu.roll(rp, 64, 1) * sin) * q_scale)
    o2 = POOL_WIDTH + Q_RANK
    lat = _rms(h[:, o2:o2 + KV_RANK], kvg_ref[...])
    kp = h[:, o2 + KV_RANK:]
    k_rope = kp * cos + pltpu.roll(kp, 64, 1) * sin
    return u, q_lat, q_rope, lat, k_rope


def _compact_rope(r):
    lane = lax.broadcasted_iota(I32, r.shape, 1)
    moved = pltpu.roll(r, 96, 1)
    return jnp.where(lane < ROPE_HALF, r, jnp.where(lane < QK_ROPE_DIM, moved, 0.0))


def _with_ones_lane(r):
    lane = lax.broadcasted_iota(I32, r.shape, 1)
    return jnp.where(lane == ONES_LANE - KV_RANK, 1.0, r)


def _pool_group_out(diff, g, w_pool_ref, pscale_ref):
    cols = slice(POOL_GROUP_DIM * g, POOL_GROUP_DIM * (g + 1))
    return _dot(diff.astype(BF16), w_pool_ref[g]) * pscale_ref[:, cols]


def _proj_prompt_kernel(x_ref, w_in_ref, qg_ref, w_uq_ref, kvg_ref, w_ukt_ref, w_pool_ref, pscale_ref,
                        cos_ref, sin_ref,
                        qcat_ref, kcat_ref, lat_ref, kr_ref, pooly_ref, pstate_ref, ubuf):
    j = pl.program_id(1)
    tm = x_ref.shape[0]

    @pl.when(j == 0)
    def _():
        ubuf[0:POOL_HALO, :] = jnp.zeros((POOL_HALO, POOL_WIDTH), F32)

    u, q_lat, q_rope, lat, k_rope = _project(
        x_ref[...].astype(BF16), w_in_ref, qg_ref, w_uq_ref, kvg_ref, w_ukt_ref, cos_ref[...], sin_ref[...],
        ATTN_SCALE * LOG2_E)
    for hd in range(MLA_HEADS):
        qcat_ref[hd, :, 0:KV_RANK] = q_lat[hd].astype(BF16)
        qcat_ref[hd, :, KV_RANK:QK_PAD] = q_rope[hd].astype(BF16)
    lat_ref[...] = lat
    kcat_ref[:, 0:KV_RANK] = lat.astype(BF16)
    kcat_ref[:, KV_RANK:QK_PAD] = _with_ones_lane(k_rope).astype(BF16)
    kr_ref[...] = _compact_rope(k_rope)[:, :QK_ROPE_DIM]

    ubuf[POOL_HALO:POOL_HALO + tm, :] = u
    pos = j * tm + lax.broadcasted_iota(I32, (tm, 1), 0)
    for g, w in enumerate(POOL_WINDOWS):
        cols = slice(POOL_GROUP_DIM * g, POOL_GROUP_DIM * (g + 1))
        ug = u[:, cols]
        ssum = ug
        for k in range(1, w):
            ssum = ssum + ubuf[POOL_HALO - k:POOL_HALO - k + tm, cols]
        count = jnp.minimum(w, pos + 1).astype(F32)
        pooly_ref[:, cols] = _pool_group_out(ssum / count - ug, g, w_pool_ref, pscale_ref).astype(BF16)
    tail = ubuf[tm:tm + POOL_HALO, :]
    ubuf[0:POOL_HALO, :] = tail

    @pl.when(j == pl.num_programs(1) - 1)
    def _():
        pstate_ref[...] = tail


def _proj_sample_kernel(x_ref, state_ref, w_in_ref, qg_ref, w_uq_ref, kvg_ref, w_ukt_ref, w_pool_ref, pscale_ref,
                        cos_ref, sin_ref,
                        qcat_ref, kcat_ref, lat_ref, kr_ref, pooly_ref, pstate_ref, *, past_len):
    n_b = state_ref.shape[1]
    n_t = x_ref.shape[0] // n_b
    u, q_lat, q_rope, lat, k_rope = _project(
        x_ref[...].astype(BF16), w_in_ref, qg_ref, w_uq_ref, kvg_ref, w_ukt_ref, cos_ref[...], sin_ref[...],
        ATTN_SCALE)
    for hd in range(MLA_HEADS):
        qcat_ref[hd, :, 0:KV_RANK] = q_lat[hd].astype(BF16)
        qcat_ref[hd, :, KV_RANK:QK_PAD] = _compact_rope(q_rope[hd]).astype(BF16)
    lat_ref[...] = lat
    kc = _compact_rope(k_rope)
    kcat_ref[:, 0:KV_RANK] = lat.astype(BF16)
    kcat_ref[:, KV_RANK:QK_PAD] = _with_ones_lane(kc).astype(BF16)
    kr_ref[...] = kc[:, :QK_ROPE_DIM]

    def ext(jj):
        if jj < POOL_BUF:
            return state_ref[jj]
        return u[(jj - POOL_BUF) * n_b:(jj - POOL_BUF + 1) * n_b, :]

    for t in range(n_t):
        ut = ext(POOL_BUF + t)
        for g, w in enumerate(POOL_WINDOWS):
            cols = slice(POOL_GROUP_DIM * g, POOL_GROUP_DIM * (g + 1))
            ssum = ut[:, cols]
            for k in range(1, w):
                ssum = ssum + ext(POOL_BUF + t - k)[:, cols]
            count = float(min(w, past_len + t + 1))
            y = _pool_group_out(ssum / count - ut[:, cols], g, w_pool_ref, pscale_ref)
            pooly_ref[t * n_b:(t + 1) * n_b, cols] = y.astype(BF16)
    for jj in range(POOL_BUF):
        pstate_ref[jj] = ext(n_t + jj)


def _proj_weight_specs():
    return [
        _full((D_MODEL, D_MODEL)),
        _full((1, Q_RANK)),
        _full((Q_RANK, MLA_HEADS * QK_PAD)),
        _full((1, KV_RANK)),
        _full((MLA_HEADS, QK_NOPE_DIM, KV_RANK)),
        _full((len(POOL_WINDOWS), POOL_GROUP_DIM, POOL_GROUP_DIM)),
        _full((1, POOL_WIDTH)),
    ]


def _proj_prompt(x2d, weights, cos_t, sin_t, n_batch, seq):
    tm = TOKEN_BLOCK
    n = n_batch * seq
    nj = seq // tm
    tok = lambda b, j: (b * nj + j, 0)
    return pl.pallas_call(
        _proj_prompt_kernel,
        grid=(n_batch, nj),
        in_specs=[pl.BlockSpec((tm, D_MODEL), tok)] + _proj_weight_specs() + [
            pl.BlockSpec((tm, LANES), lambda b, j: (j, 0)),
            pl.BlockSpec((tm, LANES), lambda b, j: (j, 0)),
        ],
        out_specs=[
            pl.BlockSpec((MLA_HEADS, tm, QK_PAD), lambda b, j: (0, b * nj + j, 0)),
            pl.BlockSpec((tm, QK_PAD), tok),
            pl.BlockSpec((tm, KV_RANK), tok),
            pl.BlockSpec((tm, QK_ROPE_DIM), tok),
            pl.BlockSpec((tm, POOL_WIDTH), tok),
            pl.BlockSpec((None, POOL_HALO, POOL_WIDTH), lambda b, j: (b, 0, 0)),
        ],
        out_shape=[
            jax.ShapeDtypeStruct((MLA_HEADS, n, QK_PAD), BF16),
            jax.ShapeDtypeStruct((n, QK_PAD), BF16),
            jax.ShapeDtypeStruct((n, KV_RANK), F32),
            jax.ShapeDtypeStruct((n, QK_ROPE_DIM), F32),
            jax.ShapeDtypeStruct((n, POOL_WIDTH), BF16),
            jax.ShapeDtypeStruct((n_batch, POOL_HALO, POOL_WIDTH), F32),
        ],
        scratch_shapes=[pltpu.VMEM((POOL_HALO + tm, POOL_WIDTH), F32)],
        compiler_params=_params("arbitrary", "arbitrary"),
        name="proj_prompt",
    )(x2d, *weights, cos_t, sin_t)


def _proj_sample(x2d, state_t, weights, cos_t, sin_t, past_len):
    n = x2d.shape[0]
    n_b = state_t.shape[1]
    return pl.pallas_call(
        functools.partial(_proj_sample_kernel, past_len=past_len),
        grid=(1,),
        in_specs=[_full((n, D_MODEL)), _full((POOL_BUF, n_b, POOL_WIDTH))] + _proj_weight_specs() + [
            _full((n, LANES)), _full((n, LANES))],
        out_specs=[
            _full((MLA_HEADS, n, QK_PAD)), _full((n, QK_PAD)), _full((n, KV_RANK)), _full((n, QK_ROPE_DIM)),
            _full((n, POOL_WIDTH)), _full((POOL_BUF, n_b, POOL_WIDTH)),
        ],
        out_shape=[
            jax.ShapeDtypeStruct((MLA_HEADS, n, QK_PAD), BF16),
            jax.ShapeDtypeStruct((n, QK_PAD), BF16),
            jax.ShapeDtypeStruct((n, KV_RANK), F32),
            jax.ShapeDtypeStruct((n, QK_ROPE_DIM), F32),
            jax.ShapeDtypeStruct((n, POOL_WIDTH), BF16),
            jax.ShapeDtypeStruct((POOL_BUF, n_b, POOL_WIDTH), F32),
        ],
        compiler_params=_params("arbitrary"),
        name="proj_sample",
    )(x2d, state_t, *weights, cos_t, sin_t)


def _attn_block(q, k, mask, m_ref, acc_ref):
    s = _dot_nt(q, k)
    if mask is not None:
        s = jnp.where(mask, s, -jnp.inf)
    tiles = [s[:, c * LANES:(c + 1) * LANES] for c in range(s.shape[1] // LANES)]
    m_old = m_ref[...]
    m_new = jnp.maximum(m_old, jnp.max(functools.reduce(jnp.maximum, tiles), axis=1, keepdims=True))
    alpha = jnp.exp2(m_old - m_new)
    p = jnp.concatenate([jnp.exp2(t - m_new) for t in tiles], axis=1).astype(BF16)
    acc_ref[...] = jnp.concatenate([alpha, alpha], axis=1) * acc_ref[...] + _dot(p, k)
    m_ref[...] = m_new


def _attn_output(acc):
    return acc[:, :KV_RANK] / acc[:, ONES_LANE:ONES_LANE + 1]


def _attn_prompt_kernel(q_ref, k_ref, w_uv_ref, out_ref, m_ref, acc_ref):
    i = pl.program_id(1)
    tq = q_ref.shape[1]
    tk = ATTN_K_BLOCK
    hpg = MLA_HEADS // ATTN_GROUPS
    rows = hpg * tq
    m_ref[...] = jnp.full(m_ref.shape, -jnp.inf, F32)
    acc_ref[...] = jnp.zeros(acc_ref.shape, F32)

    def block(jk, masked):
        k = k_ref[pl.ds(pl.multiple_of(jk * tk, tk), tk), :]
        mask = None
        if masked:
            q_pos = i * tq + lax.broadcasted_iota(I32, (rows, tk), 0) % tq
            k_pos = jk * tk + lax.broadcasted_iota(I32, (rows, tk), 1)
            mask = k_pos <= q_pos
        for g in range(ATTN_GROUPS):
            q = q_ref[hpg * g:hpg * (g + 1)].reshape(rows, QK_PAD)
            _attn_block(q, k, mask, m_ref.at[g], acc_ref.at[g])

    n_full = (i * tq + 1) // tk
    n_kv = (i * tq + tq - 1) // tk + 1

    def full_pair(jp, c):
        block(2 * jp, False)
        block(2 * jp + 1, False)
        return c

    def diag_body(jk, c):
        block(jk, True)
        return c

    lax.fori_loop(0, n_full // 2, full_pair, 0)

    @pl.when(n_full % 2 == 1)
    def _():
        block(n_full - 1, False)

    lax.fori_loop(n_full, n_kv, diag_body, 0)

    for g in range(ATTN_GROUPS):
        o = _attn_output(acc_ref[g])
        for hl in range(hpg):
            hd = hpg * g + hl
            oh = o[hl * tq:(hl + 1) * tq, :].astype(BF16)
            out_ref[:, V_HEAD_DIM * hd:V_HEAD_DIM * (hd + 1)] = _dot(oh, w_uv_ref[hd]).astype(BF16)


def _attn_prompt(qcat, kcat, w_uv_h, n_batch, seq):
    tq = ATTN_Q_BLOCK
    nq = seq // tq
    n = n_batch * seq
    rows = (MLA_HEADS // ATTN_GROUPS) * tq
    return pl.pallas_call(
        _attn_prompt_kernel,
        grid=(n_batch, nq),
        in_specs=[
            pl.BlockSpec((MLA_HEADS, tq, QK_PAD), lambda b, i: (0, b * nq + i, 0)),
            pl.BlockSpec((seq, QK_PAD), lambda b, i: (b, 0)),
            _full((MLA_HEADS, KV_RANK, V_HEAD_DIM)),
        ],
        out_specs=pl.BlockSpec((tq, MLA_HEADS * V_HEAD_DIM), lambda b, i: (b * nq + i, 0)),
        out_shape=jax.ShapeDtypeStruct((n, MLA_HEADS * V_HEAD_DIM), BF16),
        scratch_shapes=[pltpu.VMEM((ATTN_GROUPS, rows, LANES), F32), pltpu.VMEM((ATTN_GROUPS, rows, QK_PAD), F32)],
        compiler_params=_params("arbitrary", "arbitrary"),
        name="attn_prompt",
    )(qcat, kcat, w_uv_h)


def _attn_sample_kernel(pt_ref, q_ref, knew_ref, w_uv_ref, lat_hbm, rope_hbm, out_ref,
                        lat_buf, rope_buf, m_ref, l_ref, acc_ref, res_ref, page_sem, *, n_pages_step, n_new, layer):
    s_idx = pl.program_id(1)
    n_steps = pl.num_programs(1)
    g = pl.program_id(0) * n_steps + s_idx
    slot = g % 2

    def fetch(step, dst):
        for p in range(n_pages_step):
            page = pt_ref[step * n_pages_step + p]
            pltpu.make_async_copy(lat_hbm.at[layer, page], lat_buf.at[dst, p], page_sem.at[dst]).start()
            pltpu.make_async_copy(rope_hbm.at[layer, page], rope_buf.at[dst, p], page_sem.at[dst]).start()

    @pl.when(g == 0)
    def _():
        fetch(0, 0)

    @pl.when(g + 1 < pl.num_programs(0) * n_steps)
    def _():
        fetch(g + 1, 1 - slot)

    pltpu.make_async_copy(lat_buf.at[slot], lat_buf.at[slot], page_sem.at[slot]).wait()
    pltpu.make_async_copy(rope_buf.at[slot], rope_buf.at[slot], page_sem.at[slot]).wait()

    @pl.when(s_idx == 0)
    def _():
        m_ref[...] = jnp.full(m_ref.shape, -jnp.inf, F32)
        l_ref[...] = jnp.zeros(l_ref.shape, F32)
        acc_ref[...] = jnp.zeros(acc_ref.shape, F32)

    def update(s_tiles, values):
        m_old = m_ref[...]
        m_new = jnp.maximum(m_old, jnp.max(functools.reduce(jnp.maximum, s_tiles), axis=1, keepdims=True))
        alpha = jnp.exp(m_old - m_new)
        p_tiles = [jnp.exp(t - m_new) for t in s_tiles]
        row_sum = jnp.sum(functools.reduce(jnp.add, p_tiles), axis=1, keepdims=True)
        pv = functools.reduce(jnp.add, [_dot(p.astype(BF16), v) for p, v in zip(p_tiles, values)])
        l_ref[...] = alpha * l_ref[...] + row_sum
        acc_ref[...] = alpha * acc_ref[...] + pv
        m_ref[...] = m_new

    q = q_ref[...]
    q_lat = q[:, :KV_RANK]
    q_rope = q[:, KV_RANK:KV_RANK + QK_ROPE_DIM]
    lats = [lat_buf[slot, p].astype(BF16) for p in range(n_pages_step)]
    update([_dot_nt(q_lat, lats[p]) + _dot(q_rope, rope_buf[slot, p].astype(BF16)) for p in range(n_pages_step)],
           lats)

    @pl.when(s_idx == pl.num_programs(1) - 1)
    def _():
        kn = knew_ref[...]
        q_t = lax.broadcasted_iota(I32, (q.shape[0], kn.shape[0]), 0) % n_new
        k_t = lax.broadcasted_iota(I32, (q.shape[0], kn.shape[0]), 1)
        update([jnp.where(k_t <= q_t, _dot_nt(q, kn), -jnp.inf)], [kn[:, :KV_RANK]])
        o = (acc_ref[...] / l_ref[...]).astype(BF16)
        for hd in range(MLA_HEADS):
            res_ref[...] = _dot(o, w_uv_ref[hd])
            out_ref[:, V_HEAD_DIM * hd:V_HEAD_DIM * (hd + 1)] = res_ref[hd * n_new:(hd + 1) * n_new, :]


def _attn_sample(page_table, q_b, knew_b, w_uv_h, cache_lat, cache_rope, layer):
    n_b, n_pages = page_table.shape
    page = cache_lat.shape[2]
    pps = min(PAGES_PER_STEP, n_pages)
    assert n_pages % pps == 0
    n_steps = n_pages // pps
    rows = q_b.shape[1]
    n_new = rows // MLA_HEADS
    t_pad = knew_b.shape[1]

    anyspec = pl.BlockSpec(memory_space=pl.ANY)
    grid_spec = pltpu.PrefetchScalarGridSpec(
        num_scalar_prefetch=1,
        grid=(n_b, n_steps),
        in_specs=[
            pl.BlockSpec((None, rows, QK_PAD), lambda b, s, pt: (b, 0, 0)),
            pl.BlockSpec((None, t_pad, QK_PAD), lambda b, s, pt: (b, 0, 0)),
            pl.BlockSpec((MLA_HEADS, KV_RANK, V_HEAD_DIM), lambda b, s, pt: (0, 0, 0)),
            anyspec, anyspec,
        ],
        out_specs=pl.BlockSpec((None, n_new, MLA_HEADS * V_HEAD_DIM), lambda b, s, pt: (b, 0, 0)),
        scratch_shapes=[
            pltpu.VMEM((2, pps, page, KV_RANK), F32), pltpu.VMEM((2, pps, QK_ROPE_DIM, page), F32),
            pltpu.VMEM((rows, LANES), F32), pltpu.VMEM((rows, LANES), F32), pltpu.VMEM((rows, KV_RANK), F32),
            pltpu.VMEM((rows, V_HEAD_DIM), F32), pltpu.SemaphoreType.DMA((2,)),
        ],
    )
    return pl.pallas_call(
        functools.partial(_attn_sample_kernel, n_pages_step=pps, n_new=n_new, layer=layer),
        grid_spec=grid_spec,
        out_shape=jax.ShapeDtypeStruct((n_b, n_new, MLA_HEADS * V_HEAD_DIM), F32),
        compiler_params=_params("arbitrary", "arbitrary"),
        name="attn_sample",
    )(page_table.reshape(-1), q_b, knew_b, w_uv_h, cache_lat, cache_rope)


def _mem_kv_kernel(mem_ref, w_mk_ref, w_mv_ref, k_ref, v_ref, kb_ref, vb_ref):
    m = mem_ref[...].astype(BF16)
    k = _dot(m, w_mk_ref[...])
    v = _dot(m, w_mv_ref[...])
    k_ref[...] = k
    v_ref[...] = v
    kb_ref[...] = k.astype(BF16)
    vb_ref[...] = v.astype(BF16)


def _mem_kv(mem2d, w_mk, w_mv, n_batch, n_mem):
    blk = pl.BlockSpec((n_mem, MEM_WIDTH), lambda b: (b, 0))
    n = n_batch * n_mem
    return pl.pallas_call(
        _mem_kv_kernel,
        grid=(n_batch,),
        in_specs=[pl.BlockSpec((n_mem, D_MODEL), lambda b: (b, 0)), _full((D_MODEL, MEM_WIDTH)),
                  _full((D_MODEL, MEM_WIDTH))],
        out_specs=[blk, blk, blk, blk],
        out_shape=[jax.ShapeDtypeStruct((n, MEM_WIDTH), F32), jax.ShapeDtypeStruct((n, MEM_WIDTH), F32),
                   jax.ShapeDtypeStruct((n, MEM_WIDTH), BF16), jax.ShapeDtypeStruct((n, MEM_WIDTH), BF16)],
        compiler_params=_params("arbitrary"),
        name="mem_kv",
    )(mem2d, w_mk, w_mv)


def _mix_ln1(x, pool_y, mla_y, w_out_a_ref, w_out_b_ref, g_ref, b_ref):
    mix = _dot(pool_y, w_out_a_ref[...]) + _dot(mla_y, w_out_b_ref[...])
    return _layer_norm(DEEPNORM_ALPHA * x + mix, g_ref[...], b_ref[...])


def _route(x2, w_r_ref, b_r_ref, carry):
    m = x2.shape[0]
    x_hi = x2.astype(BF16)
    x_lo = (x2 - x_hi.astype(F32)).astype(BF16)
    logits_t = _dot_nt(w_r_ref[0], x_hi) + (_dot_nt(w_r_ref[0], x_lo) + _dot_nt(w_r_ref[1], x_hi))
    logits = logits_t.T + b_r_ref[...]
    e_idx = lax.broadcasted_iota(I32, (m, N_EXPERTS), 1).astype(F32)
    work = logits
    vals, picks = [], []
    for _ in range(TOP_K):
        mx = jnp.max(work, axis=1, keepdims=True)
        pick = jnp.min(jnp.where(work == mx, e_idx, float(N_EXPERTS)), axis=1, keepdims=True)
        vals.append(mx)
        picks.append(pick)
        work = jnp.where(e_idx == pick, -jnp.inf, work)
    hot = jnp.where(work == -jnp.inf, 1.0, 0.0)
    exps = [jnp.exp(v - vals[0]) for v in vals]
    denom = exps[0] + exps[1] + exps[2] + exps[3]
    r_i = lax.broadcasted_iota(I32, (m, m), 0)
    c_i = lax.broadcasted_iota(I32, (m, m), 1)
    strict_lower = jnp.where(c_i < r_i, 1.0, 0.0).astype(BF16)
    before = _dot(strict_lower, hot.astype(BF16)) + carry
    lane4 = lax.broadcasted_iota(I32, (m, TOP_K), 1)
    idx = jnp.zeros((m, TOP_K), I32)
    gates = jnp.zeros((m, TOP_K), F32)
    rank = jnp.zeros((m, TOP_K), I32)
    for k in range(TOP_K):
        rk = jnp.sum(jnp.where(e_idx == picks[k], before, 0.0), axis=1, keepdims=True).astype(I32)
        idx = jnp.where(lane4 == k, picks[k].astype(I32), idx)
        gates = jnp.where(lane4 == k, exps[k] / denom, gates)
        rank = jnp.where(lane4 == k, rk, rank)
    return idx, gates, rank, carry + jnp.sum(hot, axis=0, keepdims=True)


def _mo_ln2_route(x1, o_bf, w_mo_ref, g_ref, b_ref, w_r_ref, b_r_ref, carry):
    x2 = _layer_norm(DEEPNORM_ALPHA * x1 + _dot(o_bf, w_mo_ref[...]), g_ref[...], b_ref[...])
    return (x2,) + _route(x2, w_r_ref, b_r_ref, carry)


def _post_prompt_kernel(x_ref, pooly_ref, mlay_ref, mk_ref, mv_ref,
                        w_out_a_ref, w_out_b_ref, g1_ref, b1_ref, w_mq_ref, w_mo_ref, g2_ref, b2_ref,
                        w_r_ref, b_r_ref,
                        x2_ref, idx_ref, gate_ref, rank_ref, cnt_ref, carry_ref):
    first = jnp.logical_and(pl.program_id(0) == 0, pl.program_id(1) == 0)

    @pl.when(first)
    def _():
        carry_ref[...] = jnp.zeros(carry_ref.shape, F32)

    x1 = _mix_ln1(x_ref[...], pooly_ref[...], mlay_ref[...], w_out_a_ref, w_out_b_ref, g1_ref, b1_ref)
    qm = (_dot(x1.astype(BF16), w_mq_ref[...]) * MEM_SCALE).astype(BF16)
    outs = []
    for hd in range(MEM_HEADS):
        cols = slice(MEM_HEAD_DIM * hd, MEM_HEAD_DIM * (hd + 1))
        s = _dot_nt(qm[:, cols], mk_ref[:, cols])
        p = jnp.exp(s - jnp.max(s, axis=1, keepdims=True))
        o = _dot(p.astype(BF16), mv_ref[:, cols]) / jnp.sum(p, axis=1, keepdims=True)
        outs.append(o.astype(BF16))
    o_all = jnp.concatenate(outs, axis=1)
    x2, idx, gates, rank, carry = _mo_ln2_route(x1, o_all, w_mo_ref, g2_ref, b2_ref, w_r_ref, b_r_ref,
                                                carry_ref[...])
    _store_row_tiles(x2_ref, (), x2)
    idx_ref[...] = idx
    gate_ref[...] = gates
    rank_ref[...] = rank
    carry_ref[...] = carry
    cnt_ref[...] = carry


def _post_weight_specs():
    half = (POOL_WIDTH, D_MODEL)
    return [
        _full(half), _full(half), _full((1, D_MODEL)), _full((1, D_MODEL)),
        _full((D_MODEL, MEM_WIDTH)), _full((MEM_WIDTH, D_MODEL)), _full((1, D_MODEL)), _full((1, D_MODEL)),
        _full((2, N_EXPERTS, D_MODEL)), _full((1, N_EXPERTS)),
    ]


def _post_prompt(x2d, pool_y, mla_y, mk_b, mv_b, weights, n_batch, seq, n_mem, n_all):
    tm = TOKEN_BLOCK
    nj = seq // tm
    tok = lambda b, j: (b * nj + j, 0)
    return pl.pallas_call(
        _post_prompt_kernel,
        grid=(n_batch, nj),
        in_specs=[
            pl.BlockSpec((tm, D_MODEL), tok), pl.BlockSpec((tm, POOL_WIDTH), tok), pl.BlockSpec((tm, POOL_WIDTH), tok),
            pl.BlockSpec((n_mem, MEM_WIDTH), lambda b, j: (b, 0)), pl.BlockSpec((n_mem, MEM_WIDTH), lambda b, j: (b, 0)),
        ] + _post_weight_specs(),
        out_specs=[
            pl.BlockSpec((tm * ROW_TILE, LANES), tok), pl.BlockSpec((tm, TOP_K), tok), pl.BlockSpec((tm, TOP_K), tok),
            pl.BlockSpec((tm, TOP_K), tok), _full((1, N_EXPERTS)),
        ],
        out_shape=[
            jax.ShapeDtypeStruct((n_all * ROW_TILE, LANES), F32), jax.ShapeDtypeStruct((n_all, TOP_K), I32),
            jax.ShapeDtypeStruct((n_all, TOP_K), F32), jax.ShapeDtypeStruct((n_all, TOP_K), I32),
            jax.ShapeDtypeStruct((1, N_EXPERTS), F32),
        ],
        scratch_shapes=[pltpu.VMEM((1, N_EXPERTS), F32)],
        compiler_params=_params("arbitrary", "arbitrary"),
        name="post_prompt",
    )(x2d, pool_y, mla_y, mk_b, mv_b, *weights)


def _mix_sample_kernel(x_ref, pooly_ref, mlay_ref, w_out_a_ref, w_out_b_ref, g1_ref, b1_ref, w_mq_ref,
                       x1_ref, qm_ref):
    x1 = _mix_ln1(x_ref[...], pooly_ref[...], mlay_ref[...], w_out_a_ref, w_out_b_ref, g1_ref, b1_ref)
    x1_ref[...] = x1
    qm_ref[...] = (_dot(x1.astype(BF16), w_mq_ref[...]) * MEM_SCALE).astype(BF16)


def _mix_sample(x2d, pool_y, mla_y, weights):
    n = x2d.shape[0]
    w_out_a, w_out_b, g1, b1, w_mq = weights[:5]
    return pl.pallas_call(
        _mix_sample_kernel,
        grid=(1,),
        in_specs=[_full((n, D_MODEL)), _full((n, POOL_WIDTH)), _full((n, POOL_WIDTH))] + _post_weight_specs()[:5],
        out_specs=[_full((n, D_MODEL)), _full((n, MEM_WIDTH))],
        out_shape=[jax.ShapeDtypeStruct((n, D_MODEL), F32), jax.ShapeDtypeStruct((n, MEM_WIDTH), BF16)],
        compiler_params=_params("arbitrary"),
        name="mix_sample",
    )(x2d, pool_y, mla_y, w_out_a, w_out_b, g1, b1, w_mq)


def _mem_attn_sample_kernel(q_ref, k_ref, v_ref, out_ref, res_ref, *, n_new):
    n_mem = k_ref.shape[1] // MEM_HEADS
    rows = q_ref.shape[1]
    row_head = lax.broadcasted_iota(I32, (rows, 1), 0) // n_new
    for b in range(q_ref.shape[0]):
        q = q_ref[b]
        s = jnp.zeros((rows, n_mem), F32)
        for hd in range(MEM_HEADS):
            k_h = k_ref[b, pl.ds(hd, n_mem, stride=MEM_HEADS), :].astype(BF16)
            s = jnp.where(row_head == hd, _dot_nt(q, k_h), s)
        p = jnp.exp(s - jnp.max(s, axis=1, keepdims=True))
        p_bf = p.astype(BF16)
        o = jnp.zeros((rows, MEM_HEAD_DIM), F32)
        for hd in range(MEM_HEADS):
            v_h = v_ref[b, pl.ds(hd, n_mem, stride=MEM_HEADS), :].astype(BF16)
            o = jnp.where(row_head == hd, _dot(p_bf, v_h), o)
        res_ref[...] = o / jnp.sum(p, axis=1, keepdims=True)
        for hd in range(MEM_HEADS):
            cols = slice(MEM_HEAD_DIM * hd, MEM_HEAD_DIM * (hd + 1))
            out_ref[b, :, cols] = res_ref[hd * n_new:(hd + 1) * n_new, :].astype(BF16)


def _mem_attn_sample(q_b, mem_k, mem_v, n_new):
    n_b, rows, _ = q_b.shape
    mh = mem_k.shape[1]
    gb = min(MEM_BATCH_BLOCK, n_b)
    assert n_b % gb == 0
    blk = lambda i: (i, 0, 0)
    return pl.pallas_call(
        functools.partial(_mem_attn_sample_kernel, n_new=n_new),
        grid=(n_b // gb,),
        in_specs=[pl.BlockSpec((gb, rows, MEM_HEAD_DIM), blk), pl.BlockSpec((gb, mh, MEM_HEAD_DIM), blk),
                  pl.BlockSpec((gb, mh, MEM_HEAD_DIM), blk)],
        out_specs=pl.BlockSpec((gb, n_new, MEM_WIDTH), blk),
        out_shape=jax.ShapeDtypeStruct((n_b, n_new, MEM_WIDTH), BF16),
        scratch_shapes=[pltpu.VMEM((rows, MEM_HEAD_DIM), F32)],
        compiler_params=_params("arbitrary"),
        name="mem_attn_sample",
    )(q_b, mem_k, mem_v)


def _route_sample_kernel(x1_ref, o_ref, cnt_in_ref, w_mo_ref, g2_ref, b2_ref, w_r_ref, b_r_ref,
                         x2_in, idx_in, gate_in, rank_in,
                         x2_ref, idx_ref, gate_ref, rank_ref, cnt_ref):
    del x2_in, idx_in, gate_in, rank_in
    x2, idx, gates, rank, carry = _mo_ln2_route(x1_ref[...], o_ref[...], w_mo_ref, g2_ref, b2_ref, w_r_ref, b_r_ref,
                                                cnt_in_ref[...])
    _store_row_tiles(x2_ref, (), x2)
    idx_ref[...] = idx
    gate_ref[...] = gates
    rank_ref[...] = rank
    cnt_ref[...] = carry


def _route_sample(x1, o_bf, cnt_in, weights, x2_all, idx_all, gate_all, rank_all, n_prompt):
    n = x1.shape[0]
    assert n_prompt % n == 0
    tail = lambda i: (n_prompt // n, 0)
    anyspec = pl.BlockSpec(memory_space=pl.ANY)
    w_mo, g2, b2, w_r, b_r = weights[5:]
    return pl.pallas_call(
        _route_sample_kernel,
        grid=(1,),
        in_specs=[_full((n, D_MODEL)), _full((n, MEM_WIDTH)), _full((1, N_EXPERTS))] + _post_weight_specs()[5:] + [
            anyspec, anyspec, anyspec, anyspec],
        out_specs=[pl.BlockSpec((n * ROW_TILE, LANES), tail), pl.BlockSpec((n, TOP_K), tail), pl.BlockSpec((n, TOP_K), tail),
                   pl.BlockSpec((n, TOP_K), tail), _full((1, N_EXPERTS))],
        out_shape=[jax.ShapeDtypeStruct(x2_all.shape, F32), jax.ShapeDtypeStruct(idx_all.shape, I32),
                   jax.ShapeDtypeStruct(gate_all.shape, F32), jax.ShapeDtypeStruct(rank_all.shape, I32),
                   jax.ShapeDtypeStruct((1, N_EXPERTS), F32)],
        input_output_aliases={8: 0, 9: 1, 10: 2, 11: 3},
        compiler_params=_params("arbitrary"),
        name="route_sample",
    )(x1, o_bf, cnt_in, w_mo, g2, b2, w_r, b_r, x2_all, idx_all, gate_all, rank_all)


def _dispatch_kernel(pad_start_ref, n_pad_ref, n_used_ref, idx_hbm, x_ref, xrows_hbm,
                     idx_smem, zbuf, idx_sem, row_sem, pad_sem):
    i = pl.program_id(0)
    last = pl.num_programs(0) - 1
    tm = x_ref.shape[0] // ROW_TILE
    blk = zbuf.shape[0]
    n_blk = xrows_hbm.shape[0] // blk

    def idx_copy(row, s):
        return pltpu.make_async_copy(idx_hbm.at[row], idx_smem.at[s], idx_sem.at[s])

    def pad_copy(e, r):
        dst = pl.multiple_of((pad_start_ref[e] + r) * ROW_TILE, ROW_TILE)
        return pltpu.make_async_copy(zbuf.at[pl.ds(0, ROW_TILE)], xrows_hbm.at[pl.ds(dst, ROW_TILE)], pad_sem)

    def tail_copy(j):
        return pltpu.make_async_copy(zbuf, xrows_hbm.at[pl.ds(pl.multiple_of(j * blk, blk), blk)], pad_sem)

    @pl.when(i == 0)
    def _():
        idx_copy(0, 0).start()
        zbuf[...] = jnp.zeros(zbuf.shape, F32)
        for e in range(N_EXPERTS):
            def start(r, c, e=e):
                pad_copy(e, r).start()
                return c

            lax.fori_loop(0, n_pad_ref[e], start, 0)

        def start_tail(j, c):
            tail_copy(j).start()
            return c

        lax.fori_loop(n_used_ref[0], n_blk, start_tail, 0)
        for e in range(N_EXPERTS):
            def wait(r, c, e=e):
                pad_copy(e, r).wait()
                return c

            lax.fori_loop(0, n_pad_ref[e], wait, 0)

        def wait_tail(j, c):
            tail_copy(j).wait()
            return c

        lax.fori_loop(n_used_ref[0], n_blk, wait_tail, 0)

    def step(slot):
        idx_copy(i, slot).wait()

        @pl.when(i < last)
        def _():
            idx_copy(i + 1, 1 - slot).start()

        for t in range(tm):
            for k in range(TOP_K):
                dst = pl.multiple_of(idx_smem[slot, TOP_K * t + k], ROW_TILE)
                pltpu.make_async_copy(x_ref.at[pl.ds(ROW_TILE * t, ROW_TILE)], xrows_hbm.at[pl.ds(dst, ROW_TILE)],
                                      row_sem).start(priority=k % 2)
        for k in range(TOP_K):
            pltpu.make_async_copy(x_ref, x_ref, row_sem).wait()

    for parity in range(2):
        pl.when(i % 2 == parity)(functools.partial(step, parity))


def _dispatch(pad_start, n_pad, n_used, idx_rows, x_all, n_rows):
    tm = DISPATCH_BLOCK
    n_steps = idx_rows.shape[0]
    assert x_all.shape[0] == n_steps * tm * ROW_TILE and idx_rows.shape[1] == TOP_K * tm
    assert n_rows % EXPERT_ROWS == 0
    grid_spec = pltpu.PrefetchScalarGridSpec(
        num_scalar_prefetch=3,
        grid=(n_steps,),
        in_specs=[pl.BlockSpec(memory_space=pl.ANY),
                  pl.BlockSpec((tm * ROW_TILE, LANES), lambda i, ps, npd, nu: (i, 0))],
        out_specs=pl.BlockSpec(memory_space=pl.ANY),
        scratch_shapes=[pltpu.SMEM((2, TOP_K * tm), I32), pltpu.VMEM((EXPERT_ROWS * ROW_TILE, LANES), F32),
                        pltpu.SemaphoreType.DMA((2,)), pltpu.SemaphoreType.DMA, pltpu.SemaphoreType.DMA],
    )
    return pl.pallas_call(
        _dispatch_kernel,
        grid_spec=grid_spec,
        out_shape=jax.ShapeDtypeStruct((n_rows * ROW_TILE, LANES), F32),
        compiler_params=_params("arbitrary"),
        name="moe_dispatch",
    )(pad_start, n_pad, n_used, idx_rows, x_all)


def _moe_kernel(blk_exp_ref, blk_row_ref, idx_hbm, x_ref, wgu_f32_ref, bgu_ref, wd_f32_ref, bd_ref, y_hbm,
                idx_smem, ybuf, wgu_ref, wd_ref, idx_sem, y_sem):
    del blk_row_ref
    i = pl.program_id(0)
    last = pl.num_programs(0) - 1
    tr = x_ref.shape[0] // ROW_TILE

    @pl.when(jnp.logical_or(i == 0, blk_exp_ref[i] != blk_exp_ref[jnp.maximum(i - 1, 0)]))
    def _():
        wgu_ref[...] = wgu_f32_ref[...].astype(BF16)
        wd_ref[...] = wd_f32_ref[...].astype(BF16)

    n_chunks = D_EXPERT // EXPERT_CHUNK

    def idx_copy(row, s):
        return pltpu.make_async_copy(idx_hbm.at[row], idx_smem.at[s], idx_sem.at[s])

    def scatter(s_idx, ys, rows):
        for r in rows:
            dst = pl.multiple_of(idx_smem[s_idx, r], ROW_TILE)
            pltpu.make_async_copy(ybuf.at[ys, pl.ds(ROW_TILE * r, ROW_TILE)], y_hbm.at[pl.ds(dst, ROW_TILE)],
                                  y_sem.at[ys]).start(priority=r % 2)

    def wait_rows(s):
        pltpu.make_async_copy(ybuf.at[s], ybuf.at[s], y_sem.at[s]).wait()

    @pl.when(i == 0)
    def _():
        idx_copy(0, 0).start()
        ybuf[1] = jnp.zeros(ybuf.shape[1:], F32)

    def step(slot):
        nxt = 1 - slot
        idx_copy(i, slot).wait()
        idx_copy(i + 1, nxt).start()
        xb = _load_row_tiles(x_ref, (), tr).astype(BF16)
        acc = jnp.zeros((tr, D_MODEL), F32) + bd_ref[...]

        groups = 3 * n_chunks
        per_group = -(-tr // groups)

        def issue(g):
            scatter(slot, nxt, range(g * per_group, min((g + 1) * per_group, tr)))

        for c in range(n_chunks):
            cg = slice(c * EXPERT_CHUNK, (c + 1) * EXPERT_CHUNK)
            cu = slice(D_EXPERT + c * EXPERT_CHUNK, D_EXPERT + (c + 1) * EXPERT_CHUNK)
            gate = jnp.minimum(_dot(xb, wgu_ref[:, cg]) + bgu_ref[:, cg], SWIGLU_LIMIT)
            issue(3 * c)
            up = jnp.clip(_dot(xb, wgu_ref[:, cu]) + bgu_ref[:, cu], -SWIGLU_LIMIT, SWIGLU_LIMIT)
            issue(3 * c + 1)
            act = gate * (1.0 / (1.0 + jnp.exp(-SWIGLU_ALPHA * gate))) * (up + 1.0)
            acc = acc + _dot(act.astype(BF16), wd_ref[cg, :])
            issue(3 * c + 2)

        @pl.when(i >= 1)
        def _():
            wait_rows(slot)

        _store_row_tiles(ybuf, (slot,), acc)

        @pl.when(i == last)
        def _():
            idx_copy(i + 1, nxt).wait()
            wait_rows(nxt)

    for parity in range(2):
        pl.when(i % 2 == parity)(functools.partial(step, parity))


def _moe(blk_exp, blk_row, idx_rows, x_rows, w_gu, b_gu, w_d, b_d, n_assign):
    tr = EXPERT_ROWS
    n_steps = blk_exp.shape[0]
    assert n_steps >= 2 and idx_rows.shape == (n_steps + 1, tr)
    anyspec = pl.BlockSpec(memory_space=pl.ANY)
    emap3 = lambda i, be, br: (be[i], 0, 0)
    grid_spec = pltpu.PrefetchScalarGridSpec(
        num_scalar_prefetch=2,
        grid=(n_steps,),
        in_specs=[
            anyspec, pl.BlockSpec((tr * ROW_TILE, LANES), lambda i, be, br: (br[i], 0)),
            pl.BlockSpec((None, D_MODEL, 2 * D_EXPERT), emap3), pl.BlockSpec((None, 1, 2 * D_EXPERT), emap3),
            pl.BlockSpec((None, D_EXPERT, D_MODEL), emap3), pl.BlockSpec((None, 1, D_MODEL), emap3),
        ],
        out_specs=anyspec,
        scratch_shapes=[
            pltpu.SMEM((2, tr), I32), pltpu.VMEM((2, tr * ROW_TILE, LANES), F32),
            pltpu.VMEM((D_MODEL, 2 * D_EXPERT), BF16), pltpu.VMEM((D_EXPERT, D_MODEL), BF16),
            pltpu.SemaphoreType.DMA((2,)), pltpu.SemaphoreType.DMA((2,)),
        ],
    )
    return pl.pallas_call(
        _moe_kernel,
        grid_spec=grid_spec,
        out_shape=jax.ShapeDtypeStruct(((n_assign + 2 * tr) * ROW_TILE, LANES), F32),
        compiler_params=_params("arbitrary"),
        name="moe_experts",
    )(blk_exp, blk_row, idx_rows, x_rows, w_gu, b_gu, w_d, b_d)


def _combine_kernel(x2_ref, gate_ref, g3_ref, b3_ref, *rest):
    y_refs, out_ref = rest[:TOP_K], rest[TOP_K]
    gates = gate_ref[...]
    tm = out_ref.shape[0]
    moe = jnp.zeros(out_ref.shape, F32)
    for k in range(TOP_K):
        moe = moe + gates[:, k:k + 1] * _load_row_tiles(y_refs[k], (), tm)
    out_ref[...] = _layer_norm(DEEPNORM_ALPHA * _load_row_tiles(x2_ref, (), tm) + moe, g3_ref[...], b3_ref[...])


def _combine(x2_all, y_tok, gates_all, g3, b3, row0, n_rows, tm):
    n_all = gates_all.shape[0]
    assert row0 % tm == 0 and n_rows % tm == 0 and n_all % tm == 0
    off = row0 // tm
    tok = lambda i: (off + i, 0)
    tiles = (tm * ROW_TILE, LANES)
    slot_specs = [pl.BlockSpec(tiles, functools.partial(lambda i, k: (k * (n_all // tm) + off + i, 0), k=k))
                  for k in range(TOP_K)]
    return pl.pallas_call(
        _combine_kernel,
        grid=(n_rows // tm,),
        in_specs=[pl.BlockSpec(tiles, tok), pl.BlockSpec((tm, TOP_K), tok), _full((1, D_MODEL)),
                  _full((1, D_MODEL))] + slot_specs,
        out_specs=pl.BlockSpec((tm, D_MODEL), lambda i: (i, 0)),
        out_shape=jax.ShapeDtypeStruct((n_rows, D_MODEL), F32),
        compiler_params=_params("arbitrary"),
        name="combine_ln3",
    )(x2_all, gates_all, g3, b3, *([y_tok] * TOP_K))


def _rope_tables(pos):
    inv_freq = ROPE_THETA ** (-jnp.arange(ROPE_HALF, dtype=F32) / ROPE_HALF)
    ang = pos.astype(F32)[:, None] * inv_freq[None, :]
    c, s, z = jnp.cos(ang), jnp.sin(ang), jnp.zeros_like(ang)
    return jnp.concatenate([c, z, c, z], axis=1), jnp.concatenate([-s, z, s, z], axis=1)


def _pad_rope_cols(w):
    z = jnp.zeros(w.shape[:-1] + (ROPE_HALF,), w.dtype)
    return jnp.concatenate([w[..., :ROPE_HALF], z, w[..., ROPE_HALF:], z], axis=-1)


def _layer_weights(l, w_in, w_pool, pool_scale, q_norm_g, w_uq, kv_norm_g, w_uk, w_uv, w_out, ln1_g, ln1_b,
                   w_mq, w_mo, ln2_g, ln2_b, w_router, b_router):
    o3 = POOL_WIDTH + Q_RANK + KV_RANK
    w_in_p = jnp.concatenate([w_in[l][:, :o3], _pad_rope_cols(w_in[l][:, o3:])], axis=1).astype(BF16)
    uq = w_uq[l].reshape(Q_RANK, MLA_HEADS, QK_NOPE_DIM + QK_ROPE_DIM)
    uq_p = jnp.concatenate([uq[..., :QK_NOPE_DIM], _pad_rope_cols(uq[..., QK_NOPE_DIM:])], axis=-1)
    uq_p = uq_p.reshape(Q_RANK, MLA_HEADS * QK_PAD).astype(BF16)
    w_ukt = jnp.transpose(w_uk[l], (1, 2, 0)).astype(BF16)
    w_uv_h = jnp.transpose(w_uv[l], (1, 0, 2)).astype(BF16)
    row = lambda v: v[l].reshape(1, -1)
    w_rt = w_router[l].T
    w_rt_hi = w_rt.astype(BF16)
    w_r_split = jnp.stack([w_rt_hi, (w_rt - w_rt_hi.astype(F32)).astype(BF16)])
    proj =[w_in_p, row(q_norm_g), uq_p, row(kv_norm_g), w_ukt, w_pool[l].astype(BF16), row(pool_scale)]
    post = [w_out[l][:POOL_WIDTH].astype(BF16), w_out[l][POOL_WIDTH:].astype(BF16), row(ln1_g), row(ln1_b),
            w_mq[l].astype(BF16), w_mo[l].astype(BF16), row(ln2_g), row(ln2_b), w_r_split, row(b_router)]
    return proj, w_uv_h, post


def _routing_tables(idx_all, rank_all, counts, n_blocks):
    tr = EXPERT_ROWS
    cnt = counts.reshape(-1).astype(I32)
    padded = ((cnt + tr - 1) // tr) * tr
    pend = jnp.cumsum(padded)
    pstart = pend - padded
    dest2d = pstart[idx_all] + rank_all
    dest = dest2d.reshape(-1)
    n_all = idx_all.shape[0]
    m = dest.shape[0]
    n_rows = n_blocks * tr
    disp_rows = (dest2d * ROW_TILE).reshape(n_all // DISPATCH_BLOCK, DISPATCH_BLOCK * TOP_K)
    assign = (jnp.arange(TOP_K, dtype=I32)[None, :] * n_all + jnp.arange(n_all, dtype=I32)[:, None]).reshape(-1)
    _, packed = lax.sort_key_val(dest, assign)
    start = jnp.cumsum(cnt) - cnt
    shift = pstart - start
    none = jnp.full((n_rows,), -1, I32)
    doubled = jnp.concatenate([none, packed, none[:n_rows - m]])
    rows = jnp.arange(n_rows, dtype=I32)
    row_src = none
    for e in range(N_EXPERTS):
        moved = lax.dynamic_slice(doubled, (n_rows - shift[e],), (n_rows,))
        row_src = jnp.where((rows >= pstart[e]) & (rows < pstart[e] + cnt[e]), moved, row_src)
    spare = m + jnp.arange(n_rows, dtype=I32) % (2 * tr)
    dst_rows = (jnp.where(row_src >= 0, row_src, spare) * ROW_TILE).reshape(n_blocks, tr)
    spare_row = spare[:tr][None] * ROW_TILE
    idx_rows = jnp.concatenate([spare_row, dst_rows, spare_row], axis=0)
    n_used = pend[-1] // tr
    blk_row = jnp.minimum(jnp.arange(n_blocks + 1, dtype=I32), n_used - 1)
    blk_exp = jnp.sum((pend[None, :] <= (blk_row * tr)[:, None]).astype(I32), axis=1)
    return (jnp.clip(blk_exp, 0, N_EXPERTS - 1), blk_row, idx_rows, disp_rows, pstart + cnt, padded - cnt,
            n_used.astype(I32).reshape(1))


def kernel(x_prompt, x_sample, mem_prompt, cache_kv_latent, cache_k_rope, cache_mem_k, cache_mem_v, state_pool,
           page_table, w_in, w_pool, pool_scale, q_norm_g, w_uq, kv_norm_g, w_uk, w_uv, w_out, ln1_g, ln1_b,
           w_mq, w_mk, w_mv, w_mo, ln2_g, ln2_b, w_router, b_router, w_gate_up, b_gate_up, w_down, b_down,
           ln3_g, ln3_b):
    n_bp, seq, _ = x_prompt.shape
    n_bs, n_new, _ = x_sample.shape
    n_mem = mem_prompt.shape[1]
    page = cache_kv_latent.shape[2]
    past_len = page_table.shape[1] * page
    n_p = n_bp * seq
    n_s = n_bs * n_new
    n_all = n_p + n_s
    n_assign = n_all * TOP_K
    assert w_in.shape[0] == DEPTH == 1
    assert seq % TOKEN_BLOCK == 0 and seq % ATTN_K_BLOCK == 0 and seq % ATTN_Q_BLOCK == 0 and seq >= POOL_HALO
    assert n_p % n_s == 0 and (n_s % TOKEN_BLOCK == 0 or TOKEN_BLOCK % n_s == 0)
    l = 0

    proj_w, w_uv_h, post_w = _layer_weights(l, w_in, w_pool, pool_scale, q_norm_g, w_uq, kv_norm_g, w_uk, w_uv,
                                            w_out, ln1_g, ln1_b, w_mq, w_mo, ln2_g, ln2_b, w_router, b_router)

    xp = x_prompt.reshape(n_p, D_MODEL)
    cos_p, sin_p = _rope_tables(jnp.arange(seq))
    qcat, kcat, lat_p, kr_p, pooly_p, pstate_p = _proj_prompt(xp, proj_w, cos_p, sin_p, n_bp, seq)
    mlay_p = _attn_prompt(qcat, kcat, w_uv_h, n_bp, seq)
    mk, mv, mk_b, mv_b = _mem_kv(mem_prompt.reshape(n_bp * n_mem, D_MODEL), w_mk[l].astype(BF16),
                                 w_mv[l].astype(BF16), n_bp, n_mem)
    x2_all, idx_all, gate_all, rank_all, cnt_p = _post_prompt(xp, pooly_p, mlay_p, mk_b, mv_b, post_w,
                                                              n_bp, seq, n_mem, n_all)

    xs = jnp.transpose(x_sample, (1, 0, 2)).reshape(n_s, D_MODEL)
    state_t = jnp.transpose(state_pool[l], (1, 0, 2))
    cos_s, sin_s = _rope_tables(jnp.repeat(past_len + jnp.arange(n_new), n_bs))
    qcat_s, kcat_s, lat_s, kr_s, pooly_s, pstate_s = _proj_sample(xs, state_t, proj_w, cos_s, sin_s, past_len)
    q_b = jnp.transpose(qcat_s.reshape(MLA_HEADS, n_new, n_bs, QK_PAD), (2, 0, 1, 3)).reshape(
        n_bs, MLA_HEADS * n_new, QK_PAD)
    knew_b = jnp.transpose(kcat_s.reshape(n_new, n_bs, QK_PAD), (1, 0, 2))
    knew_b = jnp.pad(knew_b, ((0, 0), (0, LANES - n_new), (0, 0)))
    mlay_b = _attn_sample(page_table, q_b, knew_b, w_uv_h, cache_kv_latent,
                          jnp.transpose(cache_k_rope, (0, 1, 3, 2)), l)
    mlay_s = jnp.transpose(mlay_b, (1, 0, 2)).reshape(n_s, POOL_WIDTH).astype(BF16)
    x1_s, qm_s = _mix_sample(xs, pooly_s, mlay_s, post_w)
    qm_b = jnp.transpose(qm_s.reshape(n_new, n_bs, MEM_HEADS, MEM_HEAD_DIM), (1, 2, 0, 3)).reshape(
        n_bs, MEM_HEADS * n_new, MEM_HEAD_DIM)
    o_b = _mem_attn_sample(qm_b, cache_mem_k[l].reshape(n_bs, n_mem * MEM_HEADS, MEM_HEAD_DIM),
                           cache_mem_v[l].reshape(n_bs, n_mem * MEM_HEADS, MEM_HEAD_DIM), n_new)
    o_s = jnp.transpose(o_b, (1, 0, 2)).reshape(n_s, MEM_WIDTH)
    x2_all, idx_all, gate_all, rank_all, counts = _route_sample(x1_s, o_s, cnt_p, post_w, x2_all, idx_all, gate_all,
                                                               rank_all, n_p)

    n_blocks = -(-n_assign // EXPERT_ROWS) + N_EXPERTS + 1
    blk_exp, blk_row, idx_rows, disp_rows, pad_start, n_pad, n_used = _routing_tables(idx_all, rank_all, counts,
                                                                                      n_blocks)
    x_rows = _dispatch(pad_start, n_pad, n_used, disp_rows, x2_all, n_blocks * EXPERT_ROWS)
    y_tok = _moe(blk_exp, blk_row, idx_rows, x_rows, w_gate_up[l], b_gate_up[l][:, None, :],
                 w_down[l], b_down[l][:, None, :], n_assign)
    g3, b3 = ln3_g[l].reshape(1, -1), ln3_b[l].reshape(1, -1)
    y_p = _combine(x2_all, y_tok, gate_all, g3, b3, 0, n_p, TOKEN_BLOCK)
    y_s = _combine(x2_all, y_tok, gate_all, g3, b3, n_p, n_s, min(TOKEN_BLOCK, n_s))

    def from_steps(a, width):
        return jnp.transpose(a.reshape(n_new, n_bs, width), (1, 0, 2))[None]

    return (
        y_p.reshape(n_bp, seq, D_MODEL),
        from_steps(y_s, D_MODEL)[0],
        lat_p.reshape(1, n_bp, seq, KV_RANK),
        kr_p.reshape(1, n_bp, seq, QK_ROPE_DIM),
        pstate_p[None, :, POOL_HALO - POOL_BUF:, :],
        mk.reshape(1, n_bp, n_mem, MEM_HEADS, MEM_HEAD_DIM),
        mv.reshape(1, n_bp, n_mem, MEM_HEADS, MEM_HEAD_DIM),
        from_steps(lat_s, KV_RANK),
        from_steps(kr_s, QK_ROPE_DIM),
        jnp.transpose(pstate_s, (1, 0, 2))[None],
    )
```
